```python
import math
import jax
import jax.numpy as jnp
from jax import lax

D_MODEL = 1024
BATCH = 4
SEQ = 4096
DEPTH = 2

F32 = jnp.float32
MEM_LEN = 256
MAX_OFFSET = 1024
BLOCK = 128
EPS = 1e-6
NEG_INF = -1e30
ROPE_BASE = 10000.0
BRANCH_W = 512
N_BRANCH = 4

RET_HEADS = 4
RET_DK = 64
RET_DV = BRANCH_W // RET_HEADS
RET_QK = RET_HEADS * RET_DK

SSM_HEADDIM = 64
SSM_INNER = BRANCH_W
SSM_HEADS = SSM_INNER // SSM_HEADDIM
SSM_GROUPS = 2
SSM_STATE = 128
SSM_CONV = 4
SSM_CHUNK = 128
SSM_XBC = SSM_INNER + 2 * SSM_GROUPS * SSM_STATE
DT_MIN = 1e-3
DT_MAX = 1e-1

MLA_HEADS = 4
MLA_Q_RANK = 256
MLA_KV_RANK = 128
MLA_NOPE = 64
MLA_ROPE = 32
MLA_V = BRANCH_W // MLA_HEADS
MLA_QH = MLA_NOPE + MLA_ROPE

DIFF_HEADS = 4
DIFF_DH = BRANCH_W // (2 * DIFF_HEADS)
DIFF_W = BRANCH_W

REL_BUCKETS = 32
REL_MAX_DIST = 128

XA_HEADS = 4
XA_DH = D_MODEL // XA_HEADS

D_FF = ((8 * D_MODEL // 3 + 127) // 128) * 128

IN_SIZES = (RET_QK, RET_QK, BRANCH_W, BRANCH_W,
            SSM_INNER, SSM_XBC, SSM_HEADS,
            MLA_Q_RANK, MLA_KV_RANK, MLA_ROPE,
            DIFF_W, DIFF_W, DIFF_W,
            N_BRANCH * D_MODEL)
IN_TOTAL = sum(IN_SIZES)

kernel_name = 'hybrid_gated_retention_ssd_mla_diffattn'


def split_last(x, sizes):
    out, start = [], 0
    for s in sizes:
        out.append(x[..., start:start + s])
        start += s
    return out


def rms_norm(x, g):
    xf = x.astype(F32)
    y = xf * lax.rsqrt(jnp.mean(xf * xf, axis=-1, keepdims=True) + EPS)
    return (y * g.astype(F32)).astype(x.dtype)


def group_norm(x):
    xf = x.astype(F32)
    mu = jnp.mean(xf, axis=-1, keepdims=True)
    xc = xf - mu
    return xc * lax.rsqrt(jnp.mean(xc * xc, axis=-1, keepdims=True) + EPS)


def rope(x, pos):
    d = x.shape[-1]
    half = d // 2
    inv = jnp.exp(-math.log(ROPE_BASE) * jnp.arange(half, dtype=F32) / half)
    ang = pos.astype(F32)[:, :, None, None] * inv
    cos, sin = jnp.cos(ang), jnp.sin(ang)
    xf = x.astype(F32)
    x1, x2 = xf[..., :half], xf[..., half:]
    return jnp.concatenate([x1 * cos - x2 * sin, x1 * sin + x2 * cos], axis=-1).astype(x.dtype)


def swiglu(x, w_gate, w_up, w_down):
    return (jax.nn.silu(x @ w_gate) * (x @ w_up)) @ w_down


def t5_bucket(rel):
    n = jnp.maximum(rel, 0)
    exact = REL_BUCKETS // 2
    nf = jnp.maximum(n, 1).astype(F32)
    large = exact + (jnp.log(nf / exact) / math.log(REL_MAX_DIST / exact)
                     * (REL_BUCKETS - exact)).astype(jnp.int32)
    return jnp.where(n < exact, n, jnp.minimum(large, REL_BUCKETS - 1))


def retention(rq, rk, rv, rg, pos):
    B_, S, _ = rq.shape
    H, dk, dv, C = RET_HEADS, RET_DK, RET_DV, BLOCK
    N = S // C
    q = rope(rq.reshape(B_, S, H, dk), pos).astype(F32)
    k = rope(rk.reshape(B_, S, H, dk), pos).astype(F32) * (dk ** -0.5)
    v = rv.reshape(B_, S, H, dv).astype(F32)
    log_g = jnp.log1p(-jnp.exp2(-5.0 - jnp.arange(H, dtype=F32)))
    i = jnp.arange(C, dtype=F32)
    rel = i[:, None] - i[None, :]
    intra = jnp.where(rel >= 0, jnp.exp(log_g[:, None, None] * jnp.maximum(rel, 0.0)), 0.0)
    q_dec = jnp.exp(log_g[:, None] * (i + 1.0))
    k_dec = jnp.exp(log_g[:, None] * (C - 1.0 - i))
    c_dec = jnp.exp(log_g * C)
    qc = q.reshape(B_, N, C, H, dk)
    kc = k.reshape(B_, N, C, H, dk)
    vc = v.reshape(B_, N, C, H, dv)
    att = jnp.einsum('bnihd,bnjhd->bnhij', qc, kc) * intra
    inner = jnp.einsum('bnhij,bnjhe->bnihe', att, vc)
    kv = jnp.einsum('bnjhd,hj,bnjhe->nbhde', kc, k_dec, vc)

    def step(state, kv_n):
        return state * c_dec[None, :, None, None] + kv_n, state

    _, prev = lax.scan(step, jnp.zeros_like(kv[0]), kv)
    cross = jnp.einsum('bnihd,nbhde,hi->bnihe', qc, prev, q_dec)
    o = group_norm((inner + cross).reshape(B_, S, H, dv))
    return (o.reshape(B_, S, H * dv) * jax.nn.silu(rg.astype(F32))).astype(rq.dtype)


def mamba2_ssd(z, xbc, dt, conv_w, conv_b, dt_bias, a_log, d_skip, norm_g):
    B_, S, ch = xbc.shape
    H, P, G, NS, L = SSM_HEADS, SSM_HEADDIM, SSM_GROUPS, SSM_STATE, SSM_CHUNK
    HG = H // G
    Nc = S // L
    conv = lax.conv_general_dilated(xbc, conv_w[:, None, :], window_strides=(1,),
                                    padding=[(SSM_CONV - 1, 0)],
                                    dimension_numbers=('NWC', 'WIO', 'NWC'),
                                    feature_group_count=ch)
    xbc = jax.nn.silu((conv + conv_b).astype(F32))
    xs, Bm, Cm = split_last(xbc, (SSM_INNER, G * NS, G * NS))
    dt = jax.nn.softplus(dt.astype(F32) + dt_bias.astype(F32))
    A = -jnp.exp(a_log.astype(F32))
    x_h = xs.reshape(B_, S, H, P)
    xdt = (x_h * dt[..., None]).reshape(B_, Nc, L, G, HG, P)
    dA = (dt * A).reshape(B_, Nc, L, G, HG)
    Bc = Bm.reshape(B_, Nc, L, G, NS)
    Cc = Cm.reshape(B_, Nc, L, G, NS)
    cs = jnp.cumsum(dA, axis=2)
    causal = (jnp.arange(L)[:, None] >= jnp.arange(L)[None, :]).astype(F32)
    seg = cs[:, :, :, None] - cs[:, :, None, :]
    decay = jnp.exp(jnp.minimum(seg, 0.0)) * causal[None, None, :, :, None, None]
    CB = jnp.einsum('bclgn,bcsgn->bclsg', Cc, Bc)
    y_diag = jnp.einsum('bclsg,bclsgk,bcsgkp->bclgkp', CB, decay, xdt)
    decay_states = jnp.exp(cs[:, :, -1:] - cs)
    states = jnp.einsum('bclgn,bclgk,bclgkp->cbgkpn', Bc, decay_states, xdt)
    chunk_decay = jnp.moveaxis(jnp.exp(cs[:, :, -1]), 1, 0)

    def step(hs, inp):
        s_c, a_c = inp
        return hs * a_c[..., None, None] + s_c, hs

    _, h_prev = lax.scan(step, jnp.zeros_like(states[0]), (states, chunk_decay))
    y_off = jnp.einsum('bclgn,cbgkpn,bclgk->bclgkp', Cc, h_prev, jnp.exp(cs))
    y = (y_diag + y_off).reshape(B_, S, H, P) + x_h * d_skip.astype(F32)[:, None]
    y = y.reshape(B_, S, SSM_INNER) * jax.nn.silu(z.astype(F32))
    return rms_norm(y, norm_g).astype(z.dtype)


def causal_block_attention(q, k, v, scale):
    B_, S, H, dk = q.shape
    nb = S // BLOCK
    qb = jnp.moveaxis(q.reshape(B_, nb, BLOCK, H, dk), 1, 0)
    kidx = jnp.arange(S)

    def one(args):
        qi, bi = args
        s = jnp.einsum('bqhd,bkhd->bhqk', qi, k).astype(F32) * scale
        qidx = bi * BLOCK + jnp.arange(BLOCK)
        s = jnp.where(kidx[None, :] <= qidx[:, None], s, NEG_INF)
        p = jax.nn.softmax(s, axis=-1).astype(v.dtype)
        return jnp.einsum('bhqk,bkhe->bqhe', p, v)

    o = lax.map(one, (qb, jnp.arange(nb)))
    return jnp.moveaxis(o, 0, 1).reshape(B_, S, H, v.shape[-1])


def mla(c_q, c_kv, k_rope, pos, q_norm, kv_norm, w_uq, w_ukv):
    B_, S, _ = c_q.shape
    H = MLA_HEADS
    q = (rms_norm(c_q, q_norm) @ w_uq).reshape(B_, S, H, MLA_QH)
    q_nope, q_pe = q[..., :MLA_NOPE], q[..., MLA_NOPE:]
    q = jnp.concatenate([q_nope, rope(q_pe, pos)], axis=-1)
    kv = (rms_norm(c_kv, kv_norm) @ w_ukv).reshape(B_, S, H, MLA_NOPE + MLA_V)
    k_nope, v = kv[..., :MLA_NOPE], kv[..., MLA_NOPE:]
    k_pe = jnp.broadcast_to(rope(k_rope[:, :, None, :], pos), (B_, S, H, MLA_ROPE))
    k = jnp.concatenate([k_nope, k_pe], axis=-1)
    o = causal_block_attention(q, k, v, MLA_QH ** -0.5)
    return o.reshape(B_, S, H * MLA_V)


def diff_attention(dq, dk, dv, pos, rel_bias, lam_params, lambda_init, sub_g):
    B_, S, _ = dq.shape
    H, dh = DIFF_HEADS, DIFF_DH
    nb = S // BLOCK
    scale = dh ** -0.5
    lp = lam_params.astype(F32)
    lam = jnp.exp(jnp.sum(lp[0] * lp[1])) - jnp.exp(jnp.sum(lp[2] * lp[3])) + lambda_init
    k = dk.reshape(B_, S, 2 * H, dh)
    v = dv.reshape(B_, S, H, 2 * dh)
    qb = jnp.moveaxis(dq.reshape(B_, nb, BLOCK, 2 * H, dh), 1, 0)
    pb = jnp.moveaxis(pos.reshape(B_, nb, BLOCK), 1, 0)
    kidx = jnp.arange(S)
    table = rel_bias.astype(F32)

    def one(args):
        qi, pi, bi = args
        s = jnp.einsum('bqnd,bknd->bnqk', qi, k).astype(F32) * scale
        bias = jnp.moveaxis(table[t5_bucket(pi[:, :, None] - pos[:, None, :])], -1, 1)
        s = s.reshape(B_, H, 2, BLOCK, S) + bias[:, :, None]
        qidx = bi * BLOCK + jnp.arange(BLOCK)
        s = jnp.where(kidx[None, :] <= qidx[:, None], s, NEG_INF)
        p = jax.nn.softmax(s, axis=-1)
        a = (p[:, :, 0] - lam * p[:, :, 1]).astype(v.dtype)
        return jnp.einsum('bhqk,bkhe->bqhe', a, v)

    o = lax.map(one, (qb, pb, jnp.arange(nb)))
    o = jnp.moveaxis(o, 0, 1).reshape(B_, S, H, 2 * dh)
    o = rms_norm(o, sub_g) * (1.0 - lambda_init)
    return o.reshape(B_, S, H * 2 * dh)


def cross_attention(hn, mn, wq, wk, wv, wo):
    B_, S, D = hn.shape
    M = mn.shape[1]
    q = (hn @ wq).reshape(B_, S, XA_HEADS, XA_DH)
    k = (mn @ wk).reshape(B_, M, XA_HEADS, XA_DH)
    v = (mn @ wv).reshape(B_, M, XA_HEADS, XA_DH)
    s = jnp.einsum('bqhd,bkhd->bhqk', q, k).astype(F32) * (XA_DH ** -0.5)
    p = jax.nn.softmax(s, axis=-1).astype(v.dtype)
    o = jnp.einsum('bhqk,bkhd->bqhd', p, v).reshape(B_, S, D)
    return o @ wo


def setup_inputs(seed: int = 0) -> dict:
    key = jax.random.key(seed)
    keys = iter(list(jax.random.split(key, 48)))
    L, D = DEPTH, D_MODEL

    def nrm(shape, fan_in):
        return jax.random.normal(next(keys), shape, F32) * (fan_in ** -0.5)

    def gain(shape):
        return 1.0 + 0.02 * jax.random.normal(next(keys), shape, F32)

    x = jax.random.normal(next(keys), (BATCH, SEQ, D), F32)
    mem = jax.random.normal(next(keys), (BATCH, MEM_LEN, D), F32)
    offsets = jax.random.randint(next(keys), (BATCH, 1), 0, MAX_OFFSET, dtype=jnp.int32)
    positions = offsets + jnp.arange(SEQ, dtype=jnp.int32)[None, :]
    dt0 = jnp.exp(jax.random.uniform(next(keys), (L, SSM_HEADS), F32, math.log(DT_MIN), math.log(DT_MAX)))
    return {
        'x': x,
        'mem': mem,
        'positions': positions,
        'ffn1_norm': gain((L, D)),
        'ffn1_w_gate': nrm((L, D, D_FF), D),
        'ffn1_w_up': nrm((L, D, D_FF), D),
        'ffn1_w_down': nrm((L, D_FF, D), D_FF),
        'mix_norm': gain((L, D)),
        'w_in': nrm((L, D, IN_TOTAL), D),
        'ssm_conv_w': nrm((L, SSM_CONV, SSM_XBC), SSM_CONV),
        'ssm_conv_b': 0.02 * jax.random.normal(next(keys), (L, SSM_XBC), F32),
        'ssm_dt_bias': dt0 + jnp.log(-jnp.expm1(-dt0)),
        'ssm_a_log': jnp.log(jax.random.uniform(next(keys), (L, SSM_HEADS), F32, 1.0, 16.0)),
        'ssm_d': gain((L, SSM_HEADS)),
        'ssm_norm': gain((L, SSM_INNER)),
        'mla_q_norm': gain((L, MLA_Q_RANK)),
        'mla_kv_norm': gain((L, MLA_KV_RANK)),
        'mla_w_uq': nrm((L, MLA_Q_RANK, MLA_HEADS * MLA_QH), MLA_Q_RANK),
        'mla_w_ukv': nrm((L, MLA_KV_RANK, MLA_HEADS * (MLA_NOPE + MLA_V)), MLA_KV_RANK),
        'diff_lambda': 0.1 * jax.random.normal(next(keys), (L, 4, DIFF_DH), F32),
        'diff_norm': gain((L, 2 * DIFF_DH)),
        'rel_bias': 0.5 * jax.random.normal(next(keys), (REL_BUCKETS, DIFF_HEADS), F32),
        'w_branch': nrm((L, N_BRANCH, BRANCH_W, D), BRANCH_W),
        'w_out': nrm((L, D, D), D),
        'xa_norm': gain((L, D)),
        'mem_norm': gain((L, D)),
        'xa_wq': nrm((L, D, D), D),
        'xa_wk': nrm((L, D, D), D),
        'xa_wv': nrm((L, D, D), D),
        'xa_wo': nrm((L, D, D), D),
        'ffn2_norm': gain((L, D)),
        'ffn2_w_gate': nrm((L, D, D_FF), D),
        'ffn2_w_up': nrm((L, D, D_FF), D),
        'ffn2_w_down': nrm((L, D_FF, D), D_FF),
        'final_norm': gain((D,)),
    }


def reference(x, mem, positions, ffn1_norm, ffn1_w_gate, ffn1_w_up, ffn1_w_down,
              mix_norm, w_in, ssm_conv_w, ssm_conv_b, ssm_dt_bias, ssm_a_log, ssm_d, ssm_norm,
              mla_q_norm, mla_kv_norm, mla_w_uq, mla_w_ukv, diff_lambda, diff_norm, rel_bias,
              w_branch, w_out, xa_norm, mem_norm, xa_wq, xa_wk, xa_wv, xa_wo,
              ffn2_norm, ffn2_w_gate, ffn2_w_up, ffn2_w_down, final_norm):
    B_, S, D = x.shape
    h = x
    for l in range(DEPTH):
        h = h + 0.5 * swiglu(rms_norm(h, ffn1_norm[l]), ffn1_w_gate[l], ffn1_w_up[l], ffn1_w_down[l])
        u = rms_norm(h, mix_norm[l])
        (rq, rk, rv, rg, sz, sxbc, sdt, cq, ckv, kr, dq, dk, dv, gl) = split_last(u @ w_in[l], IN_SIZES)
        y_ret = retention(rq, rk, rv, rg, positions)
        y_ssm = mamba2_ssd(sz, sxbc, sdt, ssm_conv_w[l], ssm_conv_b[l], ssm_dt_bias[l],
                           ssm_a_log[l], ssm_d[l], ssm_norm[l])
        y_mla = mla(cq, ckv, kr, positions, mla_q_norm[l], mla_kv_norm[l], mla_w_uq[l], mla_w_ukv[l])
        lambda_init = 0.8 - 0.6 * math.exp(-0.3 * l)
        y_diff = diff_attention(dq, dk, dv, positions, rel_bias, diff_lambda[l], lambda_init, diff_norm[l])
        ys = jnp.stack([y_ret, y_ssm.astype(y_ret.dtype), y_mla.astype(y_ret.dtype),
                        y_diff.astype(y_ret.dtype)], axis=2)
        gates = jax.nn.sigmoid(gl.reshape(B_, S, N_BRANCH, D))
        merged = jnp.sum(gates * jnp.einsum('bsnw,nwd->bsnd', ys, w_branch[l]), axis=2)
        h = h + merged @ w_out[l]
        h = h + cross_attention(rms_norm(h, xa_norm[l]), rms_norm(mem, mem_norm[l]),
                                xa_wq[l], xa_wk[l], xa_wv[l], xa_wo[l])
        h = h + 0.5 * swiglu(rms_norm(h, ffn2_norm[l]), ffn2_w_gate[l], ffn2_w_up[l], ffn2_w_down[l])
    return rms_norm(h, final_norm)
```

```python
import functools
import math

import jax
import jax.numpy as jnp
from jax import lax
from jax.experimental import pallas as pl
from jax.experimental.pallas import tpu as pltpu

F32 = jnp.float32
BF16 = jnp.bfloat16

EPS = 1e-6
NEG_INF = -1e30
ROPE_BASE = 10000.0
CHUNK = 128
BRANCH_W = 512
N_BRANCH = 4

RET_HEADS = 4
RET_DK = 64
RET_DV = 128
RET_QK = RET_HEADS * RET_DK

SSM_HEADDIM = 64
SSM_INNER = 512
SSM_HEADS = 8
SSM_GROUPS = 2
SSM_STATE = 128
SSM_CONV = 4
SSM_XBC = SSM_INNER + 2 * SSM_GROUPS * SSM_STATE
SSM_TAIL = 8

MLA_HEADS = 4
MLA_Q_RANK = 256
MLA_KV_RANK = 128
MLA_NOPE = 64
MLA_ROPE = 32
MLA_V = 128
MLA_QH = MLA_NOPE + MLA_ROPE
MLA_PAD = 128

DIFF_HEADS = 4
DIFF_DH = 64

REL_BUCKETS = 32
REL_MAX_DIST = 128
REL_TABLE = 128

XA_HEADS = 4

LANES = 128
VMEM_LIMIT = 56 * 1024 * 1024

IN_SIZES = (RET_QK, RET_QK, BRANCH_W, BRANCH_W,
            SSM_INNER, SSM_XBC, SSM_HEADS,
            MLA_Q_RANK, MLA_KV_RANK, MLA_ROPE,
            BRANCH_W, BRANCH_W, BRANCH_W,
            N_BRANCH * 1024)


def _params(*sem):
    return pltpu.CompilerParams(dimension_semantics=sem, vmem_limit_bytes=VMEM_LIMIT)


def _resident(shape):
    nd = len(shape)
    return pl.BlockSpec(shape, lambda *_: (0,) * nd, pipeline_mode=pl.Buffered(1))


def _rms(x, g):
    return x * lax.rsqrt(jnp.mean(x * x, axis=-1, keepdims=True) + EPS) * g


def _silu(x):
    return x * jax.nn.sigmoid(x)


def _dot(a, b):
    return jnp.dot(a, b, preferred_element_type=F32)


def _dot_nt(a, b):
    return lax.dot_general(a, b, (((1,), (1,)), ((), ())), preferred_element_type=F32)


def _dot_tn(a, b):
    return lax.dot_general(a, b, (((0,), (0,)), ((), ())), preferred_element_type=F32)


def _ffn_kernel(h_ref, g_ref, wg_ref, wu_ref, wd_ref, *rest, ff_chunk, final):
    if final:
        gf_ref, o_ref = rest
    else:
        (o_ref,) = rest
    h = h_ref[...]
    xn = _rms(h, g_ref[...]).astype(BF16)
    acc = jnp.zeros(h.shape, F32)
    d_ff = wg_ref.shape[1]
    for c0 in range(0, d_ff, ff_chunk):
        a = _dot(xn, wg_ref[:, c0:c0 + ff_chunk])
        b = _dot(xn, wu_ref[:, c0:c0 + ff_chunk])
        acc = acc + _dot((_silu(a) * b).astype(BF16), wd_ref[c0:c0 + ff_chunk, :])
    out = h + 0.5 * acc
    if final:
        out = _rms(out, gf_ref[...])
    o_ref[...] = out


def _ffn(h, g, wg, wu, wd, final_g=None, *, tm=512, ff_chunk=256):
    m, d = h.shape
    d_ff = wg.shape[1]
    final = final_g is not None
    in_specs = [pl.BlockSpec((tm, d), lambda i: (i, 0)), _resident((1, d)),
                _resident((d, d_ff)), _resident((d, d_ff)), _resident((d_ff, d))]
    args = [h, g, wg, wu, wd]
    if final:
        in_specs.append(_resident((1, d)))
        args.append(final_g)
    return pl.pallas_call(
        functools.partial(_ffn_kernel, ff_chunk=ff_chunk, final=final),
        grid=(m // tm,),
        in_specs=in_specs,
        out_specs=pl.BlockSpec((tm, d), lambda i: (i, 0)),
        out_shape=jax.ShapeDtypeStruct((m, d), F32),
        compiler_params=_params("parallel"),
        name="ffn",
    )(*args)


def _norm_proj_kernel(x_ref, g_ref, w_ref, *o_refs, col_chunk):
    xn = _rms(x_ref[...], g_ref[...]).astype(BF16)
    off = 0
    for o_ref in o_refs:
        width = o_ref.shape[-1]
        for c0 in range(0, width, col_chunk):
            c1 = min(c0 + col_chunk, width)
            o_ref[:, c0:c1] = _dot(xn, w_ref[:, off + c0:off + c1]).astype(o_ref.dtype)
        off += width


def _norm_proj(x, g, w, widths, dtypes, *, tm=512, col_chunk=512, name="norm_proj"):
    m, d = x.shape
    assert sum(widths) == w.shape[1]
    return pl.pallas_call(
        functools.partial(_norm_proj_kernel, col_chunk=col_chunk),
        grid=(m // tm,),
        in_specs=[pl.BlockSpec((tm, d), lambda i: (i, 0)), _resident((1, d)), _resident(w.shape)],
        out_specs=[pl.BlockSpec((tm, wd), lambda i: (i, 0)) for wd in widths],
        out_shape=[jax.ShapeDtypeStruct((m, wd), dt) for wd, dt in zip(widths, dtypes)],
        compiler_params=_params("parallel"),
        name=name,
    )(x, g, w)


def _rope_table_kernel(pos_ref, inv_ref, sign_ref, cos_ref, sin_ref):
    ang = pos_ref[...].astype(F32) * inv_ref[...]
    cos_ref[...] = jnp.cos(ang)
    sin_ref[...] = jnp.sin(ang) * sign_ref[...]


def _rope_tables(pos_col, inv_row, sign_row, *, tm=1024):
    m = pos_col.shape[0]
    return pl.pallas_call(
        _rope_table_kernel,
        grid=(m // tm,),
        in_specs=[pl.BlockSpec((tm, 1), lambda i: (i, 0)), _resident((1, LANES)), _resident((1, LANES))],
        out_specs=[pl.BlockSpec((tm, LANES), lambda i: (i, 0))] * 2,
        out_shape=[jax.ShapeDtypeStruct((m, LANES), F32)] * 2,
        compiler_params=_params("parallel"),
        name="rope_tables",
    )(pos_col, inv_row, sign_row)


def _rope_rows(period, half, start):
    lane = jnp.arange(LANES) % period
    inside = (lane >= start) & (lane < start + 2 * half)
    idx = (lane - start) % half
    inv = jnp.exp(-math.log(ROPE_BASE) * idx.astype(F32) / half)
    inv = jnp.where(inside, inv, 0.0)
    sign = jnp.where(inside, jnp.where(lane < start + half, -1.0, 1.0), 0.0)
    return inv.reshape(1, LANES).astype(F32), sign.reshape(1, LANES).astype(F32)


def _apply_rope(x, cos, sin_signed, period, half, start):
    width = x.shape[-1]
    reps = width // LANES
    if reps > 1:
        cos = jnp.concatenate([cos] * reps, axis=-1)
        sin_signed = jnp.concatenate([sin_signed] * reps, axis=-1)
    lane = lax.broadcasted_iota(jnp.int32, (1, width), 1) % period
    first = lane < start + half
    partner = jnp.where(first, pltpu.roll(x, width - half, 1), pltpu.roll(x, half, 1))
    return x * cos + partner * sin_signed


def _ret_kernel(q_ref, k_ref, v_ref, g_ref, cos_ref, sin_ref, intra_ref, qdec_ref, kdec_ref,
                sdec_ref, bmask_ref, o_ref, state_sc):
    @pl.when(pl.program_id(1) == 0)
    def _():
        state_sc[...] = jnp.zeros_like(state_sc)

    cos, sin = cos_ref[...], sin_ref[...]
    q = _apply_rope(q_ref[...], cos, sin, RET_DK, RET_DK // 2, 0) * (RET_DK ** -0.5)
    k = _apply_rope(k_ref[...], cos, sin, RET_DK, RET_DK // 2, 0)
    vb = v_ref[...].astype(BF16)
    kb = k.astype(BF16)
    state = state_sc[...]
    cross = _dot((q * qdec_ref[...]).astype(BF16), state.astype(BF16))
    head = lax.broadcasted_iota(jnp.int32, (1, RET_QK), 1) // RET_DK
    outs = []
    for h in range(RET_HEADS):
        qh = jnp.where(head == h, q, 0.0).astype(BF16)
        att = _dot_nt(qh, kb) * intra_ref[h]
        sl = slice(h * RET_DV, (h + 1) * RET_DV)
        o = _dot(att.astype(BF16), vb[:, sl]) + cross[:, sl]
        mu = jnp.mean(o, axis=-1, keepdims=True)
        oc = o - mu
        outs.append(oc * lax.rsqrt(jnp.mean(oc * oc, axis=-1, keepdims=True) + EPS))
    o_ref[...] = (jnp.concatenate(outs, axis=-1) * _silu(g_ref[...])).astype(o_ref.dtype)
    kv = _dot_tn((k * kdec_ref[...]).astype(BF16), vb)
    state_sc[...] = state * sdec_ref[...] + kv * bmask_ref[...]


def _retention_consts():
    c = CHUNK
    log_g = jnp.log1p(-jnp.exp2(-5.0 - jnp.arange(RET_HEADS, dtype=F32)))
    i = jnp.arange(c, dtype=F32)
    rel = i[:, None] - i[None, :]
    intra = jnp.where(rel >= 0, jnp.exp(log_g[:, None, None] * jnp.maximum(rel, 0.0)), 0.0)
    lg_lane = jnp.repeat(log_g, RET_DK)
    qdec = jnp.exp(lg_lane[None, :] * (i[:, None] + 1.0))
    kdec = jnp.exp(lg_lane[None, :] * (c - 1.0 - i[:, None]))
    row_h = jnp.repeat(jnp.arange(RET_HEADS), RET_DK)
    col_h = jnp.repeat(jnp.arange(RET_HEADS), RET_DV)
    bmask = (row_h[:, None] == col_h[None, :]).astype(F32)
    sdec = bmask * jnp.exp(lg_lane * c)[:, None]
    return intra, qdec, kdec, sdec, bmask


def _retention(ret_p, cos, sin, batch, seq):
    c = CHUNK
    n = seq // c
    intra, qdec, kdec, sdec, bmask = _retention_consts()
    row = lambda b, j: b * n + j
    return pl.pallas_call(
        _ret_kernel,
        grid=(batch, n),
        in_specs=[pl.BlockSpec((c, RET_QK), lambda b, j: (row(b, j), 0)),
                  pl.BlockSpec((c, RET_QK), lambda b, j: (row(b, j), 1)),
                  pl.BlockSpec((c, BRANCH_W), lambda b, j: (row(b, j), 1)),
                  pl.BlockSpec((c, BRANCH_W), lambda b, j: (row(b, j), 2)),
                  pl.BlockSpec((c, LANES), lambda b, j: (row(b, j), 0)),
                  pl.BlockSpec((c, LANES), lambda b, j: (row(b, j), 0)),
                  _resident(intra.shape), _resident(qdec.shape), _resident(kdec.shape),
                  _resident(sdec.shape), _resident(bmask.shape)],
        out_specs=pl.BlockSpec((c, BRANCH_W), lambda b, j: (row(b, j), 0)),
        out_shape=jax.ShapeDtypeStruct((batch * seq, BRANCH_W), BF16),
        scratch_shapes=[pltpu.VMEM((RET_QK, BRANCH_W), F32)],
        compiler_params=_params("parallel", "arbitrary"),
        name="retention",
    )(ret_p, ret_p, ret_p, ret_p, cos, sin, intra, qdec, kdec, sdec, bmask)


def _ssd_kernel(xbc_ref, z_ref, dt_ref, cw_ref, cb_ref, dtb_ref, alog_ref, dskip_ref, ng_ref, tri_ref,
                o_ref, xe_sc, state_sc):
    L = CHUNK
    T = SSM_TAIL
    GW = SSM_INNER // SSM_GROUPS

    @pl.when(pl.program_id(1) == 0)
    def _():
        xe_sc[0:T, :] = jnp.zeros((T, SSM_XBC), F32)
        state_sc[...] = jnp.zeros_like(state_sc)

    x_raw = xbc_ref[...]
    xe_sc[T:T + L, :] = x_raw
    conv = cb_ref[...]
    for kk in range(SSM_CONV):
        s0 = T - (SSM_CONV - 1) + kk
        conv = conv + cw_ref[kk:kk + 1, :] * xe_sc[s0:s0 + L, :]
    xe_sc[0:T, :] = x_raw[L - T:L, :]
    xa = _silu(conv)
    xs = xa[:, :SSM_INNER]
    bm = xa[:, SSM_INNER:SSM_INNER + SSM_GROUPS * SSM_STATE]
    cm = xa[:, SSM_INNER + SSM_GROUPS * SSM_STATE:]

    dtr = dt_ref[...] + dtb_ref[...]
    dt = jnp.maximum(dtr, 0.0) + jnp.log1p(jnp.exp(-jnp.abs(dtr)))
    da = dt * (-jnp.exp(alog_ref[...]))
    cs = jnp.dot(tri_ref[...], da, preferred_element_type=F32, precision=lax.Precision.HIGHEST)
    cs_last = cs[L - 1:L, :]
    xdt = xs * dt
    xdec = (xdt * jnp.exp(cs_last - cs)).astype(BF16)
    ecs = jnp.exp(cs)
    state = state_sc[...]
    li = lax.broadcasted_iota(jnp.int32, (L, L), 0)
    si = lax.broadcasted_iota(jnp.int32, (L, L), 1)
    causal = li >= si
    lane_half = lax.broadcasted_iota(jnp.int32, (1, LANES), 1) // SSM_HEADDIM
    ys = []
    new_states = []
    for g in range(SSM_GROUPS):
        bg = bm[:, g * SSM_STATE:(g + 1) * SSM_STATE].astype(BF16)
        cg = cm[:, g * SSM_STATE:(g + 1) * SSM_STATE].astype(BF16)
        cb = _dot_nt(cg, bg)
        gsl = slice(g * GW, (g + 1) * GW)
        y_off = _dot(cg, state[:, gsl].astype(BF16)) * ecs[:, gsl]
        for p in range(GW // LANES):
            sl = slice(g * GW + p * LANES, g * GW + (p + 1) * LANES)
            cs_blk = cs[:, sl]
            cs_t = cs_blk.T
            xdt_blk = xdt[:, sl]
            acc = y_off[:, p * LANES:(p + 1) * LANES]
            for hh in range(LANES // SSM_HEADDIM):
                col = cs_blk[:, hh * SSM_HEADDIM:hh * SSM_HEADDIM + 1]
                rowv = cs_t[hh * SSM_HEADDIM:hh * SSM_HEADDIM + 1, :]
                dec = jnp.where(causal, jnp.exp(jnp.minimum(col - rowv, 0.0)), 0.0)
                xm = jnp.where(lane_half == hh, xdt_blk, 0.0).astype(BF16)
                acc = acc + _dot((cb * dec).astype(BF16), xm)
            ys.append(acc)
        new_states.append(_dot_tn(bg, xdec[:, gsl]))
    y = jnp.concatenate(ys, axis=-1) + xs * dskip_ref[...]
    y = y * _silu(z_ref[...])
    o_ref[...] = _rms(y, ng_ref[...]).astype(o_ref.dtype)
    state_sc[...] = state * jnp.exp(cs_last) + jnp.concatenate(new_states, axis=-1)


def _ssd(ssm_p, conv_w, conv_b, dt_bias_full, a_log_full, d_full, norm_g, batch, seq):
    L = CHUNK
    n = seq // L
    tri = (jnp.arange(L)[:, None] >= jnp.arange(L)[None, :]).astype(F32)
    row = lambda b, j: b * n + j
    return pl.pallas_call(
        _ssd_kernel,
        grid=(batch, n),
        in_specs=[pl.BlockSpec((L, SSM_XBC), lambda b, j: (row(b, j), 0)),
                  pl.BlockSpec((L, SSM_INNER), lambda b, j: (row(b, j), 2)),
                  pl.BlockSpec((L, SSM_INNER), lambda b, j: (row(b, j), 3)),
                  _resident(conv_w.shape), _resident(conv_b.shape), _resident(dt_bias_full.shape),
                  _resident(a_log_full.shape), _resident(d_full.shape), _resident(norm_g.shape),
                  _resident(tri.shape)],
        out_specs=pl.BlockSpec((L, SSM_INNER), lambda b, j: (row(b, j), 0)),
        out_shape=jax.ShapeDtypeStruct((batch * seq, SSM_INNER), BF16),
        scratch_shapes=[pltpu.VMEM((SSM_TAIL + L, SSM_XBC), F32),
                        pltpu.VMEM((SSM_STATE, SSM_INNER), F32)],
        compiler_params=_params("parallel", "arbitrary"),
        name="ssd",
    )(ssm_p, ssm_p, ssm_p, conv_w, conv_b, dt_bias_full, a_log_full, d_full, norm_g, tri)


def _mla_prep_kernel(p_ref, cos_ref, sin_ref, qn_ref, kvn_ref, wuq_ref, wukv_ref, q_ref, k_ref, v_ref):
    cos, sin = cos_ref[...], sin_ref[...]
    kvw = MLA_HEADS * MLA_PAD
    cq = p_ref[:, :MLA_Q_RANK]
    ckv = p_ref[:, MLA_Q_RANK:MLA_Q_RANK + MLA_KV_RANK]
    kr = p_ref[:, MLA_Q_RANK + MLA_KV_RANK:]
    q = _dot(_rms(cq, qn_ref[...]).astype(BF16), wuq_ref[...])
    q = _apply_rope(q, cos, sin, MLA_PAD, MLA_ROPE // 2, MLA_NOPE) * (MLA_QH ** -0.5)
    q_ref[...] = q.astype(q_ref.dtype)
    kv = _dot(_rms(ckv, kvn_ref[...]).astype(BF16), wukv_ref[...])
    kpe = _apply_rope(kr, cos, sin, MLA_PAD, MLA_ROPE // 2, MLA_NOPE)
    k_ref[...] = (kv[:, :kvw] + jnp.concatenate([kpe] * MLA_HEADS, axis=-1)).astype(k_ref.dtype)
    v_ref[...] = kv[:, kvw:].astype(v_ref.dtype)


def _mla_prep(mla_p, cos, sin, q_norm, kv_norm, w_uq_p, w_ukv_p, *, tm=512):
    m = mla_p.shape[0]
    w = MLA_HEADS * MLA_PAD
    rows = lambda width: pl.BlockSpec((tm, width), lambda i: (i, 0))
    return pl.pallas_call(
        _mla_prep_kernel,
        grid=(m // tm,),
        in_specs=[rows(mla_p.shape[1]), rows(LANES), rows(LANES), _resident(q_norm.shape),
                  _resident(kv_norm.shape), _resident(w_uq_p.shape), _resident(w_ukv_p.shape)],
        out_specs=[rows(w)] * 3,
        out_shape=[jax.ShapeDtypeStruct((m, w), BF16)] * 3,
        compiler_params=_params("parallel"),
        name="mla_prep",
    )(mla_p, cos, sin, q_norm, kv_norm, w_uq_p, w_ukv_p)


def _softmax_step(s, v, m, l, acc):
    m_new = jnp.maximum(m, jnp.max(s, axis=-1, keepdims=True))
    alpha = jnp.exp(m - m_new)
    p = jnp.exp(s - m_new)
    l_new = alpha * l + jnp.sum(p, axis=-1, keepdims=True)
    acc_new = alpha * acc + _dot(p.astype(BF16), v)
    return m_new, l_new, acc_new


def _causal_mask(t):
    r = lax.broadcasted_iota(jnp.int32, (t, t), 0)
    c = lax.broadcasted_iota(jnp.int32, (t, t), 1)
    return c <= r


def _mla_attn_kernel(q_ref, k_ref, v_ref, o_ref, *, t):
    qi = pl.program_id(2)
    q = q_ref[...]
    dv = v_ref.shape[-1]

    def tile(j):
        off = pl.multiple_of(j * t, t)
        return k_ref[pl.ds(off, t), :], v_ref[pl.ds(off, t), :]

    def body(j, carry):
        k, v = tile(j)
        return _softmax_step(_dot_nt(q, k), v, *carry)

    init = (jnp.full((t, 1), NEG_INF, F32), jnp.zeros((t, 1), F32), jnp.zeros((t, dv), F32))
    carry = lax.fori_loop(0, qi, body, init)
    k, v = tile(qi)
    s = jnp.where(_causal_mask(t), _dot_nt(q, k), NEG_INF)
    _, l, acc = _softmax_step(s, v, *carry)
    o_ref[...] = (acc / l).astype(o_ref.dtype)


def _mla_attention(q, k, v, *, t=256):
    b, s, w = q.shape
    heads = w // MLA_PAD
    return pl.pallas_call(
        functools.partial(_mla_attn_kernel, t=t),
        grid=(b, heads, s // t),
        in_specs=[pl.BlockSpec((None, t, MLA_PAD), lambda bi, h, i: (bi, i, h)),
                  pl.BlockSpec((None, s, MLA_PAD), lambda bi, h, i: (bi, 0, h)),
                  pl.BlockSpec((None, s, MLA_V), lambda bi, h, i: (bi, 0, h))],
        out_specs=pl.BlockSpec((None, t, MLA_V), lambda bi, h, i: (bi, i, h)),
        out_shape=jax.ShapeDtypeStruct((b, s, heads * MLA_V), BF16),
        compiler_params=_params("parallel", "parallel", "arbitrary"),
        name="mla_attention",
    )(q, k, v)


def _diff_attn_kernel(pmin_ref, pmax_ref, q_ref, k_ref, v_ref, pq_ref, pk_ref, ftab_ref, lam_ref, sg_ref,
                      o_ref, *, t, lambda_init):
    bi = pl.program_id(0)
    qi = pl.program_id(2)
    q = q_ref[...]
    lane_half = lax.broadcasted_iota(jnp.int32, (1, 2 * DIFF_DH), 1) // DIFF_DH
    scale = DIFF_DH ** -0.5
    q1 = jnp.where(lane_half == 0, q, jnp.zeros_like(q)) * scale
    q2 = jnp.where(lane_half == 1, q, jnp.zeros_like(q)) * scale
    pq = pq_ref[...]
    ftab = jnp.broadcast_to(ftab_ref[...], (t, REL_TABLE))
    far_bias = ftab_ref[:, REL_TABLE - 1:REL_TABLE]
    dv = v_ref.shape[-1]

    def tile(j):
        off = pl.multiple_of(j * t, t)
        return k_ref[pl.ds(off, t), :], v_ref[pl.ds(off, t), :], pk_ref[:, pl.ds(off, t)]

    def near_bias(pk):
        idx = jnp.clip(pq - pk, 0, REL_TABLE - 1)
        cols = [jnp.take_along_axis(ftab, idx[:, c:c + LANES], axis=1) for c in range(0, t, LANES)]
        return jnp.concatenate(cols, axis=-1)

    def update(s1, s2, v, carry):
        c1 = _softmax_step(s1, v, *carry[:3])
        c2 = _softmax_step(s2, v, *carry[3:])
        return c1 + c2

    def body(j, carry):
        k, v, pk = tile(j)
        far = pmin_ref[bi, qi] - pmax_ref[bi, j] >= REL_TABLE - 1
        bias = lax.cond(far, lambda: jnp.broadcast_to(far_bias, (t, t)), lambda: near_bias(pk))
        return update(_dot_nt(q1, k) + bias, _dot_nt(q2, k) + bias, v, carry)

    one = (jnp.full((t, 1), NEG_INF, F32), jnp.zeros((t, 1), F32), jnp.zeros((t, dv), F32))
    carry = lax.fori_loop(0, qi, body, one + one)
    k, v, pk = tile(qi)
    bias = near_bias(pk)
    mask = _causal_mask(t)
    s1 = jnp.where(mask, _dot_nt(q1, k) + bias, NEG_INF)
    s2 = jnp.where(mask, _dot_nt(q2, k) + bias, NEG_INF)
    _, l1, a1, _, l2, a2 = update(s1, s2, v, carry)
    lp = lam_ref[...]
    lam = (jnp.exp(jnp.sum(lp[0:1] * lp[1:2], axis=-1, keepdims=True))
           - jnp.exp(jnp.sum(lp[2:3] * lp[3:4], axis=-1, keepdims=True)) + lambda_init)
    o = a1 / l1 - lam * (a2 / l2)
    o_ref[...] = (_rms(o, sg_ref[...]) * (1.0 - lambda_init)).astype(o_ref.dtype)


def _diff_attention(qkv, pos, ftab, lam_params, sub_g, lambda_init, *, t=256):
    b, s, w = qkv.shape
    hw = 2 * DIFF_DH
    heads = w // (3 * hw)
    nblk = s // t
    pblk = pos.reshape(b, nblk, t)
    pmin = jnp.min(pblk, axis=-1)
    pmax = jnp.max(pblk, axis=-1)
    pos_col = pos.reshape(b, s, 1)
    pos_row = pos.reshape(b, 1, s)
    grid_spec = pltpu.PrefetchScalarGridSpec(
        num_scalar_prefetch=2,
        grid=(b, heads, nblk),
        in_specs=[pl.BlockSpec((None, t, hw), lambda bi, h, i, *_: (bi, i, h)),
                  pl.BlockSpec((None, s, hw), lambda bi, h, i, *_: (bi, 0, heads + h)),
                  pl.BlockSpec((None, s, hw), lambda bi, h, i, *_: (bi, 0, 2 * heads + h)),
                  pl.BlockSpec((None, t, 1), lambda bi, h, i, *_: (bi, i, 0)),
                  pl.BlockSpec((None, 1, s), lambda bi, h, i, *_: (bi, 0, 0)),
                  pl.BlockSpec((None, 1, REL_TABLE), lambda bi, h, i, *_: (h, 0, 0)),
                  pl.BlockSpec(lam_params.shape, lambda bi, h, i, *_: (0, 0)),
                  pl.BlockSpec(sub_g.shape, lambda bi, h, i, *_: (0, 0))],
        out_specs=pl.BlockSpec((None, t, hw), lambda bi, h, i, *_: (bi, i, h)),
    )
    return pl.pallas_call(
        functools.partial(_diff_attn_kernel, t=t, lambda_init=lambda_init),
        grid_spec=grid_spec,
        out_shape=jax.ShapeDtypeStruct((b, s, heads * hw), BF16),
        compiler_params=_params("parallel", "parallel", "arbitrary"),
        name="diff_attention",
    )(pmin, pmax, qkv, qkv, qkv, pos_col, pos_row, ftab, lam_params, sub_g)


def _merge_kernel(h_ref, g_ref, y0_ref, y1_ref, y2_ref, y3_ref, wg_ref, wb_ref, wo_ref, o_ref, *, col_chunk):
    h = h_ref[...]
    d = h.shape[-1]
    u = _rms(h, g_ref[...]).astype(BF16)
    ys = (y0_ref[...], y1_ref[...], y2_ref[...], y3_ref[...])
    parts = []
    for c0 in range(0, d, col_chunk):
        acc = None
        for i, y in enumerate(ys):
            gate = jax.nn.sigmoid(_dot(u, wg_ref[:, i * d + c0:i * d + c0 + col_chunk]))
            term = gate * _dot(y, wb_ref[i, :, c0:c0 + col_chunk])
            acc = term if acc is None else acc + term
        parts.append(acc.astype(BF16))
    merged = jnp.concatenate(parts, axis=-1)
    o_ref[...] = h + _dot(merged, wo_ref[...])


def _merge(h, g, ys, w_gate, w_branch, w_out, *, tm=512, col_chunk=512):
    m, d = h.shape
    rows = lambda width: pl.BlockSpec((tm, width), lambda i: (i, 0))
    return pl.pallas_call(
        functools.partial(_merge_kernel, col_chunk=col_chunk),
        grid=(m // tm,),
        in_specs=[rows(d), _resident((1, d))] + [rows(BRANCH_W)] * N_BRANCH
                 + [_resident(w_gate.shape), _resident(w_branch.shape), _resident(w_out.shape)],
        out_specs=rows(d),
        out_shape=jax.ShapeDtypeStruct((m, d), F32),
        compiler_params=_params("parallel"),
        name="merge",
    )(h, g, *ys, w_gate, w_branch, w_out)


def _xattn_kernel(h_ref, g_ref, wq_ref, k_ref, v_ref, wo_ref, o_ref):
    h = h_ref[...]
    d = h.shape[-1]
    dh = d // XA_HEADS
    u = _rms(h, g_ref[...]).astype(BF16)
    q = (_dot(u, wq_ref[...]) * (dh ** -0.5)).astype(BF16)
    out = h
    for hd in range(XA_HEADS):
        sl = slice(hd * dh, (hd + 1) * dh)
        s = _dot_nt(q[:, sl], k_ref[:, sl])
        p = jnp.exp(s - jnp.max(s, axis=-1, keepdims=True))
        p = p / jnp.sum(p, axis=-1, keepdims=True)
        o = _dot(p.astype(BF16), v_ref[:, sl])
        out = out + _dot(o.astype(BF16), wo_ref[sl, :])
    o_ref[...] = out


def _cross_attention(h, g, wq, k, v, wo, batch, seq, *, tm=512):
    m, d = h.shape
    mem_len = k.shape[0] // batch
    per = seq // tm
    return pl.pallas_call(
        _xattn_kernel,
        grid=(m // tm,),
        in_specs=[pl.BlockSpec((tm, d), lambda i: (i, 0)), _resident((1, d)), _resident(wq.shape),
                  pl.BlockSpec((mem_len, d), lambda i: (i // per, 0)),
                  pl.BlockSpec((mem_len, d), lambda i: (i // per, 0)),
                  _resident(wo.shape)],
        out_specs=pl.BlockSpec((tm, d), lambda i: (i, 0)),
        out_shape=jax.ShapeDtypeStruct((m, d), F32),
        compiler_params=_params("parallel"),
        name="cross_attention",
    )(h, g, wq, k, v, wo)


def _split_cols(w, sizes):
    out, start = [], 0
    for s in sizes:
        out.append(w[:, start:start + s])
        start += s
    return out


def _pad_cols(w, left, total):
    return jnp.pad(w, ((0, 0), (left, total - left - w.shape[1])))


def _layer_weights(w_in, w_uq, w_ukv):
    (rq, rk, rv, rg, sz, sxbc, sdt, cq, ckv, kr, dq, dk, dv, gl) = _split_cols(w_in, IN_SIZES)
    sdt_full = jnp.repeat(sdt, SSM_HEADDIM, axis=1)
    kr_pad = _pad_cols(kr, MLA_NOPE, MLA_PAD)
    w_mix = jnp.concatenate([rq, rk, rv, rg, sxbc, sz, sdt_full, cq, ckv, kr_pad, dq, dk, dv], axis=1)
    d = w_uq.shape[0]
    uq = w_uq.reshape(d, MLA_HEADS, MLA_QH)
    uq = jnp.pad(uq, ((0, 0), (0, 0), (0, MLA_PAD - MLA_QH))).reshape(d, MLA_HEADS * MLA_PAD)
    r = w_ukv.shape[0]
    ukv = w_ukv.reshape(r, MLA_HEADS, MLA_NOPE + MLA_V)
    k_nope = jnp.pad(ukv[:, :, :MLA_NOPE], ((0, 0), (0, 0), (0, MLA_PAD - MLA_NOPE)))
    ukv_p = jnp.concatenate([k_nope.reshape(r, -1), ukv[:, :, MLA_NOPE:].reshape(r, -1)], axis=1)
    return w_mix.astype(BF16), gl.astype(BF16), uq.astype(BF16), ukv_p.astype(BF16)


def _t5_bucket_const():
    n = jnp.arange(REL_TABLE)
    exact = REL_BUCKETS // 2
    nf = jnp.maximum(n, 1).astype(F32)
    large = exact + (jnp.log(nf / exact) / math.log(REL_MAX_DIST / exact) * (REL_BUCKETS - exact)).astype(jnp.int32)
    return jnp.where(n < exact, n, jnp.minimum(large, REL_BUCKETS - 1))


MIX_WIDTHS = (3 * BRANCH_W, SSM_XBC + 2 * SSM_INNER, MLA_Q_RANK + MLA_KV_RANK + MLA_PAD, 3 * BRANCH_W)


def kernel(x, mem, positions, ffn1_norm, ffn1_w_gate, ffn1_w_up, ffn1_w_down, mix_norm, w_in, ssm_conv_w,
           ssm_conv_b, ssm_dt_bias, ssm_a_log, ssm_d, ssm_norm, mla_q_norm, mla_kv_norm, mla_w_uq, mla_w_ukv,
           diff_lambda, diff_norm, rel_bias, w_branch, w_out, xa_norm, mem_norm, xa_wq, xa_wk, xa_wv, xa_wo,
           ffn2_norm, ffn2_w_gate, ffn2_w_up, ffn2_w_down, final_norm):
    batch, seq, d = x.shape
    depth = w_in.shape[0]
    m = batch * seq
    h = x.reshape(m, d)
    mem2 = mem.reshape(batch * mem.shape[1], d)
    row = lambda a: a.reshape(1, -1)

    pos_col = positions.reshape(m, 1)
    ret_cos, ret_sin = _rope_tables(pos_col, *_rope_rows(RET_DK, RET_DK // 2, 0))
    mla_cos, mla_sin = _rope_tables(pos_col, *_rope_rows(MLA_PAD, MLA_ROPE // 2, MLA_NOPE))
    ftab = rel_bias[_t5_bucket_const()].T.reshape(DIFF_HEADS, 1, REL_TABLE)

    for l in range(depth):
        w_mix, w_gate, w_uq_p, w_ukv_p = _layer_weights(w_in[l], mla_w_uq[l], mla_w_ukv[l])
        h = _ffn(h, row(ffn1_norm[l]), ffn1_w_gate[l].astype(BF16), ffn1_w_up[l].astype(BF16),
                 ffn1_w_down[l].astype(BF16))
        ret_p, ssm_p, mla_p, diff_p = _norm_proj(h, row(mix_norm[l]), w_mix, MIX_WIDTHS, (F32, F32, F32, BF16),
                                                 name="mix_proj")
        y_ret = _retention(ret_p, ret_cos, ret_sin, batch, seq)
        y_ssm = _ssd(ssm_p, ssm_conv_w[l], row(ssm_conv_b[l]),
                     row(jnp.repeat(ssm_dt_bias[l], SSM_HEADDIM)), row(jnp.repeat(ssm_a_log[l], SSM_HEADDIM)),
                     row(jnp.repeat(ssm_d[l], SSM_HEADDIM)), row(ssm_norm[l]), batch, seq)
        mq, mk, mv = _mla_prep(mla_p, mla_cos, mla_sin, row(mla_q_norm[l]), row(mla_kv_norm[l]), w_uq_p, w_ukv_p)
        y_mla = _mla_attention(mq.reshape(batch, seq, -1), mk.reshape(batch, seq, -1),
                               mv.reshape(batch, seq, -1)).reshape(m, BRANCH_W)
        lambda_init = 0.8 - 0.6 * math.exp(-0.3 * l)
        y_diff = _diff_attention(diff_p.reshape(batch, seq, -1), positions, ftab, diff_lambda[l],
                                 row(diff_norm[l]), lambda_init).reshape(m, BRANCH_W)
        h = _merge(h, row(mix_norm[l]), (y_ret, y_ssm, y_mla, y_diff), w_gate, w_branch[l].astype(BF16),
                   w_out[l].astype(BF16))
        w_kv = jnp.concatenate([xa_wk[l], xa_wv[l]], axis=1).astype(BF16)
        xk, xv = _norm_proj(mem2, row(mem_norm[l]), w_kv, (d, d), (BF16, BF16), name="mem_kv")
        h = _cross_attention(h, row(xa_norm[l]), xa_wq[l].astype(BF16), xk, xv, xa_wo[l].astype(BF16), batch, seq)
        h = _ffn(h, row(ffn2_norm[l]), ffn2_w_gate[l].astype(BF16), ffn2_w_up[l].astype(BF16),
                 ffn2_w_down[l].astype(BF16), row(final_norm) if l == depth - 1 else None)
    return h.reshape(batch, seq, d)
```

```python
import functools
import math

import jax
import jax.numpy as jnp
from jax import lax
from jax.experimental import pallas as pl
from jax.experimental.pallas import tpu as pltpu

F32 = jnp.float32
BF16 = jnp.bfloat16

EPS = 1e-6
NEG_INF = -1e30
LOG2E = math.log2(math.e)
ROPE_BASE = 10000.0
CHUNK = 128
BRANCH_W = 512
N_BRANCH = 4

RET_HEADS = 4
RET_DK = 64
RET_DV = 128
RET_QK = RET_HEADS * RET_DK

SSM_HEADDIM = 64
SSM_INNER = 512
SSM_HEADS = 8
SSM_GROUPS = 2
SSM_STATE = 128
SSM_CONV = 4
SSM_XBC = SSM_INNER + 2 * SSM_GROUPS * SSM_STATE
SSM_TAIL = 8

MLA_HEADS = 4
MLA_Q_RANK = 256
MLA_KV_RANK = 128
MLA_NOPE = 64
MLA_ROPE = 32
MLA_V = 128
MLA_QH = MLA_NOPE + MLA_ROPE
MLA_PAD = 128
HEAD_W = 128
ATTN_TQ = 512
ATTN_TK = 512

DIFF_HEADS = 4
DIFF_DH = 64

REL_BUCKETS = 32
REL_MAX_DIST = 128
REL_TABLE = 128

XA_HEADS = 4

LANES = 128
VMEM_LIMIT = 56 * 1024 * 1024

IN_SIZES = (RET_QK, RET_QK, BRANCH_W, BRANCH_W,
            SSM_INNER, SSM_XBC, SSM_HEADS,
            MLA_Q_RANK, MLA_KV_RANK, MLA_ROPE,
            BRANCH_W, BRANCH_W, BRANCH_W,
            N_BRANCH * 1024)


def _params(*sem):
    return pltpu.CompilerParams(dimension_semantics=sem, vmem_limit_bytes=VMEM_LIMIT)


def _resident(shape):
    nd = len(shape)
    return pl.BlockSpec(shape, lambda *_: (0,) * nd, pipeline_mode=pl.Buffered(1))


def _rms(x, g):
    return x * lax.rsqrt(jnp.mean(x * x, axis=-1, keepdims=True) + EPS) * g


def _silu(x):
    return x * jax.nn.sigmoid(x)


def _dot(a, b):
    return jnp.dot(a, b, preferred_element_type=F32)


def _dot_nt(a, b):
    return lax.dot_general(a, b, (((1,), (1,)), ((), ())), preferred_element_type=F32)


def _dot_tn(a, b):
    return lax.dot_general(a, b, (((0,), (0,)), ((), ())), preferred_element_type=F32)


def _ffn_kernel(h_ref, g_ref, wg_ref, wu_ref, wd_ref, *rest, ff_chunk, final):
    if final:
        gf_ref, o_ref = rest
    else:
        (o_ref,) = rest
    h = h_ref[...]
    xn = _rms(h, g_ref[...]).astype(BF16)
    acc = jnp.zeros(h.shape, F32)
    d_ff = wg_ref.shape[1]
    for c0 in range(0, d_ff, ff_chunk):
        a = _dot(xn, wg_ref[:, c0:c0 + ff_chunk])
        b = _dot(xn, wu_ref[:, c0:c0 + ff_chunk])
        acc = acc + _dot((_silu(a) * b).astype(BF16), wd_ref[c0:c0 + ff_chunk, :])
    out = h + 0.5 * acc
    if final:
        out = _rms(out, gf_ref[...])
    o_ref[...] = out


def _ffn(h, g, wg, wu, wd, final_g=None, *, tm=512, ff_chunk=256):
    m, d = h.shape
    d_ff = wg.shape[1]
    final = final_g is not None
    in_specs = [pl.BlockSpec((tm, d), lambda i: (i, 0)), _resident((1, d)),
                _resident((d, d_ff)), _resident((d, d_ff)), _resident((d_ff, d))]
    args = [h, g, wg, wu, wd]
    if final:
        in_specs.append(_resident((1, d)))
        args.append(final_g)
    return pl.pallas_call(
        functools.partial(_ffn_kernel, ff_chunk=ff_chunk, final=final),
        grid=(m // tm,),
        in_specs=in_specs,
        out_specs=pl.BlockSpec((tm, d), lambda i: (i, 0)),
        out_shape=jax.ShapeDtypeStruct((m, d), F32),
        compiler_params=_params("parallel"),
        name="ffn",
    )(*args)


def _norm_proj_kernel(x_ref, g_ref, w_ref, *o_refs, outs, col_chunk):
    xn = _rms(x_ref[...], g_ref[...]).astype(BF16)
    off = 0
    for o_ref, (width, _, scale, transposed) in zip(o_refs, outs):
        for c0 in range(0, width, col_chunk):
            c1 = min(c0 + col_chunk, width)
            y = _dot(xn, w_ref[:, off + c0:off + c1])
            if scale != 1.0:
                y = y * scale
            if transposed:
                o_ref[c0:c1, :] = y.T.astype(o_ref.dtype)
            else:
                o_ref[:, c0:c1] = y.astype(o_ref.dtype)
        off += width


def _norm_proj(x, g, w, outs, seq, *, tm=512, col_chunk=512, name="norm_proj"):
    m, d = x.shape
    assert sum(o[0] for o in outs) == w.shape[1]
    per = seq // tm
    out_specs, out_shape = [], []
    for width, dt, _, transposed in outs:
        if transposed:
            out_specs.append(pl.BlockSpec((None, width, tm), lambda i: (i // per, 0, i % per)))
            out_shape.append(jax.ShapeDtypeStruct((m // seq, width, seq), dt))
        else:
            out_specs.append(pl.BlockSpec((tm, width), lambda i: (i, 0)))
            out_shape.append(jax.ShapeDtypeStruct((m, width), dt))
    return pl.pallas_call(
        functools.partial(_norm_proj_kernel, outs=tuple(outs), col_chunk=col_chunk),
        grid=(m // tm,),
        in_specs=[pl.BlockSpec((tm, d), lambda i: (i, 0)), _resident((1, d)), _resident(w.shape)],
        out_specs=out_specs,
        out_shape=out_shape,
        compiler_params=_params("parallel"),
        name=name,
    )(x, g, w)


def _rope_table_kernel(pos_ref, inv_ref, sign_ref, cos_ref, sin_ref):
    ang = pos_ref[...].astype(F32) * inv_ref[...]
    cos_ref[...] = jnp.cos(ang)
    sin_ref[...] = jnp.sin(ang) * sign_ref[...]


def _rope_tables(pos_col, inv_row, sign_row, *, tm=1024):
    m = pos_col.shape[0]
    return pl.pallas_call(
        _rope_table_kernel,
        grid=(m // tm,),
        in_specs=[pl.BlockSpec((tm, 1), lambda i: (i, 0)), _resident((1, LANES)), _resident((1, LANES))],
        out_specs=[pl.BlockSpec((tm, LANES), lambda i: (i, 0))] * 2,
        out_shape=[jax.ShapeDtypeStruct((m, LANES), F32)] * 2,
        compiler_params=_params("parallel"),
        name="rope_tables",
    )(pos_col, inv_row, sign_row)


def _rope_rows(period, half, start):
    lane = jnp.arange(LANES) % period
    inside = (lane >= start) & (lane < start + 2 * half)
    idx = (lane - start) % half
    inv = jnp.exp(-math.log(ROPE_BASE) * idx.astype(F32) / half)
    inv = jnp.where(inside, inv, 0.0)
    sign = jnp.where(inside, jnp.where(lane < start + half, -1.0, 1.0), 0.0)
    return inv.reshape(1, LANES).astype(F32), sign.reshape(1, LANES).astype(F32)


def _apply_rope(x, cos, sin_signed, period, half, start):
    width = x.shape[-1]
    reps = width // LANES
    if reps > 1:
        cos = jnp.concatenate([cos] * reps, axis=-1)
        sin_signed = jnp.concatenate([sin_signed] * reps, axis=-1)
    lane = lax.broadcasted_iota(jnp.int32, (1, width), 1) % period
    first = lane < start + half
    partner = jnp.where(first, pltpu.roll(x, width - half, 1), pltpu.roll(x, half, 1))
    return x * cos + partner * sin_signed


def _ret_kernel(q_ref, k_ref, v_ref, g_ref, cos_ref, sin_ref, intra_ref, qdec_ref, kdec_ref,
                sdec_ref, bmask_ref, o_ref, state_sc):
    @pl.when(pl.program_id(1) == 0)
    def _():
        state_sc[...] = jnp.zeros_like(state_sc)

    cos, sin = cos_ref[...], sin_ref[...]
    q = _apply_rope(q_ref[...], cos, sin, RET_DK, RET_DK // 2, 0) * (RET_DK ** -0.5)
    k = _apply_rope(k_ref[...], cos, sin, RET_DK, RET_DK // 2, 0)
    vb = v_ref[...].astype(BF16)
    kb = k.astype(BF16)
    state = state_sc[...]
    cross = _dot((q * qdec_ref[...]).astype(BF16), state.astype(BF16))
    head = lax.broadcasted_iota(jnp.int32, (1, RET_QK), 1) // RET_DK
    outs = []
    for h in range(RET_HEADS):
        qh = jnp.where(head == h, q, 0.0).astype(BF16)
        att = _dot_nt(qh, kb) * intra_ref[h]
        sl = slice(h * RET_DV, (h + 1) * RET_DV)
        o = _dot(att.astype(BF16), vb[:, sl]) + cross[:, sl]
        mu = jnp.mean(o, axis=-1, keepdims=True)
        oc = o - mu
        outs.append(oc * lax.rsqrt(jnp.mean(oc * oc, axis=-1, keepdims=True) + EPS))
    o_ref[...] = (jnp.concatenate(outs, axis=-1) * _silu(g_ref[...])).astype(o_ref.dtype)
    kv = _dot_tn((k * kdec_ref[...]).astype(BF16), vb)
    state_sc[...] = state * sdec_ref[...] + kv * bmask_ref[...]


def _retention_consts():
    c = CHUNK
    log_g = jnp.log1p(-jnp.exp2(-5.0 - jnp.arange(RET_HEADS, dtype=F32)))
    i = jnp.arange(c, dtype=F32)
    rel = i[:, None] - i[None, :]
    intra = jnp.where(rel >= 0, jnp.exp(log_g[:, None, None] * jnp.maximum(rel, 0.0)), 0.0)
    lg_lane = jnp.repeat(log_g, RET_DK)
    qdec = jnp.exp(lg_lane[None, :] * (i[:, None] + 1.0))
    kdec = jnp.exp(lg_lane[None, :] * (c - 1.0 - i[:, None]))
    row_h = jnp.repeat(jnp.arange(RET_HEADS), RET_DK)
    col_h = jnp.repeat(jnp.arange(RET_HEADS), RET_DV)
    bmask = (row_h[:, None] == col_h[None, :]).astype(F32)
    sdec = bmask * jnp.exp(lg_lane * c)[:, None]
    return intra, qdec, kdec, sdec, bmask


def _retention(ret_p, cos, sin, batch, seq):
    c = CHUNK
    n = seq // c
    intra, qdec, kdec, sdec, bmask = _retention_consts()
    row = lambda b, j: b * n + j
    return pl.pallas_call(
        _ret_kernel,
        grid=(batch, n),
        in_specs=[pl.BlockSpec((c, RET_QK), lambda b, j: (row(b, j), 0)),
                  pl.BlockSpec((c, RET_QK), lambda b, j: (row(b, j), 1)),
                  pl.BlockSpec((c, BRANCH_W), lambda b, j: (row(b, j), 1)),
                  pl.BlockSpec((c, BRANCH_W), lambda b, j: (row(b, j), 2)),
                  pl.BlockSpec((c, LANES), lambda b, j: (row(b, j), 0)),
                  pl.BlockSpec((c, LANES), lambda b, j: (row(b, j), 0)),
                  _resident(intra.shape), _resident(qdec.shape), _resident(kdec.shape),
                  _resident(sdec.shape), _resident(bmask.shape)],
        out_specs=pl.BlockSpec((c, BRANCH_W), lambda b, j: (row(b, j), 0)),
        out_shape=jax.ShapeDtypeStruct((batch * seq, BRANCH_W), BF16),
        scratch_shapes=[pltpu.VMEM((RET_QK, BRANCH_W), F32)],
        compiler_params=_params("parallel", "arbitrary"),
        name="retention",
    )(ret_p, ret_p, ret_p, ret_p, cos, sin, intra, qdec, kdec, sdec, bmask)


def _ssd_kernel(xbc_ref, z_ref, dt_ref, cw_ref, cb_ref, dtb_ref, alog_ref, dskip_ref, ng_ref, tri_ref,
                o_ref, xe_sc, state_sc):
    L = CHUNK
    T = SSM_TAIL
    GW = SSM_INNER // SSM_GROUPS

    @pl.when(pl.program_id(1) == 0)
    def _():
        xe_sc[0:T, :] = jnp.zeros((T, SSM_XBC), F32)
        state_sc[...] = jnp.zeros_like(state_sc)

    x_raw = xbc_ref[...]
    xe_sc[T:T + L, :] = x_raw
    conv = cb_ref[...]
    for kk in range(SSM_CONV):
        s0 = T - (SSM_CONV - 1) + kk
        conv = conv + cw_ref[kk:kk + 1, :] * xe_sc[s0:s0 + L, :]
    xe_sc[0:T, :] = x_raw[L - T:L, :]
    xa = _silu(conv)
    xs = xa[:, :SSM_INNER]
    bm = xa[:, SSM_INNER:SSM_INNER + SSM_GROUPS * SSM_STATE]
    cm = xa[:, SSM_INNER + SSM_GROUPS * SSM_STATE:]

    dtr = dt_ref[...] + dtb_ref[...]
    dt = jnp.maximum(dtr, 0.0) + jnp.log1p(jnp.exp(-jnp.abs(dtr)))
    da = dt * (-jnp.exp(alog_ref[...]))
    cs = jnp.dot(tri_ref[...], da, preferred_element_type=F32, precision=lax.Precision.HIGHEST)
    cs_last = cs[L - 1:L, :]
    xdt = xs * dt
    xdec = (xdt * jnp.exp(cs_last - cs)).astype(BF16)
    ecs = jnp.exp(cs)
    state = state_sc[...]
    li = lax.broadcasted_iota(jnp.int32, (L, L), 0)
    si = lax.broadcasted_iota(jnp.int32, (L, L), 1)
    causal = li >= si
    lane_half = lax.broadcasted_iota(jnp.int32, (1, LANES), 1) // SSM_HEADDIM
    ys = []
    new_states = []
    for g in range(SSM_GROUPS):
        bg = bm[:, g * SSM_STATE:(g + 1) * SSM_STATE].astype(BF16)
        cg = cm[:, g * SSM_STATE:(g + 1) * SSM_STATE].astype(BF16)
        cb = _dot_nt(cg, bg)
        gsl = slice(g * GW, (g + 1) * GW)
        y_off = _dot(cg, state[:, gsl].astype(BF16)) * ecs[:, gsl]
        for p in range(GW // LANES):
            sl = slice(g * GW + p * LANES, g * GW + (p + 1) * LANES)
            cs_blk = cs[:, sl]
            cs_t = cs_blk.T
            xdt_blk = xdt[:, sl]
            acc = y_off[:, p * LANES:(p + 1) * LANES]
            for hh in range(LANES // SSM_HEADDIM):
                col = cs_blk[:, hh * SSM_HEADDIM:hh * SSM_HEADDIM + 1]
                rowv = cs_t[hh * SSM_HEADDIM:hh * SSM_HEADDIM + 1, :]
                dec = jnp.where(causal, jnp.exp(jnp.minimum(col - rowv, 0.0)), 0.0)
                xm = jnp.where(lane_half == hh, xdt_blk, 0.0).astype(BF16)
                acc = acc + _dot((cb * dec).astype(BF16), xm)
            ys.append(acc)
        new_states.append(_dot_tn(bg, xdec[:, gsl]))
    y = jnp.concatenate(ys, axis=-1) + xs * dskip_ref[...]
    y = y * _silu(z_ref[...])
    o_ref[...] = _rms(y, ng_ref[...]).astype(o_ref.dtype)
    state_sc[...] = state * jnp.exp(cs_last) + jnp.concatenate(new_states, axis=-1)


def _ssd(ssm_p, conv_w, conv_b, dt_bias_full, a_log_full, d_full, norm_g, batch, seq):
    L = CHUNK
    n = seq // L
    tri = (jnp.arange(L)[:, None] >= jnp.arange(L)[None, :]).astype(F32)
    row = lambda b, j: b * n + j
    return pl.pallas_call(
        _ssd_kernel,
        grid=(batch, n),
        in_specs=[pl.BlockSpec((L, SSM_XBC), lambda b, j: (row(b, j), 0)),
                  pl.BlockSpec((L, SSM_INNER), lambda b, j: (row(b, j), 2)),
                  pl.BlockSpec((L, SSM_INNER), lambda b, j: (row(b, j), 3)),
                  _resident(conv_w.shape), _resident(conv_b.shape), _resident(dt_bias_full.shape),
                  _resident(a_log_full.shape), _resident(d_full.shape), _resident(norm_g.shape),
                  _resident(tri.shape)],
        out_specs=pl.BlockSpec((L, SSM_INNER), lambda b, j: (row(b, j), 0)),
        out_shape=jax.ShapeDtypeStruct((batch * seq, SSM_INNER), BF16),
        scratch_shapes=[pltpu.VMEM((SSM_TAIL + L, SSM_XBC), F32),
                        pltpu.VMEM((SSM_STATE, SSM_INNER), F32)],
        compiler_params=_params("parallel", "arbitrary"),
        name="ssd",
    )(ssm_p, ssm_p, ssm_p, conv_w, conv_b, dt_bias_full, a_log_full, d_full, norm_g, tri)


def _mla_prep_kernel(p_ref, cos_ref, sin_ref, qn_ref, kvn_ref, wuq_ref, wukv_ref, q_ref, k_ref, v_ref):
    cos, sin = cos_ref[...], sin_ref[...]
    kvw = MLA_HEADS * MLA_PAD
    cq = p_ref[:, :MLA_Q_RANK]
    ckv = p_ref[:, MLA_Q_RANK:MLA_Q_RANK + MLA_KV_RANK]
    kr = p_ref[:, MLA_Q_RANK + MLA_KV_RANK:]
    q = _dot(_rms(cq, qn_ref[...]).astype(BF16), wuq_ref[...])
    q = _apply_rope(q, cos, sin, MLA_PAD, MLA_ROPE // 2, MLA_NOPE) * (MLA_QH ** -0.5 * LOG2E)
    q_ref[...] = q.T.astype(q_ref.dtype)
    kv = _dot(_rms(ckv, kvn_ref[...]).astype(BF16), wukv_ref[...])
    kpe = _apply_rope(kr, cos, sin, MLA_PAD, MLA_ROPE // 2, MLA_NOPE)
    k_ref[...] = (kv[:, :kvw] + jnp.concatenate([kpe] * MLA_HEADS, axis=-1)).astype(k_ref.dtype)
    v_ref[...] = kv[:, kvw:].T.astype(v_ref.dtype)


def _mla_prep(mla_p, cos, sin, q_norm, kv_norm, w_uq_p, w_ukv_p, seq, *, tm=512):
    m = mla_p.shape[0]
    w = MLA_HEADS * MLA_PAD
    per = seq // tm
    rows = lambda width: pl.BlockSpec((tm, width), lambda i: (i, 0))
    cols = pl.BlockSpec((None, w, tm), lambda i: (i // per, 0, i % per))
    t_shape = jax.ShapeDtypeStruct((m // seq, w, seq), BF16)
    return pl.pallas_call(
        _mla_prep_kernel,
        grid=(m // tm,),
        in_specs=[rows(mla_p.shape[1]), rows(LANES), rows(LANES), _resident(q_norm.shape),
                  _resident(kv_norm.shape), _resident(w_uq_p.shape), _resident(w_ukv_p.shape)],
        out_specs=[cols, rows(w), cols],
        out_shape=[t_shape, jax.ShapeDtypeStruct((m, w), BF16), t_shape],
        compiler_params=_params("parallel"),
        name="mla_prep",
    )(mla_p, cos, sin, q_norm, kv_norm, w_uq_p, w_ukv_p)


def _flash_kernel(*refs, tq, tk, n_maps, lambda_init):
    biased = n_maps == 2
    if biased:
        (pmin_ref, pmax_ref, qt_ref, k_ref, vt_ref, pq_ref, pk_ref, ftab_ref, lam_ref, sg_ref,
         o_ref, m_sc, l_sc, acc_sc) = refs
    else:
        qt_ref, k_ref, vt_ref, o_ref, m_sc, l_sc, acc_sc = refs
    bi = pl.program_id(0)
    qi = pl.program_id(1)
    hw = HEAD_W
    heads = k_ref.shape[-1] // hw
    cols = n_maps * tq
    hs = lambda h: slice(h * hw, (h + 1) * hw)
    ws = []
    for h in range(heads):
        qt = qt_ref[hs(h), :]
        if biased:
            half = lax.broadcasted_iota(jnp.int32, (hw, 1), 0) // DIFF_DH
            zero = jnp.zeros_like(qt)
            qt = jnp.concatenate([jnp.where(half == 0, qt, zero), jnp.where(half == 1, qt, zero)], axis=1)
        ws.append(qt)
    if biased:
        pq = pq_ref[...]
        ftabs = [jnp.broadcast_to(ftab_ref[h] * LOG2E, (tk, REL_TABLE)) for h in range(heads)]
    m_sc[...] = jnp.full(m_sc.shape, NEG_INF, F32)
    l_sc[...] = jnp.zeros_like(l_sc)
    acc_sc[...] = jnp.zeros_like(acc_sc)

    def near_biases(off):
        idx = jnp.clip(pq - pk_ref[pl.ds(off, tk), :], 0, REL_TABLE - 1)
        out = []
        for h in range(heads):
            b = jnp.concatenate([jnp.take_along_axis(ftabs[h], idx[:, c:c + LANES], axis=1)
                                 for c in range(0, tq, LANES)], axis=1)
            out.append(jnp.concatenate([b, b], axis=1))
        return out

    def tile(off, bias, mask):
        for h in range(heads):
            s = _dot(k_ref[pl.ds(off, tk), hs(h)], ws[h])
            if bias is not None:
                s = s + bias[h]
            if mask is not None:
                s = jnp.where(mask, s, NEG_INF)
            m = m_sc[h]
            m_new = jnp.maximum(m, jnp.max(s, axis=0, keepdims=True))
            alpha = jnp.exp2(m - m_new)
            p = jnp.exp2(s - m_new)
            l_sc[h] = alpha * l_sc[h] + jnp.sum(p, axis=0, keepdims=True)
            acc_sc[h] = alpha * acc_sc[h] + _dot(vt_ref[hs(h), pl.ds(off, tk)], p.astype(BF16))
            m_sc[h] = m_new

    n_full = qi * (tq // tk)

    def body(j, carry):
        off = pl.multiple_of(j * tk, tk)
        if biased:
            far = pmin_ref[bi, qi] - pmax_ref[bi, j] >= REL_TABLE - 1

            @pl.when(far)
            def _():
                tile(off, [ftab_ref[h][:, REL_TABLE - 1:REL_TABLE] * LOG2E for h in range(heads)], None)

            @pl.when(jnp.logical_not(far))
            def _():
                tile(off, near_biases(off), None)
        else:
            tile(off, None, None)
        return carry

    lax.fori_loop(0, n_full, body, 0)
    for d in range(tq // tk):
        off = pl.multiple_of((n_full + d) * tk, tk)
        key = lax.broadcasted_iota(jnp.int32, (tk, cols), 0) + d * tk
        qry = lax.broadcasted_iota(jnp.int32, (tk, cols), 1) % tq
        tile(off, near_biases(off) if biased else None, key <= qry)

    if biased:
        lp = lam_ref[...]
        lam = (jnp.exp(jnp.sum(lp[0:1] * lp[1:2], axis=-1, keepdims=True))
               - jnp.exp(jnp.sum(lp[2:3] * lp[3:4], axis=-1, keepdims=True)) + lambda_init)
    for h in range(heads):
        o = acc_sc[h] / l_sc[h]
        if biased:
            o = (o[:, :tq] - lam * o[:, tq:]).T
            o_ref[:, hs(h)] = (_rms(o, sg_ref[...]) * (1.0 - lambda_init)).astype(o_ref.dtype)
        else:
            o_ref[:, hs(h)] = o.T.astype(o_ref.dtype)


def _flash_scratch(heads, cols):
    return [pltpu.VMEM((heads, 1, cols), F32), pltpu.VMEM((heads, 1, cols), F32),
            pltpu.VMEM((heads, HEAD_W, cols), F32)]


def _mla_attention(qt, k, vt, *, tq=ATTN_TQ, tk=ATTN_TK):
    b, s, w = k.shape
    return pl.pallas_call(
        functools.partial(_flash_kernel, tq=tq, tk=tk, n_maps=1, lambda_init=None),
        grid=(b, s // tq),
        in_specs=[pl.BlockSpec((None, w, tq), lambda bi, i: (bi, 0, i)),
                  pl.BlockSpec((None, s, w), lambda bi, i: (bi, 0, 0), pipeline_mode=pl.Buffered(1)),
                  pl.BlockSpec((None, w, s), lambda bi, i: (bi, 0, 0), pipeline_mode=pl.Buffered(1))],
        out_specs=pl.BlockSpec((None, tq, w), lambda bi, i: (bi, i, 0)),
        out_shape=jax.ShapeDtypeStruct((b, s, w), BF16),
        scratch_shapes=_flash_scratch(w // HEAD_W, tq),
        compiler_params=_params("parallel", "arbitrary"),
        name="mla_attention",
    )(qt, k, vt)


def _diff_attention(qt, k, vt, pos, ftab, lam_params, sub_g, lambda_init, *, tq=ATTN_TQ, tk=ATTN_TK):
    b, s, w = k.shape
    pmin = jnp.min(pos.reshape(b, s // tq, tq), axis=-1)
    pmax = jnp.max(pos.reshape(b, s // tk, tk), axis=-1)
    pos_col = pos.reshape(b, s, 1)
    pos_row = pos.reshape(b, 1, s)
    grid_spec = pltpu.PrefetchScalarGridSpec(
        num_scalar_prefetch=2,
        grid=(b, s // tq),
        in_specs=[pl.BlockSpec((None, w, tq), lambda bi, i, *_: (bi, 0, i)),
                  pl.BlockSpec((None, s, w), lambda bi, i, *_: (bi, 0, 0), pipeline_mode=pl.Buffered(1)),
                  pl.BlockSpec((None, w, s), lambda bi, i, *_: (bi, 0, 0), pipeline_mode=pl.Buffered(1)),
                  pl.BlockSpec((None, 1, tq), lambda bi, i, *_: (bi, 0, i)),
                  pl.BlockSpec((None, s, 1), lambda bi, i, *_: (bi, 0, 0), pipeline_mode=pl.Buffered(1)),
                  pl.BlockSpec(ftab.shape, lambda bi, i, *_: (0, 0, 0)),
                  pl.BlockSpec(lam_params.shape, lambda bi, i, *_: (0, 0)),
                  pl.BlockSpec(sub_g.shape, lambda bi, i, *_: (0, 0))],
        out_specs=pl.BlockSpec((None, tq, w), lambda bi, i, *_: (bi, i, 0)),
        scratch_shapes=_flash_scratch(w // HEAD_W, 2 * tq),
    )
    return pl.pallas_call(
        functools.partial(_flash_kernel, tq=tq, tk=tk, n_maps=2, lambda_init=lambda_init),
        grid_spec=grid_spec,
        out_shape=jax.ShapeDtypeStruct((b, s, w), BF16),
        compiler_params=_params("parallel", "arbitrary"),
        name="diff_attention",
    )(pmin, pmax, qt, k, vt, pos_row, pos_col, ftab, lam_params, sub_g)


def _merge_kernel(h_ref, g_ref, y0_ref, y1_ref, y2_ref, y3_ref, wg_ref, wb_ref, wo_ref, o_ref, *, col_chunk):
    h = h_ref[...]
    d = h.shape[-1]
    u = _rms(h, g_ref[...]).astype(BF16)
    ys = (y0_ref[...], y1_ref[...], y2_ref[...], y3_ref[...])
    parts = []
    for c0 in range(0, d, col_chunk):
        acc = None
        for i, y in enumerate(ys):
            gate = jax.nn.sigmoid(_dot(u, wg_ref[:, i * d + c0:i * d + c0 + col_chunk]))
            term = gate * _dot(y, wb_ref[i, :, c0:c0 + col_chunk])
            acc = term if acc is None else acc + term
        parts.append(acc.astype(BF16))
    merged = jnp.concatenate(parts, axis=-1)
    o_ref[...] = h + _dot(merged, wo_ref[...])


def _merge(h, g, ys, w_gate, w_branch, w_out, *, tm=512, col_chunk=512):
    m, d = h.shape
    rows = lambda width: pl.BlockSpec((tm, width), lambda i: (i, 0))
    return pl.pallas_call(
        functools.partial(_merge_kernel, col_chunk=col_chunk),
        grid=(m // tm,),
        in_specs=[rows(d), _resident((1, d))] + [rows(BRANCH_W)] * N_BRANCH
                 + [_resident(w_gate.shape), _resident(w_branch.shape), _resident(w_out.shape)],
        out_specs=rows(d),
        out_shape=jax.ShapeDtypeStruct((m, d), F32),
        compiler_params=_params("parallel"),
        name="merge",
    )(h, g, *ys, w_gate, w_branch, w_out)


def _xattn_kernel(h_ref, g_ref, wq_ref, k_ref, v_ref, wo_ref, o_ref):
    h = h_ref[...]
    d = h.shape[-1]
    dh = d // XA_HEADS
    u = _rms(h, g_ref[...]).astype(BF16)
    q = (_dot(u, wq_ref[...]) * (dh ** -0.5)).astype(BF16)
    out = h
    for hd in range(XA_HEADS):
        sl = slice(hd * dh, (hd + 1) * dh)
        s = _dot_nt(q[:, sl], k_ref[:, sl])
        p = jnp.exp(s - jnp.max(s, axis=-1, keepdims=True))
        p = p / jnp.sum(p, axis=-1, keepdims=True)
        o = _dot(p.astype(BF16), v_ref[:, sl])
        out = out + _dot(o.astype(BF16), wo_ref[sl, :])
    o_ref[...] = out


def _cross_attention(h, g, wq, k, v, wo, batch, seq, *, tm=512):
    m, d = h.shape
    mem_len = k.shape[0] // batch
    per = seq // tm
    return pl.pallas_call(
        _xattn_kernel,
        grid=(m // tm,),
        in_specs=[pl.BlockSpec((tm, d), lambda i: (i, 0)), _resident((1, d)), _resident(wq.shape),
                  pl.BlockSpec((mem_len, d), lambda i: (i // per, 0)),
                  pl.BlockSpec((mem_len, d), lambda i: (i // per, 0)),
                  _resident(wo.shape)],
        out_specs=pl.BlockSpec((tm, d), lambda i: (i, 0)),
        out_shape=jax.ShapeDtypeStruct((m, d), F32),
        compiler_params=_params("parallel"),
        name="cross_attention",
    )(h, g, wq, k, v, wo)


def _split_cols(w, sizes):
    out, start = [], 0
    for s in sizes:
        out.append(w[:, start:start + s])
        start += s
    return out


def _pad_cols(w, left, total):
    return jnp.pad(w, ((0, 0), (left, total - left - w.shape[1])))


def _layer_weights(w_in, w_uq, w_ukv):
    (rq, rk, rv, rg, sz, sxbc, sdt, cq, ckv, kr, dq, dk, dv, gl) = _split_cols(w_in, IN_SIZES)
    sdt_full = jnp.repeat(sdt, SSM_HEADDIM, axis=1)
    kr_pad = _pad_cols(kr, MLA_NOPE, MLA_PAD)
    w_mix = jnp.concatenate([rq, rk, rv, rg, sxbc, sz, sdt_full, cq, ckv, kr_pad, dq, dk, dv], axis=1)
    d = w_uq.shape[0]
    uq = w_uq.reshape(d, MLA_HEADS, MLA_QH)
    uq = jnp.pad(uq, ((0, 0), (0, 0), (0, MLA_PAD - MLA_QH))).reshape(d, MLA_HEADS * MLA_PAD)
    r = w_ukv.shape[0]
    ukv = w_ukv.reshape(r, MLA_HEADS, MLA_NOPE + MLA_V)
    k_nope = jnp.pad(ukv[:, :, :MLA_NOPE], ((0, 0), (0, 0), (0, MLA_PAD - MLA_NOPE)))
    ukv_p = jnp.concatenate([k_nope.reshape(r, -1), ukv[:, :, MLA_NOPE:].reshape(r, -1)], axis=1)
    return w_mix.astype(BF16), gl.astype(BF16), uq.astype(BF16), ukv_p.astype(BF16)


def _t5_bucket_const():
    n = jnp.arange(REL_TABLE)
    exact = REL_BUCKETS // 2
    nf = jnp.maximum(n, 1).astype(F32)
    large = exact + (jnp.log(nf / exact) / math.log(REL_MAX_DIST / exact) * (REL_BUCKETS - exact)).astype(jnp.int32)
    return jnp.where(n < exact, n, jnp.minimum(large, REL_BUCKETS - 1))


MIX_OUTS = ((3 * BRANCH_W, F32, 1.0, False),
            (SSM_XBC + 2 * SSM_INNER, F32, 1.0, False),
            (MLA_Q_RANK + MLA_KV_RANK + MLA_PAD, F32, 1.0, False),
            (BRANCH_W, BF16, DIFF_DH ** -0.5 * LOG2E, True),
            (BRANCH_W, BF16, 1.0, False),
            (BRANCH_W, BF16, 1.0, True))


def kernel(x, mem, positions, ffn1_norm, ffn1_w_gate, ffn1_w_up, ffn1_w_down, mix_norm, w_in, ssm_conv_w,
           ssm_conv_b, ssm_dt_bias, ssm_a_log, ssm_d, ssm_norm, mla_q_norm, mla_kv_norm, mla_w_uq, mla_w_ukv,
           diff_lambda, diff_norm, rel_bias, w_branch, w_out, xa_norm, mem_norm, xa_wq, xa_wk, xa_wv, xa_wo,
           ffn2_norm, ffn2_w_gate, ffn2_w_up, ffn2_w_down, final_norm):
    batch, seq, d = x.shape
    depth = w_in.shape[0]
    m = batch * seq
    h = x.reshape(m, d)
    mem2 = mem.reshape(batch * mem.shape[1], d)
    row = lambda a: a.reshape(1, -1)

    pos_col = positions.reshape(m, 1)
    ret_cos, ret_sin = _rope_tables(pos_col, *_rope_rows(RET_DK, RET_DK // 2, 0))
    mla_cos, mla_sin = _rope_tables(pos_col, *_rope_rows(MLA_PAD, MLA_ROPE // 2, MLA_NOPE))
    ftab = rel_bias[_t5_bucket_const()].T.reshape(DIFF_HEADS, 1, REL_TABLE)

    for l in range(depth):
        w_mix, w_gate, w_uq_p, w_ukv_p = _layer_weights(w_in[l], mla_w_uq[l], mla_w_ukv[l])
        h = _ffn(h, row(ffn1_norm[l]), ffn1_w_gate[l].astype(BF16), ffn1_w_up[l].astype(BF16),
                 ffn1_w_down[l].astype(BF16))
        ret_p, ssm_p, mla_p, dqt, dk, dvt = _norm_proj(h, row(mix_norm[l]), w_mix, MIX_OUTS, seq, name="mix_proj")
        y_ret = _retention(ret_p, ret_cos, ret_sin, batch, seq)
        y_ssm = _ssd(ssm_p, ssm_conv_w[l], row(ssm_conv_b[l]),
                     row(jnp.repeat(ssm_dt_bias[l], SSM_HEADDIM)), row(jnp.repeat(ssm_a_log[l], SSM_HEADDIM)),
                     row(jnp.repeat(ssm_d[l], SSM_HEADDIM)), row(ssm_norm[l]), batch, seq)
        mqt, mk, mvt = _mla_prep(mla_p, mla_cos, mla_sin, row(mla_q_norm[l]), row(mla_kv_norm[l]), w_uq_p,
                                 w_ukv_p, seq)
        y_mla = _mla_attention(mqt, mk.reshape(batch, seq, -1), mvt).reshape(m, BRANCH_W)
        lambda_init = 0.8 - 0.6 * math.exp(-0.3 * l)
        y_diff = _diff_attention(dqt, dk.reshape(batch, seq, -1), dvt, positions, ftab, diff_lambda[l],
                                 row(diff_norm[l]), lambda_init).reshape(m, BRANCH_W)
        h = _merge(h, row(mix_norm[l]), (y_ret, y_ssm, y_mla, y_diff), w_gate, w_branch[l].astype(BF16),
                   w_out[l].astype(BF16))
        w_kv = jnp.concatenate([xa_wk[l], xa_wv[l]], axis=1).astype(BF16)
        xk, xv = _norm_proj(mem2, row(mem_norm[l]), w_kv, [(d, BF16, 1.0, False)] * 2, mem.shape[1],
                            tm=mem.shape[1], name="mem_kv")
        h = _cross_attention(h, row(xa_norm[l]), xa_wq[l].astype(BF16), xk, xv, xa_wo[l].astype(BF16), batch, seq)
        h = _ffn(h, row(ffn2_norm[l]), ffn2_w_gate[l].astype(BF16), ffn2_w_up[l].astype(BF16),
                 ffn2_w_down[l].astype(BF16), row(final_norm) if l == depth - 1 else None)
    return h.reshape(batch, seq, d)
```

```python
import functools
import math

import jax
import jax.numpy as jnp
from jax import lax
from jax.experimental import pallas as pl
from jax.experimental.pallas import tpu as pltpu

F32 = jnp.float32
BF16 = jnp.bfloat16

EPS = 1e-6
NEG_INF = -1e30
LOG2E = math.log2(math.e)
ROPE_BASE = 10000.0
CHUNK = 128
BRANCH_W = 512
N_BRANCH = 4

RET_HEADS = 4
RET_DK = 64
RET_DV = 128
RET_QK = RET_HEADS * RET_DK

SSM_HEADDIM = 64
SSM_INNER = 512
SSM_HEADS = 8
SSM_GROUPS = 2
SSM_STATE = 128
SSM_CONV = 4
SSM_XBC = SSM_INNER + 2 * SSM_GROUPS * SSM_STATE
SSM_TAIL = 8

MLA_HEADS = 4
MLA_Q_RANK = 256
MLA_KV_RANK = 128
MLA_NOPE = 64
MLA_ROPE = 32
MLA_V = 128
MLA_QH = MLA_NOPE + MLA_ROPE
MLA_PAD = 128
HEAD_W = 128
ATTN_TQ = 512
ATTN_TK = 256

DIFF_HEADS = 4
DIFF_DH = 64

REL_BUCKETS = 32
REL_MAX_DIST = 128
REL_TABLE = 128

XA_HEADS = 4

LANES = 128
VMEM_LIMIT = 56 * 1024 * 1024

IN_SIZES = (RET_QK, RET_QK, BRANCH_W, BRANCH_W,
            SSM_INNER, SSM_XBC, SSM_HEADS,
            MLA_Q_RANK, MLA_KV_RANK, MLA_ROPE,
            BRANCH_W, BRANCH_W, BRANCH_W,
            N_BRANCH * 1024)


def _params(*sem):
    return pltpu.CompilerParams(dimension_semantics=sem, vmem_limit_bytes=VMEM_LIMIT)


def _resident(shape):
    nd = len(shape)
    return pl.BlockSpec(shape, lambda *_: (0,) * nd, pipeline_mode=pl.Buffered(1))


def _rms(x, g):
    return x * lax.rsqrt(jnp.mean(x * x, axis=-1, keepdims=True) + EPS) * g


def _silu(x):
    return x * jax.nn.sigmoid(x)


def _dot(a, b):
    return jnp.dot(a, b, preferred_element_type=F32)


def _dot_nt(a, b):
    return lax.dot_general(a, b, (((1,), (1,)), ((), ())), preferred_element_type=F32)


def _dot_tn(a, b):
    return lax.dot_general(a, b, (((0,), (0,)), ((), ())), preferred_element_type=F32)


def _ffn_kernel(h_ref, g_ref, wg_ref, wu_ref, wd_ref, *rest, ff_chunk, final):
    if final:
        gf_ref, o_ref = rest
    else:
        (o_ref,) = rest
    h = h_ref[...]
    xn = _rms(h, g_ref[...]).astype(BF16)
    acc = jnp.zeros(h.shape, F32)
    d_ff = wg_ref.shape[1]
    for c0 in range(0, d_ff, ff_chunk):
        a = _dot(xn, wg_ref[:, c0:c0 + ff_chunk])
        b = _dot(xn, wu_ref[:, c0:c0 + ff_chunk])
        acc = acc + _dot((_silu(a) * b).astype(BF16), wd_ref[c0:c0 + ff_chunk, :])
    out = h + 0.5 * acc
    if final:
        out = _rms(out, gf_ref[...])
    o_ref[...] = out


def _ffn(h, g, wg, wu, wd, final_g=None, *, tm=512, ff_chunk=256):
    m, d = h.shape
    d_ff = wg.shape[1]
    final = final_g is not None
    in_specs = [pl.BlockSpec((tm, d), lambda i: (i, 0)), _resident((1, d)),
                _resident((d, d_ff)), _resident((d, d_ff)), _resident((d_ff, d))]
    args = [h, g, wg, wu, wd]
    if final:
        in_specs.append(_resident((1, d)))
        args.append(final_g)
    return pl.pallas_call(
        functools.partial(_ffn_kernel, ff_chunk=ff_chunk, final=final),
        grid=(m // tm,),
        in_specs=in_specs,
        out_specs=pl.BlockSpec((tm, d), lambda i: (i, 0)),
        out_shape=jax.ShapeDtypeStruct((m, d), F32),
        compiler_params=_params("parallel"),
        name="ffn",
    )(*args)


def _norm_proj_kernel(x_ref, g_ref, w_ref, *o_refs, outs, col_chunk):
    xn = _rms(x_ref[...], g_ref[...]).astype(BF16)
    off = 0
    for o_ref, (width, _, scale, transposed) in zip(o_refs, outs):
        for c0 in range(0, width, col_chunk):
            c1 = min(c0 + col_chunk, width)
            y = _dot(xn, w_ref[:, off + c0:off + c1])
            if scale != 1.0:
                y = y * scale
            if transposed:
                o_ref[c0:c1, :] = y.T.astype(o_ref.dtype)
            else:
                o_ref[:, c0:c1] = y.astype(o_ref.dtype)
        off += width


def _norm_proj(x, g, w, outs, seq, *, tm=512, col_chunk=512, name="norm_proj"):
    m, d = x.shape
    assert sum(o[0] for o in outs) == w.shape[1]
    per = seq // tm
    out_specs, out_shape = [], []
    for width, dt, _, transposed in outs:
        if transposed:
            out_specs.append(pl.BlockSpec((None, width, tm), lambda i: (i // per, 0, i % per)))
            out_shape.append(jax.ShapeDtypeStruct((m // seq, width, seq), dt))
        else:
            out_specs.append(pl.BlockSpec((tm, width), lambda i: (i, 0)))
            out_shape.append(jax.ShapeDtypeStruct((m, width), dt))
    return pl.pallas_call(
        functools.partial(_norm_proj_kernel, outs=tuple(outs), col_chunk=col_chunk),
        grid=(m // tm,),
        in_specs=[pl.BlockSpec((tm, d), lambda i: (i, 0)), _resident((1, d)), _resident(w.shape)],
        out_specs=out_specs,
        out_shape=out_shape,
        compiler_params=_params("parallel"),
        name=name,
    )(x, g, w)


def _rope_table_kernel(pos_ref, inv_ref, sign_ref, cos_ref, sin_ref):
    ang = pos_ref[...].astype(F32) * inv_ref[...]
    cos_ref[...] = jnp.cos(ang)
    sin_ref[...] = jnp.sin(ang) * sign_ref[...]


def _rope_tables(pos_col, inv_row, sign_row, *, tm=1024):
    m = pos_col.shape[0]
    return pl.pallas_call(
        _rope_table_kernel,
        grid=(m // tm,),
        in_specs=[pl.BlockSpec((tm, 1), lambda i: (i, 0)), _resident((1, LANES)), _resident((1, LANES))],
        out_specs=[pl.BlockSpec((tm, LANES), lambda i: (i, 0))] * 2,
        out_shape=[jax.ShapeDtypeStruct((m, LANES), F32)] * 2,
        compiler_params=_params("parallel"),
        name="rope_tables",
    )(pos_col, inv_row, sign_row)


def _rope_rows(period, half, start):
    lane = jnp.arange(LANES) % period
    inside = (lane >= start) & (lane < start + 2 * half)
    idx = (lane - start) % half
    inv = jnp.exp(-math.log(ROPE_BASE) * idx.astype(F32) / half)
    inv = jnp.where(inside, inv, 0.0)
    sign = jnp.where(inside, jnp.where(lane < start + half, -1.0, 1.0), 0.0)
    return inv.reshape(1, LANES).astype(F32), sign.reshape(1, LANES).astype(F32)


def _apply_rope(x, cos, sin_signed, period, half, start):
    width = x.shape[-1]
    reps = width // LANES
    if reps > 1:
        cos = jnp.concatenate([cos] * reps, axis=-1)
        sin_signed = jnp.concatenate([sin_signed] * reps, axis=-1)
    lane = lax.broadcasted_iota(jnp.int32, (1, width), 1) % period
    first = lane < start + half
    partner = jnp.where(first, pltpu.roll(x, width - half, 1), pltpu.roll(x, half, 1))
    return x * cos + partner * sin_signed


def _ret_kernel(q_ref, k_ref, v_ref, g_ref, cos_ref, sin_ref, intra_ref, qdec_ref, kdec_ref,
                sdec_ref, bmask_ref, o_ref, state_sc):
    @pl.when(pl.program_id(1) == 0)
    def _():
        state_sc[...] = jnp.zeros_like(state_sc)

    cos, sin = cos_ref[...], sin_ref[...]
    q = _apply_rope(q_ref[...], cos, sin, RET_DK, RET_DK // 2, 0) * (RET_DK ** -0.5)
    k = _apply_rope(k_ref[...], cos, sin, RET_DK, RET_DK // 2, 0)
    vb = v_ref[...].astype(BF16)
    kb = k.astype(BF16)
    state = state_sc[...]
    cross = _dot((q * qdec_ref[...]).astype(BF16), state.astype(BF16))
    head = lax.broadcasted_iota(jnp.int32, (1, RET_QK), 1) // RET_DK
    outs = []
    for h in range(RET_HEADS):
        qh = jnp.where(head == h, q, 0.0).astype(BF16)
        att = _dot_nt(qh, kb) * intra_ref[h]
        sl = slice(h * RET_DV, (h + 1) * RET_DV)
        o = _dot(att.astype(BF16), vb[:, sl]) + cross[:, sl]
        mu = jnp.mean(o, axis=-1, keepdims=True)
        oc = o - mu
        outs.append(oc * lax.rsqrt(jnp.mean(oc * oc, axis=-1, keepdims=True) + EPS))
    o_ref[...] = (jnp.concatenate(outs, axis=-1) * _silu(g_ref[...])).astype(o_ref.dtype)
    kv = _dot_tn((k * kdec_ref[...]).astype(BF16), vb)
    state_sc[...] = state * sdec_ref[...] + kv * bmask_ref[...]


def _retention_consts():
    c = CHUNK
    log_g = jnp.log1p(-jnp.exp2(-5.0 - jnp.arange(RET_HEADS, dtype=F32)))
    i = jnp.arange(c, dtype=F32)
    rel = i[:, None] - i[None, :]
    intra = jnp.where(rel >= 0, jnp.exp(log_g[:, None, None] * jnp.maximum(rel, 0.0)), 0.0)
    lg_lane = jnp.repeat(log_g, RET_DK)
    qdec = jnp.exp(lg_lane[None, :] * (i[:, None] + 1.0))
    kdec = jnp.exp(lg_lane[None, :] * (c - 1.0 - i[:, None]))
    row_h = jnp.repeat(jnp.arange(RET_HEADS), RET_DK)
    col_h = jnp.repeat(jnp.arange(RET_HEADS), RET_DV)
    bmask = (row_h[:, None] == col_h[None, :]).astype(F32)
    sdec = bmask * jnp.exp(lg_lane * c)[:, None]
    return intra, qdec, kdec, sdec, bmask


def _retention(ret_p, cos, sin, batch, seq):
    c = CHUNK
    n = seq // c
    intra, qdec, kdec, sdec, bmask = _retention_consts()
    row = lambda b, j: b * n + j
    return pl.pallas_call(
        _ret_kernel,
        grid=(batch, n),
        in_specs=[pl.BlockSpec((c, RET_QK), lambda b, j: (row(b, j), 0)),
                  pl.BlockSpec((c, RET_QK), lambda b, j: (row(b, j), 1)),
                  pl.BlockSpec((c, BRANCH_W), lambda b, j: (row(b, j), 1)),
                  pl.BlockSpec((c, BRANCH_W), lambda b, j: (row(b, j), 2)),
                  pl.BlockSpec((c, LANES), lambda b, j: (row(b, j), 0)),
                  pl.BlockSpec((c, LANES), lambda b, j: (row(b, j), 0)),
                  _resident(intra.shape), _resident(qdec.shape), _resident(kdec.shape),
                  _resident(sdec.shape), _resident(bmask.shape)],
        out_specs=pl.BlockSpec((c, BRANCH_W), lambda b, j: (row(b, j), 0)),
        out_shape=jax.ShapeDtypeStruct((batch * seq, BRANCH_W), BF16),
        scratch_shapes=[pltpu.VMEM((RET_QK, BRANCH_W), F32)],
        compiler_params=_params("parallel", "arbitrary"),
        name="retention",
    )(ret_p, ret_p, ret_p, ret_p, cos, sin, intra, qdec, kdec, sdec, bmask)


def _ssd_kernel(xbc_ref, z_ref, dt_ref, cw_ref, cb_ref, dtb_ref, alog_ref, dskip_ref, ng_ref, tri_ref,
                o_ref, xe_sc, state_sc):
    L = CHUNK
    T = SSM_TAIL
    GW = SSM_INNER // SSM_GROUPS

    @pl.when(pl.program_id(1) == 0)
    def _():
        xe_sc[0:T, :] = jnp.zeros((T, SSM_XBC), F32)
        state_sc[...] = jnp.zeros_like(state_sc)

    x_raw = xbc_ref[...]
    xe_sc[T:T + L, :] = x_raw
    conv = cb_ref[...]
    for kk in range(SSM_CONV):
        s0 = T - (SSM_CONV - 1) + kk
        conv = conv + cw_ref[kk:kk + 1, :] * xe_sc[s0:s0 + L, :]
    xe_sc[0:T, :] = x_raw[L - T:L, :]
    xa = _silu(conv)
    xs = xa[:, :SSM_INNER]
    bm = xa[:, SSM_INNER:SSM_INNER + SSM_GROUPS * SSM_STATE]
    cm = xa[:, SSM_INNER + SSM_GROUPS * SSM_STATE:]

    dtr = dt_ref[...] + dtb_ref[...]
    dt = jnp.maximum(dtr, 0.0) + jnp.log1p(jnp.exp(-jnp.abs(dtr)))
    da = dt * (-jnp.exp(alog_ref[...]))
    cs = jnp.dot(tri_ref[...], da, preferred_element_type=F32, precision=lax.Precision.HIGHEST)
    cs_last = cs[L - 1:L, :]
    xdt = xs * dt
    xdec = (xdt * jnp.exp(cs_last - cs)).astype(BF16)
    ecs = jnp.exp(cs)
    state = state_sc[...]
    li = lax.broadcasted_iota(jnp.int32, (L, L), 0)
    si = lax.broadcasted_iota(jnp.int32, (L, L), 1)
    causal = li >= si
    lane_half = lax.broadcasted_iota(jnp.int32, (1, LANES), 1) // SSM_HEADDIM
    ys = []
    new_states = []
    for g in range(SSM_GROUPS):
        bg = bm[:, g * SSM_STATE:(g + 1) * SSM_STATE].astype(BF16)
        cg = cm[:, g * SSM_STATE:(g + 1) * SSM_STATE].astype(BF16)
        cb = _dot_nt(cg, bg)
        gsl = slice(g * GW, (g + 1) * GW)
        y_off = _dot(cg, state[:, gsl].astype(BF16)) * ecs[:, gsl]
        for p in range(GW // LANES):
            sl = slice(g * GW + p * LANES, g * GW + (p + 1) * LANES)
            cs_blk = cs[:, sl]
            cs_t = cs_blk.T
            xdt_blk = xdt[:, sl]
            acc = y_off[:, p * LANES:(p + 1) * LANES]
            for hh in range(LANES // SSM_HEADDIM):
                col = cs_blk[:, hh * SSM_HEADDIM:hh * SSM_HEADDIM + 1]
                rowv = cs_t[hh * SSM_HEADDIM:hh * SSM_HEADDIM + 1, :]
                dec = jnp.where(causal, jnp.exp(jnp.minimum(col - rowv, 0.0)), 0.0)
                xm = jnp.where(lane_half == hh, xdt_blk, 0.0).astype(BF16)
                acc = acc + _dot((cb * dec).astype(BF16), xm)
            ys.append(acc)
        new_states.append(_dot_tn(bg, xdec[:, gsl]))
    y = jnp.concatenate(ys, axis=-1) + xs * dskip_ref[...]
    y = y * _silu(z_ref[...])
    o_ref[...] = _rms(y, ng_ref[...]).astype(o_ref.dtype)
    state_sc[...] = state * jnp.exp(cs_last) + jnp.concatenate(new_states, axis=-1)


def _ssd(ssm_p, conv_w, conv_b, dt_bias_full, a_log_full, d_full, norm_g, batch, seq):
    L = CHUNK
    n = seq // L
    tri = (jnp.arange(L)[:, None] >= jnp.arange(L)[None, :]).astype(F32)
    row = lambda b, j: b * n + j
    return pl.pallas_call(
        _ssd_kernel,
        grid=(batch, n),
        in_specs=[pl.BlockSpec((L, SSM_XBC), lambda b, j: (row(b, j), 0)),
                  pl.BlockSpec((L, SSM_INNER), lambda b, j: (row(b, j), 2)),
                  pl.BlockSpec((L, SSM_INNER), lambda b, j: (row(b, j), 3)),
                  _resident(conv_w.shape), _resident(conv_b.shape), _resident(dt_bias_full.shape),
                  _resident(a_log_full.shape), _resident(d_full.shape), _resident(norm_g.shape),
                  _resident(tri.shape)],
        out_specs=pl.BlockSpec((L, SSM_INNER), lambda b, j: (row(b, j), 0)),
        out_shape=jax.ShapeDtypeStruct((batch * seq, SSM_INNER), BF16),
        scratch_shapes=[pltpu.VMEM((SSM_TAIL + L, SSM_XBC), F32),
                        pltpu.VMEM((SSM_STATE, SSM_INNER), F32)],
        compiler_params=_params("parallel", "arbitrary"),
        name="ssd",
    )(ssm_p, ssm_p, ssm_p, conv_w, conv_b, dt_bias_full, a_log_full, d_full, norm_g, tri)


def _mla_prep_kernel(p_ref, cos_ref, sin_ref, qn_ref, kvn_ref, wuq_ref, wukv_ref, q_ref, k_ref, v_ref):
    cos, sin = cos_ref[...], sin_ref[...]
    kvw = MLA_HEADS * MLA_PAD
    cq = p_ref[:, :MLA_Q_RANK]
    ckv = p_ref[:, MLA_Q_RANK:MLA_Q_RANK + MLA_KV_RANK]
    kr = p_ref[:, MLA_Q_RANK + MLA_KV_RANK:]
    q = _dot(_rms(cq, qn_ref[...]).astype(BF16), wuq_ref[...])
    q = _apply_rope(q, cos, sin, MLA_PAD, MLA_ROPE // 2, MLA_NOPE) * (MLA_QH ** -0.5 * LOG2E)
    q_ref[...] = q.T.astype(q_ref.dtype)
    kv = _dot(_rms(ckv, kvn_ref[...]).astype(BF16), wukv_ref[...])
    kpe = _apply_rope(kr, cos, sin, MLA_PAD, MLA_ROPE // 2, MLA_NOPE)
    k_ref[...] = (kv[:, :kvw] + jnp.concatenate([kpe] * MLA_HEADS, axis=-1)).astype(k_ref.dtype)
    v_ref[...] = kv[:, kvw:].T.astype(v_ref.dtype)


def _mla_prep(mla_p, cos, sin, q_norm, kv_norm, w_uq_p, w_ukv_p, seq, *, tm=512):
    m = mla_p.shape[0]
    w = MLA_HEADS * MLA_PAD
    per = seq // tm
    rows = lambda width: pl.BlockSpec((tm, width), lambda i: (i, 0))
    cols = pl.BlockSpec((None, w, tm), lambda i: (i // per, 0, i % per))
    t_shape = jax.ShapeDtypeStruct((m // seq, w, seq), BF16)
    return pl.pallas_call(
        _mla_prep_kernel,
        grid=(m // tm,),
        in_specs=[rows(mla_p.shape[1]), rows(LANES), rows(LANES), _resident(q_norm.shape),
                  _resident(kv_norm.shape), _resident(w_uq_p.shape), _resident(w_ukv_p.shape)],
        out_specs=[cols, rows(w), cols],
        out_shape=[t_shape, jax.ShapeDtypeStruct((m, w), BF16), t_shape],
        compiler_params=_params("parallel"),
        name="mla_prep",
    )(mla_p, cos, sin, q_norm, kv_norm, w_uq_p, w_ukv_p)


def _flash_kernel(*refs, tq, tk, n_maps, lambda_init):
    biased = n_maps == 2
    if biased:
        (pmin_ref, pmax_ref, qt_ref, k_ref, vt_ref, pq_ref, pk_ref, ftab_ref, lam_ref, sg_ref,
         o_ref, m_sc, l_sc, acc_sc, s0_sc, s1_sc) = refs
    else:
        qt_ref, k_ref, vt_ref, o_ref, m_sc, l_sc, acc_sc, s0_sc, s1_sc = refs
    bi = pl.program_id(0)
    qi = pl.program_id(1)
    hw = HEAD_W
    heads = k_ref.shape[-1] // hw
    cols = n_maps * tq
    hs = lambda h: slice(h * hw, (h + 1) * hw)
    ws = []
    for h in range(heads):
        qt = qt_ref[hs(h), :]
        if biased:
            half = lax.broadcasted_iota(jnp.int32, (hw, 1), 0) // DIFF_DH
            zero = jnp.zeros_like(qt)
            qt = jnp.concatenate([jnp.where(half == 0, qt, zero), jnp.where(half == 1, qt, zero)], axis=1)
        ws.append(qt)
    if biased:
        pq = pq_ref[...]
        ftabs = [jnp.broadcast_to(ftab_ref[h] * LOG2E, (tk, REL_TABLE)) for h in range(heads)]
    m_sc[...] = jnp.full(m_sc.shape, NEG_INF, F32)
    l_sc[...] = jnp.zeros_like(l_sc)
    acc_sc[...] = jnp.zeros_like(acc_sc)

    def near_biases(off):
        idx = jnp.clip(pq - pk_ref[pl.ds(off, tk), :], 0, REL_TABLE - 1)
        out = []
        for h in range(heads):
            b = jnp.concatenate([jnp.take_along_axis(ftabs[h], idx[:, c:c + LANES], axis=1)
                                 for c in range(0, tq, LANES)], axis=1)
            out.append(jnp.concatenate([b, b], axis=1))
        return out

    s_scs = (s0_sc, s1_sc)

    def scores(off, slot):
        for h in range(heads):
            s_scs[slot][h] = _dot(k_ref[pl.ds(off, tk), hs(h)], ws[h])

    def consume(off, slot, bias, mask, const=None):
        for h in range(heads):
            s = s_scs[slot][h]
            if bias is not None:
                s = s + bias[h]
            if mask is not None:
                s = jnp.where(mask, s, NEG_INF)
            m = m_sc[h]
            m_tile = jnp.max(s, axis=0, keepdims=True)
            if const is not None:
                m_tile = m_tile + const[h]
            m_new = jnp.maximum(m, m_tile)
            alpha = jnp.exp2(m - m_new)
            p = jnp.exp2(s - (m_new if const is None else m_new - const[h]))
            l_sc[h] = alpha * l_sc[h] + jnp.sum(p, axis=0, keepdims=True)
            acc_sc[h] = alpha * acc_sc[h] + _dot(vt_ref[hs(h), pl.ds(off, tk)], p.astype(BF16))
            m_sc[h] = m_new

    assert tq == 2 * tk
    scores(0, 0)

    def body(jj, carry):
        off0 = pl.multiple_of(jj * tq, tq)
        off1 = pl.multiple_of(off0 + tk, tk)
        off2 = pl.multiple_of(off0 + tq, tq)

        def pair(bias_fn, const):
            scores(off1, 1)
            consume(off0, 0, bias_fn(off0), None, const)
            scores(off2, 0)
            consume(off1, 1, bias_fn(off1), None, const)

        no_bias = lambda off: None
        if biased:
            nearest = jnp.maximum(pmax_ref[bi, 2 * jj], pmax_ref[bi, 2 * jj + 1])
            far = pmin_ref[bi, qi] - nearest >= REL_TABLE - 1

            @pl.when(far)
            def _():
                pair(no_bias, [ftab_ref[h][:, REL_TABLE - 1:REL_TABLE] * LOG2E for h in range(heads)])

            @pl.when(jnp.logical_not(far))
            def _():
                pair(near_biases, None)
        else:
            pair(no_bias, None)
        return carry

    lax.fori_loop(0, qi, body, 0)
    for d in range(2):
        off = pl.multiple_of(qi * tq + d * tk, tk)
        if d == 0:
            scores(pl.multiple_of(off + tk, tk), 1)
        key = lax.broadcasted_iota(jnp.int32, (tk, cols), 0) + d * tk
        qry = lax.broadcasted_iota(jnp.int32, (tk, cols), 1) % tq
        consume(off, d, near_biases(off) if biased else None, key <= qry)

    if biased:
        lp = lam_ref[...]
        lam = (jnp.exp(jnp.sum(lp[0:1] * lp[1:2], axis=-1, keepdims=True))
               - jnp.exp(jnp.sum(lp[2:3] * lp[3:4], axis=-1, keepdims=True)) + lambda_init)
    for h in range(heads):
        o = acc_sc[h] / l_sc[h]
        if biased:
            o = (o[:, :tq] - lam * o[:, tq:]).T
            o_ref[:, hs(h)] = (_rms(o, sg_ref[...]) * (1.0 - lambda_init)).astype(o_ref.dtype)
        else:
            o_ref[:, hs(h)] = o.T.astype(o_ref.dtype)


def _flash_scratch(heads, tk, cols):
    return [pltpu.VMEM((heads, 1, cols), F32), pltpu.VMEM((heads, 1, cols), F32),
            pltpu.VMEM((heads, HEAD_W, cols), F32),
            pltpu.VMEM((heads, tk, cols), F32), pltpu.VMEM((heads, tk, cols), F32)]


def _mla_attention(qt, k, vt, *, tq=ATTN_TQ, tk=ATTN_TK):
    b, s, w = k.shape
    return pl.pallas_call(
        functools.partial(_flash_kernel, tq=tq, tk=tk, n_maps=1, lambda_init=None),
        grid=(b, s // tq),
        in_specs=[pl.BlockSpec((None, w, tq), lambda bi, i: (bi, 0, i)),
                  pl.BlockSpec((None, s, w), lambda bi, i: (bi, 0, 0), pipeline_mode=pl.Buffered(1)),
                  pl.BlockSpec((None, w, s), lambda bi, i: (bi, 0, 0), pipeline_mode=pl.Buffered(1))],
        out_specs=pl.BlockSpec((None, tq, w), lambda bi, i: (bi, i, 0)),
        out_shape=jax.ShapeDtypeStruct((b, s, w), BF16),
        scratch_shapes=_flash_scratch(w // HEAD_W, tk, tq),
        compiler_params=_params("parallel", "arbitrary"),
        name="mla_attention",
    )(qt, k, vt)


def _diff_attention(qt, k, vt, pos, ftab, lam_params, sub_g, lambda_init, *, tq=ATTN_TQ, tk=ATTN_TK):
    b, s, w = k.shape
    pmin = jnp.min(pos.reshape(b, s // tq, tq), axis=-1)
    pmax = jnp.max(pos.reshape(b, s // tk, tk), axis=-1)
    pos_col = pos.reshape(b, s, 1)
    pos_row = pos.reshape(b, 1, s)
    grid_spec = pltpu.PrefetchScalarGridSpec(
        num_scalar_prefetch=2,
        grid=(b, s // tq),
        in_specs=[pl.BlockSpec((None, w, tq), lambda bi, i, *_: (bi, 0, i)),
                  pl.BlockSpec((None, s, w), lambda bi, i, *_: (bi, 0, 0), pipeline_mode=pl.Buffered(1)),
                  pl.BlockSpec((None, w, s), lambda bi, i, *_: (bi, 0, 0), pipeline_mode=pl.Buffered(1)),
                  pl.BlockSpec((None, 1, tq), lambda bi, i, *_: (bi, 0, i)),
                  pl.BlockSpec((None, s, 1), lambda bi, i, *_: (bi, 0, 0), pipeline_mode=pl.Buffered(1)),
                  pl.BlockSpec(ftab.shape, lambda bi, i, *_: (0, 0, 0)),
                  pl.BlockSpec(lam_params.shape, lambda bi, i, *_: (0, 0)),
                  pl.BlockSpec(sub_g.shape, lambda bi, i, *_: (0, 0))],
        out_specs=pl.BlockSpec((None, tq, w), lambda bi, i, *_: (bi, i, 0)),
        scratch_shapes=_flash_scratch(w // HEAD_W, tk, 2 * tq),
    )
    return pl.pallas_call(
        functools.partial(_flash_kernel, tq=tq, tk=tk, n_maps=2, lambda_init=lambda_init),
        grid_spec=grid_spec,
        out_shape=jax.ShapeDtypeStruct((b, s, w), BF16),
        compiler_params=_params("parallel", "arbitrary"),
        name="diff_attention",
    )(pmin, pmax, qt, k, vt, pos_row, pos_col, ftab, lam_params, sub_g)


def _merge_kernel(h_ref, g_ref, y0_ref, y1_ref, y2_ref, y3_ref, wg_ref, wb_ref, wo_ref, o_ref, *, col_chunk):
    h = h_ref[...]
    d = h.shape[-1]
    u = _rms(h, g_ref[...]).astype(BF16)
    ys = (y0_ref[...], y1_ref[...], y2_ref[...], y3_ref[...])
    parts = []
    for c0 in range(0, d, col_chunk):
        acc = None
        for i, y in enumerate(ys):
            gate = jax.nn.sigmoid(_dot(u, wg_ref[:, i * d + c0:i * d + c0 + col_chunk]))
            term = gate * _dot(y, wb_ref[i, :, c0:c0 + col_chunk])
            acc = term if acc is None else acc + term
        parts.append(acc.astype(BF16))
    merged = jnp.concatenate(parts, axis=-1)
    o_ref[...] = h + _dot(merged, wo_ref[...])


def _merge(h, g, ys, w_gate, w_branch, w_out, *, tm=512, col_chunk=512):
    m, d = h.shape
    rows = lambda width: pl.BlockSpec((tm, width), lambda i: (i, 0))
    return pl.pallas_call(
        functools.partial(_merge_kernel, col_chunk=col_chunk),
        grid=(m // tm,),
        in_specs=[rows(d), _resident((1, d))] + [rows(BRANCH_W)] * N_BRANCH
                 + [_resident(w_gate.shape), _resident(w_branch.shape), _resident(w_out.shape)],
        out_specs=rows(d),
        out_shape=jax.ShapeDtypeStruct((m, d), F32),
        compiler_params=_params("parallel"),
        name="merge",
    )(h, g, *ys, w_gate, w_branch, w_out)


def _xattn_kernel(h_ref, g_ref, wq_ref, k_ref, v_ref, wo_ref, o_ref):
    h = h_ref[...]
    d = h.shape[-1]
    dh = d // XA_HEADS
    u = _rms(h, g_ref[...]).astype(BF16)
    q = (_dot(u, wq_ref[...]) * (dh ** -0.5)).astype(BF16)
    out = h
    for hd in range(XA_HEADS):
        sl = slice(hd * dh, (hd + 1) * dh)
        s = _dot_nt(q[:, sl], k_ref[:, sl])
        p = jnp.exp(s - jnp.max(s, axis=-1, keepdims=True))
        p = p / jnp.sum(p, axis=-1, keepdims=True)
        o = _dot(p.astype(BF16), v_ref[:, sl])
        out = out + _dot(o.astype(BF16), wo_ref[sl, :])
    o_ref[...] = out


def _cross_attention(h, g, wq, k, v, wo, batch, seq, *, tm=512):
    m, d = h.shape
    mem_len = k.shape[0] // batch
    per = seq // tm
    return pl.pallas_call(
        _xattn_kernel,
        grid=(m // tm,),
        in_specs=[pl.BlockSpec((tm, d), lambda i: (i, 0)), _resident((1, d)), _resident(wq.shape),
                  pl.BlockSpec((mem_len, d), lambda i: (i // per, 0)),
                  pl.BlockSpec((mem_len, d), lambda i: (i // per, 0)),
                  _resident(wo.shape)],
        out_specs=pl.BlockSpec((tm, d), lambda i: (i, 0)),
        out_shape=jax.ShapeDtypeStruct((m, d), F32),
        compiler_params=_params("parallel"),
        name="cross_attention",
    )(h, g, wq, k, v, wo)


def _split_cols(w, sizes):
    out, start = [], 0
    for s in sizes:
        out.append(w[:, start:start + s])
        start += s
    return out


def _pad_cols(w, left, total):
    return jnp.pad(w, ((0, 0), (left, total - left - w.shape[1])))


def _layer_weights(w_in, w_uq, w_ukv):
    (rq, rk, rv, rg, sz, sxbc, sdt, cq, ckv, kr, dq, dk, dv, gl) = _split_cols(w_in, IN_SIZES)
    sdt_full = jnp.repeat(sdt, SSM_HEADDIM, axis=1)
    kr_pad = _pad_cols(kr, MLA_NOPE, MLA_PAD)
    w_mix = jnp.concatenate([rq, rk, rv, rg, sxbc, sz, sdt_full, cq, ckv, kr_pad, dq, dk, dv], axis=1)
    d = w_uq.shape[0]
    uq = w_uq.reshape(d, MLA_HEADS, MLA_QH)
    uq = jnp.pad(uq, ((0, 0), (0, 0), (0, MLA_PAD - MLA_QH))).reshape(d, MLA_HEADS * MLA_PAD)
    r = w_ukv.shape[0]
    ukv = w_ukv.reshape(r, MLA_HEADS, MLA_NOPE + MLA_V)
    k_nope = jnp.pad(ukv[:, :, :MLA_NOPE], ((0, 0), (0, 0), (0, MLA_PAD - MLA_NOPE)))
    ukv_p = jnp.concatenate([k_nope.reshape(r, -1), ukv[:, :, MLA_NOPE:].reshape(r, -1)], axis=1)
    return w_mix.astype(BF16), gl.astype(BF16), uq.astype(BF16), ukv_p.astype(BF16)


def _t5_bucket_const():
    n = jnp.arange(REL_TABLE)
    exact = REL_BUCKETS // 2
    nf = jnp.maximum(n, 1).astype(F32)
    large = exact + (jnp.log(nf / exact) / math.log(REL_MAX_DIST / exact) * (REL_BUCKETS - exact)).astype(jnp.int32)
    return jnp.where(n < exact, n, jnp.minimum(large, REL_BUCKETS - 1))


MIX_OUTS = ((3 * BRANCH_W, F32, 1.0, False),
            (SSM_XBC + 2 * SSM_INNER, F32, 1.0, False),
            (MLA_Q_RANK + MLA_KV_RANK + MLA_PAD, F32, 1.0, False),
            (BRANCH_W, BF16, DIFF_DH ** -0.5 * LOG2E, True),
            (BRANCH_W, BF16, 1.0, False),
            (BRANCH_W, BF16, 1.0, True))


def kernel(x, mem, positions, ffn1_norm, ffn1_w_gate, ffn1_w_up, ffn1_w_down, mix_norm, w_in, ssm_conv_w,
           ssm_conv_b, ssm_dt_bias, ssm_a_log, ssm_d, ssm_norm, mla_q_norm, mla_kv_norm, mla_w_uq, mla_w_ukv,
           diff_lambda, diff_norm, rel_bias, w_branch, w_out, xa_norm, mem_norm, xa_wq, xa_wk, xa_wv, xa_wo,
           ffn2_norm, ffn2_w_gate, ffn2_w_up, ffn2_w_down, final_norm):
    batch, seq, d = x.shape
    depth = w_in.shape[0]
    m = batch * seq
    h = x.reshape(m, d)
    mem2 = mem.reshape(batch * mem.shape[1], d)
    row = lambda a: a.reshape(1, -1)

    pos_col = positions.reshape(m, 1)
    ret_cos, ret_sin = _rope_tables(pos_col, *_rope_rows(RET_DK, RET_DK // 2, 0))
    mla_cos, mla_sin = _rope_tables(pos_col, *_rope_rows(MLA_PAD, MLA_ROPE // 2, MLA_NOPE))
    ftab = rel_bias[_t5_bucket_const()].T.reshape(DIFF_HEADS, 1, REL_TABLE)

    for l in range(depth):
        w_mix, w_gate, w_uq_p, w_ukv_p = _layer_weights(w_in[l], mla_w_uq[l], mla_w_ukv[l])
        h = _ffn(h, row(ffn1_norm[l]), ffn1_w_gate[l].astype(BF16), ffn1_w_up[l].astype(BF16),
                 ffn1_w_down[l].astype(BF16))
        ret_p, ssm_p, mla_p, dqt, dk, dvt = _norm_proj(h, row(mix_norm[l]), w_mix, MIX_OUTS, seq, name="mix_proj")
        y_ret = _retention(ret_p, ret_cos, ret_sin, batch, seq)
        y_ssm = _ssd(ssm_p, ssm_conv_w[l], row(ssm_conv_b[l]),
                     row(jnp.repeat(ssm_dt_bias[l], SSM_HEADDIM)), row(jnp.repeat(ssm_a_log[l], SSM_HEADDIM)),
                     row(jnp.repeat(ssm_d[l], SSM_HEADDIM)), row(ssm_norm[l]), batch, seq)
        mqt, mk, mvt = _mla_prep(mla_p, mla_cos, mla_sin, row(mla_q_norm[l]), row(mla_kv_norm[l]), w_uq_p,
                                 w_ukv_p, seq)
        y_mla = _mla_attention(mqt, mk.reshape(batch, seq, -1), mvt).reshape(m, BRANCH_W)
        lambda_init = 0.8 - 0.6 * math.exp(-0.3 * l)
        y_diff = _diff_attention(dqt, dk.reshape(batch, seq, -1), dvt, positions, ftab, diff_lambda[l],
                                 row(diff_norm[l]), lambda_init).reshape(m, BRANCH_W)
        h = _merge(h, row(mix_norm[l]), (y_ret, y_ssm, y_mla, y_diff), w_gate, w_branch[l].astype(BF16),
                   w_out[l].astype(BF16))
        w_kv = jnp.concatenate([xa_wk[l], xa_wv[l]], axis=1).astype(BF16)
        xk, xv = _norm_proj(mem2, row(mem_norm[l]), w_kv, [(d, BF16, 1.0, False)] * 2, mem.shape[1],
                            tm=mem.shape[1], name="mem_kv")
        h = _cross_attention(h, row(xa_norm[l]), xa_wq[l].astype(BF16), xk, xv, xa_wo[l].astype(BF16), batch, seq)
        h = _ffn(h, row(ffn2_norm[l]), ffn2_w_gate[l].astype(BF16), ffn2_w_up[l].astype(BF16),
                 ffn2_w_down[l].astype(BF16), row(final_norm) if l == depth - 1 else None)
    return h.reshape(batch, seq, d)
```

```python
import functools
import math
from typing import NamedTuple

import jax
import jax.numpy as jnp
from jax import lax
from jax.experimental import pallas as pl
from jax.experimental.pallas import tpu as pltpu

F32 = jnp.float32
BF16 = jnp.bfloat16

EPS = 1e-6
NEG_INF = -1e30
LOG2E = math.log2(math.e)
ROPE_BASE = 10000.0
CHUNK = 128
BRANCH_W = 512
N_BRANCH = 4

RET_HEADS = 4
RET_DK = 64
RET_DV = 128
RET_QK = RET_HEADS * RET_DK

SSM_HEADDIM = 64
SSM_INNER = 512
SSM_HEADS = 8
SSM_GROUPS = 2
SSM_STATE = 128
SSM_CONV = 4
SSM_XBC = SSM_INNER + 2 * SSM_GROUPS * SSM_STATE
SSM_TAIL = 8
SSD_DT_W = 128
SSD_P_WIDTH = SSM_XBC + SSM_INNER + SSD_DT_W

MLA_HEADS = 4
MLA_Q_RANK = 256
MLA_KV_RANK = 128
MLA_NOPE = 64
MLA_ROPE = 32
MLA_V = 128
MLA_QH = MLA_NOPE + MLA_ROPE
MLA_PAD = 128
HEAD_W = 128
ATTN_TQ = 512
ATTN_TK = 256

DIFF_HEADS = 4
DIFF_DH = 64

REL_BUCKETS = 32
REL_MAX_DIST = 128
REL_TABLE = 128

XA_HEADS = 4

LANES = 128
VMEM_LIMIT = 56 * 1024 * 1024

IN_SIZES = (RET_QK, RET_QK, BRANCH_W, BRANCH_W,
            SSM_INNER, SSM_XBC, SSM_HEADS,
            MLA_Q_RANK, MLA_KV_RANK, MLA_ROPE,
            BRANCH_W, BRANCH_W, BRANCH_W,
            N_BRANCH * 1024)


def _params(*sem):
    return pltpu.CompilerParams(dimension_semantics=sem, vmem_limit_bytes=VMEM_LIMIT)


class _LayerOf(NamedTuple):
    stack: jax.Array
    layer: int

    @property
    def shape(self):
        return self.stack.shape[1:]


def _operand(x):
    return x.stack if isinstance(x, _LayerOf) else x


def _resident(x):
    if isinstance(x, _LayerOf):
        shape, lead, first = x.shape, (None,), (x.layer,)
    else:
        shape, lead, first = (x if isinstance(x, tuple) else x.shape), (), ()
    nd = len(shape)
    return pl.BlockSpec(lead + tuple(shape), lambda *_: first + (0,) * nd, pipeline_mode=pl.Buffered(1))


def _cast_kernel(x_ref, o_ref):
    o_ref[...] = x_ref[...].astype(o_ref.dtype)


def _cast_bf16(w, *, row_blocks=4):
    depth, r, c = w.shape
    tr = r // row_blocks
    spec = pl.BlockSpec((None, tr, c), lambda l, i: (l, i, 0))
    return pl.pallas_call(
        _cast_kernel,
        grid=(depth, row_blocks),
        in_specs=[spec],
        out_specs=spec,
        out_shape=jax.ShapeDtypeStruct(w.shape, BF16),
        compiler_params=_params("parallel", "parallel"),
        name="cast_bf16",
    )(w)


def _rms(x, g):
    return x * lax.rsqrt(jnp.mean(x * x, axis=-1, keepdims=True) + EPS) * g


def _silu(x):
    return x * jax.nn.sigmoid(x)


def _dot(a, b):
    return jnp.dot(a, b, preferred_element_type=F32)


def _dot_nt(a, b):
    return lax.dot_general(a, b, (((1,), (1,)), ((), ())), preferred_element_type=F32)


def _dot_tn(a, b):
    return lax.dot_general(a, b, (((0,), (0,)), ((), ())), preferred_element_type=F32)


def _ffn_kernel(h_ref, g_ref, wg_ref, wu_ref, wd_ref, *rest, ff_chunk, final):
    if final:
        gf_ref, o_ref = rest
    else:
        (o_ref,) = rest
    h = h_ref[...]
    xn = _rms(h, g_ref[...]).astype(BF16)
    acc = jnp.zeros(h.shape, F32)
    d_ff = wg_ref.shape[1]
    for c0 in range(0, d_ff, ff_chunk):
        a = _dot(xn, wg_ref[:, c0:c0 + ff_chunk])
        b = _dot(xn, wu_ref[:, c0:c0 + ff_chunk])
        acc = acc + _dot((_silu(a) * b).astype(BF16), wd_ref[c0:c0 + ff_chunk, :])
    out = h + 0.5 * acc
    if final:
        out = _rms(out, gf_ref[...])
    o_ref[...] = out


def _ffn(h, g, wg, wu, wd, final_g=None, *, tm=512, ff_chunk=256):
    m, d = h.shape
    d_ff = wg.shape[1]
    final = final_g is not None
    in_specs = [pl.BlockSpec((tm, d), lambda i: (i, 0)), _resident((1, d)),
                _resident(wg), _resident(wu), _resident(wd)]
    args = [h, g, _operand(wg), _operand(wu), _operand(wd)]
    if final:
        in_specs.append(_resident((1, d)))
        args.append(final_g)
    return pl.pallas_call(
        functools.partial(_ffn_kernel, ff_chunk=ff_chunk, final=final),
        grid=(m // tm,),
        in_specs=in_specs,
        out_specs=pl.BlockSpec((tm, d), lambda i: (i, 0)),
        out_shape=jax.ShapeDtypeStruct((m, d), F32),
        compiler_params=_params("parallel"),
        name="ffn",
    )(*args)


def _norm_proj_kernel(x_ref, g_ref, w_ref, *o_refs, outs, col_chunk):
    xn = _rms(x_ref[...], g_ref[...]).astype(BF16)
    off = 0
    for o_ref, (width, _, scale, transposed) in zip(o_refs, outs):
        for c0 in range(0, width, col_chunk):
            c1 = min(c0 + col_chunk, width)
            y = _dot(xn, w_ref[:, off + c0:off + c1])
            if scale != 1.0:
                y = y * scale
            if transposed:
                o_ref[c0:c1, :] = y.T.astype(o_ref.dtype)
            else:
                o_ref[:, c0:c1] = y.astype(o_ref.dtype)
        off += width


def _norm_proj(x, g, w, outs, seq, *, tm=512, col_chunk=512, name="norm_proj"):
    m, d = x.shape
    assert sum(o[0] for o in outs) == w.shape[1]
    per = seq // tm
    out_specs, out_shape = [], []
    for width, dt, _, transposed in outs:
        if transposed:
            out_specs.append(pl.BlockSpec((None, width, tm), lambda i: (i // per, 0, i % per)))
            out_shape.append(jax.ShapeDtypeStruct((m // seq, width, seq), dt))
        else:
            out_specs.append(pl.BlockSpec((tm, width), lambda i: (i, 0)))
            out_shape.append(jax.ShapeDtypeStruct((m, width), dt))
    return pl.pallas_call(
        functools.partial(_norm_proj_kernel, outs=tuple(outs), col_chunk=col_chunk),
        grid=(m // tm,),
        in_specs=[pl.BlockSpec((tm, d), lambda i: (i, 0)), _resident((1, d)), _resident(w)],
        out_specs=out_specs,
        out_shape=out_shape,
        compiler_params=_params("parallel"),
        name=name,
    )(x, g, _operand(w))


def _rope_table_kernel(pos_ref, inv_ref, sign_ref, cos_ref, sin_ref):
    ang = pos_ref[...].astype(F32) * inv_ref[...]
    cos_ref[...] = jnp.cos(ang)
    sin_ref[...] = jnp.sin(ang) * sign_ref[...]


def _rope_tables(pos_col, inv_row, sign_row, *, tm=1024):
    m = pos_col.shape[0]
    return pl.pallas_call(
        _rope_table_kernel,
        grid=(m // tm,),
        in_specs=[pl.BlockSpec((tm, 1), lambda i: (i, 0)), _resident((1, LANES)), _resident((1, LANES))],
        out_specs=[pl.BlockSpec((tm, LANES), lambda i: (i, 0))] * 2,
        out_shape=[jax.ShapeDtypeStruct((m, LANES), F32)] * 2,
        compiler_params=_params("parallel"),
        name="rope_tables",
    )(pos_col, inv_row, sign_row)


def _rope_rows(period, half, start):
    lane = jnp.arange(LANES) % period
    inside = (lane >= start) & (lane < start + 2 * half)
    idx = (lane - start) % half
    inv = jnp.exp(-math.log(ROPE_BASE) * idx.astype(F32) / half)
    inv = jnp.where(inside, inv, 0.0)
    sign = jnp.where(inside, jnp.where(lane < start + half, -1.0, 1.0), 0.0)
    return inv.reshape(1, LANES).astype(F32), sign.reshape(1, LANES).astype(F32)


def _apply_rope(x, cos, sin_signed, period, half, start):
    width = x.shape[-1]
    reps = width // LANES
    if reps > 1:
        cos = jnp.concatenate([cos] * reps, axis=-1)
        sin_signed = jnp.concatenate([sin_signed] * reps, axis=-1)
    lane = lax.broadcasted_iota(jnp.int32, (1, width), 1) % period
    first = lane < start + half
    partner = jnp.where(first, pltpu.roll(x, width - half, 1), pltpu.roll(x, half, 1))
    return x * cos + partner * sin_signed


def _ret_kernel(q_ref, k_ref, v_ref, g_ref, cos_ref, sin_ref, intra_ref, qdec_ref, kdec_ref,
                sdec_ref, bmask_ref, o_ref, state_sc):
    @pl.when(pl.program_id(0) == 0)
    def _():
        state_sc[...] = jnp.zeros_like(state_sc)

    head = lax.broadcasted_iota(jnp.int32, (1, RET_QK), 1) // RET_DK
    for b in range(q_ref.shape[0]):
        cos, sin = cos_ref[b], sin_ref[b]
        q = _apply_rope(q_ref[b], cos, sin, RET_DK, RET_DK // 2, 0) * (RET_DK ** -0.5)
        k = _apply_rope(k_ref[b], cos, sin, RET_DK, RET_DK // 2, 0)
        vb = v_ref[b].astype(BF16)
        kb = k.astype(BF16)
        state = state_sc[b]
        cross = _dot((q * qdec_ref[...]).astype(BF16), state.astype(BF16))
        outs = []
        for h in range(RET_HEADS):
            qh = jnp.where(head == h, q, 0.0).astype(BF16)
            att = _dot_nt(qh, kb) * intra_ref[h]
            sl = slice(h * RET_DV, (h + 1) * RET_DV)
            o = _dot(att.astype(BF16), vb[:, sl]) + cross[:, sl]
            mu = jnp.mean(o, axis=-1, keepdims=True)
            oc = o - mu
            outs.append(oc * lax.rsqrt(jnp.mean(oc * oc, axis=-1, keepdims=True) + EPS))
        o_ref[b] = (jnp.concatenate(outs, axis=-1) * _silu(g_ref[b])).astype(o_ref.dtype)
        kv = _dot_tn((k * kdec_ref[...]).astype(BF16), vb)
        state_sc[b] = state * sdec_ref[...] + kv * bmask_ref[...]


def _retention_consts():
    c = CHUNK
    log_g = jnp.log1p(-jnp.exp2(-5.0 - jnp.arange(RET_HEADS, dtype=F32)))
    i = jnp.arange(c, dtype=F32)
    rel = i[:, None] - i[None, :]
    intra = jnp.where(rel >= 0, jnp.exp(log_g[:, None, None] * jnp.maximum(rel, 0.0)), 0.0)
    lg_lane = jnp.repeat(log_g, RET_DK)
    qdec = jnp.exp(lg_lane[None, :] * (i[:, None] + 1.0))
    kdec = jnp.exp(lg_lane[None, :] * (c - 1.0 - i[:, None]))
    row_h = jnp.repeat(jnp.arange(RET_HEADS), RET_DK)
    col_h = jnp.repeat(jnp.arange(RET_HEADS), RET_DV)
    bmask = (row_h[:, None] == col_h[None, :]).astype(F32)
    sdec = bmask * jnp.exp(lg_lane * c)[:, None]
    return intra, qdec, kdec, sdec, bmask


def _retention(ret_p, cos, sin):
    c = CHUNK
    batch, seq, _ = ret_p.shape
    intra, qdec, kdec, sdec, bmask = _retention_consts()
    blk = lambda width, col: pl.BlockSpec((batch, c, width), lambda j: (0, j, col))
    return pl.pallas_call(
        _ret_kernel,
        grid=(seq // c,),
        in_specs=[blk(RET_QK, 0), blk(RET_QK, 1), blk(BRANCH_W, 1), blk(BRANCH_W, 2), blk(LANES, 0), blk(LANES, 0),
                  _resident(intra.shape), _resident(qdec.shape), _resident(kdec.shape),
                  _resident(sdec.shape), _resident(bmask.shape)],
        out_specs=blk(BRANCH_W, 0),
        out_shape=jax.ShapeDtypeStruct((batch, seq, BRANCH_W), BF16),
        scratch_shapes=[pltpu.VMEM((batch, RET_QK, BRANCH_W), F32)],
        compiler_params=_params("arbitrary"),
        name="retention",
    )(ret_p, ret_p, ret_p, ret_p, cos, sin, intra, qdec, kdec, sdec, bmask)


def _split3(x):
    hi = x.astype(BF16)
    r = x - hi.astype(F32)
    mid = r.astype(BF16)
    lo = (r - mid.astype(F32)).astype(BF16)
    return hi, mid, lo


def _dot_split3(a, b, *, left):
    if left:
        return sum(_dot(piece, b) for piece in _split3(a))
    return sum(_dot(a, piece) for piece in _split3(b))


def _ssd_kernel(xbc_ref, z_ref, dt_ref, cw_ref, cb_ref, dtb_ref, alog_ref, dskip_ref, ng_ref, tri_ref,
                expand_ref, o_ref, xe_sc, state_sc):
    L = CHUNK
    T = SSM_TAIL
    GW = SSM_INNER // SSM_GROUPS

    @pl.when(pl.program_id(1) == 0)
    def _():
        xe_sc[0:T, :] = jnp.zeros((T, SSM_XBC), F32)
        state_sc[...] = jnp.zeros_like(state_sc)

    x_raw = xbc_ref[...]
    xe_sc[T:T + L, :] = x_raw
    conv = cb_ref[...]
    for kk in range(SSM_CONV):
        s0 = T - (SSM_CONV - 1) + kk
        conv = conv + cw_ref[kk:kk + 1, :] * xe_sc[s0:s0 + L, :]
    xe_sc[0:T, :] = x_raw[L - T:L, :]
    xa = _silu(conv)
    xs = xa[:, :SSM_INNER]
    bm = xa[:, SSM_INNER:SSM_INNER + SSM_GROUPS * SSM_STATE]
    cm = xa[:, SSM_INNER + SSM_GROUPS * SSM_STATE:]

    dtr = dt_ref[...] + dtb_ref[...]
    dt_h = jnp.maximum(dtr, 0.0) + jnp.log1p(jnp.exp(-jnp.abs(dtr)))
    da = dt_h * (-jnp.exp(alog_ref[...]))
    cs_h = _dot_split3(tri_ref[...], da, left=False)
    cs_h_last = cs_h[L - 1:L, :]
    cs_t = cs_h.T
    expand = expand_ref[...]
    dt = _dot_split3(dt_h, expand, left=True)
    ecs = _dot_split3(jnp.exp(cs_h), expand, left=True)
    eds = _dot_split3(jnp.exp(cs_h_last - cs_h), expand, left=True)
    xdt = xs * dt
    xdec = (xdt * eds).astype(BF16)
    state = state_sc[...]
    li = lax.broadcasted_iota(jnp.int32, (L, L), 0)
    si = lax.broadcasted_iota(jnp.int32, (L, L), 1)
    causal = li >= si
    lane_half = lax.broadcasted_iota(jnp.int32, (1, LANES), 1) // SSM_HEADDIM
    ys = []
    new_states = []
    for g in range(SSM_GROUPS):
        bg = bm[:, g * SSM_STATE:(g + 1) * SSM_STATE].astype(BF16)
        cg = cm[:, g * SSM_STATE:(g + 1) * SSM_STATE].astype(BF16)
        cb = _dot_nt(cg, bg)
        gsl = slice(g * GW, (g + 1) * GW)
        y_off = _dot(cg, state[:, gsl].astype(BF16)) * ecs[:, gsl]
        for p in range(GW // LANES):
            sl = slice(g * GW + p * LANES, g * GW + (p + 1) * LANES)
            xdt_blk = xdt[:, sl]
            acc = y_off[:, p * LANES:(p + 1) * LANES]
            for hh in range(LANES // SSM_HEADDIM):
                head = (g * GW + p * LANES) // SSM_HEADDIM + hh
                col = cs_h[:, head:head + 1]
                rowv = cs_t[head:head + 1, :]
                dec = jnp.where(causal, jnp.exp(jnp.minimum(col - rowv, 0.0)), 0.0)
                xm = jnp.where(lane_half == hh, xdt_blk, 0.0).astype(BF16)
                acc = acc + _dot((cb * dec).astype(BF16), xm)
            ys.append(acc)
        new_states.append(_dot_tn(bg, xdec[:, gsl]))
    y = jnp.concatenate(ys, axis=-1) + xs * dskip_ref[...]
    y = y * _silu(z_ref[...])
    o_ref[...] = _rms(y, ng_ref[...]).astype(o_ref.dtype)
    state_sc[...] = state * ecs[L - 1:L, :] + jnp.concatenate(new_states, axis=-1)


def _ssd(ssm_p, conv_w, conv_b, dt_bias_pad, a_log_pad, d_full, norm_g, batch, seq):
    L = CHUNK
    n = seq // L
    tri = (jnp.arange(L)[:, None] >= jnp.arange(L)[None, :]).astype(BF16)
    expand = (jnp.arange(SSD_DT_W)[:, None] == jnp.arange(SSM_INNER)[None, :] // SSM_HEADDIM).astype(BF16)
    row = lambda b, j: b * n + j
    return pl.pallas_call(
        _ssd_kernel,
        grid=(batch, n),
        in_specs=[pl.BlockSpec((L, SSM_XBC), lambda b, j: (row(b, j), 0)),
                  pl.BlockSpec((L, SSM_INNER), lambda b, j: (row(b, j), SSM_XBC // SSM_INNER)),
                  pl.BlockSpec((L, SSD_DT_W), lambda b, j: (row(b, j), (SSM_XBC + SSM_INNER) // SSD_DT_W)),
                  _resident(conv_w.shape), _resident(conv_b.shape), _resident(dt_bias_pad.shape),
                  _resident(a_log_pad.shape), _resident(d_full.shape), _resident(norm_g.shape),
                  _resident(tri.shape), _resident(expand.shape)],
        out_specs=pl.BlockSpec((L, SSM_INNER), lambda b, j: (row(b, j), 0)),
        out_shape=jax.ShapeDtypeStruct((batch * seq, SSM_INNER), BF16),
        scratch_shapes=[pltpu.VMEM((SSM_TAIL + L, SSM_XBC), F32),
                        pltpu.VMEM((SSM_STATE, SSM_INNER), F32)],
        compiler_params=_params("parallel", "arbitrary"),
        name="ssd",
    )(ssm_p, ssm_p, ssm_p, conv_w, conv_b, dt_bias_pad, a_log_pad, d_full, norm_g, tri, expand)


def _mla_prep_kernel(p_ref, cos_ref, sin_ref, qn_ref, kvn_ref, wuq_ref, wukv_ref, q_ref, k_ref, v_ref):
    cos, sin = cos_ref[...], sin_ref[...]
    kvw = MLA_HEADS * MLA_PAD
    cq = p_ref[:, :MLA_Q_RANK]
    ckv = p_ref[:, MLA_Q_RANK:MLA_Q_RANK + MLA_KV_RANK]
    kr = p_ref[:, MLA_Q_RANK + MLA_KV_RANK:]
    q = _dot(_rms(cq, qn_ref[...]).astype(BF16), wuq_ref[...])
    q = _apply_rope(q, cos, sin, MLA_PAD, MLA_ROPE // 2, MLA_NOPE) * (MLA_QH ** -0.5 * LOG2E)
    q_ref[...] = q.T.astype(q_ref.dtype)
    kv = _dot(_rms(ckv, kvn_ref[...]).astype(BF16), wukv_ref[...])
    kpe = _apply_rope(kr, cos, sin, MLA_PAD, MLA_ROPE // 2, MLA_NOPE)
    k_ref[...] = (kv[:, :kvw] + jnp.concatenate([kpe] * MLA_HEADS, axis=-1)).astype(k_ref.dtype)
    v_ref[...] = kv[:, kvw:].T.astype(v_ref.dtype)


def _mla_prep(mla_p, cos, sin, q_norm, kv_norm, w_uq_p, w_ukv_p, seq, *, tm=512):
    m = mla_p.shape[0]
    w = MLA_HEADS * MLA_PAD
    per = seq // tm
    rows = lambda width: pl.BlockSpec((tm, width), lambda i: (i, 0))
    cols = pl.BlockSpec((None, w, tm), lambda i: (i // per, 0, i % per))
    t_shape = jax.ShapeDtypeStruct((m // seq, w, seq), BF16)
    return pl.pallas_call(
        _mla_prep_kernel,
        grid=(m // tm,),
        in_specs=[rows(mla_p.shape[1]), rows(LANES), rows(LANES), _resident(q_norm.shape),
                  _resident(kv_norm.shape), _resident(w_uq_p), _resident(w_ukv_p)],
        out_specs=[cols, rows(w), cols],
        out_shape=[t_shape, jax.ShapeDtypeStruct((m, w), BF16), t_shape],
        compiler_params=_params("parallel"),
        name="mla_prep",
    )(mla_p, cos, sin, q_norm, kv_norm, _operand(w_uq_p), _operand(w_ukv_p))


def _flash_kernel(*refs, tq, tk, n_maps, lambda_init):
    biased = n_maps == 2
    if biased:
        (pmin_ref, pmax_ref, qt_ref, k_ref, vt_ref, pq_ref, pk_ref, ftab_ref, lam_ref, sg_ref,
         o_ref, m_sc, l_sc, acc_sc, s0_sc, s1_sc) = refs
    else:
        qt_ref, k_ref, vt_ref, o_ref, m_sc, l_sc, acc_sc, s0_sc, s1_sc = refs
    bi = pl.program_id(0)
    qi = pl.program_id(1)
    hw = HEAD_W
    heads = k_ref.shape[-1] // hw
    cols = n_maps * tq
    hs = lambda h: slice(h * hw, (h + 1) * hw)
    ws = []
    for h in range(heads):
        qt = qt_ref[hs(h), :]
        if biased:
            half = lax.broadcasted_iota(jnp.int32, (hw, 1), 0) // DIFF_DH
            zero = jnp.zeros_like(qt)
            qt = jnp.concatenate([jnp.where(half == 0, qt, zero), jnp.where(half == 1, qt, zero)], axis=1)
        ws.append(qt)
    if biased:
        pq = pq_ref[...]
        ftabs = [jnp.broadcast_to(ftab_ref[h] * LOG2E, (tk, REL_TABLE)) for h in range(heads)]
    m_sc[...] = jnp.full(m_sc.shape, NEG_INF, F32)
    l_sc[...] = jnp.zeros_like(l_sc)
    acc_sc[...] = jnp.zeros_like(acc_sc)

    def near_biases(off):
        idx = jnp.clip(pq - pk_ref[pl.ds(off, tk), :], 0, REL_TABLE - 1)
        out = []
        for h in range(heads):
            b = jnp.concatenate([jnp.take_along_axis(ftabs[h], idx[:, c:c + LANES], axis=1)
                                 for c in range(0, tq, LANES)], axis=1)
            out.append(jnp.concatenate([b, b], axis=1))
        return out

    s_scs = (s0_sc, s1_sc)

    def scores(off, slot):
        for h in range(heads):
            s_scs[slot][h] = _dot(k_ref[pl.ds(off, tk), hs(h)], ws[h])

    def consume(off, slot, bias, mask, const=None):
        for h in range(heads):
            s = s_scs[slot][h]
            if bias is not None:
                s = s + bias[h]
            if mask is not None:
                s = jnp.where(mask, s, NEG_INF)
            m = m_sc[h]
            m_tile = jnp.max(s, axis=0, keepdims=True)
            if const is not None:
                m_tile = m_tile + const[h]
            m_new = jnp.maximum(m, m_tile)
            alpha = jnp.exp2(m - m_new)
            p = jnp.exp2(s - (m_new if const is None else m_new - const[h]))
            l_sc[h] = alpha * l_sc[h] + jnp.sum(p, axis=0, keepdims=True)
            acc_sc[h] = alpha * acc_sc[h] + _dot(vt_ref[hs(h), pl.ds(off, tk)], p.astype(BF16))
            m_sc[h] = m_new

    assert tq == 2 * tk
    scores(0, 0)

    def body(jj, carry):
        off0 = pl.multiple_of(jj * tq, tq)
        off1 = pl.multiple_of(off0 + tk, tk)
        off2 = pl.multiple_of(off0 + tq, tq)

        def pair(bias_fn, const):
            scores(off1, 1)
            consume(off0, 0, bias_fn(off0), None, const)
            scores(off2, 0)
            consume(off1, 1, bias_fn(off1), None, const)

        no_bias = lambda off: None
        if biased:
            nearest = jnp.maximum(pmax_ref[bi, 2 * jj], pmax_ref[bi, 2 * jj + 1])
            far = pmin_ref[bi, qi] - nearest >= REL_TABLE - 1

            @pl.when(far)
            def _():
                pair(no_bias, [ftab_ref[h][:, REL_TABLE - 1:REL_TABLE] * LOG2E for h in range(heads)])

            @pl.when(jnp.logical_not(far))
            def _():
                pair(near_biases, None)
        else:
            pair(no_bias, None)
        return carry

    lax.fori_loop(0, qi, body, 0)
    for d in range(2):
        off = pl.multiple_of(qi * tq + d * tk, tk)
        if d == 0:
            scores(pl.multiple_of(off + tk, tk), 1)
        key = lax.broadcasted_iota(jnp.int32, (tk, cols), 0) + d * tk
        qry = lax.broadcasted_iota(jnp.int32, (tk, cols), 1) % tq
        consume(off, d, near_biases(off) if biased else None, key <= qry)

    if biased:
        lp = lam_ref[...]
        lam = (jnp.exp(jnp.sum(lp[0:1] * lp[1:2], axis=-1, keepdims=True))
               - jnp.exp(jnp.sum(lp[2:3] * lp[3:4], axis=-1, keepdims=True)) + lambda_init)
    for h in range(heads):
        o = acc_sc[h] / l_sc[h]
        if biased:
            o = (o[:, :tq] - lam * o[:, tq:]).T
            o_ref[:, hs(h)] = (_rms(o, sg_ref[...]) * (1.0 - lambda_init)).astype(o_ref.dtype)
        else:
            o_ref[:, hs(h)] = o.T.astype(o_ref.dtype)


def _flash_scratch(heads, tk, cols):
    return [pltpu.VMEM((heads, 1, cols), F32), pltpu.VMEM((heads, 1, cols), F32),
            pltpu.VMEM((heads, HEAD_W, cols), F32),
            pltpu.VMEM((heads, tk, cols), F32), pltpu.VMEM((heads, tk, cols), F32)]


def _mla_attention(qt, k, vt, *, tq=ATTN_TQ, tk=ATTN_TK):
    b, s, w = k.shape
    return pl.pallas_call(
        functools.partial(_flash_kernel, tq=tq, tk=tk, n_maps=1, lambda_init=None),
        grid=(b, s // tq),
        in_specs=[pl.BlockSpec((None, w, tq), lambda bi, i: (bi, 0, i)),
                  pl.BlockSpec((None, s, w), lambda bi, i: (bi, 0, 0), pipeline_mode=pl.Buffered(1)),
                  pl.BlockSpec((None, w, s), lambda bi, i: (bi, 0, 0), pipeline_mode=pl.Buffered(1))],
        out_specs=pl.BlockSpec((None, tq, w), lambda bi, i: (bi, i, 0)),
        out_shape=jax.ShapeDtypeStruct((b, s, w), BF16),
        scratch_shapes=_flash_scratch(w // HEAD_W, tk, tq),
        compiler_params=_params("parallel", "arbitrary"),
        name="mla_attention",
    )(qt, k, vt)


def _diff_attention(qt, k, vt, pos, ftab, lam_params, sub_g, lambda_init, *, tq=ATTN_TQ, tk=ATTN_TK):
    b, s, w = k.shape
    pmin = jnp.min(pos.reshape(b, s // tq, tq), axis=-1)
    pmax = jnp.max(pos.reshape(b, s // tk, tk), axis=-1)
    pos_col = pos.reshape(b, s, 1)
    pos_row = pos.reshape(b, 1, s)
    grid_spec = pltpu.PrefetchScalarGridSpec(
        num_scalar_prefetch=2,
        grid=(b, s // tq),
        in_specs=[pl.BlockSpec((None, w, tq), lambda bi, i, *_: (bi, 0, i)),
                  pl.BlockSpec((None, s, w), lambda bi, i, *_: (bi, 0, 0), pipeline_mode=pl.Buffered(1)),
                  pl.BlockSpec((None, w, s), lambda bi, i, *_: (bi, 0, 0), pipeline_mode=pl.Buffered(1)),
                  pl.BlockSpec((None, 1, tq), lambda bi, i, *_: (bi, 0, i)),
                  pl.BlockSpec((None, s, 1), lambda bi, i, *_: (bi, 0, 0), pipeline_mode=pl.Buffered(1)),
                  pl.BlockSpec(ftab.shape, lambda bi, i, *_: (0, 0, 0)),
                  pl.BlockSpec(lam_params.shape, lambda bi, i, *_: (0, 0)),
                  pl.BlockSpec(sub_g.shape, lambda bi, i, *_: (0, 0))],
        out_specs=pl.BlockSpec((None, tq, w), lambda bi, i, *_: (bi, i, 0)),
        scratch_shapes=_flash_scratch(w // HEAD_W, tk, 2 * tq),
    )
    return pl.pallas_call(
        functools.partial(_flash_kernel, tq=tq, tk=tk, n_maps=2, lambda_init=lambda_init),
        grid_spec=grid_spec,
        out_shape=jax.ShapeDtypeStruct((b, s, w), BF16),
        compiler_params=_params("parallel", "arbitrary"),
        name="diff_attention",
    )(pmin, pmax, qt, k, vt, pos_row, pos_col, ftab, lam_params, sub_g)


def _merge_kernel(h_ref, g_ref, y0_ref, y1_ref, y2_ref, y3_ref, wg_ref, wb_ref, wo_ref, o_ref, *, col_chunk):
    h = h_ref[...]
    d = h.shape[-1]
    u = _rms(h, g_ref[...]).astype(BF16)
    ys = (y0_ref[...], y1_ref[...], y2_ref[...], y3_ref[...])
    parts = []
    for c0 in range(0, d, col_chunk):
        acc = None
        for i, y in enumerate(ys):
            gate = jax.nn.sigmoid(_dot(u, wg_ref[:, i * d + c0:i * d + c0 + col_chunk]))
            term = gate * _dot(y, wb_ref[i, :, c0:c0 + col_chunk])
            acc = term if acc is None else acc + term
        parts.append(acc.astype(BF16))
    merged = jnp.concatenate(parts, axis=-1)
    o_ref[...] = h + _dot(merged, wo_ref[...])


def _merge(h, g, ys, w_gate, w_branch, w_out, *, tm=512, col_chunk=512):
    m, d = h.shape
    rows = lambda width: pl.BlockSpec((tm, width), lambda i: (i, 0))
    return pl.pallas_call(
        functools.partial(_merge_kernel, col_chunk=col_chunk),
        grid=(m // tm,),
        in_specs=[rows(d), _resident((1, d))] + [rows(BRANCH_W)] * N_BRANCH
                 + [_resident(w_gate), _resident(w_branch), _resident(w_out)],
        out_specs=rows(d),
        out_shape=jax.ShapeDtypeStruct((m, d), F32),
        compiler_params=_params("parallel"),
        name="merge",
    )(h, g, *ys, _operand(w_gate), _operand(w_branch), _operand(w_out))


def _xattn_kernel(h_ref, g_ref, wq_ref, k_ref, v_ref, wo_ref, o_ref):
    h = h_ref[...]
    d = h.shape[-1]
    dh = d // XA_HEADS
    u = _rms(h, g_ref[...]).astype(BF16)
    q = (_dot(u, wq_ref[...]) * (dh ** -0.5)).astype(BF16)
    out = h
    for hd in range(XA_HEADS):
        sl = slice(hd * dh, (hd + 1) * dh)
        s = _dot_nt(q[:, sl], k_ref[:, sl])
        p = jnp.exp(s - jnp.max(s, axis=-1, keepdims=True))
        p = p / jnp.sum(p, axis=-1, keepdims=True)
        o = _dot(p.astype(BF16), v_ref[:, sl])
        out = out + _dot(o.astype(BF16), wo_ref[sl, :])
    o_ref[...] = out


def _cross_attention(h, g, wq, k, v, wo, batch, seq, *, tm=512):
    m, d = h.shape
    mem_len = k.shape[0] // batch
    per = seq // tm
    return pl.pallas_call(
        _xattn_kernel,
        grid=(m // tm,),
        in_specs=[pl.BlockSpec((tm, d), lambda i: (i, 0)), _resident((1, d)), _resident(wq),
                  pl.BlockSpec((mem_len, d), lambda i: (i // per, 0)),
                  pl.BlockSpec((mem_len, d), lambda i: (i // per, 0)),
                  _resident(wo)],
        out_specs=pl.BlockSpec((tm, d), lambda i: (i, 0)),
        out_shape=jax.ShapeDtypeStruct((m, d), F32),
        compiler_params=_params("parallel"),
        name="cross_attention",
    )(h, g, _operand(wq), k, v, _operand(wo))


def _split_cols(w, sizes):
    out, start = [], 0
    for s in sizes:
        out.append(w[..., start:start + s])
        start += s
    return out


def _pad_cols(w, left, total):
    return jnp.pad(w, ((0, 0),) * (w.ndim - 1) + ((left, total - left - w.shape[-1]),))


def _relaid_weights(w_in, w_uq, w_ukv):
    (rq, rk, rv, rg, sz, sxbc, sdt, cq, ckv, kr, dq, dk, dv, gl) = _split_cols(w_in.astype(BF16), IN_SIZES)
    sdt_pad = _pad_cols(sdt, 0, SSD_DT_W)
    kr_pad = _pad_cols(kr, MLA_NOPE, MLA_PAD)
    w_mix = jnp.concatenate([rq, rk, rv, rg, sxbc, sz, sdt_pad, cq, ckv, kr_pad, dq, dk, dv], axis=-1)
    depth, d, _ = w_uq.shape
    uq = w_uq.astype(BF16).reshape(depth, d, MLA_HEADS, MLA_QH)
    uq = _pad_cols(uq, 0, MLA_PAD).reshape(depth, d, MLA_HEADS * MLA_PAD)
    r = w_ukv.shape[1]
    ukv = w_ukv.astype(BF16).reshape(depth, r, MLA_HEADS, MLA_NOPE + MLA_V)
    k_nope = _pad_cols(ukv[..., :MLA_NOPE], 0, MLA_PAD)
    ukv_p = jnp.concatenate([k_nope.reshape(depth, r, -1), ukv[..., MLA_NOPE:].reshape(depth, r, -1)], axis=-1)
    return w_mix, gl, uq, ukv_p


def _t5_bucket_const():
    n = jnp.arange(REL_TABLE)
    exact = REL_BUCKETS // 2
    nf = jnp.maximum(n, 1).astype(F32)
    large = exact + (jnp.log(nf / exact) / math.log(REL_MAX_DIST / exact) * (REL_BUCKETS - exact)).astype(jnp.int32)
    return jnp.where(n < exact, n, jnp.minimum(large, REL_BUCKETS - 1))


MIX_OUTS = ((3 * BRANCH_W, F32, 1.0, False),
            (SSD_P_WIDTH, F32, 1.0, False),
            (MLA_Q_RANK + MLA_KV_RANK + MLA_PAD, F32, 1.0, False),
            (BRANCH_W, BF16, DIFF_DH ** -0.5 * LOG2E, True),
            (BRANCH_W, BF16, 1.0, False),
            (BRANCH_W, BF16, 1.0, True))


def kernel(x, mem, positions, ffn1_norm, ffn1_w_gate, ffn1_w_up, ffn1_w_down, mix_norm, w_in, ssm_conv_w,
           ssm_conv_b, ssm_dt_bias, ssm_a_log, ssm_d, ssm_norm, mla_q_norm, mla_kv_norm, mla_w_uq, mla_w_ukv,
           diff_lambda, diff_norm, rel_bias, w_branch, w_out, xa_norm, mem_norm, xa_wq, xa_wk, xa_wv, xa_wo,
           ffn2_norm, ffn2_w_gate, ffn2_w_up, ffn2_w_down, final_norm):
    batch, seq, d = x.shape
    depth = w_in.shape[0]
    m = batch * seq
    h = x.reshape(m, d)
    mem2 = mem.reshape(batch * mem.shape[1], d)
    row = lambda a: a.reshape(1, -1)

    pos_col = positions.reshape(m, 1)
    ret_cos, ret_sin = _rope_tables(pos_col, *_rope_rows(RET_DK, RET_DK // 2, 0))
    mla_cos, mla_sin = _rope_tables(pos_col, *_rope_rows(MLA_PAD, MLA_ROPE // 2, MLA_NOPE))
    ftab = rel_bias[_t5_bucket_const()].T.reshape(DIFF_HEADS, 1, REL_TABLE)

    w_mix, w_gate, w_uq_p, w_ukv_p = _relaid_weights(_cast_bf16(w_in), mla_w_uq, mla_w_ukv)
    ffn1 = [_cast_bf16(w) for w in (ffn1_w_gate, ffn1_w_up, ffn1_w_down)]
    ffn2 = [_cast_bf16(w) for w in (ffn2_w_gate, ffn2_w_up, ffn2_w_down)]
    w_branch_b = _cast_bf16(w_branch.reshape(depth, N_BRANCH * BRANCH_W, d)).reshape(w_branch.shape)
    w_out_b, xa_wq_b, xa_wo_b = _cast_bf16(w_out), _cast_bf16(xa_wq), _cast_bf16(xa_wo)
    w_kv = jnp.concatenate([_cast_bf16(xa_wk), _cast_bf16(xa_wv)], axis=-1)

    for l in range(depth):
        at = lambda stack: _LayerOf(stack, l)
        h = _ffn(h, row(ffn1_norm[l]), *(at(w) for w in ffn1))
        ret_p, ssm_p, mla_p, dqt, dk, dvt = _norm_proj(h, row(mix_norm[l]), at(w_mix), MIX_OUTS, seq,
                                                       name="mix_proj")
        y_ret = _retention(ret_p.reshape(batch, seq, -1), ret_cos.reshape(batch, seq, -1),
                           ret_sin.reshape(batch, seq, -1)).reshape(m, BRANCH_W)
        y_ssm = _ssd(ssm_p, ssm_conv_w[l], row(ssm_conv_b[l]),
                     _pad_cols(row(ssm_dt_bias[l]), 0, SSD_DT_W), _pad_cols(row(ssm_a_log[l]), 0, SSD_DT_W),
                     row(jnp.repeat(ssm_d[l], SSM_HEADDIM)), row(ssm_norm[l]), batch, seq)
        mqt, mk, mvt = _mla_prep(mla_p, mla_cos, mla_sin, row(mla_q_norm[l]), row(mla_kv_norm[l]), at(w_uq_p),
                                 at(w_ukv_p), seq)
        y_mla = _mla_attention(mqt, mk.reshape(batch, seq, -1), mvt).reshape(m, BRANCH_W)
        lambda_init = 0.8 - 0.6 * math.exp(-0.3 * l)
        y_diff = _diff_attention(dqt, dk.reshape(batch, seq, -1), dvt, positions, ftab, diff_lambda[l],
                                 row(diff_norm[l]), lambda_init).reshape(m, BRANCH_W)
        h = _merge(h, row(mix_norm[l]), (y_ret, y_ssm, y_mla, y_diff), at(w_gate), at(w_branch_b), at(w_out_b))
        xk, xv = _norm_proj(mem2, row(mem_norm[l]), at(w_kv), [(d, BF16, 1.0, False)] * 2, mem.shape[1],
                            tm=mem.shape[1], name="mem_kv")
        h = _cross_attention(h, row(xa_norm[l]), at(xa_wq_b), xk, xv, at(xa_wo_b), batch, seq)
        h = _ffn(h, row(ffn2_norm[l]), *(at(w) for w in ffn2),
                 final_g=row(final_norm) if l == depth - 1 else None)
    return h.reshape(batch, seq, d)
```

```python
import functools
import math
from typing import NamedTuple

import jax
import jax.numpy as jnp
from jax import lax
from jax.experimental import pallas as pl
from jax.experimental.pallas import tpu as pltpu

F32 = jnp.float32
BF16 = jnp.bfloat16

EPS = 1e-6
NEG_INF = -1e30
LOG2E = math.log2(math.e)
ROPE_BASE = 10000.0
CHUNK = 128
BRANCH_W = 512
N_BRANCH = 4

RET_HEADS = 4
RET_DK = 64
RET_DV = 128
RET_QK = RET_HEADS * RET_DK

SSM_HEADDIM = 64
SSM_INNER = 512
SSM_HEADS = 8
SSM_GROUPS = 2
SSM_STATE = 128
SSM_CONV = 4
SSM_XBC = SSM_INNER + 2 * SSM_GROUPS * SSM_STATE
SSM_TAIL = 8
SSD_DT_W = 128
SSD_P_WIDTH = SSM_XBC + SSM_INNER + SSD_DT_W

MLA_HEADS = 4
MLA_Q_RANK = 256
MLA_KV_RANK = 128
MLA_NOPE = 64
MLA_ROPE = 32
MLA_V = 128
MLA_QH = MLA_NOPE + MLA_ROPE
MLA_PAD = 128
HEAD_W = 128
ATTN_TQ = 512
ATTN_TK = 256

DIFF_HEADS = 4
DIFF_DH = 64

REL_BUCKETS = 32
REL_MAX_DIST = 128
REL_TABLE = 128

XA_HEADS = 4

LANES = 128
VMEM_LIMIT = 56 * 1024 * 1024

IN_SIZES = (RET_QK, RET_QK, BRANCH_W, BRANCH_W,
            SSM_INNER, SSM_XBC, SSM_HEADS,
            MLA_Q_RANK, MLA_KV_RANK, MLA_ROPE,
            BRANCH_W, BRANCH_W, BRANCH_W,
            N_BRANCH * 1024)


def _params(*sem):
    return pltpu.CompilerParams(dimension_semantics=sem, vmem_limit_bytes=VMEM_LIMIT)


class _LayerOf(NamedTuple):
    stack: jax.Array
    layer: int

    @property
    def shape(self):
        return self.stack.shape[1:]


def _operand(x):
    return x.stack if isinstance(x, _LayerOf) else x


def _resident(x):
    if isinstance(x, _LayerOf):
        shape, lead, first = x.shape, (None,), (x.layer,)
    else:
        shape, lead, first = (x if isinstance(x, tuple) else x.shape), (), ()
    nd = len(shape)
    return pl.BlockSpec(lead + tuple(shape), lambda *_: first + (0,) * nd, pipeline_mode=pl.Buffered(1))


def _cast_kernel(x_ref, o_ref):
    o_ref[...] = x_ref[...].astype(o_ref.dtype)


def _cast_bf16(w, *, row_blocks=4):
    depth, r, c = w.shape
    tr = r // row_blocks
    spec = pl.BlockSpec((None, tr, c), lambda l, i: (l, i, 0))
    return pl.pallas_call(
        _cast_kernel,
        grid=(depth, row_blocks),
        in_specs=[spec],
        out_specs=spec,
        out_shape=jax.ShapeDtypeStruct(w.shape, BF16),
        compiler_params=_params("parallel", "parallel"),
        name="cast_bf16",
    )(w)


def _rms(x, g):
    return x * lax.rsqrt(jnp.mean(x * x, axis=-1, keepdims=True) + EPS) * g


def _silu(x):
    return x * jax.nn.sigmoid(x)


def _dot(a, b):
    return jnp.dot(a, b, preferred_element_type=F32)


def _dot_nt(a, b):
    return lax.dot_general(a, b, (((1,), (1,)), ((), ())), preferred_element_type=F32)


def _dot_tn(a, b):
    return lax.dot_general(a, b, (((0,), (0,)), ((), ())), preferred_element_type=F32)


def _ffn_kernel(h_ref, g_ref, wg_ref, wu_ref, wd_ref, *rest, ff_chunk, final):
    if final:
        gf_ref, o_ref = rest
    else:
        (o_ref,) = rest
    h = h_ref[...]
    xn = _rms(h, g_ref[...]).astype(BF16)
    acc = jnp.zeros(h.shape, F32)
    d_ff = wg_ref.shape[1]
    for c0 in range(0, d_ff, ff_chunk):
        a = _dot(xn, wg_ref[:, c0:c0 + ff_chunk])
        b = _dot(xn, wu_ref[:, c0:c0 + ff_chunk])
        acc = acc + _dot((_silu(a) * b).astype(BF16), wd_ref[c0:c0 + ff_chunk, :])
    out = h + 0.5 * acc
    if final:
        out = _rms(out, gf_ref[...])
    o_ref[...] = out


def _ffn(h, g, wg, wu, wd, final_g=None, *, tm=512, ff_chunk=256):
    m, d = h.shape
    d_ff = wg.shape[1]
    final = final_g is not None
    in_specs = [pl.BlockSpec((tm, d), lambda i: (i, 0)), _resident((1, d)),
                _resident(wg), _resident(wu), _resident(wd)]
    args = [h, g, _operand(wg), _operand(wu), _operand(wd)]
    if final:
        in_specs.append(_resident((1, d)))
        args.append(final_g)
    return pl.pallas_call(
        functools.partial(_ffn_kernel, ff_chunk=ff_chunk, final=final),
        grid=(m // tm,),
        in_specs=in_specs,
        out_specs=pl.BlockSpec((tm, d), lambda i: (i, 0)),
        out_shape=jax.ShapeDtypeStruct((m, d), F32),
        compiler_params=_params("parallel"),
        name="ffn",
    )(*args)


def _norm_proj_kernel(x_ref, g_ref, w_ref, *o_refs, outs, col_chunk):
    xn = _rms(x_ref[...], g_ref[...]).astype(BF16)
    off = 0
    for o_ref, (width, _, scale, transposed) in zip(o_refs, outs):
        for c0 in range(0, width, col_chunk):
            c1 = min(c0 + col_chunk, width)
            y = _dot(xn, w_ref[:, off + c0:off + c1])
            if scale != 1.0:
                y = y * scale
            if transposed:
                o_ref[c0:c1, :] = y.astype(o_ref.dtype).T
            else:
                o_ref[:, c0:c1] = y.astype(o_ref.dtype)
        off += width


def _norm_proj(x, g, w, outs, seq, *, tm=512, col_chunk=512, name="norm_proj"):
    m, d = x.shape
    assert sum(o[0] for o in outs) == w.shape[1]
    per = seq // tm
    out_specs, out_shape = [], []
    for width, dt, _, transposed in outs:
        if transposed:
            out_specs.append(pl.BlockSpec((None, width, tm), lambda i: (i // per, 0, i % per)))
            out_shape.append(jax.ShapeDtypeStruct((m // seq, width, seq), dt))
        else:
            out_specs.append(pl.BlockSpec((tm, width), lambda i: (i, 0)))
            out_shape.append(jax.ShapeDtypeStruct((m, width), dt))
    return pl.pallas_call(
        functools.partial(_norm_proj_kernel, outs=tuple(outs), col_chunk=col_chunk),
        grid=(m // tm,),
        in_specs=[pl.BlockSpec((tm, d), lambda i: (i, 0)), _resident((1, d)), _resident(w)],
        out_specs=out_specs,
        out_shape=out_shape,
        compiler_params=_params("parallel"),
        name=name,
    )(x, g, _operand(w))


def _rope_table_kernel(pos_ref, inv_ref, sign_ref, cos_ref, sin_ref):
    ang = pos_ref[...].astype(F32) * inv_ref[...]
    cos_ref[...] = jnp.cos(ang)
    sin_ref[...] = jnp.sin(ang) * sign_ref[...]


def _rope_tables(pos_col, inv_row, sign_row, *, tm=1024):
    m = pos_col.shape[0]
    return pl.pallas_call(
        _rope_table_kernel,
        grid=(m // tm,),
        in_specs=[pl.BlockSpec((tm, 1), lambda i: (i, 0)), _resident((1, LANES)), _resident((1, LANES))],
        out_specs=[pl.BlockSpec((tm, LANES), lambda i: (i, 0))] * 2,
        out_shape=[jax.ShapeDtypeStruct((m, LANES), F32)] * 2,
        compiler_params=_params("parallel"),
        name="rope_tables",
    )(pos_col, inv_row, sign_row)


def _rope_rows(period, half, start):
    lane = jnp.arange(LANES) % period
    inside = (lane >= start) & (lane < start + 2 * half)
    idx = (lane - start) % half
    inv = jnp.exp(-math.log(ROPE_BASE) * idx.astype(F32) / half)
    inv = jnp.where(inside, inv, 0.0)
    sign = jnp.where(inside, jnp.where(lane < start + half, -1.0, 1.0), 0.0)
    return inv.reshape(1, LANES).astype(F32), sign.reshape(1, LANES).astype(F32)


def _apply_rope(x, cos, sin_signed, period, half, start):
    width = x.shape[-1]
    reps = width // LANES
    if reps > 1:
        cos = jnp.concatenate([cos] * reps, axis=-1)
        sin_signed = jnp.concatenate([sin_signed] * reps, axis=-1)
    lane = lax.broadcasted_iota(jnp.int32, (1, width), 1) % period
    first = lane < start + half
    partner = jnp.where(first, pltpu.roll(x, width - half, 1), pltpu.roll(x, half, 1))
    return x * cos + partner * sin_signed


def _ret_kernel(q_ref, k_ref, v_ref, g_ref, cos_ref, sin_ref, intra_ref, qdec_ref, kdec_ref,
                sdec_ref, bmask_ref, o_ref, state_sc):
    @pl.when(pl.program_id(0) == 0)
    def _():
        state_sc[...] = jnp.zeros_like(state_sc)

    head = lax.broadcasted_iota(jnp.int32, (1, RET_QK), 1) // RET_DK
    for b in range(q_ref.shape[0]):
        cos, sin = cos_ref[b], sin_ref[b]
        q = _apply_rope(q_ref[b], cos, sin, RET_DK, RET_DK // 2, 0) * (RET_DK ** -0.5)
        k = _apply_rope(k_ref[b], cos, sin, RET_DK, RET_DK // 2, 0)
        vb = v_ref[b].astype(BF16)
        kb = k.astype(BF16)
        state = state_sc[b]
        cross = _dot((q * qdec_ref[...]).astype(BF16), state.astype(BF16))
        outs = []
        for h in range(RET_HEADS):
            qh = jnp.where(head == h, q, 0.0).astype(BF16)
            att = _dot_nt(qh, kb) * intra_ref[h]
            sl = slice(h * RET_DV, (h + 1) * RET_DV)
            o = _dot(att.astype(BF16), vb[:, sl]) + cross[:, sl]
            mu = jnp.mean(o, axis=-1, keepdims=True)
            oc = o - mu
            outs.append(oc * lax.rsqrt(jnp.mean(oc * oc, axis=-1, keepdims=True) + EPS))
        o_ref[b] = (jnp.concatenate(outs, axis=-1) * _silu(g_ref[b])).astype(o_ref.dtype)
        kv = _dot_tn((k * kdec_ref[...]).astype(BF16), vb)
        state_sc[b] = state * sdec_ref[...] + kv * bmask_ref[...]


def _retention_consts():
    c = CHUNK
    log_g = jnp.log1p(-jnp.exp2(-5.0 - jnp.arange(RET_HEADS, dtype=F32)))
    i = jnp.arange(c, dtype=F32)
    rel = i[:, None] - i[None, :]
    intra = jnp.where(rel >= 0, jnp.exp(log_g[:, None, None] * jnp.maximum(rel, 0.0)), 0.0)
    lg_lane = jnp.repeat(log_g, RET_DK)
    qdec = jnp.exp(lg_lane[None, :] * (i[:, None] + 1.0))
    kdec = jnp.exp(lg_lane[None, :] * (c - 1.0 - i[:, None]))
    row_h = jnp.repeat(jnp.arange(RET_HEADS), RET_DK)
    col_h = jnp.repeat(jnp.arange(RET_HEADS), RET_DV)
    bmask = (row_h[:, None] == col_h[None, :]).astype(F32)
    sdec = bmask * jnp.exp(lg_lane * c)[:, None]
    return intra, qdec, kdec, sdec, bmask


def _retention(ret_p, cos, sin):
    c = CHUNK
    batch, seq, _ = ret_p.shape
    intra, qdec, kdec, sdec, bmask = _retention_consts()
    blk = lambda width, col: pl.BlockSpec((batch, c, width), lambda j: (0, j, col))
    return pl.pallas_call(
        _ret_kernel,
        grid=(seq // c,),
        in_specs=[blk(RET_QK, 0), blk(RET_QK, 1), blk(BRANCH_W, 1), blk(BRANCH_W, 2), blk(LANES, 0), blk(LANES, 0),
                  _resident(intra.shape), _resident(qdec.shape), _resident(kdec.shape),
                  _resident(sdec.shape), _resident(bmask.shape)],
        out_specs=blk(BRANCH_W, 0),
        out_shape=jax.ShapeDtypeStruct((batch, seq, BRANCH_W), BF16),
        scratch_shapes=[pltpu.VMEM((batch, RET_QK, BRANCH_W), F32)],
        compiler_params=_params("arbitrary"),
        name="retention",
    )(ret_p, ret_p, ret_p, ret_p, cos, sin, intra, qdec, kdec, sdec, bmask)


def _split3(x):
    hi = x.astype(BF16)
    r = x - hi.astype(F32)
    mid = r.astype(BF16)
    lo = (r - mid.astype(F32)).astype(BF16)
    return hi, mid, lo


def _dot_split3(a, b, *, left):
    if left:
        return sum(_dot(piece, b) for piece in _split3(a))
    return sum(_dot(a, piece) for piece in _split3(b))


def _ssd_kernel(xbc_ref, z_ref, dt_ref, cw_ref, cb_ref, dtb_ref, alog_ref, dskip_ref, ng_ref, tri_ref,
                expand_ref, o_ref, xe_sc, state_sc):
    L = CHUNK
    T = SSM_TAIL
    GW = SSM_INNER // SSM_GROUPS

    @pl.when(pl.program_id(1) == 0)
    def _():
        xe_sc[0:T, :] = jnp.zeros((T, SSM_XBC), F32)
        state_sc[...] = jnp.zeros_like(state_sc)

    x_raw = xbc_ref[...]
    xe_sc[T:T + L, :] = x_raw
    conv = cb_ref[...]
    for kk in range(SSM_CONV):
        s0 = T - (SSM_CONV - 1) + kk
        conv = conv + cw_ref[kk:kk + 1, :] * xe_sc[s0:s0 + L, :]
    xe_sc[0:T, :] = x_raw[L - T:L, :]
    xa = _silu(conv)
    xs = xa[:, :SSM_INNER]
    bm = xa[:, SSM_INNER:SSM_INNER + SSM_GROUPS * SSM_STATE]
    cm = xa[:, SSM_INNER + SSM_GROUPS * SSM_STATE:]

    dtr = dt_ref[...] + dtb_ref[...]
    dt_h = jnp.maximum(dtr, 0.0) + jnp.log1p(jnp.exp(-jnp.abs(dtr)))
    da = dt_h * (-jnp.exp(alog_ref[...]))
    cs_h = _dot_split3(tri_ref[...], da, left=False)
    cs_h_last = cs_h[L - 1:L, :]
    cs_t = cs_h.T
    expand = expand_ref[...]
    dt = _dot_split3(dt_h, expand, left=True)
    ecs = _dot_split3(jnp.exp(cs_h), expand, left=True)
    eds = _dot_split3(jnp.exp(cs_h_last - cs_h), expand, left=True)
    xdt = xs * dt
    xdec = (xdt * eds).astype(BF16)
    state = state_sc[...]
    li = lax.broadcasted_iota(jnp.int32, (L, L), 0)
    si = lax.broadcasted_iota(jnp.int32, (L, L), 1)
    causal = li >= si
    lane_half = lax.broadcasted_iota(jnp.int32, (1, LANES), 1) // SSM_HEADDIM
    ys = []
    new_states = []
    for g in range(SSM_GROUPS):
        bg = bm[:, g * SSM_STATE:(g + 1) * SSM_STATE].astype(BF16)
        cg = cm[:, g * SSM_STATE:(g + 1) * SSM_STATE].astype(BF16)
        cb = _dot_nt(cg, bg)
        gsl = slice(g * GW, (g + 1) * GW)
        y_off = _dot(cg, state[:, gsl].astype(BF16)) * ecs[:, gsl]
        for p in range(GW // LANES):
            sl = slice(g * GW + p * LANES, g * GW + (p + 1) * LANES)
            xdt_blk = xdt[:, sl]
            acc = y_off[:, p * LANES:(p + 1) * LANES]
            for hh in range(LANES // SSM_HEADDIM):
                head = (g * GW + p * LANES) // SSM_HEADDIM + hh
                col = cs_h[:, head:head + 1]
                rowv = cs_t[head:head + 1, :]
                dec = jnp.where(causal, jnp.exp(jnp.minimum(col - rowv, 0.0)), 0.0)
                xm = jnp.where(lane_half == hh, xdt_blk, 0.0).astype(BF16)
                acc = acc + _dot((cb * dec).astype(BF16), xm)
            ys.append(acc)
        new_states.append(_dot_tn(bg, xdec[:, gsl]))
    y = jnp.concatenate(ys, axis=-1) + xs * dskip_ref[...]
    y = y * _silu(z_ref[...])
    o_ref[...] = _rms(y, ng_ref[...]).astype(o_ref.dtype)
    state_sc[...] = state * ecs[L - 1:L, :] + jnp.concatenate(new_states, axis=-1)


def _ssd(ssm_p, conv_w, conv_b, dt_bias_pad, a_log_pad, d_full, norm_g, batch, seq):
    L = CHUNK
    n = seq // L
    tri = (jnp.arange(L)[:, None] >= jnp.arange(L)[None, :]).astype(BF16)
    expand = (jnp.arange(SSD_DT_W)[:, None] == jnp.arange(SSM_INNER)[None, :] // SSM_HEADDIM).astype(BF16)
    row = lambda b, j: b * n + j
    return pl.pallas_call(
        _ssd_kernel,
        grid=(batch, n),
        in_specs=[pl.BlockSpec((L, SSM_XBC), lambda b, j: (row(b, j), 0)),
                  pl.BlockSpec((L, SSM_INNER), lambda b, j: (row(b, j), SSM_XBC // SSM_INNER)),
                  pl.BlockSpec((L, SSD_DT_W), lambda b, j: (row(b, j), (SSM_XBC + SSM_INNER) // SSD_DT_W)),
                  _resident(conv_w.shape), _resident(conv_b.shape), _resident(dt_bias_pad.shape),
                  _resident(a_log_pad.shape), _resident(d_full.shape), _resident(norm_g.shape),
                  _resident(tri.shape), _resident(expand.shape)],
        out_specs=pl.BlockSpec((L, SSM_INNER), lambda b, j: (row(b, j), 0)),
        out_shape=jax.ShapeDtypeStruct((batch * seq, SSM_INNER), BF16),
        scratch_shapes=[pltpu.VMEM((SSM_TAIL + L, SSM_XBC), F32),
                        pltpu.VMEM((SSM_STATE, SSM_INNER), F32)],
        compiler_params=_params("parallel", "arbitrary"),
        name="ssd",
    )(ssm_p, ssm_p, ssm_p, conv_w, conv_b, dt_bias_pad, a_log_pad, d_full, norm_g, tri, expand)


def _mla_prep_kernel(p_ref, cos_ref, sin_ref, qn_ref, kvn_ref, wuq_ref, wukv_ref, q_ref, k_ref, v_ref):
    cos, sin = cos_ref[...], sin_ref[...]
    kvw = MLA_HEADS * MLA_PAD
    cq = p_ref[:, :MLA_Q_RANK]
    ckv = p_ref[:, MLA_Q_RANK:MLA_Q_RANK + MLA_KV_RANK]
    kr = p_ref[:, MLA_Q_RANK + MLA_KV_RANK:]
    q = _dot(_rms(cq, qn_ref[...]).astype(BF16), wuq_ref[...])
    q = _apply_rope(q, cos, sin, MLA_PAD, MLA_ROPE // 2, MLA_NOPE) * (MLA_QH ** -0.5 * LOG2E)
    q_ref[...] = q.astype(q_ref.dtype).T
    kv = _dot(_rms(ckv, kvn_ref[...]).astype(BF16), wukv_ref[...])
    kpe = _apply_rope(kr, cos, sin, MLA_PAD, MLA_ROPE // 2, MLA_NOPE)
    k_ref[...] = (kv[:, :kvw] + jnp.concatenate([kpe] * MLA_HEADS, axis=-1)).astype(k_ref.dtype)
    v_ref[...] = kv[:, kvw:].astype(v_ref.dtype).T


def _mla_prep(mla_p, cos, sin, q_norm, kv_norm, w_uq_p, w_ukv_p, seq, *, tm=512):
    m = mla_p.shape[0]
    w = MLA_HEADS * MLA_PAD
    per = seq // tm
    rows = lambda width: pl.BlockSpec((tm, width), lambda i: (i, 0))
    cols = pl.BlockSpec((None, w, tm), lambda i: (i // per, 0, i % per))
    t_shape = jax.ShapeDtypeStruct((m // seq, w, seq), BF16)
    return pl.pallas_call(
        _mla_prep_kernel,
        grid=(m // tm,),
        in_specs=[rows(mla_p.shape[1]), rows(LANES), rows(LANES), _resident(q_norm.shape),
                  _resident(kv_norm.shape), _resident(w_uq_p), _resident(w_ukv_p)],
        out_specs=[cols, rows(w), cols],
        out_shape=[t_shape, jax.ShapeDtypeStruct((m, w), BF16), t_shape],
        compiler_params=_params("parallel"),
        name="mla_prep",
    )(mla_p, cos, sin, q_norm, kv_norm, _operand(w_uq_p), _operand(w_ukv_p))


def _flash_kernel(*refs, tq, tk, n_maps, lambda_init):
    biased = n_maps == 2
    if biased:
        (pmin_ref, pmax_ref, qt_ref, k_ref, vt_ref, pq_ref, pk_ref, ftab_ref, lam_ref, sg_ref,
         o_ref, m_sc, l_sc, acc_sc, s0_sc, s1_sc) = refs
    else:
        qt_ref, k_ref, vt_ref, o_ref, m_sc, l_sc, acc_sc, s0_sc, s1_sc = refs
    bi = pl.program_id(0)
    qi = pl.program_id(1)
    hw = HEAD_W
    heads = k_ref.shape[-1] // hw
    cols = n_maps * tq
    hs = lambda h: slice(h * hw, (h + 1) * hw)
    ws = []
    for h in range(heads):
        qt = qt_ref[hs(h), :]
        if biased:
            half = lax.broadcasted_iota(jnp.int32, (hw, 1), 0) // DIFF_DH
            zero = jnp.zeros_like(qt)
            qt = jnp.concatenate([jnp.where(half == 0, qt, zero), jnp.where(half == 1, qt, zero)], axis=1)
        ws.append(qt)
    if biased:
        pq = pq_ref[...]
        ftabs = [jnp.broadcast_to(ftab_ref[h] * LOG2E, (tk, REL_TABLE)) for h in range(heads)]
    m_sc[...] = jnp.full(m_sc.shape, NEG_INF, F32)
    l_sc[...] = jnp.zeros_like(l_sc)
    acc_sc[...] = jnp.zeros_like(acc_sc)

    def near_biases(off, q_lo=0):
        idx = jnp.clip(pq[:, q_lo:] - pk_ref[pl.ds(off, tk), :], 0, REL_TABLE - 1)
        tiles = [jnp.concatenate([jnp.take_along_axis(ftabs[h], idx[:, c:c + LANES], axis=1)
                                  for c in range(0, tq - q_lo, LANES)], axis=1) for h in range(heads)]

        def bias(h, cs):
            base = (cs.start // tq) * tq + q_lo
            return tiles[h][:, cs.start - base:cs.stop - base]
        return bias

    s_scs = (s0_sc, s1_sc)

    all_cols = tuple(slice(mi * tq, (mi + 1) * tq) for mi in range(n_maps))
    late_cols = tuple(slice(mi * tq + tk, (mi + 1) * tq) for mi in range(n_maps))

    def scores(off, slot, col_ranges=all_cols):
        for h in range(heads):
            for cs in col_ranges:
                s_scs[slot][h, :, cs] = _dot(k_ref[pl.ds(off, tk), hs(h)], ws[h][:, cs])

    def consume(off, slot, bias, mask, const=None, col_ranges=all_cols):
        for h in range(heads):
            for cs in col_ranges:
                s = s_scs[slot][h, :, cs]
                if bias is not None:
                    s = s + bias(h, cs)
                if mask is not None:
                    s = jnp.where(mask[:, cs], s, NEG_INF)
                m = m_sc[h, :, cs]
                m_tile = jnp.max(s, axis=0, keepdims=True)
                if const is not None:
                    m_tile = m_tile + const[h]
                m_new = jnp.maximum(m, m_tile)
                alpha = jnp.exp2(m - m_new)
                p = jnp.exp2(s - (m_new if const is None else m_new - const[h]))
                l_sc[h, :, cs] = alpha * l_sc[h, :, cs] + jnp.sum(p, axis=0, keepdims=True)
                acc_sc[h, :, cs] = (alpha * acc_sc[h, :, cs]
                                    + _dot(vt_ref[hs(h), pl.ds(off, tk)], p.astype(BF16)))
                m_sc[h, :, cs] = m_new

    assert tq == 2 * tk
    scores(0, 0)

    def body(jj, carry):
        off0 = pl.multiple_of(jj * tq, tq)
        off1 = pl.multiple_of(off0 + tk, tk)
        off2 = pl.multiple_of(off0 + tq, tq)

        def pair(bias_fn, const):
            scores(off1, 1)
            consume(off0, 0, bias_fn(off0), None, const)
            scores(off2, 0)
            consume(off1, 1, bias_fn(off1), None, const)

        no_bias = lambda off: None
        if biased:
            nearest = jnp.maximum(pmax_ref[bi, 2 * jj], pmax_ref[bi, 2 * jj + 1])
            far = pmin_ref[bi, qi] - nearest >= REL_TABLE - 1

            @pl.when(far)
            def _():
                pair(no_bias, [ftab_ref[h][:, REL_TABLE - 1:REL_TABLE] * LOG2E for h in range(heads)])

            @pl.when(jnp.logical_not(far))
            def _():
                pair(near_biases, None)
        else:
            pair(no_bias, None)
        return carry

    lax.fori_loop(0, qi, body, 0)
    for d in range(2):
        off = pl.multiple_of(qi * tq + d * tk, tk)
        if d == 0:
            scores(pl.multiple_of(off + tk, tk), 1, late_cols)
        key = lax.broadcasted_iota(jnp.int32, (tk, cols), 0) + d * tk
        qry = lax.broadcasted_iota(jnp.int32, (tk, cols), 1) % tq
        consume(off, d, near_biases(off, d * tk) if biased else None, key <= qry,
                col_ranges=all_cols if d == 0 else late_cols)

    if biased:
        lp = lam_ref[...]
        lam = (jnp.exp(jnp.sum(lp[0:1] * lp[1:2], axis=-1, keepdims=True))
               - jnp.exp(jnp.sum(lp[2:3] * lp[3:4], axis=-1, keepdims=True)) + lambda_init)
    for h in range(heads):
        o = acc_sc[h] / l_sc[h]
        if biased:
            o = (o[:, :tq] - lam * o[:, tq:]).T
            o_ref[:, hs(h)] = (_rms(o, sg_ref[...]) * (1.0 - lambda_init)).astype(o_ref.dtype)
        else:
            o_ref[:, hs(h)] = o.T.astype(o_ref.dtype)


def _flash_scratch(heads, tk, cols):
    return [pltpu.VMEM((heads, 1, cols), F32), pltpu.VMEM((heads, 1, cols), F32),
            pltpu.VMEM((heads, HEAD_W, cols), F32),
            pltpu.VMEM((heads, tk, cols), F32), pltpu.VMEM((heads, tk, cols), F32)]


def _mla_attention(qt, k, vt, *, tq=ATTN_TQ, tk=ATTN_TK):
    b, s, w = k.shape
    return pl.pallas_call(
        functools.partial(_flash_kernel, tq=tq, tk=tk, n_maps=1, lambda_init=None),
        grid=(b, s // tq),
        in_specs=[pl.BlockSpec((None, w, tq), lambda bi, i: (bi, 0, i)),
                  pl.BlockSpec((None, s, w), lambda bi, i: (bi, 0, 0), pipeline_mode=pl.Buffered(1)),
                  pl.BlockSpec((None, w, s), lambda bi, i: (bi, 0, 0), pipeline_mode=pl.Buffered(1))],
        out_specs=pl.BlockSpec((None, tq, w), lambda bi, i: (bi, i, 0)),
        out_shape=jax.ShapeDtypeStruct((b, s, w), BF16),
        scratch_shapes=_flash_scratch(w // HEAD_W, tk, tq),
        compiler_params=_params("parallel", "arbitrary"),
        name="mla_attention",
    )(qt, k, vt)


def _diff_attention(qt, k, vt, pos, ftab, lam_params, sub_g, lambda_init, *, tq=ATTN_TQ, tk=ATTN_TK):
    b, s, w = k.shape
    pmin = jnp.min(pos.reshape(b, s // tq, tq), axis=-1)
    pmax = jnp.max(pos.reshape(b, s // tk, tk), axis=-1)
    pos_col = pos.reshape(b, s, 1)
    pos_row = pos.reshape(b, 1, s)
    grid_spec = pltpu.PrefetchScalarGridSpec(
        num_scalar_prefetch=2,
        grid=(b, s // tq),
        in_specs=[pl.BlockSpec((None, w, tq), lambda bi, i, *_: (bi, 0, i)),
                  pl.BlockSpec((None, s, w), lambda bi, i, *_: (bi, 0, 0), pipeline_mode=pl.Buffered(1)),
                  pl.BlockSpec((None, w, s), lambda bi, i, *_: (bi, 0, 0), pipeline_mode=pl.Buffered(1)),
                  pl.BlockSpec((None, 1, tq), lambda bi, i, *_: (bi, 0, i)),
                  pl.BlockSpec((None, s, 1), lambda bi, i, *_: (bi, 0, 0), pipeline_mode=pl.Buffered(1)),
                  pl.BlockSpec(ftab.shape, lambda bi, i, *_: (0, 0, 0)),
                  pl.BlockSpec(lam_params.shape, lambda bi, i, *_: (0, 0)),
                  pl.BlockSpec(sub_g.shape, lambda bi, i, *_: (0, 0))],
        out_specs=pl.BlockSpec((None, tq, w), lambda bi, i, *_: (bi, i, 0)),
        scratch_shapes=_flash_scratch(w // HEAD_W, tk, 2 * tq),
    )
    return pl.pallas_call(
        functools.partial(_flash_kernel, tq=tq, tk=tk, n_maps=2, lambda_init=lambda_init),
        grid_spec=grid_spec,
        out_shape=jax.ShapeDtypeStruct((b, s, w), BF16),
        compiler_params=_params("parallel", "arbitrary"),
        name="diff_attention",
    )(pmin, pmax, qt, k, vt, pos_row, pos_col, ftab, lam_params, sub_g)


def _merge_kernel(h_ref, g_ref, y0_ref, y1_ref, y2_ref, y3_ref, wg_ref, wb_ref, wo_ref, o_ref, *, col_chunk):
    h = h_ref[...]
    d = h.shape[-1]
    u = _rms(h, g_ref[...]).astype(BF16)
    ys = (y0_ref[...], y1_ref[...], y2_ref[...], y3_ref[...])
    parts = []
    for c0 in range(0, d, col_chunk):
        acc = None
        for i, y in enumerate(ys):
            gate = jax.nn.sigmoid(_dot(u, wg_ref[:, i * d + c0:i * d + c0 + col_chunk]))
            term = gate * _dot(y, wb_ref[i, :, c0:c0 + col_chunk])
            acc = term if acc is None else acc + term
        parts.append(acc.astype(BF16))
    merged = jnp.concatenate(parts, axis=-1)
    o_ref[...] = h + _dot(merged, wo_ref[...])


def _merge(h, g, ys, w_gate, w_branch, w_out, *, tm=1024, col_chunk=512):
    m, d = h.shape
    rows = lambda width: pl.BlockSpec((tm, width), lambda i: (i, 0))
    return pl.pallas_call(
        functools.partial(_merge_kernel, col_chunk=col_chunk),
        grid=(m // tm,),
        in_specs=[rows(d), _resident((1, d))] + [rows(BRANCH_W)] * N_BRANCH
                 + [_resident(w_gate), _resident(w_branch), _resident(w_out)],
        out_specs=rows(d),
        out_shape=jax.ShapeDtypeStruct((m, d), F32),
        compiler_params=_params("parallel"),
        name="merge",
    )(h, g, *ys, _operand(w_gate), _operand(w_branch), _operand(w_out))


def _xattn_kernel(h_ref, g_ref, wq_ref, k_ref, v_ref, wo_ref, o_ref):
    h = h_ref[...]
    d = h.shape[-1]
    dh = d // XA_HEADS
    u = _rms(h, g_ref[...]).astype(BF16)
    q = (_dot(u, wq_ref[...]) * (dh ** -0.5 * LOG2E)).astype(BF16)
    out = h
    for hd in range(XA_HEADS):
        sl = slice(hd * dh, (hd + 1) * dh)
        s = _dot_nt(q[:, sl], k_ref[:, sl])
        p = jnp.exp2(s - jnp.max(s, axis=-1, keepdims=True))
        inv = 1.0 / jnp.sum(p, axis=-1, keepdims=True)
        o = _dot(p.astype(BF16), v_ref[:, sl]) * inv
        out = out + _dot(o.astype(BF16), wo_ref[sl, :])
    o_ref[...] = out


def _cross_attention(h, g, wq, k, v, wo, batch, seq, *, tm=1024):
    m, d = h.shape
    mem_len = k.shape[0] // batch
    per = seq // tm
    return pl.pallas_call(
        _xattn_kernel,
        grid=(m // tm,),
        in_specs=[pl.BlockSpec((tm, d), lambda i: (i, 0)), _resident((1, d)), _resident(wq),
                  pl.BlockSpec((mem_len, d), lambda i: (i // per, 0)),
                  pl.BlockSpec((mem_len, d), lambda i: (i // per, 0)),
                  _resident(wo)],
        out_specs=pl.BlockSpec((tm, d), lambda i: (i, 0)),
        out_shape=jax.ShapeDtypeStruct((m, d), F32),
        compiler_params=_params("parallel"),
        name="cross_attention",
    )(h, g, _operand(wq), k, v, _operand(wo))


def _split_cols(w, sizes):
    out, start = [], 0
    for s in sizes:
        out.append(w[..., start:start + s])
        start += s
    return out


def _pad_cols(w, left, total):
    return jnp.pad(w, ((0, 0),) * (w.ndim - 1) + ((left, total - left - w.shape[-1]),))


def _relaid_weights(w_in, w_uq, w_ukv):
    (rq, rk, rv, rg, sz, sxbc, sdt, cq, ckv, kr, dq, dk, dv, gl) = _split_cols(w_in.astype(BF16), IN_SIZES)
    sdt_pad = _pad_cols(sdt, 0, SSD_DT_W)
    kr_pad = _pad_cols(kr, MLA_NOPE, MLA_PAD)
    w_mix = jnp.concatenate([rq, rk, rv, rg, sxbc, sz, sdt_pad, cq, ckv, kr_pad, dq, dk, dv], axis=-1)
    depth, d, _ = w_uq.shape
    uq = w_uq.astype(BF16).reshape(depth, d, MLA_HEADS, MLA_QH)
    uq = _pad_cols(uq, 0, MLA_PAD).reshape(depth, d, MLA_HEADS * MLA_PAD)
    r = w_ukv.shape[1]
    ukv = w_ukv.astype(BF16).reshape(depth, r, MLA_HEADS, MLA_NOPE + MLA_V)
    k_nope = _pad_cols(ukv[..., :MLA_NOPE], 0, MLA_PAD)
    ukv_p = jnp.concatenate([k_nope.reshape(depth, r, -1), ukv[..., MLA_NOPE:].reshape(depth, r, -1)], axis=-1)
    return w_mix, gl, uq, ukv_p


def _t5_bucket_const():
    n = jnp.arange(REL_TABLE)
    exact = REL_BUCKETS // 2
    nf = jnp.maximum(n, 1).astype(F32)
    large = exact + (jnp.log(nf / exact) / math.log(REL_MAX_DIST / exact) * (REL_BUCKETS - exact)).astype(jnp.int32)
    return jnp.where(n < exact, n, jnp.minimum(large, REL_BUCKETS - 1))


MIX_OUTS = ((3 * BRANCH_W, F32, 1.0, False),
            (SSD_P_WIDTH, F32, 1.0, False),
            (MLA_Q_RANK + MLA_KV_RANK + MLA_PAD, F32, 1.0, False),
            (BRANCH_W, BF16, DIFF_DH ** -0.5 * LOG2E, True),
            (BRANCH_W, BF16, 1.0, False),
            (BRANCH_W, BF16, 1.0, True))


def kernel(x, mem, positions, ffn1_norm, ffn1_w_gate, ffn1_w_up, ffn1_w_down, mix_norm, w_in, ssm_conv_w,
           ssm_conv_b, ssm_dt_bias, ssm_a_log, ssm_d, ssm_norm, mla_q_norm, mla_kv_norm, mla_w_uq, mla_w_ukv,
           diff_lambda, diff_norm, rel_bias, w_branch, w_out, xa_norm, mem_norm, xa_wq, xa_wk, xa_wv, xa_wo,
           ffn2_norm, ffn2_w_gate, ffn2_w_up, ffn2_w_down, final_norm):
    batch, seq, d = x.shape
    depth = w_in.shape[0]
    m = batch * seq
    h = x.reshape(m, d)
    mem2 = mem.reshape(batch * mem.shape[1], d)
    row = lambda a: a.reshape(1, -1)

    pos_col = positions.reshape(m, 1)
    ret_cos, ret_sin = _rope_tables(pos_col, *_rope_rows(RET_DK, RET_DK // 2, 0))
    mla_cos, mla_sin = _rope_tables(pos_col, *_rope_rows(MLA_PAD, MLA_ROPE // 2, MLA_NOPE))
    ftab = rel_bias[_t5_bucket_const()].T.reshape(DIFF_HEADS, 1, REL_TABLE)

    w_mix, w_gate, w_uq_p, w_ukv_p = _relaid_weights(_cast_bf16(w_in), mla_w_uq, mla_w_ukv)
    ffn1 = [_cast_bf16(w) for w in (ffn1_w_gate, ffn1_w_up, ffn1_w_down)]
    ffn2 = [_cast_bf16(w) for w in (ffn2_w_gate, ffn2_w_up, ffn2_w_down)]
    w_branch_b = _cast_bf16(w_branch.reshape(depth, N_BRANCH * BRANCH_W, d)).reshape(w_branch.shape)
    w_out_b, xa_wq_b, xa_wo_b = _cast_bf16(w_out), _cast_bf16(xa_wq), _cast_bf16(xa_wo)
    w_kv = jnp.concatenate([_cast_bf16(xa_wk), _cast_bf16(xa_wv)], axis=-1)

    for l in range(depth):
        at = lambda stack: _LayerOf(stack, l)
        h = _ffn(h, row(ffn1_norm[l]), *(at(w) for w in ffn1))
        ret_p, ssm_p, mla_p, dqt, dk, dvt = _norm_proj(h, row(mix_norm[l]), at(w_mix), MIX_OUTS, seq,
                                                       name="mix_proj")
        y_ret = _retention(ret_p.reshape(batch, seq, -1), ret_cos.reshape(batch, seq, -1),
                           ret_sin.reshape(batch, seq, -1)).reshape(m, BRANCH_W)
        y_ssm = _ssd(ssm_p, ssm_conv_w[l], row(ssm_conv_b[l]),
                     _pad_cols(row(ssm_dt_bias[l]), 0, SSD_DT_W), _pad_cols(row(ssm_a_log[l]), 0, SSD_DT_W),
                     row(jnp.repeat(ssm_d[l], SSM_HEADDIM)), row(ssm_norm[l]), batch, seq)
        mqt, mk, mvt = _mla_prep(mla_p, mla_cos, mla_sin, row(mla_q_norm[l]), row(mla_kv_norm[l]), at(w_uq_p),
                                 at(w_ukv_p), seq)
        y_mla = _mla_attention(mqt, mk.reshape(batch, seq, -1), mvt).reshape(m, BRANCH_W)
        lambda_init = 0.8 - 0.6 * math.exp(-0.3 * l)
        y_diff = _diff_attention(dqt, dk.reshape(batch, seq, -1), dvt, positions, ftab, diff_lambda[l],
                                 row(diff_norm[l]), lambda_init).reshape(m, BRANCH_W)
        h = _merge(h, row(mix_norm[l]), (y_ret, y_ssm, y_mla, y_diff), at(w_gate), at(w_branch_b), at(w_out_b))
        xk, xv = _norm_proj(mem2, row(mem_norm[l]), at(w_kv), [(d, BF16, 1.0, False)] * 2, mem.shape[1],
                            tm=mem.shape[1], name="mem_kv")
        h = _cross_attention(h, row(xa_norm[l]), at(xa_wq_b), xk, xv, at(xa_wo_b), batch, seq)
        h = _ffn(h, row(ffn2_norm[l]), *(at(w) for w in ffn2),
                 final_g=row(final_norm) if l == depth - 1 else None)
    return h.reshape(batch, seq, d)
```

```python
import functools
import math
from typing import NamedTuple

import jax
import jax.numpy as jnp
from jax import lax
from jax.experimental import pallas as pl
from jax.experimental.pallas import tpu as pltpu

F32 = jnp.float32
BF16 = jnp.bfloat16

EPS = 1e-6
NEG_INF = -1e30
LOG2E = math.log2(math.e)
ROPE_BASE = 10000.0
CHUNK = 128
BRANCH_W = 512
N_BRANCH = 4

RET_HEADS = 4
RET_DK = 64
RET_DV = 128
RET_QK = RET_HEADS * RET_DK

SSM_HEADDIM = 64
SSM_INNER = 512
SSM_HEADS = 8
SSM_GROUPS = 2
SSM_STATE = 128
SSM_CONV = 4
SSM_XBC = SSM_INNER + 2 * SSM_GROUPS * SSM_STATE
SSM_TAIL = 8
SSD_DT_W = 128
SSD_P_WIDTH = SSM_XBC + SSM_INNER + SSD_DT_W

MLA_HEADS = 4
MLA_Q_RANK = 256
MLA_KV_RANK = 128
MLA_NOPE = 64
MLA_ROPE = 32
MLA_V = 128
MLA_QH = MLA_NOPE + MLA_ROPE
MLA_PAD = 128
HEAD_W = 128
ATTN_TQ = 512
ATTN_TK = 256

DIFF_HEADS = 4
DIFF_DH = 64

REL_BUCKETS = 32
REL_MAX_DIST = 128
REL_TABLE = 128

XA_HEADS = 4

LANES = 128
VMEM_LIMIT = 56 * 1024 * 1024

IN_SIZES = (RET_QK, RET_QK, BRANCH_W, BRANCH_W,
            SSM_INNER, SSM_XBC, SSM_HEADS,
            MLA_Q_RANK, MLA_KV_RANK, MLA_ROPE,
            BRANCH_W, BRANCH_W, BRANCH_W,
            N_BRANCH * 1024)


def _params(*sem):
    return pltpu.CompilerParams(dimension_semantics=sem, vmem_limit_bytes=VMEM_LIMIT)


class _LayerOf(NamedTuple):
    stack: jax.Array
    layer: int

    @property
    def shape(self):
        return self.stack.shape[1:]


def _operand(x):
    return x.stack if isinstance(x, _LayerOf) else x


def _resident(x):
    if isinstance(x, _LayerOf):
        shape, lead, first = x.shape, (None,), (x.layer,)
    else:
        shape, lead, first = (x if isinstance(x, tuple) else x.shape), (), ()
    nd = len(shape)
    return pl.BlockSpec(lead + tuple(shape), lambda *_: first + (0,) * nd, pipeline_mode=pl.Buffered(1))


def _cast_kernel(x_ref, o_ref):
    o_ref[...] = x_ref[...].astype(o_ref.dtype)


def _cast_bf16(w, *, row_blocks=4):
    depth, r, c = w.shape
    tr = r // row_blocks
    spec = pl.BlockSpec((None, tr, c), lambda l, i: (l, i, 0))
    return pl.pallas_call(
        _cast_kernel,
        grid=(depth, row_blocks),
        in_specs=[spec],
        out_specs=spec,
        out_shape=jax.ShapeDtypeStruct(w.shape, BF16),
        compiler_params=_params("parallel", "parallel"),
        name="cast_bf16",
    )(w)


def _rms(x, g):
    return x * lax.rsqrt(jnp.mean(x * x, axis=-1, keepdims=True) + EPS) * g


def _silu(x):
    return x * jax.nn.sigmoid(x)


def _dot(a, b):
    return jnp.dot(a, b, preferred_element_type=F32)


def _dot_nt(a, b):
    return lax.dot_general(a, b, (((1,), (1,)), ((), ())), preferred_element_type=F32)


def _dot_tn(a, b):
    return lax.dot_general(a, b, (((0,), (0,)), ((), ())), preferred_element_type=F32)


def _ffn_kernel(h_ref, g_ref, wg_ref, wu_ref, wd_ref, *rest, ff_chunk, final):
    if final:
        gf_ref, o_ref = rest
    else:
        (o_ref,) = rest
    h = h_ref[...]
    xn = _rms(h, g_ref[...]).astype(BF16)
    acc = jnp.zeros(h.shape, F32)
    d_ff = wg_ref.shape[1]
    for c0 in range(0, d_ff, ff_chunk):
        a = _dot(xn, wg_ref[:, c0:c0 + ff_chunk])
        b = _dot(xn, wu_ref[:, c0:c0 + ff_chunk])
        acc = acc + _dot((_silu(a) * b).astype(BF16), wd_ref[c0:c0 + ff_chunk, :])
    out = h + 0.5 * acc
    if final:
        out = _rms(out, gf_ref[...])
    o_ref[...] = out


def _ffn(h, g, wg, wu, wd, final_g=None, *, tm=512, ff_chunk=256):
    m, d = h.shape
    d_ff = wg.shape[1]
    final = final_g is not None
    in_specs = [pl.BlockSpec((tm, d), lambda i: (i, 0)), _resident((1, d)),
                _resident(wg), _resident(wu), _resident(wd)]
    args = [h, g, _operand(wg), _operand(wu), _operand(wd)]
    if final:
        in_specs.append(_resident((1, d)))
        args.append(final_g)
    return pl.pallas_call(
        functools.partial(_ffn_kernel, ff_chunk=ff_chunk, final=final),
        grid=(m // tm,),
        in_specs=in_specs,
        out_specs=pl.BlockSpec((tm, d), lambda i: (i, 0)),
        out_shape=jax.ShapeDtypeStruct((m, d), F32),
        compiler_params=_params("parallel"),
        name="ffn",
    )(*args)


def _norm_proj_kernel(x_ref, g_ref, w_ref, *o_refs, outs, col_chunk):
    xn = _rms(x_ref[...], g_ref[...]).astype(BF16)
    off = 0
    for o_ref, (width, _, scale, transposed) in zip(o_refs, outs):
        for c0 in range(0, width, col_chunk):
            c1 = min(c0 + col_chunk, width)
            y = _dot(xn, w_ref[:, off + c0:off + c1])
            if scale != 1.0:
                y = y * scale
            if transposed:
                o_ref[c0:c1, :] = y.astype(o_ref.dtype).T
            else:
                o_ref[:, c0:c1] = y.astype(o_ref.dtype)
        off += width


def _norm_proj(x, g, w, outs, seq, *, tm=512, col_chunk=512, name="norm_proj"):
    m, d = x.shape
    assert sum(o[0] for o in outs) == w.shape[1]
    per = seq // tm
    out_specs, out_shape = [], []
    for width, dt, _, transposed in outs:
        if transposed:
            out_specs.append(pl.BlockSpec((None, width, tm), lambda i: (i // per, 0, i % per)))
            out_shape.append(jax.ShapeDtypeStruct((m // seq, width, seq), dt))
        else:
            out_specs.append(pl.BlockSpec((tm, width), lambda i: (i, 0)))
            out_shape.append(jax.ShapeDtypeStruct((m, width), dt))
    return pl.pallas_call(
        functools.partial(_norm_proj_kernel, outs=tuple(outs), col_chunk=col_chunk),
        grid=(m // tm,),
        in_specs=[pl.BlockSpec((tm, d), lambda i: (i, 0)), _resident((1, d)), _resident(w)],
        out_specs=out_specs,
        out_shape=out_shape,
        compiler_params=_params("parallel"),
        name=name,
    )(x, g, _operand(w))


def _rope_table_kernel(pos_ref, inv_ref, sign_ref, cos_ref, sin_ref):
    ang = pos_ref[...].astype(F32) * inv_ref[...]
    cos_ref[...] = jnp.cos(ang)
    sin_ref[...] = jnp.sin(ang) * sign_ref[...]


def _rope_tables(pos_col, inv_row, sign_row, *, tm=1024):
    m = pos_col.shape[0]
    return pl.pallas_call(
        _rope_table_kernel,
        grid=(m // tm,),
        in_specs=[pl.BlockSpec((tm, 1), lambda i: (i, 0)), _resident((1, LANES)), _resident((1, LANES))],
        out_specs=[pl.BlockSpec((tm, LANES), lambda i: (i, 0))] * 2,
        out_shape=[jax.ShapeDtypeStruct((m, LANES), F32)] * 2,
        compiler_params=_params("parallel"),
        name="rope_tables",
    )(pos_col, inv_row, sign_row)


def _rope_rows(period, half, start):
    lane = jnp.arange(LANES) % period
    inside = (lane >= start) & (lane < start + 2 * half)
    idx = (lane - start) % half
    inv = jnp.exp(-math.log(ROPE_BASE) * idx.astype(F32) / half)
    inv = jnp.where(inside, inv, 0.0)
    sign = jnp.where(inside, jnp.where(lane < start + half, -1.0, 1.0), 0.0)
    return inv.reshape(1, LANES).astype(F32), sign.reshape(1, LANES).astype(F32)


def _apply_rope(x, cos, sin_signed, period, half, start):
    width = x.shape[-1]
    reps = width // LANES
    if reps > 1:
        cos = jnp.concatenate([cos] * reps, axis=-1)
        sin_signed = jnp.concatenate([sin_signed] * reps, axis=-1)
    lane = lax.broadcasted_iota(jnp.int32, (1, width), 1) % period
    first = lane < start + half
    partner = jnp.where(first, pltpu.roll(x, width - half, 1), pltpu.roll(x, half, 1))
    return x * cos + partner * sin_signed


def _ret_kernel(q_ref, k_ref, v_ref, g_ref, cos_ref, sin_ref, intra_ref, qdec_ref, kdec_ref,
                sdec_ref, bmask_ref, o_ref, state_sc):
    @pl.when(pl.program_id(0) == 0)
    def _():
        state_sc[...] = jnp.zeros_like(state_sc)

    head = lax.broadcasted_iota(jnp.int32, (1, RET_QK), 1) // RET_DK
    for b in range(q_ref.shape[0]):
        cos, sin = cos_ref[b], sin_ref[b]
        q = _apply_rope(q_ref[b], cos, sin, RET_DK, RET_DK // 2, 0) * (RET_DK ** -0.5)
        k = _apply_rope(k_ref[b], cos, sin, RET_DK, RET_DK // 2, 0)
        vb = v_ref[b].astype(BF16)
        kb = k.astype(BF16)
        state = state_sc[b]
        cross = _dot((q * qdec_ref[...]).astype(BF16), state.astype(BF16))
        outs = []
        for h in range(RET_HEADS):
            qh = jnp.where(head == h, q, 0.0).astype(BF16)
            att = _dot_nt(qh, kb) * intra_ref[h]
            sl = slice(h * RET_DV, (h + 1) * RET_DV)
            o = _dot(att.astype(BF16), vb[:, sl]) + cross[:, sl]
            mu = jnp.mean(o, axis=-1, keepdims=True)
            oc = o - mu
            outs.append(oc * lax.rsqrt(jnp.mean(oc * oc, axis=-1, keepdims=True) + EPS))
        o_ref[b] = (jnp.concatenate(outs, axis=-1) * _silu(g_ref[b])).astype(o_ref.dtype)
        kv = _dot_tn((k * kdec_ref[...]).astype(BF16), vb)
        state_sc[b] = state * sdec_ref[...] + kv * bmask_ref[...]


def _retention_consts():
    c = CHUNK
    log_g = jnp.log1p(-jnp.exp2(-5.0 - jnp.arange(RET_HEADS, dtype=F32)))
    i = jnp.arange(c, dtype=F32)
    rel = i[:, None] - i[None, :]
    intra = jnp.where(rel >= 0, jnp.exp(log_g[:, None, None] * jnp.maximum(rel, 0.0)), 0.0)
    lg_lane = jnp.repeat(log_g, RET_DK)
    qdec = jnp.exp(lg_lane[None, :] * (i[:, None] + 1.0))
    kdec = jnp.exp(lg_lane[None, :] * (c - 1.0 - i[:, None]))
    row_h = jnp.repeat(jnp.arange(RET_HEADS), RET_DK)
    col_h = jnp.repeat(jnp.arange(RET_HEADS), RET_DV)
    bmask = (row_h[:, None] == col_h[None, :]).astype(F32)
    sdec = bmask * jnp.exp(lg_lane * c)[:, None]
    return intra, qdec, kdec, sdec, bmask


def _retention(ret_p, cos, sin):
    c = CHUNK
    batch, seq, _ = ret_p.shape
    intra, qdec, kdec, sdec, bmask = _retention_consts()
    blk = lambda width, col: pl.BlockSpec((batch, c, width), lambda j: (0, j, col))
    return pl.pallas_call(
        _ret_kernel,
        grid=(seq // c,),
        in_specs=[blk(RET_QK, 0), blk(RET_QK, 1), blk(BRANCH_W, 1), blk(BRANCH_W, 2), blk(LANES, 0), blk(LANES, 0),
                  _resident(intra.shape), _resident(qdec.shape), _resident(kdec.shape),
                  _resident(sdec.shape), _resident(bmask.shape)],
        out_specs=blk(BRANCH_W, 0),
        out_shape=jax.ShapeDtypeStruct((batch, seq, BRANCH_W), BF16),
        scratch_shapes=[pltpu.VMEM((batch, RET_QK, BRANCH_W), F32)],
        compiler_params=_params("arbitrary"),
        name="retention",
    )(ret_p, ret_p, ret_p, ret_p, cos, sin, intra, qdec, kdec, sdec, bmask)


def _split3(x):
    hi = x.astype(BF16)
    r = x - hi.astype(F32)
    mid = r.astype(BF16)
    lo = (r - mid.astype(F32)).astype(BF16)
    return hi, mid, lo


def _dot_split3(a, b, *, left):
    if left:
        return sum(_dot(piece, b) for piece in _split3(a))
    return sum(_dot(a, piece) for piece in _split3(b))


def _ssd_kernel(xbc_ref, z_ref, dt_ref, cw_ref, cb_ref, dtb_ref, alog_ref, dskip_ref, ng_ref, tri_ref,
                expand_ref, o_ref, xe_sc, state_sc):
    L = CHUNK
    T = SSM_TAIL
    GW = SSM_INNER // SSM_GROUPS

    @pl.when(pl.program_id(1) == 0)
    def _():
        xe_sc[0:T, :] = jnp.zeros((T, SSM_XBC), F32)
        state_sc[...] = jnp.zeros_like(state_sc)

    x_raw = xbc_ref[...]
    xe_sc[T:T + L, :] = x_raw
    conv = cb_ref[...]
    for kk in range(SSM_CONV):
        s0 = T - (SSM_CONV - 1) + kk
        conv = conv + cw_ref[kk:kk + 1, :] * xe_sc[s0:s0 + L, :]
    xe_sc[0:T, :] = x_raw[L - T:L, :]
    xa = _silu(conv)
    xs = xa[:, :SSM_INNER]
    bm = xa[:, SSM_INNER:SSM_INNER + SSM_GROUPS * SSM_STATE]
    cm = xa[:, SSM_INNER + SSM_GROUPS * SSM_STATE:]

    dtr = dt_ref[...] + dtb_ref[...]
    dt_h = jnp.maximum(dtr, 0.0) + jnp.log1p(jnp.exp(-jnp.abs(dtr)))
    da = dt_h * (-jnp.exp(alog_ref[...]))
    cs_h = _dot_split3(tri_ref[...], da, left=False)
    cs_h_last = cs_h[L - 1:L, :]
    cs_t = cs_h.T
    expand = expand_ref[...]
    dt = _dot_split3(dt_h, expand, left=True)
    ecs = _dot_split3(jnp.exp(cs_h), expand, left=True)
    eds = _dot_split3(jnp.exp(cs_h_last - cs_h), expand, left=True)
    xdt = xs * dt
    xdec = (xdt * eds).astype(BF16)
    state = state_sc[...]
    li = lax.broadcasted_iota(jnp.int32, (L, L), 0)
    si = lax.broadcasted_iota(jnp.int32, (L, L), 1)
    causal = li >= si
    lane_half = lax.broadcasted_iota(jnp.int32, (1, LANES), 1) // SSM_HEADDIM
    ys = []
    new_states = []
    for g in range(SSM_GROUPS):
        bg = bm[:, g * SSM_STATE:(g + 1) * SSM_STATE].astype(BF16)
        cg = cm[:, g * SSM_STATE:(g + 1) * SSM_STATE].astype(BF16)
        cb = _dot_nt(cg, bg)
        gsl = slice(g * GW, (g + 1) * GW)
        y_off = _dot(cg, state[:, gsl].astype(BF16)) * ecs[:, gsl]
        for p in range(GW // LANES):
            sl = slice(g * GW + p * LANES, g * GW + (p + 1) * LANES)
            xdt_blk = xdt[:, sl]
            acc = y_off[:, p * LANES:(p + 1) * LANES]
            for hh in range(LANES // SSM_HEADDIM):
                head = (g * GW + p * LANES) // SSM_HEADDIM + hh
                col = cs_h[:, head:head + 1]
                rowv = cs_t[head:head + 1, :]
                dec = jnp.where(causal, jnp.exp(jnp.minimum(col - rowv, 0.0)), 0.0)
                xm = jnp.where(lane_half == hh, xdt_blk, 0.0).astype(BF16)
                acc = acc + _dot((cb * dec).astype(BF16), xm)
            ys.append(acc)
        new_states.append(_dot_tn(bg, xdec[:, gsl]))
    y = jnp.concatenate(ys, axis=-1) + xs * dskip_ref[...]
    y = y * _silu(z_ref[...])
    o_ref[...] = _rms(y, ng_ref[...]).astype(o_ref.dtype)
    state_sc[...] = state * ecs[L - 1:L, :] + jnp.concatenate(new_states, axis=-1)


def _ssd(ssm_p, conv_w, conv_b, dt_bias_pad, a_log_pad, d_full, norm_g, batch, seq):
    L = CHUNK
    n = seq // L
    tri = (jnp.arange(L)[:, None] >= jnp.arange(L)[None, :]).astype(BF16)
    expand = (jnp.arange(SSD_DT_W)[:, None] == jnp.arange(SSM_INNER)[None, :] // SSM_HEADDIM).astype(BF16)
    row = lambda b, j: b * n + j
    return pl.pallas_call(
        _ssd_kernel,
        grid=(batch, n),
        in_specs=[pl.BlockSpec((L, SSM_XBC), lambda b, j: (row(b, j), 0)),
                  pl.BlockSpec((L, SSM_INNER), lambda b, j: (row(b, j), SSM_XBC // SSM_INNER)),
                  pl.BlockSpec((L, SSD_DT_W), lambda b, j: (row(b, j), (SSM_XBC + SSM_INNER) // SSD_DT_W)),
                  _resident(conv_w.shape), _resident(conv_b.shape), _resident(dt_bias_pad.shape),
                  _resident(a_log_pad.shape), _resident(d_full.shape), _resident(norm_g.shape),
                  _resident(tri.shape), _resident(expand.shape)],
        out_specs=pl.BlockSpec((L, SSM_INNER), lambda b, j: (row(b, j), 0)),
        out_shape=jax.ShapeDtypeStruct((batch * seq, SSM_INNER), BF16),
        scratch_shapes=[pltpu.VMEM((SSM_TAIL + L, SSM_XBC), F32),
                        pltpu.VMEM((SSM_STATE, SSM_INNER), F32)],
        compiler_params=_params("parallel", "arbitrary"),
        name="ssd",
    )(ssm_p, ssm_p, ssm_p, conv_w, conv_b, dt_bias_pad, a_log_pad, d_full, norm_g, tri, expand)


def _mla_prep_kernel(p_ref, cos_ref, sin_ref, qn_ref, kvn_ref, wuq_ref, wukv_ref, q_ref, k_ref, v_ref):
    cos, sin = cos_ref[...], sin_ref[...]
    kvw = MLA_HEADS * MLA_PAD
    cq = p_ref[:, :MLA_Q_RANK]
    ckv = p_ref[:, MLA_Q_RANK:MLA_Q_RANK + MLA_KV_RANK]
    kr0 = MLA_Q_RANK + MLA_KV_RANK
    kr = p_ref[:, kr0:kr0 + MLA_PAD]
    kr_partner = p_ref[:, kr0 + MLA_PAD:]
    qq = _dot(_rms(cq, qn_ref[...]).astype(BF16), wuq_ref[...])
    cos4 = jnp.concatenate([cos] * MLA_HEADS, axis=-1)
    sin4 = jnp.concatenate([sin] * MLA_HEADS, axis=-1)
    q = (qq[:, :kvw] * cos4 + qq[:, kvw:] * sin4) * (MLA_QH ** -0.5 * LOG2E)
    q_ref[...] = q.astype(q_ref.dtype).T
    kv = _dot(_rms(ckv, kvn_ref[...]).astype(BF16), wukv_ref[...])
    kpe = kr * cos + kr_partner * sin
    k_ref[...] = (kv[:, :kvw] + jnp.concatenate([kpe] * MLA_HEADS, axis=-1)).astype(k_ref.dtype)
    v_ref[...] = kv[:, kvw:].astype(v_ref.dtype).T


def _mla_prep(mla_p, cos, sin, q_norm, kv_norm, w_uq_p, w_ukv_p, seq, *, tm=512):
    m = mla_p.shape[0]
    w = MLA_HEADS * MLA_PAD
    per = seq // tm
    rows = lambda width: pl.BlockSpec((tm, width), lambda i: (i, 0))
    cols = pl.BlockSpec((None, w, tm), lambda i: (i // per, 0, i % per))
    t_shape = jax.ShapeDtypeStruct((m // seq, w, seq), BF16)
    return pl.pallas_call(
        _mla_prep_kernel,
        grid=(m // tm,),
        in_specs=[rows(mla_p.shape[1]), rows(LANES), rows(LANES), _resident(q_norm.shape),
                  _resident(kv_norm.shape), _resident(w_uq_p), _resident(w_ukv_p)],
        out_specs=[cols, rows(w), cols],
        out_shape=[t_shape, jax.ShapeDtypeStruct((m, w), BF16), t_shape],
        compiler_params=_params("parallel"),
        name="mla_prep",
    )(mla_p, cos, sin, q_norm, kv_norm, _operand(w_uq_p), _operand(w_ukv_p))


def _flash_kernel(*refs, tq, tk, n_maps, lambda_init):
    biased = n_maps == 2
    if biased:
        (pmin_ref, pmax_ref, qt_ref, k_ref, vt_ref, pq_ref, pk_ref, ftab_ref, lam_ref, sg_ref,
         o_ref, m_sc, l_sc, acc_sc, s0_sc, s1_sc) = refs
    else:
        qt_ref, k_ref, vt_ref, o_ref, m_sc, l_sc, acc_sc, s0_sc, s1_sc = refs
    bi = pl.program_id(0)
    qi = pl.program_id(1)
    hw = HEAD_W
    heads = k_ref.shape[-1] // hw
    cols = n_maps * tq
    hs = lambda h: slice(h * hw, (h + 1) * hw)
    ws = []
    for h in range(heads):
        qt = qt_ref[hs(h), :]
        if biased:
            half = lax.broadcasted_iota(jnp.int32, (hw, 1), 0) // DIFF_DH
            zero = jnp.zeros_like(qt)
            qt = jnp.concatenate([jnp.where(half == 0, qt, zero), jnp.where(half == 1, qt, zero)], axis=1)
        ws.append(qt)
    if biased:
        pq = pq_ref[...]
        ftabs = [jnp.broadcast_to(ftab_ref[h] * LOG2E, (tk, REL_TABLE)) for h in range(heads)]
    m_sc[...] = jnp.full(m_sc.shape, NEG_INF, F32)
    l_sc[...] = jnp.zeros_like(l_sc)
    acc_sc[...] = jnp.zeros_like(acc_sc)

    def near_biases(off, q_lo=0):
        idx = jnp.clip(pq[:, q_lo:] - pk_ref[pl.ds(off, tk), :], 0, REL_TABLE - 1)
        tiles = [jnp.concatenate([jnp.take_along_axis(ftabs[h], idx[:, c:c + LANES], axis=1)
                                  for c in range(0, tq - q_lo, LANES)], axis=1) for h in range(heads)]

        def bias(h, cs):
            base = (cs.start // tq) * tq + q_lo
            return tiles[h][:, cs.start - base:cs.stop - base]
        return bias

    s_scs = (s0_sc, s1_sc)

    all_cols = tuple(slice(mi * tq, (mi + 1) * tq) for mi in range(n_maps))
    late_cols = tuple(slice(mi * tq + tk, (mi + 1) * tq) for mi in range(n_maps))

    def scores(off, slot, col_ranges=all_cols):
        for h in range(heads):
            for cs in col_ranges:
                s_scs[slot][h, :, cs] = _dot(k_ref[pl.ds(off, tk), hs(h)], ws[h][:, cs])

    def consume(off, slot, bias, mask, const=None, col_ranges=all_cols):
        for h in range(heads):
            for cs in col_ranges:
                s = s_scs[slot][h, :, cs]
                if bias is not None:
                    s = s + bias(h, cs)
                if mask is not None:
                    s = jnp.where(mask[:, cs], s, NEG_INF)
                m = m_sc[h, :, cs]
                m_tile = jnp.max(s, axis=0, keepdims=True)
                if const is not None:
                    m_tile = m_tile + const[h]
                m_new = jnp.maximum(m, m_tile)
                alpha = jnp.exp2(m - m_new)
                p = jnp.exp2(s - (m_new if const is None else m_new - const[h]))
                l_sc[h, :, cs] = alpha * l_sc[h, :, cs] + jnp.sum(p, axis=0, keepdims=True)
                acc_sc[h, :, cs] = (alpha * acc_sc[h, :, cs]
                                    + _dot(vt_ref[hs(h), pl.ds(off, tk)], p.astype(BF16)))
                m_sc[h, :, cs] = m_new

    assert tq == 2 * tk
    scores(0, 0)

    def body(jj, carry):
        off0 = pl.multiple_of(jj * tq, tq)
        off1 = pl.multiple_of(off0 + tk, tk)
        off2 = pl.multiple_of(off0 + tq, tq)

        def pair(bias_fn, const):
            scores(off1, 1)
            consume(off0, 0, bias_fn(off0), None, const)
            scores(off2, 0)
            consume(off1, 1, bias_fn(off1), None, const)

        no_bias = lambda off: None
        if biased:
            nearest = jnp.maximum(pmax_ref[bi, 2 * jj], pmax_ref[bi, 2 * jj + 1])
            far = pmin_ref[bi, qi] - nearest >= REL_TABLE - 1

            @pl.when(far)
            def _():
                pair(no_bias, [ftab_ref[h][:, REL_TABLE - 1:REL_TABLE] * LOG2E for h in range(heads)])

            @pl.when(jnp.logical_not(far))
            def _():
                pair(near_biases, None)
        else:
            pair(no_bias, None)
        return carry

    lax.fori_loop(0, qi, body, 0)
    for d in range(2):
        off = pl.multiple_of(qi * tq + d * tk, tk)
        if d == 0:
            scores(pl.multiple_of(off + tk, tk), 1, late_cols)
        key = lax.broadcasted_iota(jnp.int32, (tk, cols), 0) + d * tk
        qry = lax.broadcasted_iota(jnp.int32, (tk, cols), 1) % tq
        consume(off, d, near_biases(off, d * tk) if biased else None, key <= qry,
                col_ranges=all_cols if d == 0 else late_cols)

    if biased:
        lp = lam_ref[...]
        lam = (jnp.exp(jnp.sum(lp[0:1] * lp[1:2], axis=-1, keepdims=True))
               - jnp.exp(jnp.sum(lp[2:3] * lp[3:4], axis=-1, keepdims=True)) + lambda_init)
    for h in range(heads):
        o = acc_sc[h] / l_sc[h]
        if biased:
            o = (o[:, :tq] - lam * o[:, tq:]).T
            o_ref[:, hs(h)] = (_rms(o, sg_ref[...]) * (1.0 - lambda_init)).astype(o_ref.dtype)
        else:
            o_ref[:, hs(h)] = o.T.astype(o_ref.dtype)


def _flash_scratch(heads, tk, cols):
    return [pltpu.VMEM((heads, 1, cols), F32), pltpu.VMEM((heads, 1, cols), F32),
            pltpu.VMEM((heads, HEAD_W, cols), F32),
            pltpu.VMEM((heads, tk, cols), F32), pltpu.VMEM((heads, tk, cols), F32)]


def _mla_attention(qt, k, vt, *, tq=ATTN_TQ, tk=ATTN_TK):
    b, s, w = k.shape
    return pl.pallas_call(
        functools.partial(_flash_kernel, tq=tq, tk=tk, n_maps=1, lambda_init=None),
        grid=(b, s // tq),
        in_specs=[pl.BlockSpec((None, w, tq), lambda bi, i: (bi, 0, i)),
                  pl.BlockSpec((None, s, w), lambda bi, i: (bi, 0, 0), pipeline_mode=pl.Buffered(1)),
                  pl.BlockSpec((None, w, s), lambda bi, i: (bi, 0, 0), pipeline_mode=pl.Buffered(1))],
        out_specs=pl.BlockSpec((None, tq, w), lambda bi, i: (bi, i, 0)),
        out_shape=jax.ShapeDtypeStruct((b, s, w), BF16),
        scratch_shapes=_flash_scratch(w // HEAD_W, tk, tq),
        compiler_params=_params("parallel", "arbitrary"),
        name="mla_attention",
    )(qt, k, vt)


def _diff_attention(qt, k, vt, pos, ftab, lam_params, sub_g, lambda_init, *, tq=ATTN_TQ, tk=ATTN_TK):
    b, s, w = k.shape
    pmin = jnp.min(pos.reshape(b, s // tq, tq), axis=-1)
    pmax = jnp.max(pos.reshape(b, s // tk, tk), axis=-1)
    pos_col = pos.reshape(b, s, 1)
    pos_row = pos.reshape(b, 1, s)
    grid_spec = pltpu.PrefetchScalarGridSpec(
        num_scalar_prefetch=2,
        grid=(b, s // tq),
        in_specs=[pl.BlockSpec((None, w, tq), lambda bi, i, *_: (bi, 0, i)),
                  pl.BlockSpec((None, s, w), lambda bi, i, *_: (bi, 0, 0), pipeline_mode=pl.Buffered(1)),
                  pl.BlockSpec((None, w, s), lambda bi, i, *_: (bi, 0, 0), pipeline_mode=pl.Buffered(1)),
                  pl.BlockSpec((None, 1, tq), lambda bi, i, *_: (bi, 0, i)),
                  pl.BlockSpec((None, s, 1), lambda bi, i, *_: (bi, 0, 0), pipeline_mode=pl.Buffered(1)),
                  pl.BlockSpec(ftab.shape, lambda bi, i, *_: (0, 0, 0)),
                  pl.BlockSpec(lam_params.shape, lambda bi, i, *_: (0, 0)),
                  pl.BlockSpec(sub_g.shape, lambda bi, i, *_: (0, 0))],
        out_specs=pl.BlockSpec((None, tq, w), lambda bi, i, *_: (bi, i, 0)),
        scratch_shapes=_flash_scratch(w // HEAD_W, tk, 2 * tq),
    )
    return pl.pallas_call(
        functools.partial(_flash_kernel, tq=tq, tk=tk, n_maps=2, lambda_init=lambda_init),
        grid_spec=grid_spec,
        out_shape=jax.ShapeDtypeStruct((b, s, w), BF16),
        compiler_params=_params("parallel", "arbitrary"),
        name="diff_attention",
    )(pmin, pmax, qt, k, vt, pos_row, pos_col, ftab, lam_params, sub_g)


def _merge_kernel(h_ref, g_ref, y0_ref, y1_ref, y2_ref, y3_ref, wg_ref, wb_ref, wo_ref, o_ref, *, col_chunk):
    h = h_ref[...]
    d = h.shape[-1]
    u = _rms(h, g_ref[...]).astype(BF16)
    ys = (y0_ref[...], y1_ref[...], y2_ref[...], y3_ref[...])
    parts = []
    for c0 in range(0, d, col_chunk):
        acc = None
        for i, y in enumerate(ys):
            gate = jax.nn.sigmoid(_dot(u, wg_ref[:, i * d + c0:i * d + c0 + col_chunk]))
            term = gate * _dot(y, wb_ref[i, :, c0:c0 + col_chunk])
            acc = term if acc is None else acc + term
        parts.append(acc.astype(BF16))
    merged = jnp.concatenate(parts, axis=-1)
    o_ref[...] = h + _dot(merged, wo_ref[...])


def _merge(h, g, ys, w_gate, w_branch, w_out, *, tm=1024, col_chunk=512):
    m, d = h.shape
    rows = lambda width: pl.BlockSpec((tm, width), lambda i: (i, 0))
    return pl.pallas_call(
        functools.partial(_merge_kernel, col_chunk=col_chunk),
        grid=(m // tm,),
        in_specs=[rows(d), _resident((1, d))] + [rows(BRANCH_W)] * N_BRANCH
                 + [_resident(w_gate), _resident(w_branch), _resident(w_out)],
        out_specs=rows(d),
        out_shape=jax.ShapeDtypeStruct((m, d), F32),
        compiler_params=_params("parallel"),
        name="merge",
    )(h, g, *ys, _operand(w_gate), _operand(w_branch), _operand(w_out))


def _xattn_kernel(h_ref, g_ref, wq_ref, k_ref, v_ref, wo_ref, o_ref):
    h = h_ref[...]
    d = h.shape[-1]
    dh = d // XA_HEADS
    u = _rms(h, g_ref[...]).astype(BF16)
    q = (_dot(u, wq_ref[...]) * (dh ** -0.5 * LOG2E)).astype(BF16)
    out = h
    for hd in range(XA_HEADS):
        sl = slice(hd * dh, (hd + 1) * dh)
        s = _dot_nt(q[:, sl], k_ref[:, sl])
        p = jnp.exp2(s - jnp.max(s, axis=-1, keepdims=True))
        inv = 1.0 / jnp.sum(p, axis=-1, keepdims=True)
        o = _dot(p.astype(BF16), v_ref[:, sl]) * inv
        out = out + _dot(o.astype(BF16), wo_ref[sl, :])
    o_ref[...] = out


def _cross_attention(h, g, wq, k, v, wo, batch, seq, *, tm=1024):
    m, d = h.shape
    mem_len = k.shape[0] // batch
    per = seq // tm
    return pl.pallas_call(
        _xattn_kernel,
        grid=(m // tm,),
        in_specs=[pl.BlockSpec((tm, d), lambda i: (i, 0)), _resident((1, d)), _resident(wq),
                  pl.BlockSpec((mem_len, d), lambda i: (i // per, 0)),
                  pl.BlockSpec((mem_len, d), lambda i: (i // per, 0)),
                  _resident(wo)],
        out_specs=pl.BlockSpec((tm, d), lambda i: (i, 0)),
        out_shape=jax.ShapeDtypeStruct((m, d), F32),
        compiler_params=_params("parallel"),
        name="cross_attention",
    )(h, g, _operand(wq), k, v, _operand(wo))


def _pad_cols(w, left, total):
    return jnp.pad(w, ((0, 0),) * (w.ndim - 1) + ((left, total - left - w.shape[-1]),))


def _w_in_plan():
    names = ("rq", "rk", "rv", "rg", "sz", "sxbc", "sdt", "cq", "ckv", "kr", "dq", "dk", "dv", "gl")
    src, start = {}, 0
    for n, s in zip(names, IN_SIZES):
        src[n] = (start, s)
        start += s
    half = MLA_ROPE // 2
    kr0 = src["kr"][0]
    moves, zeros, dst = [], [], 0
    for n in ("rq", "rk", "rv", "rg", "sxbc", "sz"):
        moves.append((*src[n], dst))
        dst += src[n][1]
    zeros.append((dst, SSD_DT_W))
    moves.append((*src["sdt"], dst))
    dst += SSD_DT_W
    for n in ("cq", "ckv"):
        moves.append((*src[n], dst))
        dst += src[n][1]
    zeros.append((dst, 2 * MLA_PAD))
    moves.append((kr0, MLA_ROPE, dst + MLA_NOPE))
    moves.append((kr0 + half, half, dst + MLA_PAD + MLA_NOPE))
    moves.append((kr0, half, dst + MLA_PAD + MLA_NOPE + half))
    dst += 2 * MLA_PAD
    for n in ("dq", "dk", "dv"):
        moves.append((*src[n], dst))
        dst += src[n][1]
    return moves, zeros, src["gl"], dst


def _w_in_relayout_kernel(x_ref, mix_ref, gate_ref, *, moves, zeros, gate):
    for d0, width in zeros:
        mix_ref[:, d0:d0 + width] = jnp.zeros((mix_ref.shape[0], width), mix_ref.dtype)
    for s0, width, d0 in moves:
        mix_ref[:, d0:d0 + width] = x_ref[:, s0:s0 + width].astype(mix_ref.dtype)
    gate_ref[...] = x_ref[:, gate[0]:gate[0] + gate[1]].astype(gate_ref.dtype)


def _w_in_relayout(w_in, *, row_blocks=4):
    depth, r, c = w_in.shape
    moves, zeros, gate, mix_w = _w_in_plan()
    tr = r // row_blocks
    blk = lambda width: pl.BlockSpec((None, tr, width), lambda l, i: (l, i, 0))
    return pl.pallas_call(
        functools.partial(_w_in_relayout_kernel, moves=moves, zeros=zeros, gate=gate),
        grid=(depth, row_blocks),
        in_specs=[blk(c)],
        out_specs=[blk(mix_w), blk(gate[1])],
        out_shape=[jax.ShapeDtypeStruct((depth, r, mix_w), BF16), jax.ShapeDtypeStruct((depth, r, gate[1]), BF16)],
        compiler_params=_params("parallel", "parallel"),
        name="w_in_relayout",
    )(w_in)


def _mla_up_weights(w_uq, w_ukv):
    depth, d, _ = w_uq.shape
    half = MLA_ROPE // 2
    uq = w_uq.astype(BF16).reshape(depth, d, MLA_HEADS, MLA_QH)
    pe = uq[..., MLA_NOPE:]
    partner = jnp.concatenate([jnp.zeros_like(uq[..., :MLA_NOPE]), pe[..., half:], pe[..., :half]], axis=-1)
    pad = lambda a: _pad_cols(a, 0, MLA_PAD).reshape(depth, d, MLA_HEADS * MLA_PAD)
    uq_p = jnp.concatenate([pad(uq), pad(partner)], axis=-1)
    r = w_ukv.shape[1]
    ukv = w_ukv.astype(BF16).reshape(depth, r, MLA_HEADS, MLA_NOPE + MLA_V)
    k_nope = _pad_cols(ukv[..., :MLA_NOPE], 0, MLA_PAD)
    ukv_p = jnp.concatenate([k_nope.reshape(depth, r, -1), ukv[..., MLA_NOPE:].reshape(depth, r, -1)], axis=-1)
    return uq_p, ukv_p


def _t5_bucket_const():
    n = jnp.arange(REL_TABLE)
    exact = REL_BUCKETS // 2
    nf = jnp.maximum(n, 1).astype(F32)
    large = exact + (jnp.log(nf / exact) / math.log(REL_MAX_DIST / exact) * (REL_BUCKETS - exact)).astype(jnp.int32)
    return jnp.where(n < exact, n, jnp.minimum(large, REL_BUCKETS - 1))


MIX_OUTS = ((3 * BRANCH_W, F32, 1.0, False),
            (SSD_P_WIDTH, F32, 1.0, False),
            (MLA_Q_RANK + MLA_KV_RANK + 2 * MLA_PAD, F32, 1.0, False),
            (BRANCH_W, BF16, DIFF_DH ** -0.5 * LOG2E, True),
            (BRANCH_W, BF16, 1.0, False),
            (BRANCH_W, BF16, 1.0, True))


def kernel(x, mem, positions, ffn1_norm, ffn1_w_gate, ffn1_w_up, ffn1_w_down, mix_norm, w_in, ssm_conv_w,
           ssm_conv_b, ssm_dt_bias, ssm_a_log, ssm_d, ssm_norm, mla_q_norm, mla_kv_norm, mla_w_uq, mla_w_ukv,
           diff_lambda, diff_norm, rel_bias, w_branch, w_out, xa_norm, mem_norm, xa_wq, xa_wk, xa_wv, xa_wo,
           ffn2_norm, ffn2_w_gate, ffn2_w_up, ffn2_w_down, final_norm):
    batch, seq, d = x.shape
    depth = w_in.shape[0]
    m = batch * seq
    h = x.reshape(m, d)
    mem2 = mem.reshape(batch * mem.shape[1], d)
    row = lambda a: a.reshape(1, -1)

    pos_col = positions.reshape(m, 1)
    ret_cos, ret_sin = _rope_tables(pos_col, *_rope_rows(RET_DK, RET_DK // 2, 0))
    mla_cos, mla_sin = _rope_tables(pos_col, *_rope_rows(MLA_PAD, MLA_ROPE // 2, MLA_NOPE))
    ftab = rel_bias[_t5_bucket_const()].T.reshape(DIFF_HEADS, 1, REL_TABLE)

    w_mix, w_gate = _w_in_relayout(w_in)
    w_uq_p, w_ukv_p = _mla_up_weights(mla_w_uq, mla_w_ukv)
    ffn1 = [_cast_bf16(w) for w in (ffn1_w_gate, ffn1_w_up, ffn1_w_down)]
    ffn2 = [_cast_bf16(w) for w in (ffn2_w_gate, ffn2_w_up, ffn2_w_down)]
    w_branch_b = _cast_bf16(w_branch.reshape(depth, N_BRANCH * BRANCH_W, d)).reshape(w_branch.shape)
    w_out_b, xa_wq_b, xa_wo_b = _cast_bf16(w_out), _cast_bf16(xa_wq), _cast_bf16(xa_wo)
    w_kv = jnp.concatenate([_cast_bf16(xa_wk), _cast_bf16(xa_wv)], axis=-1)

    for l in range(depth):
        at = lambda stack: _LayerOf(stack, l)
        h = _ffn(h, row(ffn1_norm[l]), *(at(w) for w in ffn1))
        ret_p, ssm_p, mla_p, dqt, dk, dvt = _norm_proj(h, row(mix_norm[l]), at(w_mix), MIX_OUTS, seq,
                                                       name="mix_proj")
        y_ret = _retention(ret_p.reshape(batch, seq, -1), ret_cos.reshape(batch, seq, -1),
                           ret_sin.reshape(batch, seq, -1)).reshape(m, BRANCH_W)
        y_ssm = _ssd(ssm_p, ssm_conv_w[l], row(ssm_conv_b[l]),
                     _pad_cols(row(ssm_dt_bias[l]), 0, SSD_DT_W), _pad_cols(row(ssm_a_log[l]), 0, SSD_DT_W),
                     row(jnp.repeat(ssm_d[l], SSM_HEADDIM)), row(ssm_norm[l]), batch, seq)
        mqt, mk, mvt = _mla_prep(mla_p, mla_cos, mla_sin, row(mla_q_norm[l]), row(mla_kv_norm[l]), at(w_uq_p),
                                 at(w_ukv_p), seq)
        y_mla = _mla_attention(mqt, mk.reshape(batch, seq, -1), mvt).reshape(m, BRANCH_W)
        lambda_init = 0.8 - 0.6 * math.exp(-0.3 * l)
        y_diff = _diff_attention(dqt, dk.reshape(batch, seq, -1), dvt, positions, ftab, diff_lambda[l],
                                 row(diff_norm[l]), lambda_init).reshape(m, BRANCH_W)
        h = _merge(h, row(mix_norm[l]), (y_ret, y_ssm, y_mla, y_diff), at(w_gate), at(w_branch_b), at(w_out_b))
        xk, xv = _norm_proj(mem2, row(mem_norm[l]), at(w_kv), [(d, BF16, 1.0, False)] * 2, mem.shape[1],
                            tm=mem.shape[1], name="mem_kv")
        h = _cross_attention(h, row(xa_norm[l]), at(xa_wq_b), xk, xv, at(xa_wo_b), batch, seq)
        h = _ffn(h, row(ffn2_norm[l]), *(at(w) for w in ffn2),
                 final_g=row(final_norm) if l == depth - 1 else None)
    return h.reshape(batch, seq, d)
```

```python
import functools
import math
from typing import NamedTuple

import jax
import jax.numpy as jnp
from jax import lax
from jax.experimental import pallas as pl
from jax.experimental.pallas import tpu as pltpu

F32 = jnp.float32
BF16 = jnp.bfloat16

EPS = 1e-6
NEG_INF = -1e30
LOG2E = math.log2(math.e)
ROPE_BASE = 10000.0
CHUNK = 128
BRANCH_W = 512
N_BRANCH = 4

RET_HEADS = 4
RET_DK = 64
RET_DV = 128
RET_QK = RET_HEADS * RET_DK

SSM_HEADDIM = 64
SSM_INNER = 512
SSM_HEADS = 8
SSM_GROUPS = 2
SSM_STATE = 128
SSM_CONV = 4
SSM_XBC = SSM_INNER + 2 * SSM_GROUPS * SSM_STATE
SSM_TAIL = 8
SSD_DT_W = 128
SSD_P_WIDTH = SSM_XBC + SSM_INNER + SSD_DT_W

MLA_HEADS = 4
MLA_Q_RANK = 256
MLA_KV_RANK = 128
MLA_NOPE = 64
MLA_ROPE = 32
MLA_V = 128
MLA_QH = MLA_NOPE + MLA_ROPE
MLA_PAD = 128
HEAD_W = 128
ATTN_TQ = 512
ATTN_TK = 256

DIFF_HEADS = 4
DIFF_DH = 64

REL_BUCKETS = 32
REL_MAX_DIST = 128
REL_TABLE = 128

XA_HEADS = 4

LANES = 128
VMEM_LIMIT = 56 * 1024 * 1024

IN_SIZES = (RET_QK, RET_QK, BRANCH_W, BRANCH_W,
            SSM_INNER, SSM_XBC, SSM_HEADS,
            MLA_Q_RANK, MLA_KV_RANK, MLA_ROPE,
            BRANCH_W, BRANCH_W, BRANCH_W,
            N_BRANCH * 1024)


def _params(*sem):
    return pltpu.CompilerParams(dimension_semantics=sem, vmem_limit_bytes=VMEM_LIMIT)


class _LayerOf(NamedTuple):
    stack: jax.Array
    layer: int

    @property
    def shape(self):
        return self.stack.shape[1:]


def _operand(x):
    return x.stack if isinstance(x, _LayerOf) else x


def _resident(x):
    if isinstance(x, _LayerOf):
        shape, lead, first = x.shape, (None,), (x.layer,)
    else:
        shape, lead, first = (x if isinstance(x, tuple) else x.shape), (), ()
    nd = len(shape)
    return pl.BlockSpec(lead + tuple(shape), lambda *_: first + (0,) * nd, pipeline_mode=pl.Buffered(1))


def _cast_kernel(x_ref, o_ref):
    o_ref[...] = x_ref[...].astype(o_ref.dtype)


def _cast_bf16(w, *, row_blocks=4):
    depth, r, c = w.shape
    tr = r // row_blocks
    spec = pl.BlockSpec((None, tr, c), lambda l, i: (l, i, 0))
    return pl.pallas_call(
        _cast_kernel,
        grid=(depth, row_blocks),
        in_specs=[spec],
        out_specs=spec,
        out_shape=jax.ShapeDtypeStruct(w.shape, BF16),
        compiler_params=_params("parallel", "parallel"),
        name="cast_bf16",
    )(w)


def _rms(x, g):
    return x * lax.rsqrt(jnp.mean(x * x, axis=-1, keepdims=True) + EPS) * g


def _silu(x):
    return x * jax.nn.sigmoid(x)


def _dot(a, b):
    return jnp.dot(a, b, preferred_element_type=F32)


def _dot_nt(a, b):
    return lax.dot_general(a, b, (((1,), (1,)), ((), ())), preferred_element_type=F32)


def _dot_tn(a, b):
    return lax.dot_general(a, b, (((0,), (0,)), ((), ())), preferred_element_type=F32)


def _ffn_kernel(h_ref, g_ref, wg_ref, wu_ref, wd_ref, *rest, ff_chunk, final):
    if final:
        gf_ref, o_ref = rest
    else:
        (o_ref,) = rest
    h = h_ref[...]
    xn = _rms(h, g_ref[...]).astype(BF16)
    acc = jnp.zeros(h.shape, F32)
    d_ff = wg_ref.shape[1]
    for c0 in range(0, d_ff, ff_chunk):
        a = _dot(xn, wg_ref[:, c0:c0 + ff_chunk])
        b = _dot(xn, wu_ref[:, c0:c0 + ff_chunk])
        acc = acc + _dot((_silu(a) * b).astype(BF16), wd_ref[c0:c0 + ff_chunk, :])
    out = h + 0.5 * acc
    if final:
        out = _rms(out, gf_ref[...])
    o_ref[...] = out


def _ffn(h, g, wg, wu, wd, final_g=None, *, tm=512, ff_chunk=256):
    m, d = h.shape
    d_ff = wg.shape[1]
    final = final_g is not None
    in_specs = [pl.BlockSpec((tm, d), lambda i: (i, 0)), _resident((1, d)),
                _resident(wg), _resident(wu), _resident(wd)]
    args = [h, g, _operand(wg), _operand(wu), _operand(wd)]
    if final:
        in_specs.append(_resident((1, d)))
        args.append(final_g)
    return pl.pallas_call(
        functools.partial(_ffn_kernel, ff_chunk=ff_chunk, final=final),
        grid=(m // tm,),
        in_specs=in_specs,
        out_specs=pl.BlockSpec((tm, d), lambda i: (i, 0)),
        out_shape=jax.ShapeDtypeStruct((m, d), F32),
        compiler_params=_params("parallel"),
        name="ffn",
    )(*args)


def _norm_proj_kernel(x_ref, g_ref, w_ref, *o_refs, outs, col_chunk):
    xn = _rms(x_ref[...], g_ref[...]).astype(BF16)
    off = 0
    for o_ref, (width, _, scale, transposed) in zip(o_refs, outs):
        for c0 in range(0, width, col_chunk):
            c1 = min(c0 + col_chunk, width)
            y = _dot(xn, w_ref[:, off + c0:off + c1])
            if scale != 1.0:
                y = y * scale
            if transposed:
                o_ref[c0:c1, :] = y.astype(o_ref.dtype).T
            else:
                o_ref[:, c0:c1] = y.astype(o_ref.dtype)
        off += width


def _norm_proj(x, g, w, outs, seq, *, tm=512, col_chunk=512, name="norm_proj"):
    m, d = x.shape
    assert sum(o[0] for o in outs) == w.shape[1]
    per = seq // tm
    out_specs, out_shape = [], []
    for width, dt, _, transposed in outs:
        if transposed:
            out_specs.append(pl.BlockSpec((None, width, tm), lambda i: (i // per, 0, i % per)))
            out_shape.append(jax.ShapeDtypeStruct((m // seq, width, seq), dt))
        else:
            out_specs.append(pl.BlockSpec((tm, width), lambda i: (i, 0)))
            out_shape.append(jax.ShapeDtypeStruct((m, width), dt))
    return pl.pallas_call(
        functools.partial(_norm_proj_kernel, outs=tuple(outs), col_chunk=col_chunk),
        grid=(m // tm,),
        in_specs=[pl.BlockSpec((tm, d), lambda i: (i, 0)), _resident((1, d)), _resident(w)],
        out_specs=out_specs,
        out_shape=out_shape,
        compiler_params=_params("parallel"),
        name=name,
    )(x, g, _operand(w))


def _rope_table_kernel(pos_ref, inv_ref, sign_ref, cos_ref, sin_ref):
    ang = pos_ref[...].astype(F32) * inv_ref[...]
    cos_ref[...] = jnp.cos(ang)
    sin_ref[...] = jnp.sin(ang) * sign_ref[...]


def _rope_tables(pos_col, inv_row, sign_row, *, tm=1024):
    m = pos_col.shape[0]
    return pl.pallas_call(
        _rope_table_kernel,
        grid=(m // tm,),
        in_specs=[pl.BlockSpec((tm, 1), lambda i: (i, 0)), _resident((1, LANES)), _resident((1, LANES))],
        out_specs=[pl.BlockSpec((tm, LANES), lambda i: (i, 0))] * 2,
        out_shape=[jax.ShapeDtypeStruct((m, LANES), F32)] * 2,
        compiler_params=_params("parallel"),
        name="rope_tables",
    )(pos_col, inv_row, sign_row)


def _rope_rows(period, half, start):
    lane = jnp.arange(LANES) % period
    inside = (lane >= start) & (lane < start + 2 * half)
    idx = (lane - start) % half
    inv = jnp.exp(-math.log(ROPE_BASE) * idx.astype(F32) / half)
    inv = jnp.where(inside, inv, 0.0)
    sign = jnp.where(inside, jnp.where(lane < start + half, -1.0, 1.0), 0.0)
    return inv.reshape(1, LANES).astype(F32), sign.reshape(1, LANES).astype(F32)


def _apply_rope(x, cos, sin_signed, period, half, start):
    width = x.shape[-1]
    reps = width // LANES
    if reps > 1:
        cos = jnp.concatenate([cos] * reps, axis=-1)
        sin_signed = jnp.concatenate([sin_signed] * reps, axis=-1)
    lane = lax.broadcasted_iota(jnp.int32, (1, width), 1) % period
    first = lane < start + half
    partner = jnp.where(first, pltpu.roll(x, width - half, 1), pltpu.roll(x, half, 1))
    return x * cos + partner * sin_signed


def _ret_kernel(q_ref, k_ref, v_ref, g_ref, cos_ref, sin_ref, intra_ref, qdec_ref, kdec_ref,
                sdec_ref, bmask_ref, o_ref, state_sc):
    @pl.when(pl.program_id(0) == 0)
    def _():
        state_sc[...] = jnp.zeros_like(state_sc)

    head = lax.broadcasted_iota(jnp.int32, (1, RET_QK), 1) // RET_DK
    for b in range(q_ref.shape[0]):
        cos, sin = cos_ref[b], sin_ref[b]
        q = _apply_rope(q_ref[b], cos, sin, RET_DK, RET_DK // 2, 0) * (RET_DK ** -0.5)
        k = _apply_rope(k_ref[b], cos, sin, RET_DK, RET_DK // 2, 0)
        vb = v_ref[b].astype(BF16)
        kb = k.astype(BF16)
        state = state_sc[b]
        cross = _dot((q * qdec_ref[...]).astype(BF16), state.astype(BF16))
        outs = []
        for h in range(RET_HEADS):
            qh = jnp.where(head == h, q, 0.0).astype(BF16)
            att = _dot_nt(qh, kb) * intra_ref[h]
            sl = slice(h * RET_DV, (h + 1) * RET_DV)
            o = _dot(att.astype(BF16), vb[:, sl]) + cross[:, sl]
            mu = jnp.mean(o, axis=-1, keepdims=True)
            oc = o - mu
            outs.append(oc * lax.rsqrt(jnp.mean(oc * oc, axis=-1, keepdims=True) + EPS))
        o_ref[b] = (jnp.concatenate(outs, axis=-1) * _silu(g_ref[b])).astype(o_ref.dtype)
        kv = _dot_tn((k * kdec_ref[...]).astype(BF16), vb)
        state_sc[b] = state * sdec_ref[...] + kv * bmask_ref[...]


def _retention_consts():
    c = CHUNK
    log_g = jnp.log1p(-jnp.exp2(-5.0 - jnp.arange(RET_HEADS, dtype=F32)))
    i = jnp.arange(c, dtype=F32)
    rel = i[:, None] - i[None, :]
    intra = jnp.where(rel >= 0, jnp.exp(log_g[:, None, None] * jnp.maximum(rel, 0.0)), 0.0)
    lg_lane = jnp.repeat(log_g, RET_DK)
    qdec = jnp.exp(lg_lane[None, :] * (i[:, None] + 1.0))
    kdec = jnp.exp(lg_lane[None, :] * (c - 1.0 - i[:, None]))
    row_h = jnp.repeat(jnp.arange(RET_HEADS), RET_DK)
    col_h = jnp.repeat(jnp.arange(RET_HEADS), RET_DV)
    bmask = (row_h[:, None] == col_h[None, :]).astype(F32)
    sdec = bmask * jnp.exp(lg_lane * c)[:, None]
    return intra, qdec, kdec, sdec, bmask


def _retention(ret_p, cos, sin):
    c = CHUNK
    batch, seq, _ = ret_p.shape
    intra, qdec, kdec, sdec, bmask = _retention_consts()
    blk = lambda width, col: pl.BlockSpec((batch, c, width), lambda j: (0, j, col))
    return pl.pallas_call(
        _ret_kernel,
        grid=(seq // c,),
        in_specs=[blk(RET_QK, 0), blk(RET_QK, 1), blk(BRANCH_W, 1), blk(BRANCH_W, 2), blk(LANES, 0), blk(LANES, 0),
                  _resident(intra.shape), _resident(qdec.shape), _resident(kdec.shape),
                  _resident(sdec.shape), _resident(bmask.shape)],
        out_specs=blk(BRANCH_W, 0),
        out_shape=jax.ShapeDtypeStruct((batch, seq, BRANCH_W), BF16),
        scratch_shapes=[pltpu.VMEM((batch, RET_QK, BRANCH_W), F32)],
        compiler_params=_params("arbitrary"),
        name="retention",
    )(ret_p, ret_p, ret_p, ret_p, cos, sin, intra, qdec, kdec, sdec, bmask)


def _split3(x):
    hi = x.astype(BF16)
    r = x - hi.astype(F32)
    mid = r.astype(BF16)
    lo = (r - mid.astype(F32)).astype(BF16)
    return hi, mid, lo


def _dot_split3(a, b, *, left):
    if left:
        return sum(_dot(piece, b) for piece in _split3(a))
    return sum(_dot(a, piece) for piece in _split3(b))


def _ssd_kernel(xbc_ref, z_ref, dt_ref, cw_ref, cb_ref, dtb_ref, alog_ref, dskip_ref, ng_ref, tri_ref,
                expand_ref, o_ref, xe_sc, state_sc):
    L = CHUNK
    T = SSM_TAIL
    GW = SSM_INNER // SSM_GROUPS

    @pl.when(pl.program_id(1) == 0)
    def _():
        xe_sc[0:T, :] = jnp.zeros((T, SSM_XBC), F32)
        state_sc[...] = jnp.zeros_like(state_sc)

    x_raw = xbc_ref[...]
    xe_sc[T:T + L, :] = x_raw
    conv = cb_ref[...]
    for kk in range(SSM_CONV):
        s0 = T - (SSM_CONV - 1) + kk
        conv = conv + cw_ref[kk:kk + 1, :] * xe_sc[s0:s0 + L, :]
    xe_sc[0:T, :] = x_raw[L - T:L, :]
    xa = _silu(conv)
    xs = xa[:, :SSM_INNER]
    bm = xa[:, SSM_INNER:SSM_INNER + SSM_GROUPS * SSM_STATE]
    cm = xa[:, SSM_INNER + SSM_GROUPS * SSM_STATE:]

    dtr = dt_ref[...] + dtb_ref[...]
    dt_h = jnp.maximum(dtr, 0.0) + jnp.log1p(jnp.exp(-jnp.abs(dtr)))
    da = dt_h * (-jnp.exp(alog_ref[...]))
    cs_h = _dot_split3(tri_ref[...], da, left=False)
    cs_h_last = cs_h[L - 1:L, :]
    cs_t = cs_h.T
    expand = expand_ref[...]
    dt = _dot_split3(dt_h, expand, left=True)
    ecs = _dot_split3(jnp.exp(cs_h), expand, left=True)
    eds = _dot_split3(jnp.exp(cs_h_last - cs_h), expand, left=True)
    xdt = xs * dt
    xdec = (xdt * eds).astype(BF16)
    state = state_sc[...]
    li = lax.broadcasted_iota(jnp.int32, (L, L), 0)
    si = lax.broadcasted_iota(jnp.int32, (L, L), 1)
    causal = li >= si
    lane_half = lax.broadcasted_iota(jnp.int32, (1, LANES), 1) // SSM_HEADDIM
    ys = []
    new_states = []
    for g in range(SSM_GROUPS):
        bg = bm[:, g * SSM_STATE:(g + 1) * SSM_STATE].astype(BF16)
        cg = cm[:, g * SSM_STATE:(g + 1) * SSM_STATE].astype(BF16)
        cb = _dot_nt(cg, bg)
        gsl = slice(g * GW, (g + 1) * GW)
        y_off = _dot(cg, state[:, gsl].astype(BF16)) * ecs[:, gsl]
        for p in range(GW // LANES):
            sl = slice(g * GW + p * LANES, g * GW + (p + 1) * LANES)
            xdt_blk = xdt[:, sl]
            acc = y_off[:, p * LANES:(p + 1) * LANES]
            for hh in range(LANES // SSM_HEADDIM):
                head = (g * GW + p * LANES) // SSM_HEADDIM + hh
                col = cs_h[:, head:head + 1]
                rowv = cs_t[head:head + 1, :]
                dec = jnp.where(causal, jnp.exp(jnp.minimum(col - rowv, 0.0)), 0.0)
                xm = jnp.where(lane_half == hh, xdt_blk, 0.0).astype(BF16)
                acc = acc + _dot((cb * dec).astype(BF16), xm)
            ys.append(acc)
        new_states.append(_dot_tn(bg, xdec[:, gsl]))
    y = jnp.concatenate(ys, axis=-1) + xs * dskip_ref[...]
    y = y * _silu(z_ref[...])
    o_ref[...] = _rms(y, ng_ref[...]).astype(o_ref.dtype)
    state_sc[...] = state * ecs[L - 1:L, :] + jnp.concatenate(new_states, axis=-1)


def _ssd(ssm_p, conv_w, conv_b, dt_bias_pad, a_log_pad, d_full, norm_g, batch, seq):
    L = CHUNK
    n = seq // L
    tri = (jnp.arange(L)[:, None] >= jnp.arange(L)[None, :]).astype(BF16)
    expand = (jnp.arange(SSD_DT_W)[:, None] == jnp.arange(SSM_INNER)[None, :] // SSM_HEADDIM).astype(BF16)
    row = lambda b, j: b * n + j
    return pl.pallas_call(
        _ssd_kernel,
        grid=(batch, n),
        in_specs=[pl.BlockSpec((L, SSM_XBC), lambda b, j: (row(b, j), 0)),
                  pl.BlockSpec((L, SSM_INNER), lambda b, j: (row(b, j), SSM_XBC // SSM_INNER)),
                  pl.BlockSpec((L, SSD_DT_W), lambda b, j: (row(b, j), (SSM_XBC + SSM_INNER) // SSD_DT_W)),
                  _resident(conv_w.shape), _resident(conv_b.shape), _resident(dt_bias_pad.shape),
                  _resident(a_log_pad.shape), _resident(d_full.shape), _resident(norm_g.shape),
                  _resident(tri.shape), _resident(expand.shape)],
        out_specs=pl.BlockSpec((L, SSM_INNER), lambda b, j: (row(b, j), 0)),
        out_shape=jax.ShapeDtypeStruct((batch * seq, SSM_INNER), BF16),
        scratch_shapes=[pltpu.VMEM((SSM_TAIL + L, SSM_XBC), F32),
                        pltpu.VMEM((SSM_STATE, SSM_INNER), F32)],
        compiler_params=_params("parallel", "arbitrary"),
        name="ssd",
    )(ssm_p, ssm_p, ssm_p, conv_w, conv_b, dt_bias_pad, a_log_pad, d_full, norm_g, tri, expand)


def _mla_prep_kernel(p_ref, cos_ref, sin_ref, qn_ref, kvn_ref, wuq_ref, wukv_ref, q_ref, k_ref, v_ref):
    cos, sin = cos_ref[...], sin_ref[...]
    kvw = MLA_HEADS * MLA_PAD
    cq = p_ref[:, :MLA_Q_RANK]
    ckv = p_ref[:, MLA_Q_RANK:MLA_Q_RANK + MLA_KV_RANK]
    kr0 = MLA_Q_RANK + MLA_KV_RANK
    kr = p_ref[:, kr0:kr0 + MLA_PAD]
    kr_partner = p_ref[:, kr0 + MLA_PAD:]
    qq = _dot(_rms(cq, qn_ref[...]).astype(BF16), wuq_ref[...])
    cos4 = jnp.concatenate([cos] * MLA_HEADS, axis=-1)
    sin4 = jnp.concatenate([sin] * MLA_HEADS, axis=-1)
    q = (qq[:, :kvw] * cos4 + qq[:, kvw:] * sin4) * (MLA_QH ** -0.5 * LOG2E)
    q_ref[...] = q.astype(q_ref.dtype).T
    kv = _dot(_rms(ckv, kvn_ref[...]).astype(BF16), wukv_ref[...])
    kpe = kr * cos + kr_partner * sin
    k_ref[...] = (kv[:, :kvw] + jnp.concatenate([kpe] * MLA_HEADS, axis=-1)).astype(k_ref.dtype)
    v_ref[...] = kv[:, kvw:].astype(v_ref.dtype).T


def _mla_prep(mla_p, cos, sin, q_norm, kv_norm, w_uq_p, w_ukv_p, seq, *, tm=512):
    m = mla_p.shape[0]
    w = MLA_HEADS * MLA_PAD
    per = seq // tm
    rows = lambda width: pl.BlockSpec((tm, width), lambda i: (i, 0))
    cols = pl.BlockSpec((None, w, tm), lambda i: (i // per, 0, i % per))
    t_shape = jax.ShapeDtypeStruct((m // seq, w, seq), BF16)
    return pl.pallas_call(
        _mla_prep_kernel,
        grid=(m // tm,),
        in_specs=[rows(mla_p.shape[1]), rows(LANES), rows(LANES), _resident(q_norm.shape),
                  _resident(kv_norm.shape), _resident(w_uq_p), _resident(w_ukv_p)],
        out_specs=[cols, rows(w), cols],
        out_shape=[t_shape, jax.ShapeDtypeStruct((m, w), BF16), t_shape],
        compiler_params=_params("parallel"),
        name="mla_prep",
    )(mla_p, cos, sin, q_norm, kv_norm, _operand(w_uq_p), _operand(w_ukv_p))


def _flash_kernel(*refs, tq, tk, n_maps, lambda_init):
    biased = n_maps == 2
    if biased:
        (pmin_ref, pmax_ref, qt_ref, k_ref, vt_ref, pq_ref, pk_ref, ftab_ref, lam_ref, sg_ref,
         o_ref, m_sc, l_sc, acc_sc, s0_sc, s1_sc) = refs
    else:
        qt_ref, k_ref, vt_ref, o_ref, m_sc, l_sc, acc_sc, s0_sc, s1_sc = refs
    bi = pl.program_id(0)
    qi = pl.program_id(1)
    hw = HEAD_W
    heads = k_ref.shape[-1] // hw
    cols = n_maps * tq
    hs = lambda h: slice(h * hw, (h + 1) * hw)
    ws = []
    for h in range(heads):
        qt = qt_ref[hs(h), :]
        if biased:
            half = lax.broadcasted_iota(jnp.int32, (hw, 1), 0) // DIFF_DH
            zero = jnp.zeros_like(qt)
            qt = jnp.concatenate([jnp.where(half == 0, qt, zero), jnp.where(half == 1, qt, zero)], axis=1)
        ws.append(qt)
    if biased:
        pq = pq_ref[...]
        ftabs = [jnp.broadcast_to(ftab_ref[h] * LOG2E, (tk, REL_TABLE)) for h in range(heads)]
    m_sc[...] = jnp.full(m_sc.shape, NEG_INF, F32)
    l_sc[...] = jnp.zeros_like(l_sc)
    acc_sc[...] = jnp.zeros_like(acc_sc)

    def near_biases(off, q_lo=0):
        idx = jnp.clip(pq[:, q_lo:] - pk_ref[pl.ds(off, tk), :], 0, REL_TABLE - 1)
        tiles = [jnp.concatenate([jnp.take_along_axis(ftabs[h], idx[:, c:c + LANES], axis=1)
                                  for c in range(0, tq - q_lo, LANES)], axis=1) for h in range(heads)]

        def bias(h, cs):
            base = (cs.start // tq) * tq + q_lo
            return tiles[h][:, cs.start - base:cs.stop - base]
        return bias

    s_scs = (s0_sc, s1_sc)

    all_cols = tuple(slice(mi * tq, (mi + 1) * tq) for mi in range(n_maps))
    late_cols = tuple(slice(mi * tq + tk, (mi + 1) * tq) for mi in range(n_maps))

    def scores(off, slot, col_ranges=all_cols):
        for h in range(heads):
            for cs in col_ranges:
                s_scs[slot][h, :, cs] = _dot(k_ref[pl.ds(off, tk), hs(h)], ws[h][:, cs])

    def consume(off, slot, bias, mask, const=None, col_ranges=all_cols):
        for h in range(heads):
            for cs in col_ranges:
                s = s_scs[slot][h, :, cs]
                if bias is not None:
                    s = s + bias(h, cs)
                if mask is not None:
                    s = jnp.where(mask[:, cs], s, NEG_INF)
                m = m_sc[h, :, cs]
                m_tile = jnp.max(s, axis=0, keepdims=True)
                if const is not None:
                    m_tile = m_tile + const[h]
                m_new = jnp.maximum(m, m_tile)
                alpha = jnp.exp2(m - m_new)
                p = jnp.exp2(s - (m_new if const is None else m_new - const[h]))
                l_sc[h, :, cs] = alpha * l_sc[h, :, cs] + jnp.sum(p, axis=0, keepdims=True)
                acc_sc[h, :, cs] = (alpha * acc_sc[h, :, cs]
                                    + _dot(vt_ref[hs(h), pl.ds(off, tk)], p.astype(BF16)))
                m_sc[h, :, cs] = m_new

    assert tq == 2 * tk
    scores(0, 0)

    def body(jj, carry):
        off0 = pl.multiple_of(jj * tq, tq)
        off1 = pl.multiple_of(off0 + tk, tk)
        off2 = pl.multiple_of(off0 + tq, tq)

        def pair(near0, near1):
            const = [ftab_ref[h][:, REL_TABLE - 1:REL_TABLE] * LOG2E for h in range(heads)] if biased else None
            scores(off1, 1)
            consume(off0, 0, near_biases(off0) if near0 else None, None, None if near0 else const)
            scores(off2, 0)
            consume(off1, 1, near_biases(off1) if near1 else None, None, None if near1 else const)

        if biased:
            far0 = pmin_ref[bi, qi] - pmax_ref[bi, 2 * jj] >= REL_TABLE - 1
            far1 = pmin_ref[bi, qi] - pmax_ref[bi, 2 * jj + 1] >= REL_TABLE - 1

            @pl.when(far0 & far1)
            def _():
                pair(False, False)

            @pl.when(far0 & jnp.logical_not(far1))
            def _():
                pair(False, True)

            @pl.when(jnp.logical_not(far0))
            def _():
                pair(True, True)
        else:
            pair(False, False)
        return carry

    lax.fori_loop(0, qi, body, 0)
    for d in range(2):
        off = pl.multiple_of(qi * tq + d * tk, tk)
        if d == 0:
            scores(pl.multiple_of(off + tk, tk), 1, late_cols)
        key = lax.broadcasted_iota(jnp.int32, (tk, cols), 0) + d * tk
        qry = lax.broadcasted_iota(jnp.int32, (tk, cols), 1) % tq
        consume(off, d, near_biases(off, d * tk) if biased else None, key <= qry,
                col_ranges=all_cols if d == 0 else late_cols)

    if biased:
        lp = lam_ref[...]
        lam = (jnp.exp(jnp.sum(lp[0:1] * lp[1:2], axis=-1, keepdims=True))
               - jnp.exp(jnp.sum(lp[2:3] * lp[3:4], axis=-1, keepdims=True)) + lambda_init)
    for h in range(heads):
        o = acc_sc[h] / l_sc[h]
        if biased:
            o = (o[:, :tq] - lam * o[:, tq:]).T
            o_ref[:, hs(h)] = (_rms(o, sg_ref[...]) * (1.0 - lambda_init)).astype(o_ref.dtype)
        else:
            o_ref[:, hs(h)] = o.astype(o_ref.dtype).T


def _flash_scratch(heads, tk, cols):
    return [pltpu.VMEM((heads, 1, cols), F32), pltpu.VMEM((heads, 1, cols), F32),
            pltpu.VMEM((heads, HEAD_W, cols), F32),
            pltpu.VMEM((heads, tk, cols), F32), pltpu.VMEM((heads, tk, cols), F32)]


def _mla_attention(qt, k, vt, *, tq=ATTN_TQ, tk=ATTN_TK):
    b, s, w = k.shape
    return pl.pallas_call(
        functools.partial(_flash_kernel, tq=tq, tk=tk, n_maps=1, lambda_init=None),
        grid=(b, s // tq),
        in_specs=[pl.BlockSpec((None, w, tq), lambda bi, i: (bi, 0, i)),
                  pl.BlockSpec((None, s, w), lambda bi, i: (bi, 0, 0), pipeline_mode=pl.Buffered(1)),
                  pl.BlockSpec((None, w, s), lambda bi, i: (bi, 0, 0), pipeline_mode=pl.Buffered(1))],
        out_specs=pl.BlockSpec((None, tq, w), lambda bi, i: (bi, i, 0)),
        out_shape=jax.ShapeDtypeStruct((b, s, w), BF16),
        scratch_shapes=_flash_scratch(w // HEAD_W, tk, tq),
        compiler_params=_params("parallel", "arbitrary"),
        name="mla_attention",
    )(qt, k, vt)


def _diff_attention(qt, k, vt, pos, ftab, lam_params, sub_g, lambda_init, *, tq=ATTN_TQ, tk=ATTN_TK):
    b, s, w = k.shape
    pmin = jnp.min(pos.reshape(b, s // tq, tq), axis=-1)
    pmax = jnp.max(pos.reshape(b, s // tk, tk), axis=-1)
    pos_col = pos.reshape(b, s, 1)
    pos_row = pos.reshape(b, 1, s)
    grid_spec = pltpu.PrefetchScalarGridSpec(
        num_scalar_prefetch=2,
        grid=(b, s // tq),
        in_specs=[pl.BlockSpec((None, w, tq), lambda bi, i, *_: (bi, 0, i)),
                  pl.BlockSpec((None, s, w), lambda bi, i, *_: (bi, 0, 0), pipeline_mode=pl.Buffered(1)),
                  pl.BlockSpec((None, w, s), lambda bi, i, *_: (bi, 0, 0), pipeline_mode=pl.Buffered(1)),
                  pl.BlockSpec((None, 1, tq), lambda bi, i, *_: (bi, 0, i)),
                  pl.BlockSpec((None, s, 1), lambda bi, i, *_: (bi, 0, 0), pipeline_mode=pl.Buffered(1)),
                  pl.BlockSpec(ftab.shape, lambda bi, i, *_: (0, 0, 0)),
                  pl.BlockSpec(lam_params.shape, lambda bi, i, *_: (0, 0)),
                  pl.BlockSpec(sub_g.shape, lambda bi, i, *_: (0, 0))],
        out_specs=pl.BlockSpec((None, tq, w), lambda bi, i, *_: (bi, i, 0)),
        scratch_shapes=_flash_scratch(w // HEAD_W, tk, 2 * tq),
    )
    return pl.pallas_call(
        functools.partial(_flash_kernel, tq=tq, tk=tk, n_maps=2, lambda_init=lambda_init),
        grid_spec=grid_spec,
        out_shape=jax.ShapeDtypeStruct((b, s, w), BF16),
        compiler_params=_params("parallel", "arbitrary"),
        name="diff_attention",
    )(pmin, pmax, qt, k, vt, pos_row, pos_col, ftab, lam_params, sub_g)


def _merge_kernel(h_ref, g_ref, y0_ref, y1_ref, y2_ref, y3_ref, wg_ref, wb_ref, wo_ref, o_ref, *, col_chunk):
    h = h_ref[...]
    d = h.shape[-1]
    u = _rms(h, g_ref[...]).astype(BF16)
    ys = (y0_ref[...], y1_ref[...], y2_ref[...], y3_ref[...])
    parts = []
    for c0 in range(0, d, col_chunk):
        acc = None
        for i, y in enumerate(ys):
            gate = jax.nn.sigmoid(_dot(u, wg_ref[:, i * d + c0:i * d + c0 + col_chunk]))
            term = gate * _dot(y, wb_ref[i, :, c0:c0 + col_chunk])
            acc = term if acc is None else acc + term
        parts.append(acc.astype(BF16))
    merged = jnp.concatenate(parts, axis=-1)
    o_ref[...] = h + _dot(merged, wo_ref[...])


def _merge(h, g, ys, w_gate, w_branch, w_out, *, tm=1024, col_chunk=512):
    m, d = h.shape
    rows = lambda width: pl.BlockSpec((tm, width), lambda i: (i, 0))
    return pl.pallas_call(
        functools.partial(_merge_kernel, col_chunk=col_chunk),
        grid=(m // tm,),
        in_specs=[rows(d), _resident((1, d))] + [rows(BRANCH_W)] * N_BRANCH
                 + [_resident(w_gate), _resident(w_branch), _resident(w_out)],
        out_specs=rows(d),
        out_shape=jax.ShapeDtypeStruct((m, d), F32),
        compiler_params=_params("parallel"),
        name="merge",
    )(h, g, *ys, _operand(w_gate), _operand(w_branch), _operand(w_out))


def _xattn_kernel(h_ref, g_ref, wq_ref, k_ref, v_ref, wo_ref, o_ref):
    h = h_ref[...]
    d = h.shape[-1]
    dh = d // XA_HEADS
    u = _rms(h, g_ref[...]).astype(BF16)
    q = (_dot(u, wq_ref[...]) * (dh ** -0.5 * LOG2E)).astype(BF16)
    out = h
    for hd in range(XA_HEADS):
        sl = slice(hd * dh, (hd + 1) * dh)
        s = _dot_nt(q[:, sl], k_ref[:, sl])
        p = jnp.exp2(s - jnp.max(s, axis=-1, keepdims=True))
        inv = 1.0 / jnp.sum(p, axis=-1, keepdims=True)
        o = _dot(p.astype(BF16), v_ref[:, sl]) * inv
        out = out + _dot(o.astype(BF16), wo_ref[sl, :])
    o_ref[...] = out


def _cross_attention(h, g, wq, k, v, wo, batch, seq, *, tm=1024):
    m, d = h.shape
    mem_len = k.shape[0] // batch
    per = seq // tm
    return pl.pallas_call(
        _xattn_kernel,
        grid=(m // tm,),
        in_specs=[pl.BlockSpec((tm, d), lambda i: (i, 0)), _resident((1, d)), _resident(wq),
                  pl.BlockSpec((mem_len, d), lambda i: (i // per, 0)),
                  pl.BlockSpec((mem_len, d), lambda i: (i // per, 0)),
                  _resident(wo)],
        out_specs=pl.BlockSpec((tm, d), lambda i: (i, 0)),
        out_shape=jax.ShapeDtypeStruct((m, d), F32),
        compiler_params=_params("parallel"),
        name="cross_attention",
    )(h, g, _operand(wq), k, v, _operand(wo))


def _pad_cols(w, left, total):
    return jnp.pad(w, ((0, 0),) * (w.ndim - 1) + ((left, total - left - w.shape[-1]),))


def _w_in_plan():
    names = ("rq", "rk", "rv", "rg", "sz", "sxbc", "sdt", "cq", "ckv", "kr", "dq", "dk", "dv", "gl")
    src, start = {}, 0
    for n, s in zip(names, IN_SIZES):
        src[n] = (start, s)
        start += s
    half = MLA_ROPE // 2
    kr0 = src["kr"][0]
    moves, zeros, dst = [], [], 0
    for n in ("rq", "rk", "rv", "rg", "sxbc", "sz"):
        moves.append((*src[n], dst))
        dst += src[n][1]
    zeros.append((dst, SSD_DT_W))
    moves.append((*src["sdt"], dst))
    dst += SSD_DT_W
    for n in ("cq", "ckv"):
        moves.append((*src[n], dst))
        dst += src[n][1]
    zeros.append((dst, 2 * MLA_PAD))
    moves.append((kr0, MLA_ROPE, dst + MLA_NOPE))
    moves.append((kr0 + half, half, dst + MLA_PAD + MLA_NOPE))
    moves.append((kr0, half, dst + MLA_PAD + MLA_NOPE + half))
    dst += 2 * MLA_PAD
    for n in ("dq", "dk", "dv"):
        moves.append((*src[n], dst))
        dst += src[n][1]
    return moves, zeros, src["gl"], dst


def _w_in_relayout_kernel(x_ref, mix_ref, gate_ref, *, moves, zeros, gate):
    for d0, width in zeros:
        mix_ref[:, d0:d0 + width] = jnp.zeros((mix_ref.shape[0], width), mix_ref.dtype)
    for s0, width, d0 in moves:
        mix_ref[:, d0:d0 + width] = x_ref[:, s0:s0 + width].astype(mix_ref.dtype)
    gate_ref[...] = x_ref[:, gate[0]:gate[0] + gate[1]].astype(gate_ref.dtype)


def _w_in_relayout(w_in, *, row_blocks=4):
    depth, r, c = w_in.shape
    moves, zeros, gate, mix_w = _w_in_plan()
    tr = r // row_blocks
    blk = lambda width: pl.BlockSpec((None, tr, width), lambda l, i: (l, i, 0))
    return pl.pallas_call(
        functools.partial(_w_in_relayout_kernel, moves=moves, zeros=zeros, gate=gate),
        grid=(depth, row_blocks),
        in_specs=[blk(c)],
        out_specs=[blk(mix_w), blk(gate[1])],
        out_shape=[jax.ShapeDtypeStruct((depth, r, mix_w), BF16), jax.ShapeDtypeStruct((depth, r, gate[1]), BF16)],
        compiler_params=_params("parallel", "parallel"),
        name="w_in_relayout",
    )(w_in)


def _mla_up_weights(w_uq, w_ukv):
    depth, d, _ = w_uq.shape
    half = MLA_ROPE // 2
    uq = w_uq.astype(BF16).reshape(depth, d, MLA_HEADS, MLA_QH)
    pe = uq[..., MLA_NOPE:]
    partner = jnp.concatenate([jnp.zeros_like(uq[..., :MLA_NOPE]), pe[..., half:], pe[..., :half]], axis=-1)
    pad = lambda a: _pad_cols(a, 0, MLA_PAD).reshape(depth, d, MLA_HEADS * MLA_PAD)
    uq_p = jnp.concatenate([pad(uq), pad(partner)], axis=-1)
    r = w_ukv.shape[1]
    ukv = w_ukv.astype(BF16).reshape(depth, r, MLA_HEADS, MLA_NOPE + MLA_V)
    k_nope = _pad_cols(ukv[..., :MLA_NOPE], 0, MLA_PAD)
    ukv_p = jnp.concatenate([k_nope.reshape(depth, r, -1), ukv[..., MLA_NOPE:].reshape(depth, r, -1)], axis=-1)
    return uq_p, ukv_p


def _t5_bucket_const():
    n = jnp.arange(REL_TABLE)
    exact = REL_BUCKETS // 2
    nf = jnp.maximum(n, 1).astype(F32)
    large = exact + (jnp.log(nf / exact) / math.log(REL_MAX_DIST / exact) * (REL_BUCKETS - exact)).astype(jnp.int32)
    return jnp.where(n < exact, n, jnp.minimum(large, REL_BUCKETS - 1))


MIX_OUTS = ((3 * BRANCH_W, F32, 1.0, False),
            (SSD_P_WIDTH, F32, 1.0, False),
            (MLA_Q_RANK + MLA_KV_RANK + 2 * MLA_PAD, F32, 1.0, False),
            (BRANCH_W, BF16, DIFF_DH ** -0.5 * LOG2E, True),
            (BRANCH_W, BF16, 1.0, False),
            (BRANCH_W, BF16, 1.0, True))


def kernel(x, mem, positions, ffn1_norm, ffn1_w_gate, ffn1_w_up, ffn1_w_down, mix_norm, w_in, ssm_conv_w,
           ssm_conv_b, ssm_dt_bias, ssm_a_log, ssm_d, ssm_norm, mla_q_norm, mla_kv_norm, mla_w_uq, mla_w_ukv,
           diff_lambda, diff_norm, rel_bias, w_branch, w_out, xa_norm, mem_norm, xa_wq, xa_wk, xa_wv, xa_wo,
           ffn2_norm, ffn2_w_gate, ffn2_w_up, ffn2_w_down, final_norm):
    batch, seq, d = x.shape
    depth = w_in.shape[0]
    m = batch * seq
    h = x.reshape(m, d)
    mem2 = mem.reshape(batch * mem.shape[1], d)
    row = lambda a: a.reshape(1, -1)

    pos_col = positions.reshape(m, 1)
    ret_cos, ret_sin = _rope_tables(pos_col, *_rope_rows(RET_DK, RET_DK // 2, 0))
    mla_cos, mla_sin = _rope_tables(pos_col, *_rope_rows(MLA_PAD, MLA_ROPE // 2, MLA_NOPE))
    ftab = rel_bias[_t5_bucket_const()].T.reshape(DIFF_HEADS, 1, REL_TABLE)

    w_mix, w_gate = _w_in_relayout(w_in)
    w_uq_p, w_ukv_p = _mla_up_weights(mla_w_uq, mla_w_ukv)
    ffn1 = [_cast_bf16(w) for w in (ffn1_w_gate, ffn1_w_up, ffn1_w_down)]
    ffn2 = [_cast_bf16(w) for w in (ffn2_w_gate, ffn2_w_up, ffn2_w_down)]
    w_branch_b = _cast_bf16(w_branch.reshape(depth, N_BRANCH * BRANCH_W, d)).reshape(w_branch.shape)
    w_out_b, xa_wq_b, xa_wo_b = _cast_bf16(w_out), _cast_bf16(xa_wq), _cast_bf16(xa_wo)
    w_kv = jnp.concatenate([_cast_bf16(xa_wk), _cast_bf16(xa_wv)], axis=-1)

    for l in range(depth):
        at = lambda stack: _LayerOf(stack, l)
        h = _ffn(h, row(ffn1_norm[l]), *(at(w) for w in ffn1))
        ret_p, ssm_p, mla_p, dqt, dk, dvt = _norm_proj(h, row(mix_norm[l]), at(w_mix), MIX_OUTS, seq,
                                                       name="mix_proj")
        y_ret = _retention(ret_p.reshape(batch, seq, -1), ret_cos.reshape(batch, seq, -1),
                           ret_sin.reshape(batch, seq, -1)).reshape(m, BRANCH_W)
        y_ssm = _ssd(ssm_p, ssm_conv_w[l], row(ssm_conv_b[l]),
                     _pad_cols(row(ssm_dt_bias[l]), 0, SSD_DT_W), _pad_cols(row(ssm_a_log[l]), 0, SSD_DT_W),
                     row(jnp.repeat(ssm_d[l], SSM_HEADDIM)), row(ssm_norm[l]), batch, seq)
        mqt, mk, mvt = _mla_prep(mla_p, mla_cos, mla_sin, row(mla_q_norm[l]), row(mla_kv_norm[l]), at(w_uq_p),
                                 at(w_ukv_p), seq)
        y_mla = _mla_attention(mqt, mk.reshape(batch, seq, -1), mvt).reshape(m, BRANCH_W)
        lambda_init = 0.8 - 0.6 * math.exp(-0.3 * l)
        y_diff = _diff_attention(dqt, dk.reshape(batch, seq, -1), dvt, positions, ftab, diff_lambda[l],
                                 row(diff_norm[l]), lambda_init).reshape(m, BRANCH_W)
        h = _merge(h, row(mix_norm[l]), (y_ret, y_ssm, y_mla, y_diff), at(w_gate), at(w_branch_b), at(w_out_b))
        xk, xv = _norm_proj(mem2, row(mem_norm[l]), at(w_kv), [(d, BF16, 1.0, False)] * 2, mem.shape[1],
                            tm=mem.shape[1], name="mem_kv")
        h = _cross_attention(h, row(xa_norm[l]), at(xa_wq_b), xk, xv, at(xa_wo_b), batch, seq)
        h = _ffn(h, row(ffn2_norm[l]), *(at(w) for w in ffn2),
                 final_g=row(final_norm) if l == depth - 1 else None)
    return h.reshape(batch, seq, d)
```

```python
import functools
import math
from typing import NamedTuple

import jax
import jax.numpy as jnp
from jax import lax
from jax.experimental import pallas as pl
from jax.experimental.pallas import tpu as pltpu

F32 = jnp.float32
BF16 = jnp.bfloat16

EPS = 1e-6
NEG_INF = -1e30
LOG2E = math.log2(math.e)
ROPE_BASE = 10000.0
CHUNK = 128
BRANCH_W = 512
N_BRANCH = 4

RET_HEADS = 4
RET_DK = 64
RET_DV = 128
RET_QK = RET_HEADS * RET_DK

SSM_HEADDIM = 64
SSM_INNER = 512
SSM_HEADS = 8
SSM_GROUPS = 2
SSM_STATE = 128
SSM_CONV = 4
SSM_XBC = SSM_INNER + 2 * SSM_GROUPS * SSM_STATE
SSM_TAIL = 8
SSD_DT_W = 128
SSD_P_WIDTH = SSM_XBC + SSM_INNER + SSD_DT_W

MLA_HEADS = 4
MLA_Q_RANK = 256
MLA_KV_RANK = 128
MLA_NOPE = 64
MLA_ROPE = 32
MLA_V = 128
MLA_QH = MLA_NOPE + MLA_ROPE
MLA_PAD = 128
MLA_IN_W = MLA_Q_RANK + MLA_KV_RANK + 2 * MLA_PAD
MLA_OUTS = ((MLA_HEADS * MLA_PAD, jnp.bfloat16, 1.0, True), (MLA_HEADS * MLA_PAD, jnp.bfloat16, 1.0, False),
            (MLA_HEADS * MLA_V, jnp.bfloat16, 1.0, True))
HEAD_W = 128
ATTN_TQ = 512
ATTN_TK = 256

DIFF_HEADS = 4
DIFF_DH = 64

REL_BUCKETS = 32
REL_MAX_DIST = 128
REL_TABLE = 128

XA_HEADS = 4

LANES = 128
VMEM_LIMIT = 56 * 1024 * 1024

IN_SIZES = (RET_QK, RET_QK, BRANCH_W, BRANCH_W,
            SSM_INNER, SSM_XBC, SSM_HEADS,
            MLA_Q_RANK, MLA_KV_RANK, MLA_ROPE,
            BRANCH_W, BRANCH_W, BRANCH_W,
            N_BRANCH * 1024)


def _params(*sem):
    return pltpu.CompilerParams(dimension_semantics=sem, vmem_limit_bytes=VMEM_LIMIT)


class _LayerOf(NamedTuple):
    stack: jax.Array
    layer: int

    @property
    def shape(self):
        return self.stack.shape[1:]


def _operand(x):
    return x.stack if isinstance(x, _LayerOf) else x


def _resident(x):
    if isinstance(x, _LayerOf):
        shape, lead, first = x.shape, (None,), (x.layer,)
    else:
        shape, lead, first = (x if isinstance(x, tuple) else x.shape), (), ()
    nd = len(shape)
    return pl.BlockSpec(lead + tuple(shape), lambda *_: first + (0,) * nd, pipeline_mode=pl.Buffered(1))


def _cast_kernel(x_ref, o_ref):
    o_ref[...] = x_ref[...].astype(o_ref.dtype)


def _cast_bf16(w, *, row_blocks=4):
    depth, r, c = w.shape
    tr = r // row_blocks
    spec = pl.BlockSpec((None, tr, c), lambda l, i: (l, i, 0))
    return pl.pallas_call(
        _cast_kernel,
        grid=(depth, row_blocks),
        in_specs=[spec],
        out_specs=spec,
        out_shape=jax.ShapeDtypeStruct(w.shape, BF16),
        compiler_params=_params("parallel", "parallel"),
        name="cast_bf16",
    )(w)


def _rms(x, g):
    return x * lax.rsqrt(jnp.mean(x * x, axis=-1, keepdims=True) + EPS) * g


def _silu(x):
    return x * jax.nn.sigmoid(x)


def _dot(a, b):
    return jnp.dot(a, b, preferred_element_type=F32)


def _dot_nt(a, b):
    return lax.dot_general(a, b, (((1,), (1,)), ((), ())), preferred_element_type=F32)


def _dot_tn(a, b):
    return lax.dot_general(a, b, (((0,), (0,)), ((), ())), preferred_element_type=F32)


def _ffn_kernel(h_ref, g_ref, wg_ref, wu_ref, wd_ref, *rest, ff_chunk, final):
    if final:
        gf_ref, o_ref = rest
    else:
        (o_ref,) = rest
    h = h_ref[...]
    xn = _rms(h, g_ref[...]).astype(BF16)
    acc = jnp.zeros(h.shape, F32)
    d_ff = wg_ref.shape[1]
    for c0 in range(0, d_ff, ff_chunk):
        a = _dot(xn, wg_ref[:, c0:c0 + ff_chunk])
        b = _dot(xn, wu_ref[:, c0:c0 + ff_chunk])
        acc = acc + _dot((_silu(a) * b).astype(BF16), wd_ref[c0:c0 + ff_chunk, :])
    out = h + 0.5 * acc
    if final:
        out = _rms(out, gf_ref[...])
    o_ref[...] = out


def _ffn(h, g, wg, wu, wd, final_g=None, *, tm=512, ff_chunk=256):
    m, d = h.shape
    d_ff = wg.shape[1]
    final = final_g is not None
    in_specs = [pl.BlockSpec((tm, d), lambda i: (i, 0)), _resident((1, d)),
                _resident(wg), _resident(wu), _resident(wd)]
    args = [h, g, _operand(wg), _operand(wu), _operand(wd)]
    if final:
        in_specs.append(_resident((1, d)))
        args.append(final_g)
    return pl.pallas_call(
        functools.partial(_ffn_kernel, ff_chunk=ff_chunk, final=final),
        grid=(m // tm,),
        in_specs=in_specs,
        out_specs=pl.BlockSpec((tm, d), lambda i: (i, 0)),
        out_shape=jax.ShapeDtypeStruct((m, d), F32),
        compiler_params=_params("parallel"),
        name="ffn",
    )(*args)


def _mla_qkv(p, cos, sin, q_norm, kv_norm, w_uq, w_ukv):
    kvw = MLA_HEADS * MLA_PAD
    cq = p[:, :MLA_Q_RANK]
    ckv = p[:, MLA_Q_RANK:MLA_Q_RANK + MLA_KV_RANK]
    kr0 = MLA_Q_RANK + MLA_KV_RANK
    kr = p[:, kr0:kr0 + MLA_PAD]
    kr_partner = p[:, kr0 + MLA_PAD:]
    qq = _dot(_rms(cq, q_norm).astype(BF16), w_uq)
    cos4 = jnp.concatenate([cos] * MLA_HEADS, axis=-1)
    sin4 = jnp.concatenate([sin] * MLA_HEADS, axis=-1)
    q = (qq[:, :kvw] * cos4 + qq[:, kvw:] * sin4) * (MLA_QH ** -0.5 * LOG2E)
    kv = _dot(_rms(ckv, kv_norm).astype(BF16), w_ukv)
    kpe = kr * cos + kr_partner * sin
    k = kv[:, :kvw] + jnp.concatenate([kpe] * MLA_HEADS, axis=-1)
    return q.astype(BF16).T, k.astype(BF16), kv[:, kvw:].astype(BF16).T


def _norm_proj_kernel(x_ref, g_ref, w_ref, *rest, outs, col_chunk, mla):
    if mla:
        cos_ref, sin_ref, qn_ref, kvn_ref, wuq_ref, wukv_ref = rest[:6]
        o_refs = rest[6:]
    else:
        o_refs = rest
    xn = _rms(x_ref[...], g_ref[...]).astype(BF16)
    off = 0
    for o_ref, (width, _, scale, transposed) in zip(o_refs, outs):
        for c0 in range(0, width, col_chunk):
            c1 = min(c0 + col_chunk, width)
            y = _dot(xn, w_ref[:, off + c0:off + c1])
            if scale != 1.0:
                y = y * scale
            if transposed:
                o_ref[c0:c1, :] = y.astype(o_ref.dtype).T
            else:
                o_ref[:, c0:c1] = y.astype(o_ref.dtype)
        off += width
    if mla:
        q_ref, k_ref, v_ref = o_refs[len(outs):]
        p = _dot(xn, w_ref[:, off:])
        q_ref[...], k_ref[...], v_ref[...] = _mla_qkv(p, cos_ref[...], sin_ref[...], qn_ref[...], kvn_ref[...],
                                                      wuq_ref[...], wukv_ref[...])


def _norm_proj(x, g, w, outs, seq, *, mla=None, tm=512, col_chunk=512, name="norm_proj"):
    m, d = x.shape
    assert sum(o[0] for o in outs) + (MLA_IN_W if mla else 0) == w.shape[1]
    per = seq // tm
    rows = lambda width: pl.BlockSpec((tm, width), lambda i: (i, 0))
    cols = lambda width: pl.BlockSpec((None, width, tm), lambda i: (i // per, 0, i % per))
    out_specs, out_shape = [], []
    for width, dt, _, transposed in tuple(outs) + (MLA_OUTS if mla else ()):
        if transposed:
            out_specs.append(cols(width))
            out_shape.append(jax.ShapeDtypeStruct((m // seq, width, seq), dt))
        else:
            out_specs.append(rows(width))
            out_shape.append(jax.ShapeDtypeStruct((m, width), dt))
    in_specs = [rows(d), _resident((1, d)), _resident(w)]
    args = [x, g, _operand(w)]
    if mla:
        cos, sin, q_norm, kv_norm, w_uq, w_ukv = mla
        in_specs += [rows(LANES), rows(LANES), _resident(q_norm), _resident(kv_norm), _resident(w_uq),
                     _resident(w_ukv)]
        args += [cos, sin, q_norm, kv_norm, _operand(w_uq), _operand(w_ukv)]
    return pl.pallas_call(
        functools.partial(_norm_proj_kernel, outs=tuple(outs), col_chunk=col_chunk, mla=mla is not None),
        grid=(m // tm,),
        in_specs=in_specs,
        out_specs=out_specs,
        out_shape=out_shape,
        compiler_params=_params("parallel"),
        name=name,
    )(*args)


def _rope_table_kernel(pos_ref, inv_ref, sign_ref, cos_ref, sin_ref):
    ang = pos_ref[...].astype(F32) * inv_ref[...]
    cos_ref[...] = jnp.cos(ang)
    sin_ref[...] = jnp.sin(ang) * sign_ref[...]


def _rope_tables(pos_col, inv_row, sign_row, *, tm=1024):
    m = pos_col.shape[0]
    return pl.pallas_call(
        _rope_table_kernel,
        grid=(m // tm,),
        in_specs=[pl.BlockSpec((tm, 1), lambda i: (i, 0)), _resident((1, LANES)), _resident((1, LANES))],
        out_specs=[pl.BlockSpec((tm, LANES), lambda i: (i, 0))] * 2,
        out_shape=[jax.ShapeDtypeStruct((m, LANES), F32)] * 2,
        compiler_params=_params("parallel"),
        name="rope_tables",
    )(pos_col, inv_row, sign_row)


def _rope_rows(period, half, start):
    lane = jnp.arange(LANES) % period
    inside = (lane >= start) & (lane < start + 2 * half)
    idx = (lane - start) % half
    inv = jnp.exp(-math.log(ROPE_BASE) * idx.astype(F32) / half)
    inv = jnp.where(inside, inv, 0.0)
    sign = jnp.where(inside, jnp.where(lane < start + half, -1.0, 1.0), 0.0)
    return inv.reshape(1, LANES).astype(F32), sign.reshape(1, LANES).astype(F32)


def _apply_rope(x, cos, sin_signed, period, half, start):
    width = x.shape[-1]
    reps = width // LANES
    if reps > 1:
        cos = jnp.concatenate([cos] * reps, axis=-1)
        sin_signed = jnp.concatenate([sin_signed] * reps, axis=-1)
    lane = lax.broadcasted_iota(jnp.int32, (1, width), 1) % period
    first = lane < start + half
    partner = jnp.where(first, pltpu.roll(x, width - half, 1), pltpu.roll(x, half, 1))
    return x * cos + partner * sin_signed


def _ret_kernel(q_ref, k_ref, v_ref, g_ref, cos_ref, sin_ref, intra_ref, qdec_ref, kdec_ref,
                sdec_ref, bmask_ref, o_ref, state_sc):
    @pl.when(pl.program_id(0) == 0)
    def _():
        state_sc[...] = jnp.zeros_like(state_sc)

    head = lax.broadcasted_iota(jnp.int32, (1, RET_QK), 1) // RET_DK
    for b in range(q_ref.shape[0]):
        cos, sin = cos_ref[b], sin_ref[b]
        q = _apply_rope(q_ref[b], cos, sin, RET_DK, RET_DK // 2, 0) * (RET_DK ** -0.5)
        k = _apply_rope(k_ref[b], cos, sin, RET_DK, RET_DK // 2, 0)
        vb = v_ref[b].astype(BF16)
        kb = k.astype(BF16)
        state = state_sc[b]
        cross = _dot((q * qdec_ref[...]).astype(BF16), state.astype(BF16))
        outs = []
        for h in range(RET_HEADS):
            qh = jnp.where(head == h, q, 0.0).astype(BF16)
            att = _dot_nt(qh, kb) * intra_ref[h]
            sl = slice(h * RET_DV, (h + 1) * RET_DV)
            o = _dot(att.astype(BF16), vb[:, sl]) + cross[:, sl]
            mu = jnp.mean(o, axis=-1, keepdims=True)
            oc = o - mu
            outs.append(oc * lax.rsqrt(jnp.mean(oc * oc, axis=-1, keepdims=True) + EPS))
        o_ref[b] = (jnp.concatenate(outs, axis=-1) * _silu(g_ref[b])).astype(o_ref.dtype)
        kv = _dot_tn((k * kdec_ref[...]).astype(BF16), vb)
        state_sc[b] = state * sdec_ref[...] + kv * bmask_ref[...]


def _retention_consts():
    c = CHUNK
    log_g = jnp.log1p(-jnp.exp2(-5.0 - jnp.arange(RET_HEADS, dtype=F32)))
    i = jnp.arange(c, dtype=F32)
    rel = i[:, None] - i[None, :]
    intra = jnp.where(rel >= 0, jnp.exp(log_g[:, None, None] * jnp.maximum(rel, 0.0)), 0.0)
    lg_lane = jnp.repeat(log_g, RET_DK)
    qdec = jnp.exp(lg_lane[None, :] * (i[:, None] + 1.0))
    kdec = jnp.exp(lg_lane[None, :] * (c - 1.0 - i[:, None]))
    row_h = jnp.repeat(jnp.arange(RET_HEADS), RET_DK)
    col_h = jnp.repeat(jnp.arange(RET_HEADS), RET_DV)
    bmask = (row_h[:, None] == col_h[None, :]).astype(F32)
    sdec = bmask * jnp.exp(lg_lane * c)[:, None]
    return intra, qdec, kdec, sdec, bmask


def _retention(ret_p, cos, sin):
    c = CHUNK
    batch, seq, _ = ret_p.shape
    intra, qdec, kdec, sdec, bmask = _retention_consts()
    blk = lambda width, col: pl.BlockSpec((batch, c, width), lambda j: (0, j, col))
    return pl.pallas_call(
        _ret_kernel,
        grid=(seq // c,),
        in_specs=[blk(RET_QK, 0), blk(RET_QK, 1), blk(BRANCH_W, 1), blk(BRANCH_W, 2), blk(LANES, 0), blk(LANES, 0),
                  _resident(intra.shape), _resident(qdec.shape), _resident(kdec.shape),
                  _resident(sdec.shape), _resident(bmask.shape)],
        out_specs=blk(BRANCH_W, 0),
        out_shape=jax.ShapeDtypeStruct((batch, seq, BRANCH_W), BF16),
        scratch_shapes=[pltpu.VMEM((batch, RET_QK, BRANCH_W), F32)],
        compiler_params=_params("arbitrary"),
        name="retention",
    )(ret_p, ret_p, ret_p, ret_p, cos, sin, intra, qdec, kdec, sdec, bmask)


def _split3(x):
    hi = x.astype(BF16)
    r = x - hi.astype(F32)
    mid = r.astype(BF16)
    lo = (r - mid.astype(F32)).astype(BF16)
    return hi, mid, lo


def _dot_split3(a, b, *, left):
    if left:
        return sum(_dot(piece, b) for piece in _split3(a))
    return sum(_dot(a, piece) for piece in _split3(b))


def _ssd_kernel(xbc_ref, z_ref, dt_ref, cw_ref, cb_ref, dtb_ref, alog_ref, dskip_ref, ng_ref, tri_ref,
                expand_ref, o_ref, xe_sc, state_sc):
    L = CHUNK
    T = SSM_TAIL
    GW = SSM_INNER // SSM_GROUPS

    @pl.when(pl.program_id(1) == 0)
    def _():
        xe_sc[...] = jnp.zeros_like(xe_sc)
        state_sc[...] = jnp.zeros_like(state_sc)

    x_raw = xbc_ref[...]
    tail = xe_sc[...]
    rows8 = lax.broadcasted_iota(jnp.int32, (T, 1), 0)
    conv = cb_ref[...] + cw_ref[SSM_CONV - 1:SSM_CONV, :] * x_raw
    for s in range(1, SSM_CONV):
        shifted = pltpu.roll(x_raw, s, 0)
        head = jnp.where(rows8 < s, pltpu.roll(tail, s, 0), shifted[0:T, :])
        shifted = jnp.concatenate([head, shifted[T:, :]], axis=0)
        conv = conv + cw_ref[SSM_CONV - 1 - s:SSM_CONV - s, :] * shifted
    xe_sc[...] = x_raw[L - T:L, :]
    xa = _silu(conv)
    xs = xa[:, :SSM_INNER]
    bm = xa[:, SSM_INNER:SSM_INNER + SSM_GROUPS * SSM_STATE]
    cm = xa[:, SSM_INNER + SSM_GROUPS * SSM_STATE:]

    dtr = dt_ref[...] + dtb_ref[...]
    dt_h = jnp.maximum(dtr, 0.0) + jnp.log1p(jnp.exp(-jnp.abs(dtr)))
    da = dt_h * (-jnp.exp(alog_ref[...]))
    cs_h = _dot_split3(tri_ref[...], da, left=False)
    cs_h_last = cs_h[L - 1:L, :]
    cs_t = cs_h.T
    expand = expand_ref[...]
    dt = _dot_split3(dt_h, expand, left=True)
    ecs = _dot_split3(jnp.exp(cs_h), expand, left=True)
    eds = _dot_split3(jnp.exp(cs_h_last - cs_h), expand, left=True)
    xdt = xs * dt
    xdec = (xdt * eds).astype(BF16)
    state = state_sc[...]
    li = lax.broadcasted_iota(jnp.int32, (L, L), 0)
    si = lax.broadcasted_iota(jnp.int32, (L, L), 1)
    causal = li >= si
    lane_half = lax.broadcasted_iota(jnp.int32, (1, LANES), 1) // SSM_HEADDIM
    ys = []
    new_states = []
    for g in range(SSM_GROUPS):
        bg = bm[:, g * SSM_STATE:(g + 1) * SSM_STATE].astype(BF16)
        cg = cm[:, g * SSM_STATE:(g + 1) * SSM_STATE].astype(BF16)
        cb = _dot_nt(cg, bg)
        gsl = slice(g * GW, (g + 1) * GW)
        y_off = _dot(cg, state[:, gsl].astype(BF16)) * ecs[:, gsl]
        for p in range(GW // LANES):
            sl = slice(g * GW + p * LANES, g * GW + (p + 1) * LANES)
            xdt_blk = xdt[:, sl]
            acc = y_off[:, p * LANES:(p + 1) * LANES]
            for hh in range(LANES // SSM_HEADDIM):
                head = (g * GW + p * LANES) // SSM_HEADDIM + hh
                col = cs_h[:, head:head + 1]
                rowv = cs_t[head:head + 1, :]
                dec = jnp.where(causal, jnp.exp(jnp.minimum(col - rowv, 0.0)), 0.0)
                xm = jnp.where(lane_half == hh, xdt_blk, 0.0).astype(BF16)
                acc = acc + _dot((cb * dec).astype(BF16), xm)
            ys.append(acc)
        new_states.append(_dot_tn(bg, xdec[:, gsl]))
    y = jnp.concatenate(ys, axis=-1) + xs * dskip_ref[...]
    y = y * _silu(z_ref[...])
    o_ref[...] = _rms(y, ng_ref[...]).astype(o_ref.dtype)
    state_sc[...] = state * ecs[L - 1:L, :] + jnp.concatenate(new_states, axis=-1)


def _ssd(ssm_p, conv_w, conv_b, dt_bias_pad, a_log_pad, d_full, norm_g, batch, seq):
    L = CHUNK
    n = seq // L
    tri = (jnp.arange(L)[:, None] >= jnp.arange(L)[None, :]).astype(BF16)
    expand = (jnp.arange(SSD_DT_W)[:, None] == jnp.arange(SSM_INNER)[None, :] // SSM_HEADDIM).astype(BF16)
    row = lambda b, j: b * n + j
    return pl.pallas_call(
        _ssd_kernel,
        grid=(batch, n),
        in_specs=[pl.BlockSpec((L, SSM_XBC), lambda b, j: (row(b, j), 0)),
                  pl.BlockSpec((L, SSM_INNER), lambda b, j: (row(b, j), SSM_XBC // SSM_INNER)),
                  pl.BlockSpec((L, SSD_DT_W), lambda b, j: (row(b, j), (SSM_XBC + SSM_INNER) // SSD_DT_W)),
                  _resident(conv_w.shape), _resident(conv_b.shape), _resident(dt_bias_pad.shape),
                  _resident(a_log_pad.shape), _resident(d_full.shape), _resident(norm_g.shape),
                  _resident(tri.shape), _resident(expand.shape)],
        out_specs=pl.BlockSpec((L, SSM_INNER), lambda b, j: (row(b, j), 0)),
        out_shape=jax.ShapeDtypeStruct((batch * seq, SSM_INNER), BF16),
        scratch_shapes=[pltpu.VMEM((SSM_TAIL, SSM_XBC), F32),
                        pltpu.VMEM((SSM_STATE, SSM_INNER), F32)],
        compiler_params=_params("parallel", "arbitrary"),
        name="ssd",
    )(ssm_p, ssm_p, ssm_p, conv_w, conv_b, dt_bias_pad, a_log_pad, d_full, norm_g, tri, expand)


def _flash_kernel(*refs, tq, tk, n_maps, lambda_init):
    biased = n_maps == 2
    if biased:
        (pmin_ref, pmax_ref, qt_ref, k_ref, vt_ref, pq_ref, pk_ref, ftab_ref, lam_ref, sg_ref,
         o_ref, m_sc, l_sc, acc_sc, s0_sc, s1_sc) = refs
    else:
        qt_ref, k_ref, vt_ref, o_ref, m_sc, l_sc, acc_sc, s0_sc, s1_sc = refs
    bi = pl.program_id(0)
    qi = pl.program_id(1)
    hw = HEAD_W
    heads = k_ref.shape[-1] // hw
    cols = n_maps * tq
    hs = lambda h: slice(h * hw, (h + 1) * hw)
    ws = []
    for h in range(heads):
        qt = qt_ref[hs(h), :]
        if biased:
            half = lax.broadcasted_iota(jnp.int32, (hw, 1), 0) // DIFF_DH
            zero = jnp.zeros_like(qt)
            qt = jnp.concatenate([jnp.where(half == 0, qt, zero), jnp.where(half == 1, qt, zero)], axis=1)
        ws.append(qt)
    if biased:
        pq = pq_ref[...]
        ftabs = [jnp.broadcast_to(ftab_ref[h] * LOG2E, (tk, REL_TABLE)) for h in range(heads)]
    m_sc[...] = jnp.full(m_sc.shape, NEG_INF, F32)
    l_sc[...] = jnp.zeros_like(l_sc)
    acc_sc[...] = jnp.zeros_like(acc_sc)

    def near_biases(off, q_lo=0):
        idx = jnp.clip(pq[:, q_lo:] - pk_ref[pl.ds(off, tk), :], 0, REL_TABLE - 1)
        tiles = [jnp.concatenate([jnp.take_along_axis(ftabs[h], idx[:, c:c + LANES], axis=1)
                                  for c in range(0, tq - q_lo, LANES)], axis=1) for h in range(heads)]

        def bias(h, cs):
            base = (cs.start // tq) * tq + q_lo
            return tiles[h][:, cs.start - base:cs.stop - base]
        return bias

    s_scs = (s0_sc, s1_sc)

    all_cols = tuple(slice(mi * tq, (mi + 1) * tq) for mi in range(n_maps))
    late_cols = tuple(slice(mi * tq + tk, (mi + 1) * tq) for mi in range(n_maps))

    def scores(off, slot, col_ranges=all_cols):
        for h in range(heads):
            for cs in col_ranges:
                s_scs[slot][h, :, cs] = _dot(k_ref[pl.ds(off, tk), hs(h)], ws[h][:, cs])

    def consume(off, slot, bias, mask, const=None, col_ranges=all_cols):
        for h in range(heads):
            for cs in col_ranges:
                s = s_scs[slot][h, :, cs]
                if bias is not None:
                    s = s + bias(h, cs)
                if mask is not None:
                    s = jnp.where(mask[:, cs], s, NEG_INF)
                m = m_sc[h, :, cs]
                m_tile = jnp.max(s, axis=0, keepdims=True)
                if const is not None:
                    m_tile = m_tile + const[h]
                m_new = jnp.maximum(m, m_tile)
                alpha = jnp.exp2(m - m_new)
                p = jnp.exp2(s - (m_new if const is None else m_new - const[h]))
                l_sc[h, :, cs] = alpha * l_sc[h, :, cs] + jnp.sum(p, axis=0, keepdims=True)
                acc_sc[h, :, cs] = (alpha * acc_sc[h, :, cs]
                                    + _dot(vt_ref[hs(h), pl.ds(off, tk)], p.astype(BF16)))
                m_sc[h, :, cs] = m_new

    assert tq == 2 * tk
    scores(0, 0)

    def body(jj, carry):
        off0 = pl.multiple_of(jj * tq, tq)
        off1 = pl.multiple_of(off0 + tk, tk)
        off2 = pl.multiple_of(off0 + tq, tq)

        def pair(near0, near1):
            const = [ftab_ref[h][:, REL_TABLE - 1:REL_TABLE] * LOG2E for h in range(heads)] if biased else None
            scores(off1, 1)
            consume(off0, 0, near_biases(off0) if near0 else None, None, None if near0 else const)
            scores(off2, 0)
            consume(off1, 1, near_biases(off1) if near1 else None, None, None if near1 else const)

        if biased:
            far0 = pmin_ref[bi, qi] - pmax_ref[bi, 2 * jj] >= REL_TABLE - 1
            far1 = pmin_ref[bi, qi] - pmax_ref[bi, 2 * jj + 1] >= REL_TABLE - 1

            @pl.when(far0 & far1)
            def _():
                pair(False, False)

            @pl.when(far0 & jnp.logical_not(far1))
            def _():
                pair(False, True)

            @pl.when(jnp.logical_not(far0))
            def _():
                pair(True, True)
        else:
            pair(False, False)
        return carry

    lax.fori_loop(0, qi, body, 0)
    for d in range(2):
        off = pl.multiple_of(qi * tq + d * tk, tk)
        if d == 0:
            scores(pl.multiple_of(off + tk, tk), 1, late_cols)
        key = lax.broadcasted_iota(jnp.int32, (tk, cols), 0) + d * tk
        qry = lax.broadcasted_iota(jnp.int32, (tk, cols), 1) % tq
        consume(off, d, near_biases(off, d * tk) if biased else None, key <= qry,
                col_ranges=all_cols if d == 0 else late_cols)

    if biased:
        lp = lam_ref[...]
        lam = (jnp.exp(jnp.sum(lp[0:1] * lp[1:2], axis=-1, keepdims=True))
               - jnp.exp(jnp.sum(lp[2:3] * lp[3:4], axis=-1, keepdims=True)) + lambda_init)
    for h in range(heads):
        o = acc_sc[h] / l_sc[h]
        if biased:
            o = (o[:, :tq] - lam * o[:, tq:]).T
            o_ref[:, hs(h)] = (_rms(o, sg_ref[...]) * (1.0 - lambda_init)).astype(o_ref.dtype)
        else:
            o_ref[:, hs(h)] = o.astype(o_ref.dtype).T


def _flash_scratch(heads, tk, cols):
    return [pltpu.VMEM((heads, 1, cols), F32), pltpu.VMEM((heads, 1, cols), F32),
            pltpu.VMEM((heads, HEAD_W, cols), F32),
            pltpu.VMEM((heads, tk, cols), F32), pltpu.VMEM((heads, tk, cols), F32)]


def _mla_attention(qt, k, vt, *, tq=ATTN_TQ, tk=ATTN_TK):
    b, s, w = k.shape
    return pl.pallas_call(
        functools.partial(_flash_kernel, tq=tq, tk=tk, n_maps=1, lambda_init=None),
        grid=(b, s // tq),
        in_specs=[pl.BlockSpec((None, w, tq), lambda bi, i: (bi, 0, i)),
                  pl.BlockSpec((None, s, w), lambda bi, i: (bi, 0, 0), pipeline_mode=pl.Buffered(1)),
                  pl.BlockSpec((None, w, s), lambda bi, i: (bi, 0, 0), pipeline_mode=pl.Buffered(1))],
        out_specs=pl.BlockSpec((None, tq, w), lambda bi, i: (bi, i, 0)),
        out_shape=jax.ShapeDtypeStruct((b, s, w), BF16),
        scratch_shapes=_flash_scratch(w // HEAD_W, tk, tq),
        compiler_params=_params("parallel", "arbitrary"),
        name="mla_attention",
    )(qt, k, vt)


def _diff_attention(qt, k, vt, pos, ftab, lam_params, sub_g, lambda_init, *, tq=ATTN_TQ, tk=ATTN_TK):
    b, s, w = k.shape
    pmin = jnp.min(pos.reshape(b, s // tq, tq), axis=-1)
    pmax = jnp.max(pos.reshape(b, s // tk, tk), axis=-1)
    pos_col = pos.reshape(b, s, 1)
    pos_row = pos.reshape(b, 1, s)
    grid_spec = pltpu.PrefetchScalarGridSpec(
        num_scalar_prefetch=2,
        grid=(b, s // tq),
        in_specs=[pl.BlockSpec((None, w, tq), lambda bi, i, *_: (bi, 0, i)),
                  pl.BlockSpec((None, s, w), lambda bi, i, *_: (bi, 0, 0), pipeline_mode=pl.Buffered(1)),
                  pl.BlockSpec((None, w, s), lambda bi, i, *_: (bi, 0, 0), pipeline_mode=pl.Buffered(1)),
                  pl.BlockSpec((None, 1, tq), lambda bi, i, *_: (bi, 0, i)),
                  pl.BlockSpec((None, s, 1), lambda bi, i, *_: (bi, 0, 0), pipeline_mode=pl.Buffered(1)),
                  pl.BlockSpec(ftab.shape, lambda bi, i, *_: (0, 0, 0)),
                  pl.BlockSpec(lam_params.shape, lambda bi, i, *_: (0, 0)),
                  pl.BlockSpec(sub_g.shape, lambda bi, i, *_: (0, 0))],
        out_specs=pl.BlockSpec((None, tq, w), lambda bi, i, *_: (bi, i, 0)),
        scratch_shapes=_flash_scratch(w // HEAD_W, tk, 2 * tq),
    )
    return pl.pallas_call(
        functools.partial(_flash_kernel, tq=tq, tk=tk, n_maps=2, lambda_init=lambda_init),
        grid_spec=grid_spec,
        out_shape=jax.ShapeDtypeStruct((b, s, w), BF16),
        compiler_params=_params("parallel", "arbitrary"),
        name="diff_attention",
    )(pmin, pmax, qt, k, vt, pos_row, pos_col, ftab, lam_params, sub_g)


def _merge_kernel(h_ref, g_ref, y0_ref, y1_ref, y2_ref, y3_ref, wg_ref, wb_ref, wo_ref, o_ref, *, col_chunk):
    h = h_ref[...]
    d = h.shape[-1]
    u = _rms(h, g_ref[...]).astype(BF16)
    ys = (y0_ref[...], y1_ref[...], y2_ref[...], y3_ref[...])
    parts = []
    for c0 in range(0, d, col_chunk):
        acc = None
        for i, y in enumerate(ys):
            gate = jax.nn.sigmoid(_dot(u, wg_ref[:, i * d + c0:i * d + c0 + col_chunk]))
            term = gate * _dot(y, wb_ref[i, :, c0:c0 + col_chunk])
            acc = term if acc is None else acc + term
        parts.append(acc.astype(BF16))
    merged = jnp.concatenate(parts, axis=-1)
    o_ref[...] = h + _dot(merged, wo_ref[...])


def _merge(h, g, ys, w_gate, w_branch, w_out, *, tm=1024, col_chunk=256):
    m, d = h.shape
    rows = lambda width: pl.BlockSpec((tm, width), lambda i: (i, 0))
    return pl.pallas_call(
        functools.partial(_merge_kernel, col_chunk=col_chunk),
        grid=(m // tm,),
        in_specs=[rows(d), _resident((1, d))] + [rows(BRANCH_W)] * N_BRANCH
                 + [_resident(w_gate), _resident(w_branch), _resident(w_out)],
        out_specs=rows(d),
        out_shape=jax.ShapeDtypeStruct((m, d), F32),
        compiler_params=_params("parallel"),
        name="merge",
    )(h, g, *ys, _operand(w_gate), _operand(w_branch), _operand(w_out))


def _xattn_kernel(h_ref, g_ref, wq_ref, k_ref, v_ref, wo_ref, o_ref):
    h = h_ref[...]
    d = h.shape[-1]
    dh = d // XA_HEADS
    u = _rms(h, g_ref[...]).astype(BF16)
    q = (_dot(u, wq_ref[...]) * (dh ** -0.5 * LOG2E)).astype(BF16)
    out = h
    for hd in range(XA_HEADS):
        sl = slice(hd * dh, (hd + 1) * dh)
        s = _dot_nt(q[:, sl], k_ref[:, sl])
        p = jnp.exp2(s - jnp.max(s, axis=-1, keepdims=True))
        inv = 1.0 / jnp.sum(p, axis=-1, keepdims=True)
        o = _dot(p.astype(BF16), v_ref[:, sl]) * inv
        out = out + _dot(o.astype(BF16), wo_ref[sl, :])
    o_ref[...] = out


def _cross_attention(h, g, wq, k, v, wo, batch, seq, *, tm=1024):
    m, d = h.shape
    mem_len = k.shape[0] // batch
    per = seq // tm
    return pl.pallas_call(
        _xattn_kernel,
        grid=(m // tm,),
        in_specs=[pl.BlockSpec((tm, d), lambda i: (i, 0)), _resident((1, d)), _resident(wq),
                  pl.BlockSpec((mem_len, d), lambda i: (i // per, 0)),
                  pl.BlockSpec((mem_len, d), lambda i: (i // per, 0)),
                  _resident(wo)],
        out_specs=pl.BlockSpec((tm, d), lambda i: (i, 0)),
        out_shape=jax.ShapeDtypeStruct((m, d), F32),
        compiler_params=_params("parallel"),
        name="cross_attention",
    )(h, g, _operand(wq), k, v, _operand(wo))


def _pad_cols(w, left, total):
    return jnp.pad(w, ((0, 0),) * (w.ndim - 1) + ((left, total - left - w.shape[-1]),))


def _w_in_plan():
    names = ("rq", "rk", "rv", "rg", "sz", "sxbc", "sdt", "cq", "ckv", "kr", "dq", "dk", "dv", "gl")
    src, start = {}, 0
    for n, s in zip(names, IN_SIZES):
        src[n] = (start, s)
        start += s
    half = MLA_ROPE // 2
    kr0 = src["kr"][0]
    moves, zeros, dst = [], [], 0
    for n in ("rq", "rk", "rv", "rg", "sxbc", "sz"):
        moves.append((*src[n], dst))
        dst += src[n][1]
    zeros.append((dst, SSD_DT_W))
    moves.append((*src["sdt"], dst))
    dst += SSD_DT_W
    for n in ("dq", "dk", "dv", "cq", "ckv"):
        moves.append((*src[n], dst))
        dst += src[n][1]
    zeros.append((dst, 2 * MLA_PAD))
    moves.append((kr0, MLA_ROPE, dst + MLA_NOPE))
    moves.append((kr0 + half, half, dst + MLA_PAD + MLA_NOPE))
    moves.append((kr0, half, dst + MLA_PAD + MLA_NOPE + half))
    dst += 2 * MLA_PAD
    return moves, zeros, src["gl"], dst


def _w_in_relayout_kernel(x_ref, mix_ref, gate_ref, *, moves, zeros, gate):
    for d0, width in zeros:
        mix_ref[:, d0:d0 + width] = jnp.zeros((mix_ref.shape[0], width), mix_ref.dtype)
    for s0, width, d0 in moves:
        mix_ref[:, d0:d0 + width] = x_ref[:, s0:s0 + width].astype(mix_ref.dtype)
    gate_ref[...] = x_ref[:, gate[0]:gate[0] + gate[1]].astype(gate_ref.dtype)


def _w_in_relayout(w_in, *, row_blocks=4):
    depth, r, c = w_in.shape
    moves, zeros, gate, mix_w = _w_in_plan()
    tr = r // row_blocks
    blk = lambda width: pl.BlockSpec((None, tr, width), lambda l, i: (l, i, 0))
    return pl.pallas_call(
        functools.partial(_w_in_relayout_kernel, moves=moves, zeros=zeros, gate=gate),
        grid=(depth, row_blocks),
        in_specs=[blk(c)],
        out_specs=[blk(mix_w), blk(gate[1])],
        out_shape=[jax.ShapeDtypeStruct((depth, r, mix_w), BF16), jax.ShapeDtypeStruct((depth, r, gate[1]), BF16)],
        compiler_params=_params("parallel", "parallel"),
        name="w_in_relayout",
    )(w_in)


def _mla_up_weights(w_uq, w_ukv):
    depth, d, _ = w_uq.shape
    half = MLA_ROPE // 2
    uq = w_uq.astype(BF16).reshape(depth, d, MLA_HEADS, MLA_QH)
    pe = uq[..., MLA_NOPE:]
    partner = jnp.concatenate([jnp.zeros_like(uq[..., :MLA_NOPE]), pe[..., half:], pe[..., :half]], axis=-1)
    pad = lambda a: _pad_cols(a, 0, MLA_PAD).reshape(depth, d, MLA_HEADS * MLA_PAD)
    uq_p = jnp.concatenate([pad(uq), pad(partner)], axis=-1)
    r = w_ukv.shape[1]
    ukv = w_ukv.astype(BF16).reshape(depth, r, MLA_HEADS, MLA_NOPE + MLA_V)
    k_nope = _pad_cols(ukv[..., :MLA_NOPE], 0, MLA_PAD)
    ukv_p = jnp.concatenate([k_nope.reshape(depth, r, -1), ukv[..., MLA_NOPE:].reshape(depth, r, -1)], axis=-1)
    return uq_p, ukv_p


def _t5_bucket_const():
    n = jnp.arange(REL_TABLE)
    exact = REL_BUCKETS // 2
    nf = jnp.maximum(n, 1).astype(F32)
    large = exact + (jnp.log(nf / exact) / math.log(REL_MAX_DIST / exact) * (REL_BUCKETS - exact)).astype(jnp.int32)
    return jnp.where(n < exact, n, jnp.minimum(large, REL_BUCKETS - 1))


MIX_OUTS = ((3 * BRANCH_W, F32, 1.0, False),
            (SSD_P_WIDTH, F32, 1.0, False),
            (BRANCH_W, BF16, DIFF_DH ** -0.5 * LOG2E, True),
            (BRANCH_W, BF16, 1.0, False),
            (BRANCH_W, BF16, 1.0, True))


def kernel(x, mem, positions, ffn1_norm, ffn1_w_gate, ffn1_w_up, ffn1_w_down, mix_norm, w_in, ssm_conv_w,
           ssm_conv_b, ssm_dt_bias, ssm_a_log, ssm_d, ssm_norm, mla_q_norm, mla_kv_norm, mla_w_uq, mla_w_ukv,
           diff_lambda, diff_norm, rel_bias, w_branch, w_out, xa_norm, mem_norm, xa_wq, xa_wk, xa_wv, xa_wo,
           ffn2_norm, ffn2_w_gate, ffn2_w_up, ffn2_w_down, final_norm):
    batch, seq, d = x.shape
    depth = w_in.shape[0]
    m = batch * seq
    h = x.reshape(m, d)
    mem2 = mem.reshape(batch * mem.shape[1], d)
    row = lambda a: a.reshape(1, -1)

    pos_col = positions.reshape(m, 1)
    ret_cos, ret_sin = _rope_tables(pos_col, *_rope_rows(RET_DK, RET_DK // 2, 0))
    mla_cos, mla_sin = _rope_tables(pos_col, *_rope_rows(MLA_PAD, MLA_ROPE // 2, MLA_NOPE))
    ftab = rel_bias[_t5_bucket_const()].T.reshape(DIFF_HEADS, 1, REL_TABLE)

    w_mix, w_gate = _w_in_relayout(w_in)
    w_uq_p, w_ukv_p = _mla_up_weights(mla_w_uq, mla_w_ukv)
    ffn1 = [_cast_bf16(w) for w in (ffn1_w_gate, ffn1_w_up, ffn1_w_down)]
    ffn2 = [_cast_bf16(w) for w in (ffn2_w_gate, ffn2_w_up, ffn2_w_down)]
    w_branch_b = _cast_bf16(w_branch.reshape(depth, N_BRANCH * BRANCH_W, d)).reshape(w_branch.shape)
    w_out_b, xa_wq_b, xa_wo_b = _cast_bf16(w_out), _cast_bf16(xa_wq), _cast_bf16(xa_wo)
    w_kv = jnp.concatenate([_cast_bf16(xa_wk), _cast_bf16(xa_wv)], axis=-1)

    for l in range(depth):
        at = lambda stack: _LayerOf(stack, l)
        h = _ffn(h, row(ffn1_norm[l]), *(at(w) for w in ffn1))
        mla = (mla_cos, mla_sin, row(mla_q_norm[l]), row(mla_kv_norm[l]), at(w_uq_p), at(w_ukv_p))
        ret_p, ssm_p, dqt, dk, dvt, mqt, mk, mvt = _norm_proj(h, row(mix_norm[l]), at(w_mix), MIX_OUTS, seq,
                                                             mla=mla, name="mix_proj")
        y_ret = _retention(ret_p.reshape(batch, seq, -1), ret_cos.reshape(batch, seq, -1),
                           ret_sin.reshape(batch, seq, -1)).reshape(m, BRANCH_W)
        y_ssm = _ssd(ssm_p, ssm_conv_w[l], row(ssm_conv_b[l]),
                     _pad_cols(row(ssm_dt_bias[l]), 0, SSD_DT_W), _pad_cols(row(ssm_a_log[l]), 0, SSD_DT_W),
                     row(jnp.repeat(ssm_d[l], SSM_HEADDIM)), row(ssm_norm[l]), batch, seq)
        y_mla = _mla_attention(mqt, mk.reshape(batch, seq, -1), mvt).reshape(m, BRANCH_W)
        lambda_init = 0.8 - 0.6 * math.exp(-0.3 * l)
        y_diff = _diff_attention(dqt, dk.reshape(batch, seq, -1), dvt, positions, ftab, diff_lambda[l],
                                 row(diff_norm[l]), lambda_init).reshape(m, BRANCH_W)
        h = _merge(h, row(mix_norm[l]), (y_ret, y_ssm, y_mla, y_diff), at(w_gate), at(w_branch_b), at(w_out_b))
        xk, xv = _norm_proj(mem2, row(mem_norm[l]), at(w_kv), [(d, BF16, 1.0, False)] * 2, mem.shape[1],
                            tm=mem.shape[1], name="mem_kv")
        h = _cross_attention(h, row(xa_norm[l]), at(xa_wq_b), xk, xv, at(xa_wo_b), batch, seq)
        h = _ffn(h, row(ffn2_norm[l]), *(at(w) for w in ffn2),
                 final_g=row(final_norm) if l == depth - 1 else None)
    return h.reshape(batch, seq, d)
```

```python
import functools
import math
from typing import NamedTuple

import jax
import jax.numpy as jnp
from jax import lax
from jax.experimental import pallas as pl
from jax.experimental.pallas import tpu as pltpu

F32 = jnp.float32
BF16 = jnp.bfloat16

EPS = 1e-6
NEG_INF = -1e30
LOG2E = math.log2(math.e)
ROPE_BASE = 10000.0
CHUNK = 128
BRANCH_W = 512
N_BRANCH = 4

RET_HEADS = 4
RET_DK = 64
RET_DV = 128
RET_QK = RET_HEADS * RET_DK

SSM_HEADDIM = 64
SSM_INNER = 512
SSM_HEADS = 8
SSM_GROUPS = 2
SSM_STATE = 128
SSM_CONV = 4
SSM_XBC = SSM_INNER + 2 * SSM_GROUPS * SSM_STATE
SSM_TAIL = 8
SSD_DT_W = 128

MLA_HEADS = 4
MLA_Q_RANK = 256
MLA_KV_RANK = 128
MLA_NOPE = 64
MLA_ROPE = 32
MLA_V = 128
MLA_QH = MLA_NOPE + MLA_ROPE
MLA_PAD = 128
MLA_IN_W = MLA_Q_RANK + MLA_KV_RANK + 2 * MLA_PAD
MLA_OUTS = ((MLA_HEADS * MLA_PAD, jnp.bfloat16, 1.0, True), (MLA_HEADS * MLA_PAD, jnp.bfloat16, 1.0, False),
            (MLA_HEADS * MLA_V, jnp.bfloat16, 1.0, True))
HEAD_W = 128
ATTN_TQ = 512
ATTN_TK = 256

DIFF_HEADS = 4
DIFF_DH = 64

REL_BUCKETS = 32
REL_MAX_DIST = 128
REL_TABLE = 128

XA_HEADS = 4

LANES = 128
VMEM_LIMIT = 56 * 1024 * 1024

IN_SIZES = (RET_QK, RET_QK, BRANCH_W, BRANCH_W,
            SSM_INNER, SSM_XBC, SSM_HEADS,
            MLA_Q_RANK, MLA_KV_RANK, MLA_ROPE,
            BRANCH_W, BRANCH_W, BRANCH_W,
            N_BRANCH * 1024)


def _params(*sem):
    return pltpu.CompilerParams(dimension_semantics=sem, vmem_limit_bytes=VMEM_LIMIT)


class _LayerOf(NamedTuple):
    stack: jax.Array
    layer: int

    @property
    def shape(self):
        return self.stack.shape[1:]


def _operand(x):
    return x.stack if isinstance(x, _LayerOf) else x


def _resident(x):
    if isinstance(x, _LayerOf):
        shape, lead, first = x.shape, (None,), (x.layer,)
    else:
        shape, lead, first = (x if isinstance(x, tuple) else x.shape), (), ()
    nd = len(shape)
    return pl.BlockSpec(lead + tuple(shape), lambda *_: first + (0,) * nd, pipeline_mode=pl.Buffered(1))


def _cast_kernel(x_ref, o_ref):
    o_ref[...] = x_ref[...].astype(o_ref.dtype)


def _cast_bf16(w, *, row_blocks=4):
    depth, r, c = w.shape
    tr = r // row_blocks
    spec = pl.BlockSpec((None, tr, c), lambda l, i: (l, i, 0))
    return pl.pallas_call(
        _cast_kernel,
        grid=(depth, row_blocks),
        in_specs=[spec],
        out_specs=spec,
        out_shape=jax.ShapeDtypeStruct(w.shape, BF16),
        compiler_params=_params("parallel", "parallel"),
        name="cast_bf16",
    )(w)


def _rms(x, g):
    return x * lax.rsqrt(jnp.mean(x * x, axis=-1, keepdims=True) + EPS) * g


def _silu(x):
    half = 0.5 * x
    return half + half * jnp.tanh(half)


def _dot(a, b):
    return jnp.dot(a, b, preferred_element_type=F32)


def _dot_nt(a, b):
    return lax.dot_general(a, b, (((1,), (1,)), ((), ())), preferred_element_type=F32)


def _dot_tn(a, b):
    return lax.dot_general(a, b, (((0,), (0,)), ((), ())), preferred_element_type=F32)


def _ffn_kernel(h_ref, g_ref, wg_ref, wu_ref, wd_ref, *rest, ff_chunk, final):
    if final:
        gf_ref, o_ref = rest
    else:
        (o_ref,) = rest
    h = h_ref[...]
    xn = _rms(h, g_ref[...]).astype(BF16)
    acc = jnp.zeros(h.shape, F32)
    d_ff = wg_ref.shape[1]
    for c0 in range(0, d_ff, ff_chunk):
        a = _dot(xn, wg_ref[:, c0:c0 + ff_chunk])
        b = _dot(xn, wu_ref[:, c0:c0 + ff_chunk])
        acc = acc + _dot((_silu(a) * b).astype(BF16), wd_ref[c0:c0 + ff_chunk, :])
    out = h + 0.5 * acc
    if final:
        out = _rms(out, gf_ref[...])
    o_ref[...] = out


def _ffn(h, g, wg, wu, wd, final_g=None, *, tm=512, ff_chunk=256):
    m, d = h.shape
    d_ff = wg.shape[1]
    final = final_g is not None
    in_specs = [pl.BlockSpec((tm, d), lambda i: (i, 0)), _resident((1, d)),
                _resident(wg), _resident(wu), _resident(wd)]
    args = [h, g, _operand(wg), _operand(wu), _operand(wd)]
    if final:
        in_specs.append(_resident((1, d)))
        args.append(final_g)
    return pl.pallas_call(
        functools.partial(_ffn_kernel, ff_chunk=ff_chunk, final=final),
        grid=(m // tm,),
        in_specs=in_specs,
        out_specs=pl.BlockSpec((tm, d), lambda i: (i, 0)),
        out_shape=jax.ShapeDtypeStruct((m, d), F32),
        compiler_params=_params("parallel"),
        name="ffn",
    )(*args)


def _mla_qkv(p, cos, sin, q_norm, kv_norm, w_uq, w_ukv):
    kvw = MLA_HEADS * MLA_PAD
    cq = p[:, :MLA_Q_RANK]
    ckv = p[:, MLA_Q_RANK:MLA_Q_RANK + MLA_KV_RANK]
    kr0 = MLA_Q_RANK + MLA_KV_RANK
    kr = p[:, kr0:kr0 + MLA_PAD]
    kr_partner = p[:, kr0 + MLA_PAD:]
    qq = _dot(_rms(cq, q_norm).astype(BF16), w_uq)
    cos4 = jnp.concatenate([cos] * MLA_HEADS, axis=-1)
    sin4 = jnp.concatenate([sin] * MLA_HEADS, axis=-1)
    q = (qq[:, :kvw] * cos4 + qq[:, kvw:] * sin4) * (MLA_QH ** -0.5 * LOG2E)
    kv = _dot(_rms(ckv, kv_norm).astype(BF16), w_ukv)
    kpe = kr * cos + kr_partner * sin
    k = kv[:, :kvw] + jnp.concatenate([kpe] * MLA_HEADS, axis=-1)
    return q.astype(BF16).T, k.astype(BF16), kv[:, kvw:].astype(BF16).T


def _norm_proj_kernel(x_ref, g_ref, w_ref, *rest, outs, col_chunk, mla):
    if mla:
        cos_ref, sin_ref, qn_ref, kvn_ref, wuq_ref, wukv_ref = rest[:6]
        o_refs = rest[6:]
    else:
        o_refs = rest
    xn = _rms(x_ref[...], g_ref[...]).astype(BF16)
    off = 0
    for o_ref, (width, _, scale, transposed) in zip(o_refs, outs):
        for c0 in range(0, width, col_chunk):
            c1 = min(c0 + col_chunk, width)
            y = _dot(xn, w_ref[:, off + c0:off + c1])
            if scale != 1.0:
                y = y * scale
            if transposed:
                o_ref[c0:c1, :] = y.astype(o_ref.dtype).T
            else:
                o_ref[:, c0:c1] = y.astype(o_ref.dtype)
        off += width
    if mla:
        q_ref, k_ref, v_ref = o_refs[len(outs):]
        p = _dot(xn, w_ref[:, off:])
        q_ref[...], k_ref[...], v_ref[...] = _mla_qkv(p, cos_ref[...], sin_ref[...], qn_ref[...], kvn_ref[...],
                                                      wuq_ref[...], wukv_ref[...])


def _norm_proj(x, g, w, outs, seq, *, mla=None, tm=512, col_chunk=512, name="norm_proj"):
    m, d = x.shape
    assert sum(o[0] for o in outs) + (MLA_IN_W if mla else 0) == w.shape[1]
    per = seq // tm
    rows = lambda width: pl.BlockSpec((tm, width), lambda i: (i, 0))
    cols = lambda width: pl.BlockSpec((None, width, tm), lambda i: (i // per, 0, i % per))
    out_specs, out_shape = [], []
    for width, dt, _, transposed in tuple(outs) + (MLA_OUTS if mla else ()):
        if transposed:
            out_specs.append(cols(width))
            out_shape.append(jax.ShapeDtypeStruct((m // seq, width, seq), dt))
        else:
            out_specs.append(rows(width))
            out_shape.append(jax.ShapeDtypeStruct((m, width), dt))
    in_specs = [rows(d), _resident((1, d)), _resident(w)]
    args = [x, g, _operand(w)]
    if mla:
        cos, sin, q_norm, kv_norm, w_uq, w_ukv = mla
        in_specs += [rows(LANES), rows(LANES), _resident(q_norm), _resident(kv_norm), _resident(w_uq),
                     _resident(w_ukv)]
        args += [cos, sin, q_norm, kv_norm, _operand(w_uq), _operand(w_ukv)]
    return pl.pallas_call(
        functools.partial(_norm_proj_kernel, outs=tuple(outs), col_chunk=col_chunk, mla=mla is not None),
        grid=(m // tm,),
        in_specs=in_specs,
        out_specs=out_specs,
        out_shape=out_shape,
        compiler_params=_params("parallel"),
        name=name,
    )(*args)


def _rope_table_kernel(pos_ref, inv_ref, sign_ref, cos_ref, sin_ref):
    ang = pos_ref[...].astype(F32) * inv_ref[...]
    cos_ref[...] = jnp.cos(ang)
    sin_ref[...] = jnp.sin(ang) * sign_ref[...]


def _rope_tables(pos_col, inv_row, sign_row, *, tm=1024):
    m = pos_col.shape[0]
    return pl.pallas_call(
        _rope_table_kernel,
        grid=(m // tm,),
        in_specs=[pl.BlockSpec((tm, 1), lambda i: (i, 0)), _resident((1, LANES)), _resident((1, LANES))],
        out_specs=[pl.BlockSpec((tm, LANES), lambda i: (i, 0))] * 2,
        out_shape=[jax.ShapeDtypeStruct((m, LANES), F32)] * 2,
        compiler_params=_params("parallel"),
        name="rope_tables",
    )(pos_col, inv_row, sign_row)


def _rope_rows(period, half, start):
    lane = jnp.arange(LANES) % period
    inside = (lane >= start) & (lane < start + 2 * half)
    idx = (lane - start) % half
    inv = jnp.exp(-math.log(ROPE_BASE) * idx.astype(F32) / half)
    inv = jnp.where(inside, inv, 0.0)
    sign = jnp.where(inside, jnp.where(lane < start + half, -1.0, 1.0), 0.0)
    return inv.reshape(1, LANES).astype(F32), sign.reshape(1, LANES).astype(F32)


def _apply_rope(x, cos, sin_signed, period, half, start):
    width = x.shape[-1]
    reps = width // LANES
    if reps > 1:
        cos = jnp.concatenate([cos] * reps, axis=-1)
        sin_signed = jnp.concatenate([sin_signed] * reps, axis=-1)
    lane = lax.broadcasted_iota(jnp.int32, (1, width), 1) % period
    first = lane < start + half
    partner = jnp.where(first, pltpu.roll(x, width - half, 1), pltpu.roll(x, half, 1))
    return x * cos + partner * sin_signed


def _ret_kernel(q_ref, k_ref, v_ref, g_ref, cos_ref, sin_ref, intra_ref, qdec_ref, kdec_ref,
                sdec_ref, bmask_ref, o_ref, state_sc):
    @pl.when(pl.program_id(0) == 0)
    def _():
        state_sc[...] = jnp.zeros_like(state_sc)

    head = lax.broadcasted_iota(jnp.int32, (1, RET_QK), 1) // RET_DK
    for b in range(q_ref.shape[0]):
        cos, sin = cos_ref[b], sin_ref[b]
        q = _apply_rope(q_ref[b], cos, sin, RET_DK, RET_DK // 2, 0) * (RET_DK ** -0.5)
        k = _apply_rope(k_ref[b], cos, sin, RET_DK, RET_DK // 2, 0)
        vb = v_ref[b].astype(BF16)
        kb = k.astype(BF16)
        state = state_sc[b]
        cross = _dot((q * qdec_ref[...]).astype(BF16), state.astype(BF16))
        outs = []
        for h in range(RET_HEADS):
            qh = jnp.where(head == h, q, 0.0).astype(BF16)
            att = _dot_nt(qh, kb) * intra_ref[h]
            sl = slice(h * RET_DV, (h + 1) * RET_DV)
            o = _dot(att.astype(BF16), vb[:, sl]) + cross[:, sl]
            mu = jnp.mean(o, axis=-1, keepdims=True)
            oc = o - mu
            outs.append(oc * lax.rsqrt(jnp.mean(oc * oc, axis=-1, keepdims=True) + EPS))
        o_ref[b] = (jnp.concatenate(outs, axis=-1) * _silu(g_ref[b])).astype(o_ref.dtype)
        kv = _dot_tn((k * kdec_ref[...]).astype(BF16), vb)
        state_sc[b] = state * sdec_ref[...] + kv * bmask_ref[...]


def _retention_consts():
    c = CHUNK
    log_g = jnp.log1p(-jnp.exp2(-5.0 - jnp.arange(RET_HEADS, dtype=F32)))
    i = jnp.arange(c, dtype=F32)
    rel = i[:, None] - i[None, :]
    intra = jnp.where(rel >= 0, jnp.exp(log_g[:, None, None] * jnp.maximum(rel, 0.0)), 0.0)
    lg_lane = jnp.repeat(log_g, RET_DK)
    qdec = jnp.exp(lg_lane[None, :] * (i[:, None] + 1.0))
    kdec = jnp.exp(lg_lane[None, :] * (c - 1.0 - i[:, None]))
    row_h = jnp.repeat(jnp.arange(RET_HEADS), RET_DK)
    col_h = jnp.repeat(jnp.arange(RET_HEADS), RET_DV)
    bmask = (row_h[:, None] == col_h[None, :]).astype(F32)
    sdec = bmask * jnp.exp(lg_lane * c)[:, None]
    return intra, qdec, kdec, sdec, bmask


def _retention(ret_p, cos, sin):
    c = CHUNK
    batch, seq, _ = ret_p.shape
    intra, qdec, kdec, sdec, bmask = _retention_consts()
    blk = lambda width, col: pl.BlockSpec((batch, c, width), lambda j: (0, j, col))
    return pl.pallas_call(
        _ret_kernel,
        grid=(seq // c,),
        in_specs=[blk(RET_QK, 0), blk(RET_QK, 1), blk(BRANCH_W, 1), blk(BRANCH_W, 2), blk(LANES, 0), blk(LANES, 0),
                  _resident(intra.shape), _resident(qdec.shape), _resident(kdec.shape),
                  _resident(sdec.shape), _resident(bmask.shape)],
        out_specs=blk(BRANCH_W, 0),
        out_shape=jax.ShapeDtypeStruct((batch, seq, BRANCH_W), BF16),
        scratch_shapes=[pltpu.VMEM((batch, RET_QK, BRANCH_W), F32)],
        compiler_params=_params("arbitrary"),
        name="retention",
    )(ret_p, ret_p, ret_p, ret_p, cos, sin, intra, qdec, kdec, sdec, bmask)


def _split3(x):
    hi = x.astype(BF16)
    r = x - hi.astype(F32)
    mid = r.astype(BF16)
    lo = (r - mid.astype(F32)).astype(BF16)
    return hi, mid, lo


def _dot_split3(a, b, *, left):
    if left:
        return sum(_dot(piece, b) for piece in _split3(a))
    return sum(_dot(a, piece) for piece in _split3(b))


def _ssd_kernel(xbc_ref, *refs):
    xe_sc, state_sc = refs[-2:]

    @pl.when(pl.program_id(0) == 0)
    def _():
        xe_sc[...] = jnp.zeros_like(xe_sc)
        state_sc[...] = jnp.zeros_like(state_sc)

    for b in range(xbc_ref.shape[0]):
        _ssd_chunk(b, xbc_ref, *refs)


def _ssd_chunk(b, xbc_ref, z_ref, dt_ref, cw_ref, cb_ref, dtb_ref, alog_ref, dskip_ref, ng_ref, tri_ref,
               expand_ref, o_ref, xe_sc, state_sc):
    L = CHUNK
    T = SSM_TAIL
    GW = SSM_INNER // SSM_GROUPS
    x_raw = xbc_ref[b]
    tail = xe_sc[b]
    rows8 = lax.broadcasted_iota(jnp.int32, (T, 1), 0)
    conv = cb_ref[...] + cw_ref[SSM_CONV - 1:SSM_CONV, :] * x_raw
    for s in range(1, SSM_CONV):
        shifted = pltpu.roll(x_raw, s, 0)
        head = jnp.where(rows8 < s, pltpu.roll(tail, s, 0), shifted[0:T, :])
        shifted = jnp.concatenate([head, shifted[T:, :]], axis=0)
        conv = conv + cw_ref[SSM_CONV - 1 - s:SSM_CONV - s, :] * shifted
    xe_sc[b] = x_raw[L - T:L, :]
    xa = _silu(conv)
    xs = xa[:, :SSM_INNER]
    bm = xa[:, SSM_INNER:SSM_INNER + SSM_GROUPS * SSM_STATE]
    cm = xa[:, SSM_INNER + SSM_GROUPS * SSM_STATE:]

    dtr = dt_ref[b] + dtb_ref[...]
    dt_h = jnp.maximum(dtr, 0.0) + jnp.log1p(jnp.exp(-jnp.abs(dtr)))
    da = dt_h * (-jnp.exp(alog_ref[...]))
    cs_t = _dot_split3(da, tri_ref[...], left=True)
    cs_t_last = cs_t[:, L - 1:L]
    per_head = jnp.concatenate([dt_h, jnp.exp(cs_t), jnp.exp(cs_t_last - cs_t), cs_t], axis=0)
    per_chan = _dot_tn(jnp.concatenate(_split3(per_head), axis=0), expand_ref[...])
    dt, ecs, eds, cs_c = (per_chan[:, i * SSM_INNER:(i + 1) * SSM_INNER] for i in range(4))
    xdt = xs * dt
    xdec = (xdt * eds).astype(BF16)
    state = state_sc[b]
    li = lax.broadcasted_iota(jnp.int32, (L, L), 0)
    si = lax.broadcasted_iota(jnp.int32, (L, L), 1)
    causal = li >= si
    lane_half = lax.broadcasted_iota(jnp.int32, (1, LANES), 1) // SSM_HEADDIM
    ys = []
    new_states = []
    for g in range(SSM_GROUPS):
        bg = bm[:, g * SSM_STATE:(g + 1) * SSM_STATE].astype(BF16)
        cg = cm[:, g * SSM_STATE:(g + 1) * SSM_STATE].astype(BF16)
        cb = _dot_nt(cg, bg)
        gsl = slice(g * GW, (g + 1) * GW)
        y_off = _dot(cg, state[:, gsl].astype(BF16)) * ecs[:, gsl]
        for p in range(GW // LANES):
            sl = slice(g * GW + p * LANES, g * GW + (p + 1) * LANES)
            xdt_blk = xdt[:, sl]
            acc = y_off[:, p * LANES:(p + 1) * LANES]
            for hh in range(LANES // SSM_HEADDIM):
                head = (g * GW + p * LANES) // SSM_HEADDIM + hh
                col = cs_c[:, head * SSM_HEADDIM:head * SSM_HEADDIM + 1]
                rowv = cs_t[head:head + 1, :]
                dec = jnp.where(causal, jnp.exp(jnp.minimum(col - rowv, 0.0)), 0.0)
                xm = jnp.where(lane_half == hh, xdt_blk, 0.0).astype(BF16)
                acc = acc + _dot((cb * dec).astype(BF16), xm)
            ys.append(acc)
        new_states.append(_dot_tn(bg, xdec[:, gsl]))
    y = jnp.concatenate(ys, axis=-1) + xs * dskip_ref[...]
    y = y * _silu(z_ref[b])
    o_ref[b] = _rms(y, ng_ref[...]).astype(o_ref.dtype)
    state_sc[b] = state * ecs[L - 1:L, :] + jnp.concatenate(new_states, axis=-1)


def _ssd(ssm_p, dt_t, conv_w, conv_b, dt_bias_col, a_log_col, d_full, norm_g):
    L = CHUNK
    batch, seq, _ = ssm_p.shape
    tri = (jnp.arange(L)[:, None] <= jnp.arange(L)[None, :]).astype(BF16)
    r = jnp.arange(3 * 4 * SSM_HEADS)
    c = jnp.arange(4 * SSM_INNER)
    expand = (((r[:, None] // SSM_HEADS) % 4 == c[None, :] // SSM_INNER)
              & (r[:, None] % SSM_HEADS == (c[None, :] % SSM_INNER) // SSM_HEADDIM)).astype(BF16)
    return pl.pallas_call(
        _ssd_kernel,
        grid=(seq // L,),
        in_specs=[pl.BlockSpec((batch, L, SSM_XBC), lambda j: (0, j, 0)),
                  pl.BlockSpec((batch, L, SSM_INNER), lambda j: (0, j, SSM_XBC // SSM_INNER)),
                  pl.BlockSpec((batch, SSM_HEADS, L), lambda j: (0, 0, j)),
                  _resident(conv_w.shape), _resident(conv_b.shape), _resident(dt_bias_col.shape),
                  _resident(a_log_col.shape), _resident(d_full.shape), _resident(norm_g.shape),
                  _resident(tri.shape), _resident(expand.shape)],
        out_specs=pl.BlockSpec((batch, L, SSM_INNER), lambda j: (0, j, 0)),
        out_shape=jax.ShapeDtypeStruct((batch, seq, SSM_INNER), BF16),
        scratch_shapes=[pltpu.VMEM((batch, SSM_TAIL, SSM_XBC), F32),
                        pltpu.VMEM((batch, SSM_STATE, SSM_INNER), F32)],
        compiler_params=_params("arbitrary"),
        name="ssd",
    )(ssm_p, ssm_p, dt_t, conv_w, conv_b, dt_bias_col, a_log_col, d_full, norm_g, tri, expand)


def _flash_kernel(*refs, tq, tk, n_maps, lambda_init):
    biased = n_maps == 2
    if biased:
        (pmin_ref, pmax_ref, qt_ref, k_ref, vt_ref, pq_ref, pk_ref, ftab_ref, lam_ref, sg_ref,
         o_ref, m_sc, l_sc, acc_sc, s0_sc, s1_sc) = refs
    else:
        qt_ref, k_ref, vt_ref, o_ref, m_sc, l_sc, acc_sc, s0_sc, s1_sc = refs
    bi = pl.program_id(0)
    qi = pl.program_id(1)
    hw = HEAD_W
    heads = k_ref.shape[-1] // hw
    cols = n_maps * tq
    hs = lambda h: slice(h * hw, (h + 1) * hw)
    ws = []
    for h in range(heads):
        qt = qt_ref[hs(h), :]
        if biased:
            half = lax.broadcasted_iota(jnp.int32, (hw, 1), 0) // DIFF_DH
            zero = jnp.zeros_like(qt)
            qt = jnp.concatenate([jnp.where(half == 0, qt, zero), jnp.where(half == 1, qt, zero)], axis=1)
        ws.append(qt)
    if biased:
        pq = pq_ref[...]
        ftabs = [jnp.broadcast_to(ftab_ref[h] * LOG2E, (tk, REL_TABLE)) for h in range(heads)]
    m_sc[...] = jnp.full(m_sc.shape, NEG_INF, F32)
    l_sc[...] = jnp.zeros_like(l_sc)
    acc_sc[...] = jnp.zeros_like(acc_sc)

    def near_biases(off, q_lo=0):
        idx = jnp.clip(pq[:, q_lo:] - pk_ref[pl.ds(off, tk), :], 0, REL_TABLE - 1)
        tiles = [jnp.concatenate([jnp.take_along_axis(ftabs[h], idx[:, c:c + LANES], axis=1)
                                  for c in range(0, tq - q_lo, LANES)], axis=1) for h in range(heads)]

        def bias(h, cs):
            base = (cs.start // tq) * tq + q_lo
            return tiles[h][:, cs.start - base:cs.stop - base]
        return bias

    s_scs = (s0_sc, s1_sc)

    all_cols = tuple(slice(mi * tq, (mi + 1) * tq) for mi in range(n_maps))
    late_cols = tuple(slice(mi * tq + tk, (mi + 1) * tq) for mi in range(n_maps))

    def scores(off, slot, col_ranges=all_cols):
        for h in range(heads):
            for cs in col_ranges:
                s_scs[slot][h, :, cs] = _dot(k_ref[pl.ds(off, tk), hs(h)], ws[h][:, cs])

    def consume(off, slot, bias, mask, const=None, col_ranges=all_cols):
        for h in range(heads):
            for cs in col_ranges:
                s = s_scs[slot][h, :, cs]
                if bias is not None:
                    s = s + bias(h, cs)
                if mask is not None:
                    s = jnp.where(mask[:, cs], s, NEG_INF)
                m = m_sc[h, :, cs]
                m_tile = jnp.max(s, axis=0, keepdims=True)
                if const is not None:
                    m_tile = m_tile + const[h]
                m_new = jnp.maximum(m, m_tile)
                alpha = jnp.exp2(m - m_new)
                p = jnp.exp2(s - (m_new if const is None else m_new - const[h]))
                l_sc[h, :, cs] = alpha * l_sc[h, :, cs] + jnp.sum(p, axis=0, keepdims=True)
                acc_sc[h, :, cs] = (alpha * acc_sc[h, :, cs]
                                    + _dot(vt_ref[hs(h), pl.ds(off, tk)], p.astype(BF16)))
                m_sc[h, :, cs] = m_new

    assert tq == 2 * tk
    scores(0, 0)

    def body(jj, carry):
        off0 = pl.multiple_of(jj * tq, tq)
        off1 = pl.multiple_of(off0 + tk, tk)
        off2 = pl.multiple_of(off0 + tq, tq)

        def pair(near0, near1):
            const = [ftab_ref[h][:, REL_TABLE - 1:REL_TABLE] * LOG2E for h in range(heads)] if biased else None
            scores(off1, 1)
            consume(off0, 0, near_biases(off0) if near0 else None, None, None if near0 else const)
            scores(off2, 0)
            consume(off1, 1, near_biases(off1) if near1 else None, None, None if near1 else const)

        if biased:
            far0 = pmin_ref[bi, qi] - pmax_ref[bi, 2 * jj] >= REL_TABLE - 1
            far1 = pmin_ref[bi, qi] - pmax_ref[bi, 2 * jj + 1] >= REL_TABLE - 1

            @pl.when(far0 & far1)
            def _():
                pair(False, False)

            @pl.when(far0 & jnp.logical_not(far1))
            def _():
                pair(False, True)

            @pl.when(jnp.logical_not(far0))
            def _():
                pair(True, True)
        else:
            pair(False, False)
        return carry

    lax.fori_loop(0, qi, body, 0)
    for d in range(2):
        off = pl.multiple_of(qi * tq + d * tk, tk)
        if d == 0:
            scores(pl.multiple_of(off + tk, tk), 1, late_cols)
        key = lax.broadcasted_iota(jnp.int32, (tk, cols), 0) + d * tk
        qry = lax.broadcasted_iota(jnp.int32, (tk, cols), 1) % tq
        consume(off, d, near_biases(off, d * tk) if biased else None, key <= qry,
                col_ranges=all_cols if d == 0 else late_cols)

    if biased:
        lp = lam_ref[...]
        lam = (jnp.exp(jnp.sum(lp[0:1] * lp[1:2], axis=-1, keepdims=True))
               - jnp.exp(jnp.sum(lp[2:3] * lp[3:4], axis=-1, keepdims=True)) + lambda_init)
    for h in range(heads):
        o = acc_sc[h] / l_sc[h]
        if biased:
            o = (o[:, :tq] - lam * o[:, tq:]).T
            o_ref[:, hs(h)] = (_rms(o, sg_ref[...]) * (1.0 - lambda_init)).astype(o_ref.dtype)
        else:
            o_ref[:, hs(h)] = o.astype(o_ref.dtype).T


def _flash_scratch(heads, tk, cols):
    return [pltpu.VMEM((heads, 1, cols), F32), pltpu.VMEM((heads, 1, cols), F32),
            pltpu.VMEM((heads, HEAD_W, cols), F32),
            pltpu.VMEM((heads, tk, cols), F32), pltpu.VMEM((heads, tk, cols), F32)]


def _mla_attention(qt, k, vt, *, tq=ATTN_TQ, tk=ATTN_TK):
    b, s, w = k.shape
    return pl.pallas_call(
        functools.partial(_flash_kernel, tq=tq, tk=tk, n_maps=1, lambda_init=None),
        grid=(b, s // tq),
        in_specs=[pl.BlockSpec((None, w, tq), lambda bi, i: (bi, 0, i)),
                  pl.BlockSpec((None, s, w), lambda bi, i: (bi, 0, 0), pipeline_mode=pl.Buffered(1)),
                  pl.BlockSpec((None, w, s), lambda bi, i: (bi, 0, 0), pipeline_mode=pl.Buffered(1))],
        out_specs=pl.BlockSpec((None, tq, w), lambda bi, i: (bi, i, 0)),
        out_shape=jax.ShapeDtypeStruct((b, s, w), BF16),
        scratch_shapes=_flash_scratch(w // HEAD_W, tk, tq),
        compiler_params=_params("parallel", "arbitrary"),
        name="mla_attention",
    )(qt, k, vt)


def _diff_attention(qt, k, vt, pos, ftab, lam_params, sub_g, lambda_init, *, tq=ATTN_TQ, tk=ATTN_TK):
    b, s, w = k.shape
    pmin = jnp.min(pos.reshape(b, s // tq, tq), axis=-1)
    pmax = jnp.max(pos.reshape(b, s // tk, tk), axis=-1)
    pos_col = pos.reshape(b, s, 1)
    pos_row = pos.reshape(b, 1, s)
    grid_spec = pltpu.PrefetchScalarGridSpec(
        num_scalar_prefetch=2,
        grid=(b, s // tq),
        in_specs=[pl.BlockSpec((None, w, tq), lambda bi, i, *_: (bi, 0, i)),
                  pl.BlockSpec((None, s, w), lambda bi, i, *_: (bi, 0, 0), pipeline_mode=pl.Buffered(1)),
                  pl.BlockSpec((None, w, s), lambda bi, i, *_: (bi, 0, 0), pipeline_mode=pl.Buffered(1)),
                  pl.BlockSpec((None, 1, tq), lambda bi, i, *_: (bi, 0, i)),
                  pl.BlockSpec((None, s, 1), lambda bi, i, *_: (bi, 0, 0), pipeline_mode=pl.Buffered(1)),
                  pl.BlockSpec(ftab.shape, lambda bi, i, *_: (0, 0, 0)),
                  pl.BlockSpec(lam_params.shape, lambda bi, i, *_: (0, 0)),
                  pl.BlockSpec(sub_g.shape, lambda bi, i, *_: (0, 0))],
        out_specs=pl.BlockSpec((None, tq, w), lambda bi, i, *_: (bi, i, 0)),
        scratch_shapes=_flash_scratch(w // HEAD_W, tk, 2 * tq),
    )
    return pl.pallas_call(
        functools.partial(_flash_kernel, tq=tq, tk=tk, n_maps=2, lambda_init=lambda_init),
        grid_spec=grid_spec,
        out_shape=jax.ShapeDtypeStruct((b, s, w), BF16),
        compiler_params=_params("parallel", "arbitrary"),
        name="diff_attention",
    )(pmin, pmax, qt, k, vt, pos_row, pos_col, ftab, lam_params, sub_g)


def _merge_kernel(h_ref, g_ref, y0_ref, y1_ref, y2_ref, y3_ref, wg_ref, wb_ref, wo_ref, o_ref, *, col_chunk):
    h = h_ref[...]
    d = h.shape[-1]
    u = _rms(h, g_ref[...]).astype(BF16)
    ys = (y0_ref[...], y1_ref[...], y2_ref[...], y3_ref[...])
    parts = []
    for c0 in range(0, d, col_chunk):
        acc = None
        for i, y in enumerate(ys):
            gate = jax.nn.sigmoid(_dot(u, wg_ref[:, i * d + c0:i * d + c0 + col_chunk]))
            term = gate * _dot(y, wb_ref[i, :, c0:c0 + col_chunk])
            acc = term if acc is None else acc + term
        parts.append(acc.astype(BF16))
    merged = jnp.concatenate(parts, axis=-1)
    o_ref[...] = h + _dot(merged, wo_ref[...])


def _merge(h, g, ys, w_gate, w_branch, w_out, *, tm=1024, col_chunk=256):
    m, d = h.shape
    rows = lambda width: pl.BlockSpec((tm, width), lambda i: (i, 0))
    return pl.pallas_call(
        functools.partial(_merge_kernel, col_chunk=col_chunk),
        grid=(m // tm,),
        in_specs=[rows(d), _resident((1, d))] + [rows(BRANCH_W)] * N_BRANCH
                 + [_resident(w_gate), _resident(w_branch), _resident(w_out)],
        out_specs=rows(d),
        out_shape=jax.ShapeDtypeStruct((m, d), F32),
        compiler_params=_params("parallel"),
        name="merge",
    )(h, g, *ys, _operand(w_gate), _operand(w_branch), _operand(w_out))


def _xattn_kernel(h_ref, g_ref, wq_ref, k_ref, v_ref, wo_ref, o_ref):
    h = h_ref[...]
    d = h.shape[-1]
    dh = d // XA_HEADS
    u = _rms(h, g_ref[...]).astype(BF16)
    q = (_dot(u, wq_ref[...]) * (dh ** -0.5 * LOG2E)).astype(BF16)
    out = h
    for hd in range(XA_HEADS):
        sl = slice(hd * dh, (hd + 1) * dh)
        s = _dot_nt(q[:, sl], k_ref[:, sl])
        p = jnp.exp2(s - jnp.max(s, axis=-1, keepdims=True))
        inv = 1.0 / jnp.sum(p, axis=-1, keepdims=True)
        o = _dot(p.astype(BF16), v_ref[:, sl]) * inv
        out = out + _dot(o.astype(BF16), wo_ref[sl, :])
    o_ref[...] = out


def _cross_attention(h, g, wq, k, v, wo, batch, seq, *, tm=1024):
    m, d = h.shape
    mem_len = k.shape[0] // batch
    per = seq // tm
    return pl.pallas_call(
        _xattn_kernel,
        grid=(m // tm,),
        in_specs=[pl.BlockSpec((tm, d), lambda i: (i, 0)), _resident((1, d)), _resident(wq),
                  pl.BlockSpec((mem_len, d), lambda i: (i // per, 0)),
                  pl.BlockSpec((mem_len, d), lambda i: (i // per, 0)),
                  _resident(wo)],
        out_specs=pl.BlockSpec((tm, d), lambda i: (i, 0)),
        out_shape=jax.ShapeDtypeStruct((m, d), F32),
        compiler_params=_params("parallel"),
        name="cross_attention",
    )(h, g, _operand(wq), k, v, _operand(wo))


def _pad_cols(w, left, total):
    return jnp.pad(w, ((0, 0),) * (w.ndim - 1) + ((left, total - left - w.shape[-1]),))


def _w_in_plan():
    names = ("rq", "rk", "rv", "rg", "sz", "sxbc", "sdt", "cq", "ckv", "kr", "dq", "dk", "dv", "gl")
    src, start = {}, 0
    for n, s in zip(names, IN_SIZES):
        src[n] = (start, s)
        start += s
    half = MLA_ROPE // 2
    kr0 = src["kr"][0]
    moves, zeros, dst = [], [], 0
    for n in ("rq", "rk", "rv", "rg", "sxbc", "sz"):
        moves.append((*src[n], dst))
        dst += src[n][1]
    zeros.append((dst, SSD_DT_W))
    moves.append((*src["sdt"], dst))
    dst += SSD_DT_W
    for n in ("dq", "dk", "dv", "cq", "ckv"):
        moves.append((*src[n], dst))
        dst += src[n][1]
    zeros.append((dst, 2 * MLA_PAD))
    moves.append((kr0, MLA_ROPE, dst + MLA_NOPE))
    moves.append((kr0 + half, half, dst + MLA_PAD + MLA_NOPE))
    moves.append((kr0, half, dst + MLA_PAD + MLA_NOPE + half))
    dst += 2 * MLA_PAD
    return moves, zeros, src["gl"], dst


def _w_in_relayout_kernel(x_ref, mix_ref, gate_ref, *, moves, zeros, gate):
    for d0, width in zeros:
        mix_ref[:, d0:d0 + width] = jnp.zeros((mix_ref.shape[0], width), mix_ref.dtype)
    for s0, width, d0 in moves:
        mix_ref[:, d0:d0 + width] = x_ref[:, s0:s0 + width].astype(mix_ref.dtype)
    gate_ref[...] = x_ref[:, gate[0]:gate[0] + gate[1]].astype(gate_ref.dtype)


def _w_in_relayout(w_in, *, row_blocks=4):
    depth, r, c = w_in.shape
    moves, zeros, gate, mix_w = _w_in_plan()
    tr = r // row_blocks
    blk = lambda width: pl.BlockSpec((None, tr, width), lambda l, i: (l, i, 0))
    return pl.pallas_call(
        functools.partial(_w_in_relayout_kernel, moves=moves, zeros=zeros, gate=gate),
        grid=(depth, row_blocks),
        in_specs=[blk(c)],
        out_specs=[blk(mix_w), blk(gate[1])],
        out_shape=[jax.ShapeDtypeStruct((depth, r, mix_w), BF16), jax.ShapeDtypeStruct((depth, r, gate[1]), BF16)],
        compiler_params=_params("parallel", "parallel"),
        name="w_in_relayout",
    )(w_in)


def _mla_up_weights(w_uq, w_ukv):
    depth, d, _ = w_uq.shape
    half = MLA_ROPE // 2
    uq = w_uq.astype(BF16).reshape(depth, d, MLA_HEADS, MLA_QH)
    pe = uq[..., MLA_NOPE:]
    partner = jnp.concatenate([jnp.zeros_like(uq[..., :MLA_NOPE]), pe[..., half:], pe[..., :half]], axis=-1)
    pad = lambda a: _pad_cols(a, 0, MLA_PAD).reshape(depth, d, MLA_HEADS * MLA_PAD)
    uq_p = jnp.concatenate([pad(uq), pad(partner)], axis=-1)
    r = w_ukv.shape[1]
    ukv = w_ukv.astype(BF16).reshape(depth, r, MLA_HEADS, MLA_NOPE + MLA_V)
    k_nope = _pad_cols(ukv[..., :MLA_NOPE], 0, MLA_PAD)
    ukv_p = jnp.concatenate([k_nope.reshape(depth, r, -1), ukv[..., MLA_NOPE:].reshape(depth, r, -1)], axis=-1)
    return uq_p, ukv_p


def _t5_bucket_const():
    n = jnp.arange(REL_TABLE)
    exact = REL_BUCKETS // 2
    nf = jnp.maximum(n, 1).astype(F32)
    large = exact + (jnp.log(nf / exact) / math.log(REL_MAX_DIST / exact) * (REL_BUCKETS - exact)).astype(jnp.int32)
    return jnp.where(n < exact, n, jnp.minimum(large, REL_BUCKETS - 1))


MIX_OUTS = ((3 * BRANCH_W, F32, 1.0, False),
            (SSM_XBC + SSM_INNER, F32, 1.0, False),
            (SSD_DT_W, F32, 1.0, True),
            (BRANCH_W, BF16, DIFF_DH ** -0.5 * LOG2E, True),
            (BRANCH_W, BF16, 1.0, False),
            (BRANCH_W, BF16, 1.0, True))


def kernel(x, mem, positions, ffn1_norm, ffn1_w_gate, ffn1_w_up, ffn1_w_down, mix_norm, w_in, ssm_conv_w,
           ssm_conv_b, ssm_dt_bias, ssm_a_log, ssm_d, ssm_norm, mla_q_norm, mla_kv_norm, mla_w_uq, mla_w_ukv,
           diff_lambda, diff_norm, rel_bias, w_branch, w_out, xa_norm, mem_norm, xa_wq, xa_wk, xa_wv, xa_wo,
           ffn2_norm, ffn2_w_gate, ffn2_w_up, ffn2_w_down, final_norm):
    batch, seq, d = x.shape
    depth = w_in.shape[0]
    m = batch * seq
    h = x.reshape(m, d)
    mem2 = mem.reshape(batch * mem.shape[1], d)
    row = lambda a: a.reshape(1, -1)

    pos_col = positions.reshape(m, 1)
    ret_cos, ret_sin = _rope_tables(pos_col, *_rope_rows(RET_DK, RET_DK // 2, 0))
    mla_cos, mla_sin = _rope_tables(pos_col, *_rope_rows(MLA_PAD, MLA_ROPE // 2, MLA_NOPE))
    ftab = rel_bias[_t5_bucket_const()].T.reshape(DIFF_HEADS, 1, REL_TABLE)

    w_mix, w_gate = _w_in_relayout(w_in)
    w_uq_p, w_ukv_p = _mla_up_weights(mla_w_uq, mla_w_ukv)
    ffn1 = [_cast_bf16(w) for w in (ffn1_w_gate, ffn1_w_up, ffn1_w_down)]
    ffn2 = [_cast_bf16(w) for w in (ffn2_w_gate, ffn2_w_up, ffn2_w_down)]
    w_branch_b = _cast_bf16(w_branch.reshape(depth, N_BRANCH * BRANCH_W, d)).reshape(w_branch.shape)
    w_out_b, xa_wq_b, xa_wo_b = _cast_bf16(w_out), _cast_bf16(xa_wq), _cast_bf16(xa_wo)
    w_kv = jnp.concatenate([_cast_bf16(xa_wk), _cast_bf16(xa_wv)], axis=-1)

    for l in range(depth):
        at = lambda stack: _LayerOf(stack, l)
        h = _ffn(h, row(ffn1_norm[l]), *(at(w) for w in ffn1))
        mla = (mla_cos, mla_sin, row(mla_q_norm[l]), row(mla_kv_norm[l]), at(w_uq_p), at(w_ukv_p))
        ret_p, ssm_p, dt_t, dqt, dk, dvt, mqt, mk, mvt = _norm_proj(h, row(mix_norm[l]), at(w_mix), MIX_OUTS, seq,
                                                                   mla=mla, name="mix_proj")
        y_ret = _retention(ret_p.reshape(batch, seq, -1), ret_cos.reshape(batch, seq, -1),
                           ret_sin.reshape(batch, seq, -1)).reshape(m, BRANCH_W)
        y_ssm = _ssd(ssm_p.reshape(batch, seq, -1), dt_t, ssm_conv_w[l], row(ssm_conv_b[l]),
                     ssm_dt_bias[l].reshape(-1, 1), ssm_a_log[l].reshape(-1, 1),
                     row(jnp.repeat(ssm_d[l], SSM_HEADDIM)), row(ssm_norm[l])).reshape(m, SSM_INNER)
        y_mla = _mla_attention(mqt, mk.reshape(batch, seq, -1), mvt).reshape(m, BRANCH_W)
        lambda_init = 0.8 - 0.6 * math.exp(-0.3 * l)
        y_diff = _diff_attention(dqt, dk.reshape(batch, seq, -1), dvt, positions, ftab, diff_lambda[l],
                                 row(diff_norm[l]), lambda_init).reshape(m, BRANCH_W)
        h = _merge(h, row(mix_norm[l]), (y_ret, y_ssm, y_mla, y_diff), at(w_gate), at(w_branch_b), at(w_out_b))
        xk, xv = _norm_proj(mem2, row(mem_norm[l]), at(w_kv), [(d, BF16, 1.0, False)] * 2, mem.shape[1],
                            tm=mem.shape[1], name="mem_kv")
        h = _cross_attention(h, row(xa_norm[l]), at(xa_wq_b), xk, xv, at(xa_wo_b), batch, seq)
        h = _ffn(h, row(ffn2_norm[l]), *(at(w) for w in ffn2),
                 final_g=row(final_norm) if l == depth - 1 else None)
    return h.reshape(batch, seq, d)
```

```python
import functools
import math
from typing import NamedTuple

import jax
import jax.numpy as jnp
from jax import lax
from jax.experimental import pallas as pl
from jax.experimental.pallas import tpu as pltpu

F32 = jnp.float32
BF16 = jnp.bfloat16

EPS = 1e-6
NEG_INF = -1e30
LOG2E = math.log2(math.e)
ROPE_BASE = 10000.0
CHUNK = 128
BRANCH_W = 512
N_BRANCH = 4

RET_HEADS = 4
RET_DK = 64
RET_DV = 128
RET_QK = RET_HEADS * RET_DK

SSM_HEADDIM = 64
SSM_INNER = 512
SSM_HEADS = 8
SSM_GROUPS = 2
SSM_STATE = 128
SSM_CONV = 4
SSM_XBC = SSM_INNER + 2 * SSM_GROUPS * SSM_STATE
SSM_TAIL = 8
SSD_DT_W = 128

MLA_HEADS = 4
MLA_Q_RANK = 256
MLA_KV_RANK = 128
MLA_NOPE = 64
MLA_ROPE = 32
MLA_V = 128
MLA_QH = MLA_NOPE + MLA_ROPE
MLA_PAD = 128
MLA_IN_W = MLA_Q_RANK + MLA_KV_RANK + 2 * MLA_PAD
MLA_OUTS = ((MLA_HEADS * MLA_PAD, jnp.bfloat16, 1.0, True), (MLA_HEADS * MLA_PAD, jnp.bfloat16, 1.0, False),
            (MLA_HEADS * MLA_V, jnp.bfloat16, 1.0, True))
HEAD_W = 128
ATTN_TQ = 512
ATTN_TK = 256

DIFF_HEADS = 4
DIFF_DH = 64

REL_BUCKETS = 32
REL_MAX_DIST = 128
REL_TABLE = 128

XA_HEADS = 4

LANES = 128
VMEM_LIMIT = 56 * 1024 * 1024

IN_SIZES = (RET_QK, RET_QK, BRANCH_W, BRANCH_W,
            SSM_INNER, SSM_XBC, SSM_HEADS,
            MLA_Q_RANK, MLA_KV_RANK, MLA_ROPE,
            BRANCH_W, BRANCH_W, BRANCH_W,
            N_BRANCH * 1024)


def _params(*sem):
    return pltpu.CompilerParams(dimension_semantics=sem, vmem_limit_bytes=VMEM_LIMIT)


class _LayerOf(NamedTuple):
    stack: jax.Array
    layer: int

    @property
    def shape(self):
        return self.stack.shape[1:]


def _operand(x):
    return x.stack if isinstance(x, _LayerOf) else x


def _resident(x):
    if isinstance(x, _LayerOf):
        shape, lead, first = x.shape, (None,), (x.layer,)
    else:
        shape, lead, first = (x if isinstance(x, tuple) else x.shape), (), ()
    nd = len(shape)
    return pl.BlockSpec(lead + tuple(shape), lambda *_: first + (0,) * nd, pipeline_mode=pl.Buffered(1))


def _cast_kernel(x_ref, o_ref):
    o_ref[...] = x_ref[...].astype(o_ref.dtype)


def _cast_bf16(w, *, row_blocks=4):
    depth, r, c = w.shape
    tr = r // row_blocks
    spec = pl.BlockSpec((None, tr, c), lambda l, i: (l, i, 0))
    return pl.pallas_call(
        _cast_kernel,
        grid=(depth, row_blocks),
        in_specs=[spec],
        out_specs=spec,
        out_shape=jax.ShapeDtypeStruct(w.shape, BF16),
        compiler_params=_params("parallel", "parallel"),
        name="cast_bf16",
    )(w)


def _rms(x, g):
    return x * lax.rsqrt(jnp.mean(x * x, axis=-1, keepdims=True) + EPS) * g


def _silu(x):
    half = 0.5 * x
    return half + half * jnp.tanh(half)


def _dot(a, b):
    return jnp.dot(a, b, preferred_element_type=F32)


def _dot_nt(a, b):
    return lax.dot_general(a, b, (((1,), (1,)), ((), ())), preferred_element_type=F32)


def _dot_tn(a, b):
    return lax.dot_general(a, b, (((0,), (0,)), ((), ())), preferred_element_type=F32)


def _ffn_kernel(h_ref, g_ref, wg_ref, wu_ref, wd_ref, *rest, ff_chunk, final):
    if final:
        gf_ref, o_ref = rest
    else:
        (o_ref,) = rest
    h = h_ref[...]
    xn = _rms(h, g_ref[...]).astype(BF16)
    acc = jnp.zeros(h.shape, F32)
    d_ff = wg_ref.shape[1]
    for c0 in range(0, d_ff, ff_chunk):
        a = _dot(xn, wg_ref[:, c0:c0 + ff_chunk].astype(BF16))
        b = _dot(xn, wu_ref[:, c0:c0 + ff_chunk].astype(BF16))
        acc = acc + _dot((_silu(a) * b).astype(BF16), wd_ref[c0:c0 + ff_chunk, :].astype(BF16))
    out = h + 0.5 * acc
    if final:
        out = _rms(out, gf_ref[...])
    o_ref[...] = out


def _ffn(h, g, wg, wu, wd, final_g=None, *, tm=512, ff_chunk=256):
    m, d = h.shape
    d_ff = wg.shape[1]
    final = final_g is not None
    in_specs = [pl.BlockSpec((tm, d), lambda i: (i, 0)), _resident((1, d)),
                _resident(wg), _resident(wu), _resident(wd)]
    args = [h, g, _operand(wg), _operand(wu), _operand(wd)]
    if final:
        in_specs.append(_resident((1, d)))
        args.append(final_g)
    return pl.pallas_call(
        functools.partial(_ffn_kernel, ff_chunk=ff_chunk, final=final),
        grid=(m // tm,),
        in_specs=in_specs,
        out_specs=pl.BlockSpec((tm, d), lambda i: (i, 0)),
        out_shape=jax.ShapeDtypeStruct((m, d), F32),
        compiler_params=_params("parallel"),
        name="ffn",
    )(*args)


def _mla_qkv(p, cos, sin, q_norm, kv_norm, w_uq, w_ukv):
    kvw = MLA_HEADS * MLA_PAD
    cq = p[:, :MLA_Q_RANK]
    ckv = p[:, MLA_Q_RANK:MLA_Q_RANK + MLA_KV_RANK]
    kr0 = MLA_Q_RANK + MLA_KV_RANK
    kr = p[:, kr0:kr0 + MLA_PAD]
    kr_partner = p[:, kr0 + MLA_PAD:]
    qq = _dot(_rms(cq, q_norm).astype(BF16), w_uq)
    cos4 = jnp.concatenate([cos] * MLA_HEADS, axis=-1)
    sin4 = jnp.concatenate([sin] * MLA_HEADS, axis=-1)
    q = (qq[:, :kvw] * cos4 + qq[:, kvw:] * sin4) * (MLA_QH ** -0.5 * LOG2E)
    kv = _dot(_rms(ckv, kv_norm).astype(BF16), w_ukv)
    kpe = kr * cos + kr_partner * sin
    k = kv[:, :kvw] + jnp.concatenate([kpe] * MLA_HEADS, axis=-1)
    return q.astype(BF16).T, k.astype(BF16), kv[:, kvw:].astype(BF16).T


def _norm_proj_kernel(x_ref, g_ref, w_ref, *rest, outs, col_chunk, mla):
    if mla:
        cos_ref, sin_ref, qn_ref, kvn_ref, wuq_ref, wukv_ref = rest[:6]
        o_refs = rest[6:]
    else:
        o_refs = rest
    xn = _rms(x_ref[...], g_ref[...]).astype(BF16)
    off = 0
    for o_ref, (width, _, scale, transposed) in zip(o_refs, outs):
        for c0 in range(0, width, col_chunk):
            c1 = min(c0 + col_chunk, width)
            y = _dot(xn, w_ref[:, off + c0:off + c1])
            if scale != 1.0:
                y = y * scale
            if transposed:
                o_ref[c0:c1, :] = y.astype(o_ref.dtype).T
            else:
                o_ref[:, c0:c1] = y.astype(o_ref.dtype)
        off += width
    if mla:
        q_ref, k_ref, v_ref = o_refs[len(outs):]
        p = _dot(xn, w_ref[:, off:])
        q_ref[...], k_ref[...], v_ref[...] = _mla_qkv(p, cos_ref[...], sin_ref[...], qn_ref[...], kvn_ref[...],
                                                      wuq_ref[...], wukv_ref[...])


def _norm_proj(x, g, w, outs, seq, *, mla=None, tm=512, col_chunk=512, name="norm_proj"):
    m, d = x.shape
    assert sum(o[0] for o in outs) + (MLA_IN_W if mla else 0) == w.shape[1]
    per = seq // tm
    rows = lambda width: pl.BlockSpec((tm, width), lambda i: (i, 0))
    cols = lambda width: pl.BlockSpec((None, width, tm), lambda i: (i // per, 0, i % per))
    out_specs, out_shape = [], []
    for width, dt, _, transposed in tuple(outs) + (MLA_OUTS if mla else ()):
        if transposed:
            out_specs.append(cols(width))
            out_shape.append(jax.ShapeDtypeStruct((m // seq, width, seq), dt))
        else:
            out_specs.append(rows(width))
            out_shape.append(jax.ShapeDtypeStruct((m, width), dt))
    in_specs = [rows(d), _resident((1, d)), _resident(w)]
    args = [x, g, _operand(w)]
    if mla:
        cos, sin, q_norm, kv_norm, w_uq, w_ukv = mla
        in_specs += [rows(LANES), rows(LANES), _resident(q_norm), _resident(kv_norm), _resident(w_uq),
                     _resident(w_ukv)]
        args += [cos, sin, q_norm, kv_norm, _operand(w_uq), _operand(w_ukv)]
    return pl.pallas_call(
        functools.partial(_norm_proj_kernel, outs=tuple(outs), col_chunk=col_chunk, mla=mla is not None),
        grid=(m // tm,),
        in_specs=in_specs,
        out_specs=out_specs,
        out_shape=out_shape,
        compiler_params=_params("parallel"),
        name=name,
    )(*args)


def _rope_table_kernel(pos_ref, inv_ref, sign_ref, cos_ref, sin_ref):
    ang = pos_ref[...].astype(F32) * inv_ref[...]
    cos_ref[...] = jnp.cos(ang)
    sin_ref[...] = jnp.sin(ang) * sign_ref[...]


def _rope_tables(pos_col, inv_row, sign_row, *, tm=1024):
    m = pos_col.shape[0]
    return pl.pallas_call(
        _rope_table_kernel,
        grid=(m // tm,),
        in_specs=[pl.BlockSpec((tm, 1), lambda i: (i, 0)), _resident((1, LANES)), _resident((1, LANES))],
        out_specs=[pl.BlockSpec((tm, LANES), lambda i: (i, 0))] * 2,
        out_shape=[jax.ShapeDtypeStruct((m, LANES), F32)] * 2,
        compiler_params=_params("parallel"),
        name="rope_tables",
    )(pos_col, inv_row, sign_row)


def _rope_rows(period, half, start):
    lane = jnp.arange(LANES) % period
    inside = (lane >= start) & (lane < start + 2 * half)
    idx = (lane - start) % half
    inv = jnp.exp(-math.log(ROPE_BASE) * idx.astype(F32) / half)
    inv = jnp.where(inside, inv, 0.0)
    sign = jnp.where(inside, jnp.where(lane < start + half, -1.0, 1.0), 0.0)
    return inv.reshape(1, LANES).astype(F32), sign.reshape(1, LANES).astype(F32)


def _apply_rope(x, cos, sin_signed, period, half, start):
    width = x.shape[-1]
    reps = width // LANES
    if reps > 1:
        cos = jnp.concatenate([cos] * reps, axis=-1)
        sin_signed = jnp.concatenate([sin_signed] * reps, axis=-1)
    lane = lax.broadcasted_iota(jnp.int32, (1, width), 1) % period
    first = lane < start + half
    partner = jnp.where(first, pltpu.roll(x, width - half, 1), pltpu.roll(x, half, 1))
    return x * cos + partner * sin_signed


def _ret_kernel(q_ref, k_ref, v_ref, g_ref, cos_ref, sin_ref, intra_ref, qdec_ref, kdec_ref,
                sdec_ref, bmask_ref, o_ref, state_sc):
    @pl.when(pl.program_id(0) == 0)
    def _():
        state_sc[...] = jnp.zeros_like(state_sc)

    head = lax.broadcasted_iota(jnp.int32, (1, RET_QK), 1) // RET_DK
    for b in range(q_ref.shape[0]):
        cos, sin = cos_ref[b], sin_ref[b]
        q = _apply_rope(q_ref[b], cos, sin, RET_DK, RET_DK // 2, 0) * (RET_DK ** -0.5)
        k = _apply_rope(k_ref[b], cos, sin, RET_DK, RET_DK // 2, 0)
        vb = v_ref[b].astype(BF16)
        kb = k.astype(BF16)
        state = state_sc[b]
        cross = _dot((q * qdec_ref[...]).astype(BF16), state.astype(BF16))
        outs = []
        for h in range(RET_HEADS):
            qh = jnp.where(head == h, q, 0.0).astype(BF16)
            att = _dot_nt(qh, kb) * intra_ref[h]
            sl = slice(h * RET_DV, (h + 1) * RET_DV)
            o = _dot(att.astype(BF16), vb[:, sl]) + cross[:, sl]
            mu = jnp.mean(o, axis=-1, keepdims=True)
            oc = o - mu
            outs.append(oc * lax.rsqrt(jnp.mean(oc * oc, axis=-1, keepdims=True) + EPS))
        o_ref[b] = (jnp.concatenate(outs, axis=-1) * _silu(g_ref[b])).astype(o_ref.dtype)
        kv = _dot_tn((k * kdec_ref[...]).astype(BF16), vb)
        state_sc[b] = state * sdec_ref[...] + kv * bmask_ref[...]


def _retention_consts():
    c = CHUNK
    log_g = jnp.log1p(-jnp.exp2(-5.0 - jnp.arange(RET_HEADS, dtype=F32)))
    i = jnp.arange(c, dtype=F32)
    rel = i[:, None] - i[None, :]
    intra = jnp.where(rel >= 0, jnp.exp(log_g[:, None, None] * jnp.maximum(rel, 0.0)), 0.0)
    lg_lane = jnp.repeat(log_g, RET_DK)
    qdec = jnp.exp(lg_lane[None, :] * (i[:, None] + 1.0))
    kdec = jnp.exp(lg_lane[None, :] * (c - 1.0 - i[:, None]))
    row_h = jnp.repeat(jnp.arange(RET_HEADS), RET_DK)
    col_h = jnp.repeat(jnp.arange(RET_HEADS), RET_DV)
    bmask = (row_h[:, None] == col_h[None, :]).astype(F32)
    sdec = bmask * jnp.exp(lg_lane * c)[:, None]
    return intra, qdec, kdec, sdec, bmask


def _retention(ret_p, cos, sin):
    c = CHUNK
    batch, seq, _ = ret_p.shape
    intra, qdec, kdec, sdec, bmask = _retention_consts()
    blk = lambda width, col: pl.BlockSpec((batch, c, width), lambda j: (0, j, col))
    return pl.pallas_call(
        _ret_kernel,
        grid=(seq // c,),
        in_specs=[blk(RET_QK, 0), blk(RET_QK, 1), blk(BRANCH_W, 1), blk(BRANCH_W, 2), blk(LANES, 0), blk(LANES, 0),
                  _resident(intra.shape), _resident(qdec.shape), _resident(kdec.shape),
                  _resident(sdec.shape), _resident(bmask.shape)],
        out_specs=blk(BRANCH_W, 0),
        out_shape=jax.ShapeDtypeStruct((batch, seq, BRANCH_W), BF16),
        scratch_shapes=[pltpu.VMEM((batch, RET_QK, BRANCH_W), F32)],
        compiler_params=_params("arbitrary"),
        name="retention",
    )(ret_p, ret_p, ret_p, ret_p, cos, sin, intra, qdec, kdec, sdec, bmask)


def _split3(x):
    hi = x.astype(BF16)
    r = x - hi.astype(F32)
    mid = r.astype(BF16)
    lo = (r - mid.astype(F32)).astype(BF16)
    return hi, mid, lo


def _dot_split3(a, b, *, left):
    if left:
        return sum(_dot(piece, b) for piece in _split3(a))
    return sum(_dot(a, piece) for piece in _split3(b))


def _ssd_kernel(xbc_ref, *refs):
    xe_sc, state_sc = refs[-2:]

    @pl.when(pl.program_id(0) == 0)
    def _():
        xe_sc[...] = jnp.zeros_like(xe_sc)
        state_sc[...] = jnp.zeros_like(state_sc)

    for b in range(xbc_ref.shape[0]):
        _ssd_chunk(b, xbc_ref, *refs)


def _ssd_chunk(b, xbc_ref, z_ref, dt_ref, cw_ref, cb_ref, dtb_ref, alog_ref, dskip_ref, ng_ref, tri_ref,
               expand_ref, o_ref, xe_sc, state_sc):
    L = CHUNK
    T = SSM_TAIL
    GW = SSM_INNER // SSM_GROUPS
    x_raw = xbc_ref[b]
    tail = xe_sc[b]
    rows8 = lax.broadcasted_iota(jnp.int32, (T, 1), 0)
    conv = cb_ref[...] + cw_ref[SSM_CONV - 1:SSM_CONV, :] * x_raw
    for s in range(1, SSM_CONV):
        shifted = pltpu.roll(x_raw, s, 0)
        head = jnp.where(rows8 < s, pltpu.roll(tail, s, 0), shifted[0:T, :])
        shifted = jnp.concatenate([head, shifted[T:, :]], axis=0)
        conv = conv + cw_ref[SSM_CONV - 1 - s:SSM_CONV - s, :] * shifted
    xe_sc[b] = x_raw[L - T:L, :]
    xa = _silu(conv)
    xs = xa[:, :SSM_INNER]
    bm = xa[:, SSM_INNER:SSM_INNER + SSM_GROUPS * SSM_STATE]
    cm = xa[:, SSM_INNER + SSM_GROUPS * SSM_STATE:]

    dtr = dt_ref[b] + dtb_ref[...]
    dt_h = jnp.maximum(dtr, 0.0) + jnp.log1p(jnp.exp(-jnp.abs(dtr)))
    da = dt_h * (-jnp.exp(alog_ref[...]))
    cs_t = _dot_split3(da, tri_ref[...], left=True)
    cs_t_last = cs_t[:, L - 1:L]
    per_head = jnp.concatenate([dt_h, jnp.exp(cs_t), jnp.exp(cs_t_last - cs_t), cs_t], axis=0)
    per_chan = _dot_tn(jnp.concatenate(_split3(per_head), axis=0), expand_ref[...])
    dt, ecs, eds, cs_c = (per_chan[:, i * SSM_INNER:(i + 1) * SSM_INNER] for i in range(4))
    xdt = xs * dt
    xdec = (xdt * eds).astype(BF16)
    state = state_sc[b]
    li = lax.broadcasted_iota(jnp.int32, (L, L), 0)
    si = lax.broadcasted_iota(jnp.int32, (L, L), 1)
    causal = li >= si
    lane_half = lax.broadcasted_iota(jnp.int32, (1, LANES), 1) // SSM_HEADDIM
    ys = []
    new_states = []
    for g in range(SSM_GROUPS):
        bg = bm[:, g * SSM_STATE:(g + 1) * SSM_STATE].astype(BF16)
        cg = cm[:, g * SSM_STATE:(g + 1) * SSM_STATE].astype(BF16)
        cb = _dot_nt(cg, bg)
        gsl = slice(g * GW, (g + 1) * GW)
        y_off = _dot(cg, state[:, gsl].astype(BF16)) * ecs[:, gsl]
        for p in range(GW // LANES):
            sl = slice(g * GW + p * LANES, g * GW + (p + 1) * LANES)
            xdt_blk = xdt[:, sl]
            acc = y_off[:, p * LANES:(p + 1) * LANES]
            for hh in range(LANES // SSM_HEADDIM):
                head = (g * GW + p * LANES) // SSM_HEADDIM + hh
                col = cs_c[:, head * SSM_HEADDIM:head * SSM_HEADDIM + 1]
                rowv = cs_t[head:head + 1, :]
                dec = jnp.where(causal, jnp.exp(jnp.minimum(col - rowv, 0.0)), 0.0)
                xm = jnp.where(lane_half == hh, xdt_blk, 0.0).astype(BF16)
                acc = acc + _dot((cb * dec).astype(BF16), xm)
            ys.append(acc)
        new_states.append(_dot_tn(bg, xdec[:, gsl]))
    y = jnp.concatenate(ys, axis=-1) + xs * dskip_ref[...]
    y = y * _silu(z_ref[b])
    o_ref[b] = _rms(y, ng_ref[...]).astype(o_ref.dtype)
    state_sc[b] = state * ecs[L - 1:L, :] + jnp.concatenate(new_states, axis=-1)


def _ssd(ssm_p, dt_t, conv_w, conv_b, dt_bias_col, a_log_col, d_full, norm_g):
    L = CHUNK
    batch, seq, _ = ssm_p.shape
    tri = (jnp.arange(L)[:, None] <= jnp.arange(L)[None, :]).astype(BF16)
    r = jnp.arange(3 * 4 * SSM_HEADS)
    c = jnp.arange(4 * SSM_INNER)
    expand = (((r[:, None] // SSM_HEADS) % 4 == c[None, :] // SSM_INNER)
              & (r[:, None] % SSM_HEADS == (c[None, :] % SSM_INNER) // SSM_HEADDIM)).astype(BF16)
    return pl.pallas_call(
        _ssd_kernel,
        grid=(seq // L,),
        in_specs=[pl.BlockSpec((batch, L, SSM_XBC), lambda j: (0, j, 0)),
                  pl.BlockSpec((batch, L, SSM_INNER), lambda j: (0, j, SSM_XBC // SSM_INNER)),
                  pl.BlockSpec((batch, SSM_HEADS, L), lambda j: (0, 0, j)),
                  _resident(conv_w.shape), _resident(conv_b.shape), _resident(dt_bias_col.shape),
                  _resident(a_log_col.shape), _resident(d_full.shape), _resident(norm_g.shape),
                  _resident(tri.shape), _resident(expand.shape)],
        out_specs=pl.BlockSpec((batch, L, SSM_INNER), lambda j: (0, j, 0)),
        out_shape=jax.ShapeDtypeStruct((batch, seq, SSM_INNER), BF16),
        scratch_shapes=[pltpu.VMEM((batch, SSM_TAIL, SSM_XBC), F32),
                        pltpu.VMEM((batch, SSM_STATE, SSM_INNER), F32)],
        compiler_params=_params("arbitrary"),
        name="ssd",
    )(ssm_p, ssm_p, dt_t, conv_w, conv_b, dt_bias_col, a_log_col, d_full, norm_g, tri, expand)


def _flash_kernel(*refs, tq, tk, n_maps, lambda_init):
    biased = n_maps == 2
    if biased:
        (pmin_ref, pmax_ref, qt_ref, k_ref, vt_ref, pq_ref, pk_ref, ftab_ref, lam_ref, sg_ref,
         o_ref, m_sc, l_sc, acc_sc, s0_sc, s1_sc) = refs
    else:
        qt_ref, k_ref, vt_ref, o_ref, m_sc, l_sc, acc_sc, s0_sc, s1_sc = refs
    bi = pl.program_id(0)
    qi = pl.program_id(1)
    hw = HEAD_W
    heads = k_ref.shape[-1] // hw
    cols = n_maps * tq
    hs = lambda h: slice(h * hw, (h + 1) * hw)
    ws = []
    for h in range(heads):
        qt = qt_ref[hs(h), :]
        if biased:
            half = lax.broadcasted_iota(jnp.int32, (hw, 1), 0) // DIFF_DH
            zero = jnp.zeros_like(qt)
            qt = jnp.concatenate([jnp.where(half == 0, qt, zero), jnp.where(half == 1, qt, zero)], axis=1)
        ws.append(qt)
    if biased:
        pq = pq_ref[...]
        ftabs = [jnp.broadcast_to(ftab_ref[h] * LOG2E, (tk, REL_TABLE)) for h in range(heads)]
    m_sc[...] = jnp.full(m_sc.shape, NEG_INF, F32)
    l_sc[...] = jnp.zeros_like(l_sc)
    acc_sc[...] = jnp.zeros_like(acc_sc)

    def near_biases(off, q_lo=0):
        idx = jnp.clip(pq[:, q_lo:] - pk_ref[pl.ds(off, tk), :], 0, REL_TABLE - 1)
        tiles = [jnp.concatenate([jnp.take_along_axis(ftabs[h], idx[:, c:c + LANES], axis=1,
                                                      mode="promise_in_bounds")
                                  for c in range(0, tq - q_lo, LANES)], axis=1) for h in range(heads)]

        def bias(h, cs):
            base = (cs.start // tq) * tq + q_lo
            return tiles[h][:, cs.start - base:cs.stop - base]
        return bias

    s_scs = (s0_sc, s1_sc)

    all_cols = tuple(slice(mi * tq, (mi + 1) * tq) for mi in range(n_maps))
    late_cols = tuple(slice(mi * tq + tk, (mi + 1) * tq) for mi in range(n_maps))

    def scores(off, slot, col_ranges=all_cols):
        for h in range(heads):
            for cs in col_ranges:
                s_scs[slot][h, :, cs] = _dot(k_ref[pl.ds(off, tk), hs(h)], ws[h][:, cs])

    def consume(off, slot, bias, mask, const=None, col_ranges=all_cols):
        for h in range(heads):
            for cs in col_ranges:
                s = s_scs[slot][h, :, cs]
                if bias is not None:
                    s = s + bias(h, cs)
                if mask is not None:
                    s = jnp.where(mask[:, cs], s, NEG_INF)
                m = m_sc[h, :, cs]
                m_tile = jnp.max(s, axis=0, keepdims=True)
                if const is not None:
                    m_tile = m_tile + const[h]
                m_new = jnp.maximum(m, m_tile)
                alpha = jnp.exp2(m - m_new)
                p = jnp.exp2(s - (m_new if const is None else m_new - const[h]))
                l_sc[h, :, cs] = alpha * l_sc[h, :, cs] + jnp.sum(p, axis=0, keepdims=True)
                acc_sc[h, :, cs] = (alpha * acc_sc[h, :, cs]
                                    + _dot(vt_ref[hs(h), pl.ds(off, tk)], p.astype(BF16)))
                m_sc[h, :, cs] = m_new

    assert tq == 2 * tk
    scores(0, 0)

    def body(jj, carry):
        off0 = pl.multiple_of(jj * tq, tq)
        off1 = pl.multiple_of(off0 + tk, tk)
        off2 = pl.multiple_of(off0 + tq, tq)

        def pair(near0, near1):
            const = [ftab_ref[h][:, REL_TABLE - 1:REL_TABLE] * LOG2E for h in range(heads)] if biased else None
            scores(off1, 1)
            consume(off0, 0, near_biases(off0) if near0 else None, None, None if near0 else const)
            scores(off2, 0)
            consume(off1, 1, near_biases(off1) if near1 else None, None, None if near1 else const)

        if biased:
            far0 = pmin_ref[bi, qi] - pmax_ref[bi, 2 * jj] >= REL_TABLE - 1
            far1 = pmin_ref[bi, qi] - pmax_ref[bi, 2 * jj + 1] >= REL_TABLE - 1

            @pl.when(far0 & far1)
            def _():
                pair(False, False)

            @pl.when(far0 & jnp.logical_not(far1))
            def _():
                pair(False, True)

            @pl.when(jnp.logical_not(far0))
            def _():
                pair(True, True)
        else:
            pair(False, False)
        return carry

    lax.fori_loop(0, qi, body, 0)
    for d in range(2):
        off = pl.multiple_of(qi * tq + d * tk, tk)
        if d == 0:
            scores(pl.multiple_of(off + tk, tk), 1, late_cols)
        key = lax.broadcasted_iota(jnp.int32, (tk, cols), 0) + d * tk
        qry = lax.broadcasted_iota(jnp.int32, (tk, cols), 1) % tq
        consume(off, d, near_biases(off, d * tk) if biased else None, key <= qry,
                col_ranges=all_cols if d == 0 else late_cols)

    if biased:
        lp = lam_ref[...]
        lam = (jnp.exp(jnp.sum(lp[0:1] * lp[1:2], axis=-1, keepdims=True))
               - jnp.exp(jnp.sum(lp[2:3] * lp[3:4], axis=-1, keepdims=True)) + lambda_init)
    for h in range(heads):
        o = acc_sc[h] / l_sc[h]
        if biased:
            o = (o[:, :tq] - lam * o[:, tq:]).T
            o_ref[:, hs(h)] = (_rms(o, sg_ref[...]) * (1.0 - lambda_init)).astype(o_ref.dtype)
        else:
            o_ref[:, hs(h)] = o.astype(o_ref.dtype).T


def _flash_scratch(heads, tk, cols):
    return [pltpu.VMEM((heads, 1, cols), F32), pltpu.VMEM((heads, 1, cols), F32),
            pltpu.VMEM((heads, HEAD_W, cols), F32),
            pltpu.VMEM((heads, tk, cols), F32), pltpu.VMEM((heads, tk, cols), F32)]


def _mla_attention(qt, k, vt, *, tq=ATTN_TQ, tk=ATTN_TK):
    b, s, w = k.shape
    return pl.pallas_call(
        functools.partial(_flash_kernel, tq=tq, tk=tk, n_maps=1, lambda_init=None),
        grid=(b, s // tq),
        in_specs=[pl.BlockSpec((None, w, tq), lambda bi, i: (bi, 0, i)),
                  pl.BlockSpec((None, s, w), lambda bi, i: (bi, 0, 0), pipeline_mode=pl.Buffered(1)),
                  pl.BlockSpec((None, w, s), lambda bi, i: (bi, 0, 0), pipeline_mode=pl.Buffered(1))],
        out_specs=pl.BlockSpec((None, tq, w), lambda bi, i: (bi, i, 0)),
        out_shape=jax.ShapeDtypeStruct((b, s, w), BF16),
        scratch_shapes=_flash_scratch(w // HEAD_W, tk, tq),
        compiler_params=_params("parallel", "arbitrary"),
        name="mla_attention",
    )(qt, k, vt)


def _diff_attention(qt, k, vt, pos, ftab, lam_params, sub_g, lambda_init, *, tq=ATTN_TQ, tk=ATTN_TK):
    b, s, w = k.shape
    pmin = jnp.min(pos.reshape(b, s // tq, tq), axis=-1)
    pmax = jnp.max(pos.reshape(b, s // tk, tk), axis=-1)
    pos_col = pos.reshape(b, s, 1)
    pos_row = pos.reshape(b, 1, s)
    grid_spec = pltpu.PrefetchScalarGridSpec(
        num_scalar_prefetch=2,
        grid=(b, s // tq),
        in_specs=[pl.BlockSpec((None, w, tq), lambda bi, i, *_: (bi, 0, i)),
                  pl.BlockSpec((None, s, w), lambda bi, i, *_: (bi, 0, 0), pipeline_mode=pl.Buffered(1)),
                  pl.BlockSpec((None, w, s), lambda bi, i, *_: (bi, 0, 0), pipeline_mode=pl.Buffered(1)),
                  pl.BlockSpec((None, 1, tq), lambda bi, i, *_: (bi, 0, i)),
                  pl.BlockSpec((None, s, 1), lambda bi, i, *_: (bi, 0, 0), pipeline_mode=pl.Buffered(1)),
                  pl.BlockSpec(ftab.shape, lambda bi, i, *_: (0, 0, 0)),
                  pl.BlockSpec(lam_params.shape, lambda bi, i, *_: (0, 0)),
                  pl.BlockSpec(sub_g.shape, lambda bi, i, *_: (0, 0))],
        out_specs=pl.BlockSpec((None, tq, w), lambda bi, i, *_: (bi, i, 0)),
        scratch_shapes=_flash_scratch(w // HEAD_W, tk, 2 * tq),
    )
    return pl.pallas_call(
        functools.partial(_flash_kernel, tq=tq, tk=tk, n_maps=2, lambda_init=lambda_init),
        grid_spec=grid_spec,
        out_shape=jax.ShapeDtypeStruct((b, s, w), BF16),
        compiler_params=_params("parallel", "arbitrary"),
        name="diff_attention",
    )(pmin, pmax, qt, k, vt, pos_row, pos_col, ftab, lam_params, sub_g)


def _merge_kernel(h_ref, g_ref, y0_ref, y1_ref, y2_ref, y3_ref, wg_ref, wb_ref, wo_ref, o_ref, *, col_chunk):
    h = h_ref[...]
    d = h.shape[-1]
    u = _rms(h, g_ref[...]).astype(BF16)
    ys = (y0_ref[...], y1_ref[...], y2_ref[...], y3_ref[...])
    parts = []
    for c0 in range(0, d, col_chunk):
        acc = None
        for i, y in enumerate(ys):
            gate = jax.nn.sigmoid(_dot(u, wg_ref[:, i * d + c0:i * d + c0 + col_chunk]))
            term = gate * _dot(y, wb_ref[i, :, c0:c0 + col_chunk])
            acc = term if acc is None else acc + term
        parts.append(acc.astype(BF16))
    merged = jnp.concatenate(parts, axis=-1)
    o_ref[...] = h + _dot(merged, wo_ref[...])


def _merge(h, g, ys, w_gate, w_branch, w_out, *, tm=1024, col_chunk=256):
    m, d = h.shape
    rows = lambda width: pl.BlockSpec((tm, width), lambda i: (i, 0))
    return pl.pallas_call(
        functools.partial(_merge_kernel, col_chunk=col_chunk),
        grid=(m // tm,),
        in_specs=[rows(d), _resident((1, d))] + [rows(BRANCH_W)] * N_BRANCH
                 + [_resident(w_gate), _resident(w_branch), _resident(w_out)],
        out_specs=rows(d),
        out_shape=jax.ShapeDtypeStruct((m, d), F32),
        compiler_params=_params("parallel"),
        name="merge",
    )(h, g, *ys, _operand(w_gate), _operand(w_branch), _operand(w_out))


def _xattn_kernel(h_ref, g_ref, wq_ref, k_ref, v_ref, wo_ref, o_ref):
    h = h_ref[...]
    d = h.shape[-1]
    dh = d // XA_HEADS
    u = _rms(h, g_ref[...]).astype(BF16)
    q = (_dot(u, wq_ref[...].astype(BF16)) * (dh ** -0.5 * LOG2E)).astype(BF16)
    out = h
    for hd in range(XA_HEADS):
        sl = slice(hd * dh, (hd + 1) * dh)
        s = _dot_nt(q[:, sl], k_ref[:, sl])
        p = jnp.exp2(s - jnp.max(s, axis=-1, keepdims=True))
        inv = 1.0 / jnp.sum(p, axis=-1, keepdims=True)
        o = _dot(p.astype(BF16), v_ref[:, sl]) * inv
        out = out + _dot(o.astype(BF16), wo_ref[sl, :].astype(BF16))
    o_ref[...] = out


def _cross_attention(h, g, wq, k, v, wo, batch, seq, *, tm=1024):
    m, d = h.shape
    mem_len = k.shape[0] // batch
    per = seq // tm
    return pl.pallas_call(
        _xattn_kernel,
        grid=(m // tm,),
        in_specs=[pl.BlockSpec((tm, d), lambda i: (i, 0)), _resident((1, d)), _resident(wq),
                  pl.BlockSpec((mem_len, d), lambda i: (i // per, 0)),
                  pl.BlockSpec((mem_len, d), lambda i: (i // per, 0)),
                  _resident(wo)],
        out_specs=pl.BlockSpec((tm, d), lambda i: (i, 0)),
        out_shape=jax.ShapeDtypeStruct((m, d), F32),
        compiler_params=_params("parallel"),
        name="cross_attention",
    )(h, g, _operand(wq), k, v, _operand(wo))


def _pad_cols(w, left, total):
    return jnp.pad(w, ((0, 0),) * (w.ndim - 1) + ((left, total - left - w.shape[-1]),))


def _w_in_plan():
    names = ("rq", "rk", "rv", "rg", "sz", "sxbc", "sdt", "cq", "ckv", "kr", "dq", "dk", "dv", "gl")
    src, start = {}, 0
    for n, s in zip(names, IN_SIZES):
        src[n] = (start, s)
        start += s
    half = MLA_ROPE // 2
    kr0 = src["kr"][0]
    moves, zeros, dst = [], [], 0
    for n in ("rq", "rk", "rv", "rg", "sxbc", "sz"):
        moves.append((*src[n], dst))
        dst += src[n][1]
    zeros.append((dst, SSD_DT_W))
    moves.append((*src["sdt"], dst))
    dst += SSD_DT_W
    for n in ("dq", "dk", "dv", "cq", "ckv"):
        moves.append((*src[n], dst))
        dst += src[n][1]
    zeros.append((dst, 2 * MLA_PAD))
    moves.append((kr0, MLA_ROPE, dst + MLA_NOPE))
    moves.append((kr0 + half, half, dst + MLA_PAD + MLA_NOPE))
    moves.append((kr0, half, dst + MLA_PAD + MLA_NOPE + half))
    dst += 2 * MLA_PAD
    return moves, zeros, src["gl"], dst


def _w_in_relayout_kernel(x_ref, mix_ref, gate_ref, *, moves, zeros, gate):
    for d0, width in zeros:
        mix_ref[:, d0:d0 + width] = jnp.zeros((mix_ref.shape[0], width), mix_ref.dtype)
    for s0, width, d0 in moves:
        mix_ref[:, d0:d0 + width] = x_ref[:, s0:s0 + width].astype(mix_ref.dtype)
    gate_ref[...] = x_ref[:, gate[0]:gate[0] + gate[1]].astype(gate_ref.dtype)


def _w_in_relayout(w_in, *, row_blocks=4):
    depth, r, c = w_in.shape
    moves, zeros, gate, mix_w = _w_in_plan()
    tr = r // row_blocks
    blk = lambda width: pl.BlockSpec((None, tr, width), lambda l, i: (l, i, 0))
    return pl.pallas_call(
        functools.partial(_w_in_relayout_kernel, moves=moves, zeros=zeros, gate=gate),
        grid=(depth, row_blocks),
        in_specs=[blk(c)],
        out_specs=[blk(mix_w), blk(gate[1])],
        out_shape=[jax.ShapeDtypeStruct((depth, r, mix_w), BF16), jax.ShapeDtypeStruct((depth, r, gate[1]), BF16)],
        compiler_params=_params("parallel", "parallel"),
        name="w_in_relayout",
    )(w_in)


def _mla_up_weights(w_uq, w_ukv):
    depth, d, _ = w_uq.shape
    half = MLA_ROPE // 2
    uq = w_uq.astype(BF16).reshape(depth, d, MLA_HEADS, MLA_QH)
    pe = uq[..., MLA_NOPE:]
    partner = jnp.concatenate([jnp.zeros_like(uq[..., :MLA_NOPE]), pe[..., half:], pe[..., :half]], axis=-1)
    pad = lambda a: _pad_cols(a, 0, MLA_PAD).reshape(depth, d, MLA_HEADS * MLA_PAD)
    uq_p = jnp.concatenate([pad(uq), pad(partner)], axis=-1)
    r = w_ukv.shape[1]
    ukv = w_ukv.astype(BF16).reshape(depth, r, MLA_HEADS, MLA_NOPE + MLA_V)
    k_nope = _pad_cols(ukv[..., :MLA_NOPE], 0, MLA_PAD)
    ukv_p = jnp.concatenate([k_nope.reshape(depth, r, -1), ukv[..., MLA_NOPE:].reshape(depth, r, -1)], axis=-1)
    return uq_p, ukv_p


def _t5_bucket_const():
    n = jnp.arange(REL_TABLE)
    exact = REL_BUCKETS // 2
    nf = jnp.maximum(n, 1).astype(F32)
    large = exact + (jnp.log(nf / exact) / math.log(REL_MAX_DIST / exact) * (REL_BUCKETS - exact)).astype(jnp.int32)
    return jnp.where(n < exact, n, jnp.minimum(large, REL_BUCKETS - 1))


MIX_OUTS = ((3 * BRANCH_W, F32, 1.0, False),
            (SSM_XBC + SSM_INNER, F32, 1.0, False),
            (SSD_DT_W, F32, 1.0, True),
            (BRANCH_W, BF16, DIFF_DH ** -0.5 * LOG2E, True),
            (BRANCH_W, BF16, 1.0, False),
            (BRANCH_W, BF16, 1.0, True))


def kernel(x, mem, positions, ffn1_norm, ffn1_w_gate, ffn1_w_up, ffn1_w_down, mix_norm, w_in, ssm_conv_w,
           ssm_conv_b, ssm_dt_bias, ssm_a_log, ssm_d, ssm_norm, mla_q_norm, mla_kv_norm, mla_w_uq, mla_w_ukv,
           diff_lambda, diff_norm, rel_bias, w_branch, w_out, xa_norm, mem_norm, xa_wq, xa_wk, xa_wv, xa_wo,
           ffn2_norm, ffn2_w_gate, ffn2_w_up, ffn2_w_down, final_norm):
    batch, seq, d = x.shape
    depth = w_in.shape[0]
    m = batch * seq
    h = x.reshape(m, d)
    mem2 = mem.reshape(batch * mem.shape[1], d)
    row = lambda a: a.reshape(1, -1)

    pos_col = positions.reshape(m, 1)
    ret_cos, ret_sin = _rope_tables(pos_col, *_rope_rows(RET_DK, RET_DK // 2, 0))
    mla_cos, mla_sin = _rope_tables(pos_col, *_rope_rows(MLA_PAD, MLA_ROPE // 2, MLA_NOPE))
    ftab = rel_bias[_t5_bucket_const()].T.reshape(DIFF_HEADS, 1, REL_TABLE)

    w_mix, w_gate = _w_in_relayout(w_in)
    w_uq_p, w_ukv_p = _mla_up_weights(mla_w_uq, mla_w_ukv)
    ffn1 = (ffn1_w_gate, ffn1_w_up, ffn1_w_down)
    ffn2 = (ffn2_w_gate, ffn2_w_up, ffn2_w_down)
    w_branch_b = _cast_bf16(w_branch.reshape(depth, N_BRANCH * BRANCH_W, d)).reshape(w_branch.shape)
    w_out_b = _cast_bf16(w_out)
    w_kv = jnp.concatenate([_cast_bf16(xa_wk), _cast_bf16(xa_wv)], axis=-1)

    for l in range(depth):
        at = lambda stack: _LayerOf(stack, l)
        h = _ffn(h, row(ffn1_norm[l]), *(at(w) for w in ffn1))
        mla = (mla_cos, mla_sin, row(mla_q_norm[l]), row(mla_kv_norm[l]), at(w_uq_p), at(w_ukv_p))
        ret_p, ssm_p, dt_t, dqt, dk, dvt, mqt, mk, mvt = _norm_proj(h, row(mix_norm[l]), at(w_mix), MIX_OUTS, seq,
                                                                   mla=mla, name="mix_proj")
        y_ret = _retention(ret_p.reshape(batch, seq, -1), ret_cos.reshape(batch, seq, -1),
                           ret_sin.reshape(batch, seq, -1)).reshape(m, BRANCH_W)
        y_ssm = _ssd(ssm_p.reshape(batch, seq, -1), dt_t, ssm_conv_w[l], row(ssm_conv_b[l]),
                     ssm_dt_bias[l].reshape(-1, 1), ssm_a_log[l].reshape(-1, 1),
                     row(jnp.repeat(ssm_d[l], SSM_HEADDIM)), row(ssm_norm[l])).reshape(m, SSM_INNER)
        y_mla = _mla_attention(mqt, mk.reshape(batch, seq, -1), mvt).reshape(m, BRANCH_W)
        lambda_init = 0.8 - 0.6 * math.exp(-0.3 * l)
        y_diff = _diff_attention(dqt, dk.reshape(batch, seq, -1), dvt, positions, ftab, diff_lambda[l],
                                 row(diff_norm[l]), lambda_init).reshape(m, BRANCH_W)
        h = _merge(h, row(mix_norm[l]), (y_ret, y_ssm, y_mla, y_diff), at(w_gate), at(w_branch_b), at(w_out_b))
        xk, xv = _norm_proj(mem2, row(mem_norm[l]), at(w_kv), [(d, BF16, 1.0, False)] * 2, mem.shape[1],
                            tm=mem.shape[1], name="mem_kv")
        h = _cross_attention(h, row(xa_norm[l]), at(xa_wq), xk, xv, at(xa_wo), batch, seq)
        h = _ffn(h, row(ffn2_norm[l]), *(at(w) for w in ffn2),
                 final_g=row(final_norm) if l == depth - 1 else None)
    return h.reshape(batch, seq, d)
```

```python
import functools
import math
from typing import NamedTuple

import jax
import jax.numpy as jnp
from jax import lax
from jax.experimental import pallas as pl
from jax.experimental.pallas import tpu as pltpu

F32 = jnp.float32
BF16 = jnp.bfloat16

EPS = 1e-6
NEG_INF = -1e30
LOG2E = math.log2(math.e)
ROPE_BASE = 10000.0
CHUNK = 128
BRANCH_W = 512
N_BRANCH = 4

RET_HEADS = 4
RET_DK = 64
RET_DV = 128
RET_QK = RET_HEADS * RET_DK

SSM_HEADDIM = 64
SSM_INNER = 512
SSM_HEADS = 8
SSM_GROUPS = 2
SSM_STATE = 128
SSM_CONV = 4
SSM_XBC = SSM_INNER + 2 * SSM_GROUPS * SSM_STATE
SSM_TAIL = 8
SSD_DT_W = 128

MLA_HEADS = 4
MLA_Q_RANK = 256
MLA_KV_RANK = 128
MLA_NOPE = 64
MLA_ROPE = 32
MLA_V = 128
MLA_QH = MLA_NOPE + MLA_ROPE
MLA_PAD = 128
MLA_IN_W = MLA_Q_RANK + MLA_KV_RANK + 2 * MLA_PAD
MLA_OUTS = ((MLA_HEADS * MLA_PAD, jnp.bfloat16, 1.0, True), (MLA_HEADS * MLA_PAD, jnp.bfloat16, 1.0, False),
            (MLA_HEADS * MLA_V, jnp.bfloat16, 1.0, True))
HEAD_W = 128
ATTN_TQ = 512
ATTN_TK = 256

DIFF_HEADS = 4
DIFF_DH = 64

REL_BUCKETS = 32
REL_MAX_DIST = 128
REL_TABLE = 128

XA_HEADS = 4

LANES = 128
VMEM_LIMIT = 56 * 1024 * 1024

IN_SIZES = (RET_QK, RET_QK, BRANCH_W, BRANCH_W,
            SSM_INNER, SSM_XBC, SSM_HEADS,
            MLA_Q_RANK, MLA_KV_RANK, MLA_ROPE,
            BRANCH_W, BRANCH_W, BRANCH_W,
            N_BRANCH * 1024)


def _params(*sem):
    return pltpu.CompilerParams(dimension_semantics=sem, vmem_limit_bytes=VMEM_LIMIT)


class _LayerOf(NamedTuple):
    stack: jax.Array
    layer: int

    @property
    def shape(self):
        return self.stack.shape[1:]


def _operand(x):
    return x.stack if isinstance(x, _LayerOf) else x


def _resident(x):
    if isinstance(x, _LayerOf):
        shape, lead, first = x.shape, (None,), (x.layer,)
    else:
        shape, lead, first = (x if isinstance(x, tuple) else x.shape), (), ()
    nd = len(shape)
    return pl.BlockSpec(lead + tuple(shape), lambda *_: first + (0,) * nd, pipeline_mode=pl.Buffered(1))


def _cast_kernel(x_ref, o_ref):
    o_ref[...] = x_ref[...].astype(o_ref.dtype)


def _cast_bf16(w, *, row_blocks=4):
    depth, r, c = w.shape
    tr = r // row_blocks
    spec = pl.BlockSpec((None, tr, c), lambda l, i: (l, i, 0))
    return pl.pallas_call(
        _cast_kernel,
        grid=(depth, row_blocks),
        in_specs=[spec],
        out_specs=spec,
        out_shape=jax.ShapeDtypeStruct(w.shape, BF16),
        compiler_params=_params("parallel", "parallel"),
        name="cast_bf16",
    )(w)


def _rms(x, g):
    return x * lax.rsqrt(jnp.mean(x * x, axis=-1, keepdims=True) + EPS) * g


def _silu(x):
    half = 0.5 * x
    return half + half * jnp.tanh(half)


def _dot(a, b):
    return jnp.dot(a, b, preferred_element_type=F32)


def _dot_nt(a, b):
    return lax.dot_general(a, b, (((1,), (1,)), ((), ())), preferred_element_type=F32)


def _dot_tn(a, b):
    return lax.dot_general(a, b, (((0,), (0,)), ((), ())), preferred_element_type=F32)


def _ffn_kernel(h_ref, g_ref, wg_ref, wu_ref, wd_ref, *rest, ff_chunk, final):
    if final:
        gf_ref, o_ref = rest
    else:
        (o_ref,) = rest
    h = h_ref[...]
    xn = _rms(h, g_ref[...]).astype(BF16)
    acc = jnp.zeros(h.shape, F32)
    d_ff = wg_ref.shape[1]
    for c0 in range(0, d_ff, ff_chunk):
        a = _dot(xn, wg_ref[:, c0:c0 + ff_chunk].astype(BF16))
        b = _dot(xn, wu_ref[:, c0:c0 + ff_chunk].astype(BF16))
        acc = acc + _dot((_silu(a) * b).astype(BF16), wd_ref[c0:c0 + ff_chunk, :].astype(BF16))
    out = h + 0.5 * acc
    if final:
        out = _rms(out, gf_ref[...])
    o_ref[...] = out


def _ffn(h, g, wg, wu, wd, final_g=None, *, tm=512, ff_chunk=256):
    m, d = h.shape
    d_ff = wg.shape[1]
    final = final_g is not None
    in_specs = [pl.BlockSpec((tm, d), lambda i: (i, 0)), _resident((1, d)),
                _resident(wg), _resident(wu), _resident(wd)]
    args = [h, g, _operand(wg), _operand(wu), _operand(wd)]
    if final:
        in_specs.append(_resident((1, d)))
        args.append(final_g)
    return pl.pallas_call(
        functools.partial(_ffn_kernel, ff_chunk=ff_chunk, final=final),
        grid=(m // tm,),
        in_specs=in_specs,
        out_specs=pl.BlockSpec((tm, d), lambda i: (i, 0)),
        out_shape=jax.ShapeDtypeStruct((m, d), F32),
        compiler_params=_params("parallel"),
        name="ffn",
    )(*args)


def _mla_qkv(p, cos, sin, q_norm, kv_norm, w_uq, w_ukv):
    kvw = MLA_HEADS * MLA_PAD
    cq = p[:, :MLA_Q_RANK]
    ckv = p[:, MLA_Q_RANK:MLA_Q_RANK + MLA_KV_RANK]
    kr0 = MLA_Q_RANK + MLA_KV_RANK
    kr = p[:, kr0:kr0 + MLA_PAD]
    kr_partner = p[:, kr0 + MLA_PAD:]
    qq = _dot(_rms(cq, q_norm).astype(BF16), w_uq)
    cos4 = jnp.concatenate([cos] * MLA_HEADS, axis=-1)
    sin4 = jnp.concatenate([sin] * MLA_HEADS, axis=-1)
    q = (qq[:, :kvw] * cos4 + qq[:, kvw:] * sin4) * (MLA_QH ** -0.5 * LOG2E)
    kv = _dot(_rms(ckv, kv_norm).astype(BF16), w_ukv)
    kpe = kr * cos + kr_partner * sin
    k = kv[:, :kvw] + jnp.concatenate([kpe] * MLA_HEADS, axis=-1)
    return q.astype(BF16).T, k.astype(BF16), kv[:, kvw:].astype(BF16).T


def _norm_proj_kernel(x_ref, g_ref, w_ref, *rest, outs, col_chunk, mla):
    if mla:
        cos_ref, sin_ref, qn_ref, kvn_ref, wuq_ref, wukv_ref = rest[:6]
        o_refs = rest[6:]
    else:
        o_refs = rest
    xn = _rms(x_ref[...], g_ref[...]).astype(BF16)
    off = 0
    for o_ref, (width, _, scale, transposed) in zip(o_refs, outs):
        for c0 in range(0, width, col_chunk):
            c1 = min(c0 + col_chunk, width)
            y = _dot(xn, w_ref[:, off + c0:off + c1])
            if scale != 1.0:
                y = y * scale
            if transposed:
                o_ref[c0:c1, :] = y.astype(o_ref.dtype).T
            else:
                o_ref[:, c0:c1] = y.astype(o_ref.dtype)
        off += width
    if mla:
        q_ref, k_ref, v_ref = o_refs[len(outs):]
        p = _dot(xn, w_ref[:, off:])
        q_ref[...], k_ref[...], v_ref[...] = _mla_qkv(p, cos_ref[...], sin_ref[...], qn_ref[...], kvn_ref[...],
                                                      wuq_ref[...], wukv_ref[...])


def _norm_proj(x, g, w, outs, seq, *, mla=None, tm=512, col_chunk=512, name="norm_proj"):
    m, d = x.shape
    assert sum(o[0] for o in outs) + (MLA_IN_W if mla else 0) == w.shape[1]
    per = seq // tm
    rows = lambda width: pl.BlockSpec((tm, width), lambda i: (i, 0))
    cols = lambda width: pl.BlockSpec((None, width, tm), lambda i: (i // per, 0, i % per))
    out_specs, out_shape = [], []
    for width, dt, _, transposed in tuple(outs) + (MLA_OUTS if mla else ()):
        if transposed:
            out_specs.append(cols(width))
            out_shape.append(jax.ShapeDtypeStruct((m // seq, width, seq), dt))
        else:
            out_specs.append(rows(width))
            out_shape.append(jax.ShapeDtypeStruct((m, width), dt))
    in_specs = [rows(d), _resident((1, d)), _resident(w)]
    args = [x, g, _operand(w)]
    if mla:
        cos, sin, q_norm, kv_norm, w_uq, w_ukv = mla
        in_specs += [rows(LANES), rows(LANES), _resident(q_norm), _resident(kv_norm), _resident(w_uq),
                     _resident(w_ukv)]
        args += [cos, sin, q_norm, kv_norm, _operand(w_uq), _operand(w_ukv)]
    return pl.pallas_call(
        functools.partial(_norm_proj_kernel, outs=tuple(outs), col_chunk=col_chunk, mla=mla is not None),
        grid=(m // tm,),
        in_specs=in_specs,
        out_specs=out_specs,
        out_shape=out_shape,
        compiler_params=_params("parallel"),
        name=name,
    )(*args)


def _rope_table_kernel(pos_ref, inv_ref, idx_ref, sign_ref, *o_refs):
    ang = pos_ref[...].astype(F32) * inv_ref[...]
    cos, sin = jnp.cos(ang), jnp.sin(ang)
    for t in range(len(o_refs) // 2):
        idx = jnp.broadcast_to(idx_ref[t:t + 1, :], ang.shape)
        o_refs[2 * t][...] = jnp.take_along_axis(cos, idx, axis=1, mode="promise_in_bounds")
        o_refs[2 * t + 1][...] = (jnp.take_along_axis(sin, idx, axis=1, mode="promise_in_bounds")
                                  * sign_ref[t:t + 1, :])


def _rope_tables(pos_col, layouts, *, tm=1024):
    m = pos_col.shape[0]
    lane = jnp.arange(LANES)
    inv_parts, idx_rows, sign_rows, base = [], [], [], 0
    for period, half, start, n_freq in layouts:
        inv_parts.append(jnp.exp(-math.log(ROPE_BASE) * jnp.arange(n_freq, dtype=F32) / n_freq))
        rel = lane % period
        inside = (rel >= start) & (rel < start + 2 * half)
        idx_rows.append(jnp.where(inside, base + (rel - start) % half % n_freq, LANES - 1))
        sign_rows.append(jnp.where(inside, jnp.where(rel < start + half, -1.0, 1.0), 0.0))
        base += n_freq
    assert base < LANES
    inv = jnp.concatenate(inv_parts + [jnp.zeros((LANES - base,), F32)]).reshape(1, LANES)
    idx = jnp.stack(idx_rows).astype(jnp.int32)
    sign = jnp.stack(sign_rows).astype(F32)
    n_out = 2 * len(layouts)
    return pl.pallas_call(
        _rope_table_kernel,
        grid=(m // tm,),
        in_specs=[pl.BlockSpec((tm, 1), lambda i: (i, 0)), _resident(inv), _resident(idx), _resident(sign)],
        out_specs=[pl.BlockSpec((tm, LANES), lambda i: (i, 0))] * n_out,
        out_shape=[jax.ShapeDtypeStruct((m, LANES), F32)] * n_out,
        compiler_params=_params("parallel"),
        name="rope_tables",
    )(pos_col, inv, idx, sign)


def _ret_head_of_lane(width):
    lane = jnp.arange(width)
    return 2 * (lane // LANES) + (lane // (RET_DK // 2)) % 2


def _apply_rope_ret(x, cos, sin_signed):
    tiles = [x[:, t:t + LANES] for t in range(0, x.shape[-1], LANES)]
    return jnp.concatenate([xt * cos + pltpu.roll(xt, LANES // 2, 1) * sin_signed for xt in tiles], axis=-1)


def _ret_kernel(q_ref, k_ref, v_ref, g_ref, cos_ref, sin_ref, intra_ref, qdec_ref, kdec_ref,
                sdec_ref, bmask_ref, o_ref, state_sc):
    @pl.when(pl.program_id(0) == 0)
    def _():
        state_sc[...] = jnp.zeros_like(state_sc)

    lane = lax.broadcasted_iota(jnp.int32, (1, RET_QK), 1)
    head = 2 * (lane // LANES) + (lane // (RET_DK // 2)) % 2
    for b in range(q_ref.shape[0]):
        cos, sin = cos_ref[b], sin_ref[b]
        q = _apply_rope_ret(q_ref[b], cos, sin) * (RET_DK ** -0.5)
        k = _apply_rope_ret(k_ref[b], cos, sin)
        vb = v_ref[b].astype(BF16)
        kb = k.astype(BF16)
        state = state_sc[b]
        cross = _dot((q * qdec_ref[...]).astype(BF16), state.astype(BF16))
        outs = []
        for h in range(RET_HEADS):
            qh = jnp.where(head == h, q, 0.0).astype(BF16)
            att = _dot_nt(qh, kb) * intra_ref[h]
            sl = slice(h * RET_DV, (h + 1) * RET_DV)
            o = _dot(att.astype(BF16), vb[:, sl]) + cross[:, sl]
            mu = jnp.mean(o, axis=-1, keepdims=True)
            oc = o - mu
            outs.append(oc * lax.rsqrt(jnp.mean(oc * oc, axis=-1, keepdims=True) + EPS))
        o_ref[b] = (jnp.concatenate(outs, axis=-1) * _silu(g_ref[b])).astype(o_ref.dtype)
        kv = _dot_tn((k * kdec_ref[...]).astype(BF16), vb)
        state_sc[b] = state * sdec_ref[...] + kv * bmask_ref[...]


def _retention_consts():
    c = CHUNK
    log_g = jnp.log1p(-jnp.exp2(-5.0 - jnp.arange(RET_HEADS, dtype=F32)))
    i = jnp.arange(c, dtype=F32)
    rel = i[:, None] - i[None, :]
    intra = jnp.where(rel >= 0, jnp.exp(log_g[:, None, None] * jnp.maximum(rel, 0.0)), 0.0)
    row_h = _ret_head_of_lane(RET_QK)
    lg_lane = log_g[row_h]
    qdec = jnp.exp(lg_lane[None, :] * (i[:, None] + 1.0))
    kdec = jnp.exp(lg_lane[None, :] * (c - 1.0 - i[:, None]))
    col_h = jnp.repeat(jnp.arange(RET_HEADS), RET_DV)
    bmask = (row_h[:, None] == col_h[None, :]).astype(F32)
    sdec = bmask * jnp.exp(lg_lane * c)[:, None]
    return intra, qdec, kdec, sdec, bmask


def _retention(ret_p, cos, sin):
    c = CHUNK
    batch, seq, _ = ret_p.shape
    intra, qdec, kdec, sdec, bmask = _retention_consts()
    blk = lambda width, col: pl.BlockSpec((batch, c, width), lambda j: (0, j, col))
    return pl.pallas_call(
        _ret_kernel,
        grid=(seq // c,),
        in_specs=[blk(RET_QK, 0), blk(RET_QK, 1), blk(BRANCH_W, 1), blk(BRANCH_W, 2), blk(LANES, 0), blk(LANES, 0),
                  _resident(intra.shape), _resident(qdec.shape), _resident(kdec.shape),
                  _resident(sdec.shape), _resident(bmask.shape)],
        out_specs=blk(BRANCH_W, 0),
        out_shape=jax.ShapeDtypeStruct((batch, seq, BRANCH_W), BF16),
        scratch_shapes=[pltpu.VMEM((batch, RET_QK, BRANCH_W), F32)],
        compiler_params=_params("arbitrary"),
        name="retention",
    )(ret_p, ret_p, ret_p, ret_p, cos, sin, intra, qdec, kdec, sdec, bmask)


def _split3(x):
    hi = x.astype(BF16)
    r = x - hi.astype(F32)
    mid = r.astype(BF16)
    lo = (r - mid.astype(F32)).astype(BF16)
    return hi, mid, lo


def _dot_split3(a, b, *, left):
    if left:
        return sum(_dot(piece, b) for piece in _split3(a))
    return sum(_dot(a, piece) for piece in _split3(b))


def _ssd_kernel(xbc_ref, *refs):
    xe_sc, state_sc = refs[-2:]

    @pl.when(pl.program_id(0) == 0)
    def _():
        xe_sc[...] = jnp.zeros_like(xe_sc)
        state_sc[...] = jnp.zeros_like(state_sc)

    for b in range(xbc_ref.shape[0]):
        _ssd_chunk(b, xbc_ref, *refs)


def _ssd_chunk(b, xbc_ref, z_ref, dt_ref, cw_ref, cb_ref, dtb_ref, alog_ref, dskip_ref, ng_ref, tri_ref,
               expand_ref, o_ref, xe_sc, state_sc):
    L = CHUNK
    T = SSM_TAIL
    GW = SSM_INNER // SSM_GROUPS
    x_raw = xbc_ref[b]
    tail = xe_sc[b]
    rows8 = lax.broadcasted_iota(jnp.int32, (T, 1), 0)
    conv = cb_ref[...] + cw_ref[SSM_CONV - 1:SSM_CONV, :] * x_raw
    for s in range(1, SSM_CONV):
        shifted = pltpu.roll(x_raw, s, 0)
        head = jnp.where(rows8 < s, pltpu.roll(tail, s, 0), shifted[0:T, :])
        shifted = jnp.concatenate([head, shifted[T:, :]], axis=0)
        conv = conv + cw_ref[SSM_CONV - 1 - s:SSM_CONV - s, :] * shifted
    xe_sc[b] = x_raw[L - T:L, :]
    xa = _silu(conv)
    xs = xa[:, :SSM_INNER]
    bm = xa[:, SSM_INNER:SSM_INNER + SSM_GROUPS * SSM_STATE]
    cm = xa[:, SSM_INNER + SSM_GROUPS * SSM_STATE:]

    dtr = dt_ref[b] + dtb_ref[...]
    dt_h = jnp.maximum(dtr, 0.0) + jnp.log1p(jnp.exp(-jnp.abs(dtr)))
    da = dt_h * (-jnp.exp(alog_ref[...]))
    cs_t = _dot_split3(da, tri_ref[...], left=True)
    cs_t_last = cs_t[:, L - 1:L]
    per_head = jnp.concatenate([dt_h, jnp.exp(cs_t), jnp.exp(cs_t_last - cs_t), cs_t], axis=0)
    per_chan = _dot_tn(jnp.concatenate(_split3(per_head), axis=0), expand_ref[...])
    dt, ecs, eds, cs_c = (per_chan[:, i * SSM_INNER:(i + 1) * SSM_INNER] for i in range(4))
    xdt = xs * dt
    xdec = (xdt * eds).astype(BF16)
    state = state_sc[b]
    li = lax.broadcasted_iota(jnp.int32, (L, L), 0)
    si = lax.broadcasted_iota(jnp.int32, (L, L), 1)
    causal = li >= si
    lane_half = lax.broadcasted_iota(jnp.int32, (1, LANES), 1) // SSM_HEADDIM
    ys = []
    new_states = []
    for g in range(SSM_GROUPS):
        bg = bm[:, g * SSM_STATE:(g + 1) * SSM_STATE].astype(BF16)
        cg = cm[:, g * SSM_STATE:(g + 1) * SSM_STATE].astype(BF16)
        cb = _dot_nt(cg, bg)
        gsl = slice(g * GW, (g + 1) * GW)
        y_off = _dot(cg, state[:, gsl].astype(BF16)) * ecs[:, gsl]
        for p in range(GW // LANES):
            sl = slice(g * GW + p * LANES, g * GW + (p + 1) * LANES)
            xdt_blk = xdt[:, sl]
            acc = y_off[:, p * LANES:(p + 1) * LANES]
            for hh in range(LANES // SSM_HEADDIM):
                head = (g * GW + p * LANES) // SSM_HEADDIM + hh
                col = cs_c[:, head * SSM_HEADDIM:head * SSM_HEADDIM + 1]
                rowv = cs_t[head:head + 1, :]
                dec = jnp.where(causal, jnp.exp(jnp.minimum(col - rowv, 0.0)), 0.0)
                xm = jnp.where(lane_half == hh, xdt_blk, 0.0).astype(BF16)
                acc = acc + _dot((cb * dec).astype(BF16), xm)
            ys.append(acc)
        new_states.append(_dot_tn(bg, xdec[:, gsl]))
    y = jnp.concatenate(ys, axis=-1) + xs * dskip_ref[...]
    y = y * _silu(z_ref[b])
    o_ref[b] = _rms(y, ng_ref[...]).astype(o_ref.dtype)
    state_sc[b] = state * ecs[L - 1:L, :] + jnp.concatenate(new_states, axis=-1)


def _ssd(ssm_p, dt_t, conv_w, conv_b, dt_bias_col, a_log_col, d_full, norm_g):
    L = CHUNK
    batch, seq, _ = ssm_p.shape
    tri = (jnp.arange(L)[:, None] <= jnp.arange(L)[None, :]).astype(BF16)
    r = jnp.arange(3 * 4 * SSM_HEADS)
    c = jnp.arange(4 * SSM_INNER)
    expand = (((r[:, None] // SSM_HEADS) % 4 == c[None, :] // SSM_INNER)
              & (r[:, None] % SSM_HEADS == (c[None, :] % SSM_INNER) // SSM_HEADDIM)).astype(BF16)
    return pl.pallas_call(
        _ssd_kernel,
        grid=(seq // L,),
        in_specs=[pl.BlockSpec((batch, L, SSM_XBC), lambda j: (0, j, 0)),
                  pl.BlockSpec((batch, L, SSM_INNER), lambda j: (0, j, SSM_XBC // SSM_INNER)),
                  pl.BlockSpec((batch, SSM_HEADS, L), lambda j: (0, 0, j)),
                  _resident(conv_w.shape), _resident(conv_b.shape), _resident(dt_bias_col.shape),
                  _resident(a_log_col.shape), _resident(d_full.shape), _resident(norm_g.shape),
                  _resident(tri.shape), _resident(expand.shape)],
        out_specs=pl.BlockSpec((batch, L, SSM_INNER), lambda j: (0, j, 0)),
        out_shape=jax.ShapeDtypeStruct((batch, seq, SSM_INNER), BF16),
        scratch_shapes=[pltpu.VMEM((batch, SSM_TAIL, SSM_XBC), F32),
                        pltpu.VMEM((batch, SSM_STATE, SSM_INNER), F32)],
        compiler_params=_params("arbitrary"),
        name="ssd",
    )(ssm_p, ssm_p, dt_t, conv_w, conv_b, dt_bias_col, a_log_col, d_full, norm_g, tri, expand)


def _flash_kernel(*refs, tq, tk, n_maps, lambda_init):
    biased = n_maps == 2
    if biased:
        (pmin_ref, pmax_ref, qt_ref, k_ref, vt_ref, pq_ref, pk_ref, ftab_ref, lam_ref, sg_ref,
         o_ref, m_sc, l_sc, acc_sc, s0_sc, s1_sc) = refs
    else:
        qt_ref, k_ref, vt_ref, o_ref, m_sc, l_sc, acc_sc, s0_sc, s1_sc = refs
    bi = pl.program_id(0)
    qi = pl.program_id(1)
    hw = HEAD_W
    heads = k_ref.shape[-1] // hw
    cols = n_maps * tq
    hs = lambda h: slice(h * hw, (h + 1) * hw)
    ws = []
    for h in range(heads):
        qt = qt_ref[hs(h), :]
        if biased:
            half = lax.broadcasted_iota(jnp.int32, (hw, 1), 0) // DIFF_DH
            zero = jnp.zeros_like(qt)
            qt = jnp.concatenate([jnp.where(half == 0, qt, zero), jnp.where(half == 1, qt, zero)], axis=1)
        ws.append(qt)
    if biased:
        pq = pq_ref[...]
        ftabs = [jnp.broadcast_to(ftab_ref[h] * LOG2E, (tk, REL_TABLE)) for h in range(heads)]
    m_sc[...] = jnp.full(m_sc.shape, NEG_INF, F32)
    l_sc[...] = jnp.zeros_like(l_sc)
    acc_sc[...] = jnp.zeros_like(acc_sc)

    def near_biases(off, q_lo=0):
        idx = jnp.clip(pq[:, q_lo:] - pk_ref[pl.ds(off, tk), :], 0, REL_TABLE - 1)
        tiles = [jnp.concatenate([jnp.take_along_axis(ftabs[h], idx[:, c:c + LANES], axis=1,
                                                      mode="promise_in_bounds")
                                  for c in range(0, tq - q_lo, LANES)], axis=1) for h in range(heads)]

        def bias(h, cs):
            base = (cs.start // tq) * tq + q_lo
            return tiles[h][:, cs.start - base:cs.stop - base]
        return bias

    s_scs = (s0_sc, s1_sc)

    all_cols = tuple(slice(mi * tq, (mi + 1) * tq) for mi in range(n_maps))
    late_cols = tuple(slice(mi * tq + tk, (mi + 1) * tq) for mi in range(n_maps))

    def scores(off, slot, col_ranges=all_cols):
        for h in range(heads):
            for cs in col_ranges:
                s_scs[slot][h, :, cs] = _dot(k_ref[pl.ds(off, tk), hs(h)], ws[h][:, cs])

    def consume(off, slot, bias, mask, const=None, col_ranges=all_cols):
        for h in range(heads):
            for cs in col_ranges:
                s = s_scs[slot][h, :, cs]
                if bias is not None:
                    s = s + bias(h, cs)
                if mask is not None:
                    s = jnp.where(mask[:, cs], s, NEG_INF)
                m = m_sc[h, :, cs]
                m_tile = jnp.max(s, axis=0, keepdims=True)
                if const is not None:
                    m_tile = m_tile + const[h]
                m_new = jnp.maximum(m, m_tile)
                alpha = jnp.exp2(m - m_new)
                p = jnp.exp2(s - (m_new if const is None else m_new - const[h]))
                l_sc[h, :, cs] = alpha * l_sc[h, :, cs] + jnp.sum(p, axis=0, keepdims=True)
                acc_sc[h, :, cs] = (alpha * acc_sc[h, :, cs]
                                    + _dot(vt_ref[hs(h), pl.ds(off, tk)], p.astype(BF16)))
                m_sc[h, :, cs] = m_new

    assert tq == 2 * tk
    scores(0, 0)

    def body(jj, carry):
        off0 = pl.multiple_of(jj * tq, tq)
        off1 = pl.multiple_of(off0 + tk, tk)
        off2 = pl.multiple_of(off0 + tq, tq)

        def pair(near0, near1):
            const = [ftab_ref[h][:, REL_TABLE - 1:REL_TABLE] * LOG2E for h in range(heads)] if biased else None
            scores(off1, 1)
            consume(off0, 0, near_biases(off0) if near0 else None, None, None if near0 else const)
            scores(off2, 0)
            consume(off1, 1, near_biases(off1) if near1 else None, None, None if near1 else const)

        if biased:
            far0 = pmin_ref[bi, qi] - pmax_ref[bi, 2 * jj] >= REL_TABLE - 1
            far1 = pmin_ref[bi, qi] - pmax_ref[bi, 2 * jj + 1] >= REL_TABLE - 1

            @pl.when(far0 & far1)
            def _():
                pair(False, False)

            @pl.when(far0 & jnp.logical_not(far1))
            def _():
                pair(False, True)

            @pl.when(jnp.logical_not(far0))
            def _():
                pair(True, True)
        else:
            pair(False, False)
        return carry

    lax.fori_loop(0, qi, body, 0)
    for d in range(2):
        off = pl.multiple_of(qi * tq + d * tk, tk)
        if d == 0:
            scores(pl.multiple_of(off + tk, tk), 1, late_cols)
        key = lax.broadcasted_iota(jnp.int32, (tk, cols), 0) + d * tk
        qry = lax.broadcasted_iota(jnp.int32, (tk, cols), 1) % tq
        consume(off, d, near_biases(off, d * tk) if biased else None, key <= qry,
                col_ranges=all_cols if d == 0 else late_cols)

    if biased:
        lp = lam_ref[...]
        lam = (jnp.exp(jnp.sum(lp[0:1] * lp[1:2], axis=-1, keepdims=True))
               - jnp.exp(jnp.sum(lp[2:3] * lp[3:4], axis=-1, keepdims=True)) + lambda_init)
    for h in range(heads):
        o = acc_sc[h] / l_sc[h]
        if biased:
            o = (o[:, :tq] - lam * o[:, tq:]).T
            o_ref[:, hs(h)] = (_rms(o, sg_ref[...]) * (1.0 - lambda_init)).astype(o_ref.dtype)
        else:
            o_ref[:, hs(h)] = o.astype(o_ref.dtype).T


def _flash_scratch(heads, tk, cols):
    return [pltpu.VMEM((heads, 1, cols), F32), pltpu.VMEM((heads, 1, cols), F32),
            pltpu.VMEM((heads, HEAD_W, cols), F32),
            pltpu.VMEM((heads, tk, cols), F32), pltpu.VMEM((heads, tk, cols), F32)]


def _mla_attention(qt, k, vt, *, tq=ATTN_TQ, tk=ATTN_TK):
    b, s, w = k.shape
    return pl.pallas_call(
        functools.partial(_flash_kernel, tq=tq, tk=tk, n_maps=1, lambda_init=None),
        grid=(b, s // tq),
        in_specs=[pl.BlockSpec((None, w, tq), lambda bi, i: (bi, 0, i)),
                  pl.BlockSpec((None, s, w), lambda bi, i: (bi, 0, 0), pipeline_mode=pl.Buffered(1)),
                  pl.BlockSpec((None, w, s), lambda bi, i: (bi, 0, 0), pipeline_mode=pl.Buffered(1))],
        out_specs=pl.BlockSpec((None, tq, w), lambda bi, i: (bi, i, 0)),
        out_shape=jax.ShapeDtypeStruct((b, s, w), BF16),
        scratch_shapes=_flash_scratch(w // HEAD_W, tk, tq),
        compiler_params=_params("parallel", "arbitrary"),
        name="mla_attention",
    )(qt, k, vt)


def _diff_attention(qt, k, vt, pos, ftab, lam_params, sub_g, lambda_init, *, tq=ATTN_TQ, tk=ATTN_TK):
    b, s, w = k.shape
    pmin = jnp.min(pos.reshape(b, s // tq, tq), axis=-1)
    pmax = jnp.max(pos.reshape(b, s // tk, tk), axis=-1)
    pos_col = pos.reshape(b, s, 1)
    pos_row = pos.reshape(b, 1, s)
    grid_spec = pltpu.PrefetchScalarGridSpec(
        num_scalar_prefetch=2,
        grid=(b, s // tq),
        in_specs=[pl.BlockSpec((None, w, tq), lambda bi, i, *_: (bi, 0, i)),
                  pl.BlockSpec((None, s, w), lambda bi, i, *_: (bi, 0, 0), pipeline_mode=pl.Buffered(1)),
                  pl.BlockSpec((None, w, s), lambda bi, i, *_: (bi, 0, 0), pipeline_mode=pl.Buffered(1)),
                  pl.BlockSpec((None, 1, tq), lambda bi, i, *_: (bi, 0, i)),
                  pl.BlockSpec((None, s, 1), lambda bi, i, *_: (bi, 0, 0), pipeline_mode=pl.Buffered(1)),
                  pl.BlockSpec(ftab.shape, lambda bi, i, *_: (0, 0, 0)),
                  pl.BlockSpec(lam_params.shape, lambda bi, i, *_: (0, 0)),
                  pl.BlockSpec(sub_g.shape, lambda bi, i, *_: (0, 0))],
        out_specs=pl.BlockSpec((None, tq, w), lambda bi, i, *_: (bi, i, 0)),
        scratch_shapes=_flash_scratch(w // HEAD_W, tk, 2 * tq),
    )
    return pl.pallas_call(
        functools.partial(_flash_kernel, tq=tq, tk=tk, n_maps=2, lambda_init=lambda_init),
        grid_spec=grid_spec,
        out_shape=jax.ShapeDtypeStruct((b, s, w), BF16),
        compiler_params=_params("parallel", "arbitrary"),
        name="diff_attention",
    )(pmin, pmax, qt, k, vt, pos_row, pos_col, ftab, lam_params, sub_g)


def _merge_kernel(h_ref, g_ref, y0_ref, y1_ref, y2_ref, y3_ref, wg_ref, wb_ref, wo_ref, o_ref, *, col_chunk):
    h = h_ref[...]
    d = h.shape[-1]
    u = _rms(h, g_ref[...]).astype(BF16)
    ys = (y0_ref[...], y1_ref[...], y2_ref[...], y3_ref[...])
    parts = []
    for c0 in range(0, d, col_chunk):
        acc = None
        for i, y in enumerate(ys):
            gate = jax.nn.sigmoid(_dot(u, wg_ref[:, i * d + c0:i * d + c0 + col_chunk]))
            term = gate * _dot(y, wb_ref[i, :, c0:c0 + col_chunk])
            acc = term if acc is None else acc + term
        parts.append(acc.astype(BF16))
    merged = jnp.concatenate(parts, axis=-1)
    o_ref[...] = h + _dot(merged, wo_ref[...])


def _merge(h, g, ys, w_gate, w_branch, w_out, *, tm=1024, col_chunk=256):
    m, d = h.shape
    rows = lambda width: pl.BlockSpec((tm, width), lambda i: (i, 0))
    return pl.pallas_call(
        functools.partial(_merge_kernel, col_chunk=col_chunk),
        grid=(m // tm,),
        in_specs=[rows(d), _resident((1, d))] + [rows(BRANCH_W)] * N_BRANCH
                 + [_resident(w_gate), _resident(w_branch), _resident(w_out)],
        out_specs=rows(d),
        out_shape=jax.ShapeDtypeStruct((m, d), F32),
        compiler_params=_params("parallel"),
        name="merge",
    )(h, g, *ys, _operand(w_gate), _operand(w_branch), _operand(w_out))


def _xattn_kernel(h_ref, g_ref, wq_ref, k_ref, v_ref, wo_ref, o_ref):
    h = h_ref[...]
    d = h.shape[-1]
    dh = d // XA_HEADS
    u = _rms(h, g_ref[...]).astype(BF16)
    q = (_dot(u, wq_ref[...].astype(BF16)) * (dh ** -0.5 * LOG2E)).astype(BF16)
    out = h
    for hd in range(XA_HEADS):
        sl = slice(hd * dh, (hd + 1) * dh)
        s = _dot_nt(q[:, sl], k_ref[:, sl])
        p = jnp.exp2(s - jnp.max(s, axis=-1, keepdims=True))
        inv = 1.0 / jnp.sum(p, axis=-1, keepdims=True)
        o = _dot(p.astype(BF16), v_ref[:, sl]) * inv
        out = out + _dot(o.astype(BF16), wo_ref[sl, :].astype(BF16))
    o_ref[...] = out


def _cross_attention(h, g, wq, k, v, wo, batch, seq, *, tm=1024):
    m, d = h.shape
    mem_len = k.shape[0] // batch
    per = seq // tm
    return pl.pallas_call(
        _xattn_kernel,
        grid=(m // tm,),
        in_specs=[pl.BlockSpec((tm, d), lambda i: (i, 0)), _resident((1, d)), _resident(wq),
                  pl.BlockSpec((mem_len, d), lambda i: (i // per, 0)),
                  pl.BlockSpec((mem_len, d), lambda i: (i // per, 0)),
                  _resident(wo)],
        out_specs=pl.BlockSpec((tm, d), lambda i: (i, 0)),
        out_shape=jax.ShapeDtypeStruct((m, d), F32),
        compiler_params=_params("parallel"),
        name="cross_attention",
    )(h, g, _operand(wq), k, v, _operand(wo))


def _pad_cols(w, left, total):
    return jnp.pad(w, ((0, 0),) * (w.ndim - 1) + ((left, total - left - w.shape[-1]),))


def _w_in_plan():
    names = ("rq", "rk", "rv", "rg", "sz", "sxbc", "sdt", "cq", "ckv", "kr", "dq", "dk", "dv", "gl")
    src, start = {}, 0
    for n, s in zip(names, IN_SIZES):
        src[n] = (start, s)
        start += s
    half = MLA_ROPE // 2
    kr0 = src["kr"][0]
    moves, zeros, dst = [], [], 0
    rh = RET_DK // 2
    for n in ("rq", "rk"):
        for tile in range(RET_QK // LANES):
            for part in range(2):
                for hh in range(2):
                    moves.append((src[n][0] + (2 * tile + hh) * RET_DK + part * rh, rh, dst))
                    dst += rh
    for n in ("rv", "rg", "sxbc", "sz"):
        moves.append((*src[n], dst))
        dst += src[n][1]
    zeros.append((dst, SSD_DT_W))
    moves.append((*src["sdt"], dst))
    dst += SSD_DT_W
    for n in ("dq", "dk", "dv", "cq", "ckv"):
        moves.append((*src[n], dst))
        dst += src[n][1]
    zeros.append((dst, 2 * MLA_PAD))
    moves.append((kr0, MLA_ROPE, dst + MLA_NOPE))
    moves.append((kr0 + half, half, dst + MLA_PAD + MLA_NOPE))
    moves.append((kr0, half, dst + MLA_PAD + MLA_NOPE + half))
    dst += 2 * MLA_PAD
    return moves, zeros, src["gl"], dst


def _w_in_relayout_kernel(x_ref, mix_ref, gate_ref, *, moves, zeros, gate):
    for d0, width in zeros:
        mix_ref[:, d0:d0 + width] = jnp.zeros((mix_ref.shape[0], width), mix_ref.dtype)
    for s0, width, d0 in moves:
        mix_ref[:, d0:d0 + width] = x_ref[:, s0:s0 + width].astype(mix_ref.dtype)
    gate_ref[...] = x_ref[:, gate[0]:gate[0] + gate[1]].astype(gate_ref.dtype)


def _w_in_relayout(w_in, *, row_blocks=4):
    depth, r, c = w_in.shape
    moves, zeros, gate, mix_w = _w_in_plan()
    tr = r // row_blocks
    blk = lambda width: pl.BlockSpec((None, tr, width), lambda l, i: (l, i, 0))
    return pl.pallas_call(
        functools.partial(_w_in_relayout_kernel, moves=moves, zeros=zeros, gate=gate),
        grid=(depth, row_blocks),
        in_specs=[blk(c)],
        out_specs=[blk(mix_w), blk(gate[1])],
        out_shape=[jax.ShapeDtypeStruct((depth, r, mix_w), BF16), jax.ShapeDtypeStruct((depth, r, gate[1]), BF16)],
        compiler_params=_params("parallel", "parallel"),
        name="w_in_relayout",
    )(w_in)


def _mla_up_weights(w_uq, w_ukv):
    depth, d, _ = w_uq.shape
    half = MLA_ROPE // 2
    uq = w_uq.astype(BF16).reshape(depth, d, MLA_HEADS, MLA_QH)
    pe = uq[..., MLA_NOPE:]
    partner = jnp.concatenate([jnp.zeros_like(uq[..., :MLA_NOPE]), pe[..., half:], pe[..., :half]], axis=-1)
    pad = lambda a: _pad_cols(a, 0, MLA_PAD).reshape(depth, d, MLA_HEADS * MLA_PAD)
    uq_p = jnp.concatenate([pad(uq), pad(partner)], axis=-1)
    r = w_ukv.shape[1]
    ukv = w_ukv.astype(BF16).reshape(depth, r, MLA_HEADS, MLA_NOPE + MLA_V)
    k_nope = _pad_cols(ukv[..., :MLA_NOPE], 0, MLA_PAD)
    ukv_p = jnp.concatenate([k_nope.reshape(depth, r, -1), ukv[..., MLA_NOPE:].reshape(depth, r, -1)], axis=-1)
    return uq_p, ukv_p


def _t5_bucket_const():
    n = jnp.arange(REL_TABLE)
    exact = REL_BUCKETS // 2
    nf = jnp.maximum(n, 1).astype(F32)
    large = exact + (jnp.log(nf / exact) / math.log(REL_MAX_DIST / exact) * (REL_BUCKETS - exact)).astype(jnp.int32)
    return jnp.where(n < exact, n, jnp.minimum(large, REL_BUCKETS - 1))


MIX_OUTS = ((3 * BRANCH_W, F32, 1.0, False),
            (SSM_XBC + SSM_INNER, F32, 1.0, False),
            (SSD_DT_W, F32, 1.0, True),
            (BRANCH_W, BF16, DIFF_DH ** -0.5 * LOG2E, True),
            (BRANCH_W, BF16, 1.0, False),
            (BRANCH_W, BF16, 1.0, True))


def kernel(x, mem, positions, ffn1_norm, ffn1_w_gate, ffn1_w_up, ffn1_w_down, mix_norm, w_in, ssm_conv_w,
           ssm_conv_b, ssm_dt_bias, ssm_a_log, ssm_d, ssm_norm, mla_q_norm, mla_kv_norm, mla_w_uq, mla_w_ukv,
           diff_lambda, diff_norm, rel_bias, w_branch, w_out, xa_norm, mem_norm, xa_wq, xa_wk, xa_wv, xa_wo,
           ffn2_norm, ffn2_w_gate, ffn2_w_up, ffn2_w_down, final_norm):
    batch, seq, d = x.shape
    depth = w_in.shape[0]
    m = batch * seq
    h = x.reshape(m, d)
    mem2 = mem.reshape(batch * mem.shape[1], d)
    row = lambda a: a.reshape(1, -1)

    pos_col = positions.reshape(m, 1)
    ret_cos, ret_sin, mla_cos, mla_sin = _rope_tables(
        pos_col, [(LANES, LANES // 2, 0, RET_DK // 2), (MLA_PAD, MLA_ROPE // 2, MLA_NOPE, MLA_ROPE // 2)])
    ftab = rel_bias[_t5_bucket_const()].T.reshape(DIFF_HEADS, 1, REL_TABLE)

    w_mix, w_gate = _w_in_relayout(w_in)
    w_uq_p, w_ukv_p = _mla_up_weights(mla_w_uq, mla_w_ukv)
    ffn1 = (ffn1_w_gate, ffn1_w_up, ffn1_w_down)
    ffn2 = (ffn2_w_gate, ffn2_w_up, ffn2_w_down)
    w_branch_b = _cast_bf16(w_branch.reshape(depth, N_BRANCH * BRANCH_W, d)).reshape(w_branch.shape)
    w_out_b = _cast_bf16(w_out)
    w_kv = jnp.concatenate([_cast_bf16(xa_wk), _cast_bf16(xa_wv)], axis=-1)

    for l in range(depth):
        at = lambda stack: _LayerOf(stack, l)
        h = _ffn(h, row(ffn1_norm[l]), *(at(w) for w in ffn1))
        mla = (mla_cos, mla_sin, row(mla_q_norm[l]), row(mla_kv_norm[l]), at(w_uq_p), at(w_ukv_p))
        ret_p, ssm_p, dt_t, dqt, dk, dvt, mqt, mk, mvt = _norm_proj(h, row(mix_norm[l]), at(w_mix), MIX_OUTS, seq,
                                                                   mla=mla, name="mix_proj")
        y_ret = _retention(ret_p.reshape(batch, seq, -1), ret_cos.reshape(batch, seq, -1),
                           ret_sin.reshape(batch, seq, -1)).reshape(m, BRANCH_W)
        y_ssm = _ssd(ssm_p.reshape(batch, seq, -1), dt_t, ssm_conv_w[l], row(ssm_conv_b[l]),
                     ssm_dt_bias[l].reshape(-1, 1), ssm_a_log[l].reshape(-1, 1),
                     row(jnp.repeat(ssm_d[l], SSM_HEADDIM)), row(ssm_norm[l])).reshape(m, SSM_INNER)
        y_mla = _mla_attention(mqt, mk.reshape(batch, seq, -1), mvt).reshape(m, BRANCH_W)
        lambda_init = 0.8 - 0.6 * math.exp(-0.3 * l)
        y_diff = _diff_attention(dqt, dk.reshape(batch, seq, -1), dvt, positions, ftab, diff_lambda[l],
                                 row(diff_norm[l]), lambda_init).reshape(m, BRANCH_W)
        h = _merge(h, row(mix_norm[l]), (y_ret, y_ssm, y_mla, y_diff), at(w_gate), at(w_branch_b), at(w_out_b))
        xk, xv = _norm_proj(mem2, row(mem_norm[l]), at(w_kv), [(d, BF16, 1.0, False)] * 2, mem.shape[1],
                            tm=mem.shape[1], name="mem_kv")
        h = _cross_attention(h, row(xa_norm[l]), at(xa_wq), xk, xv, at(xa_wo), batch, seq)
        h = _ffn(h, row(ffn2_norm[l]), *(at(w) for w in ffn2),
                 final_g=row(final_norm) if l == depth - 1 else None)
    return h.reshape(batch, seq, d)
```

```python
import functools
import math
from typing import NamedTuple

import jax
import jax.numpy as jnp
from jax import lax
from jax.experimental import pallas as pl
from jax.experimental.pallas import tpu as pltpu

F32 = jnp.float32
BF16 = jnp.bfloat16

EPS = 1e-6
NEG_INF = -1e30
LOG2E = math.log2(math.e)
ROPE_BASE = 10000.0
CHUNK = 128
BRANCH_W = 512
N_BRANCH = 4

RET_HEADS = 4
RET_DK = 64
RET_DV = 128
RET_QK = RET_HEADS * RET_DK

SSM_HEADDIM = 64
SSM_INNER = 512
SSM_HEADS = 8
SSM_GROUPS = 2
SSM_STATE = 128
SSM_CONV = 4
SSM_XBC = SSM_INNER + 2 * SSM_GROUPS * SSM_STATE
SSM_TAIL = 8
SSD_DT_W = 128

MLA_HEADS = 4
MLA_Q_RANK = 256
MLA_KV_RANK = 128
MLA_NOPE = 64
MLA_ROPE = 32
MLA_V = 128
MLA_QH = MLA_NOPE + MLA_ROPE
MLA_PAD = 128
MLA_IN_W = MLA_Q_RANK + MLA_KV_RANK + 2 * MLA_PAD
MLA_OUTS = ((MLA_HEADS * MLA_PAD, jnp.bfloat16, 1.0, True), (MLA_HEADS * MLA_PAD, jnp.bfloat16, 1.0, False),
            (MLA_HEADS * MLA_V, jnp.bfloat16, 1.0, True))
HEAD_W = 128
ATTN_TQ = 512
ATTN_TK = 256

DIFF_HEADS = 4
DIFF_DH = 64

REL_BUCKETS = 32
REL_MAX_DIST = 128
REL_TABLE = 128

XA_HEADS = 4

LANES = 128
VMEM_LIMIT = 56 * 1024 * 1024

IN_SIZES = (RET_QK, RET_QK, BRANCH_W, BRANCH_W,
            SSM_INNER, SSM_XBC, SSM_HEADS,
            MLA_Q_RANK, MLA_KV_RANK, MLA_ROPE,
            BRANCH_W, BRANCH_W, BRANCH_W,
            N_BRANCH * 1024)


def _params(*sem):
    return pltpu.CompilerParams(dimension_semantics=sem, vmem_limit_bytes=VMEM_LIMIT)


class _LayerOf(NamedTuple):
    stack: jax.Array
    layer: int

    @property
    def shape(self):
        return self.stack.shape[1:]


def _operand(x):
    return x.stack if isinstance(x, _LayerOf) else x


def _resident(x):
    if isinstance(x, _LayerOf):
        shape, lead, first = x.shape, (None,), (x.layer,)
    else:
        shape, lead, first = (x if isinstance(x, tuple) else x.shape), (), ()
    nd = len(shape)
    return pl.BlockSpec(lead + tuple(shape), lambda *_: first + (0,) * nd, pipeline_mode=pl.Buffered(1))


def _cast_kernel(x_ref, o_ref):
    o_ref[...] = x_ref[...].astype(o_ref.dtype)


def _cast_bf16(w, *, row_blocks=4):
    depth, r, c = w.shape
    tr = r // row_blocks
    spec = pl.BlockSpec((None, tr, c), lambda l, i: (l, i, 0))
    return pl.pallas_call(
        _cast_kernel,
        grid=(depth, row_blocks),
        in_specs=[spec],
        out_specs=spec,
        out_shape=jax.ShapeDtypeStruct(w.shape, BF16),
        compiler_params=_params("parallel", "parallel"),
        name="cast_bf16",
    )(w)


def _rms(x, g):
    return x * lax.rsqrt(jnp.mean(x * x, axis=-1, keepdims=True) + EPS) * g


def _silu(x):
    half = 0.5 * x
    return half + half * jnp.tanh(half)


def _dot(a, b):
    return jnp.dot(a, b, preferred_element_type=F32)


def _dot_nt(a, b):
    return lax.dot_general(a, b, (((1,), (1,)), ((), ())), preferred_element_type=F32)


def _dot_tn(a, b):
    return lax.dot_general(a, b, (((0,), (0,)), ((), ())), preferred_element_type=F32)


def _ffn_kernel(h_ref, g_ref, wg_ref, wu_ref, wd_ref, *rest, ff_chunk, final):
    if final:
        gf_ref, o_ref = rest
    else:
        (o_ref,) = rest
    h = h_ref[...]
    xn = _rms(h, g_ref[...]).astype(BF16)
    acc = jnp.zeros(h.shape, F32)
    d_ff = wg_ref.shape[1]
    for c0 in range(0, d_ff, ff_chunk):
        a = _dot(xn, wg_ref[:, c0:c0 + ff_chunk].astype(BF16))
        b = _dot(xn, wu_ref[:, c0:c0 + ff_chunk].astype(BF16))
        acc = acc + _dot((_silu(a) * b).astype(BF16), wd_ref[c0:c0 + ff_chunk, :].astype(BF16))
    out = h + 0.5 * acc
    if final:
        out = _rms(out, gf_ref[...])
    o_ref[...] = out


def _ffn(h, g, wg, wu, wd, final_g=None, *, tm=512, ff_chunk=256):
    m, d = h.shape
    d_ff = wg.shape[1]
    final = final_g is not None
    in_specs = [pl.BlockSpec((tm, d), lambda i: (i, 0)), _resident((1, d)),
                _resident(wg), _resident(wu), _resident(wd)]
    args = [h, g, _operand(wg), _operand(wu), _operand(wd)]
    if final:
        in_specs.append(_resident((1, d)))
        args.append(final_g)
    return pl.pallas_call(
        functools.partial(_ffn_kernel, ff_chunk=ff_chunk, final=final),
        grid=(m // tm,),
        in_specs=in_specs,
        out_specs=pl.BlockSpec((tm, d), lambda i: (i, 0)),
        out_shape=jax.ShapeDtypeStruct((m, d), F32),
        compiler_params=_params("parallel"),
        name="ffn",
    )(*args)


def _mla_qkv(p, cos, sin, q_norm, kv_norm, w_uq, w_ukv):
    kvw = MLA_HEADS * MLA_PAD
    cq = p[:, :MLA_Q_RANK]
    ckv = p[:, MLA_Q_RANK:MLA_Q_RANK + MLA_KV_RANK]
    kr0 = MLA_Q_RANK + MLA_KV_RANK
    kr = p[:, kr0:kr0 + MLA_PAD]
    kr_partner = p[:, kr0 + MLA_PAD:]
    qq = _dot(_rms(cq, q_norm).astype(BF16), w_uq)
    cos4 = jnp.concatenate([cos] * MLA_HEADS, axis=-1)
    sin4 = jnp.concatenate([sin] * MLA_HEADS, axis=-1)
    q = (qq[:, :kvw] * cos4 + qq[:, kvw:] * sin4) * (MLA_QH ** -0.5 * LOG2E)
    kv = _dot(_rms(ckv, kv_norm).astype(BF16), w_ukv)
    kpe = kr * cos + kr_partner * sin
    k = kv[:, :kvw] + jnp.concatenate([kpe] * MLA_HEADS, axis=-1)
    return q.astype(BF16).T, k.astype(BF16), kv[:, kvw:].astype(BF16).T


def _norm_proj_kernel(x_ref, g_ref, w_ref, *rest, outs, col_chunk, mla):
    if mla:
        cos_ref, sin_ref, qn_ref, kvn_ref, wuq_ref, wukv_ref = rest[:6]
        o_refs = rest[6:]
    else:
        o_refs = rest
    xn = _rms(x_ref[...], g_ref[...]).astype(BF16)
    off = 0
    for o_ref, (width, _, scale, transposed) in zip(o_refs, outs):
        for c0 in range(0, width, col_chunk):
            c1 = min(c0 + col_chunk, width)
            y = _dot(xn, w_ref[:, off + c0:off + c1].astype(BF16))
            if scale != 1.0:
                y = y * scale
            if transposed:
                o_ref[c0:c1, :] = y.astype(o_ref.dtype).T
            else:
                o_ref[:, c0:c1] = y.astype(o_ref.dtype)
        off += width
    if mla:
        q_ref, k_ref, v_ref = o_refs[len(outs):]
        p = _dot(xn, w_ref[:, off:])
        q_ref[...], k_ref[...], v_ref[...] = _mla_qkv(p, cos_ref[...], sin_ref[...], qn_ref[...], kvn_ref[...],
                                                      wuq_ref[...], wukv_ref[...])


def _norm_proj(x, g, w, outs, seq, *, mla=None, tm=512, col_chunk=512, name="norm_proj"):
    m, d = x.shape
    assert sum(o[0] for o in outs) + (MLA_IN_W if mla else 0) == w.shape[1]
    per = seq // tm
    rows = lambda width: pl.BlockSpec((tm, width), lambda i: (i, 0))
    cols = lambda width: pl.BlockSpec((None, width, tm), lambda i: (i // per, 0, i % per))
    out_specs, out_shape = [], []
    for width, dt, _, transposed in tuple(outs) + (MLA_OUTS if mla else ()):
        if transposed:
            out_specs.append(cols(width))
            out_shape.append(jax.ShapeDtypeStruct((m // seq, width, seq), dt))
        else:
            out_specs.append(rows(width))
            out_shape.append(jax.ShapeDtypeStruct((m, width), dt))
    in_specs = [rows(d), _resident((1, d)), _resident(w)]
    args = [x, g, _operand(w)]
    if mla:
        cos, sin, q_norm, kv_norm, w_uq, w_ukv = mla
        in_specs += [rows(LANES), rows(LANES), _resident(q_norm), _resident(kv_norm), _resident(w_uq),
                     _resident(w_ukv)]
        args += [cos, sin, q_norm, kv_norm, _operand(w_uq), _operand(w_ukv)]
    return pl.pallas_call(
        functools.partial(_norm_proj_kernel, outs=tuple(outs), col_chunk=col_chunk, mla=mla is not None),
        grid=(m // tm,),
        in_specs=in_specs,
        out_specs=out_specs,
        out_shape=out_shape,
        compiler_params=_params("parallel"),
        name=name,
    )(*args)


def _rope_table_kernel(pos_ref, inv_ref, idx_ref, sign_ref, *o_refs):
    ang = pos_ref[...].astype(F32) * inv_ref[...]
    cos, sin = jnp.cos(ang), jnp.sin(ang)
    for t in range(len(o_refs) // 2):
        idx = jnp.broadcast_to(idx_ref[t:t + 1, :], ang.shape)
        o_refs[2 * t][...] = jnp.take_along_axis(cos, idx, axis=1, mode="promise_in_bounds")
        o_refs[2 * t + 1][...] = (jnp.take_along_axis(sin, idx, axis=1, mode="promise_in_bounds")
                                  * sign_ref[t:t + 1, :])


def _rope_tables(pos_col, layouts, *, tm=1024):
    m = pos_col.shape[0]
    lane = jnp.arange(LANES)
    inv_parts, idx_rows, sign_rows, base = [], [], [], 0
    for period, half, start, n_freq in layouts:
        inv_parts.append(jnp.exp(-math.log(ROPE_BASE) * jnp.arange(n_freq, dtype=F32) / n_freq))
        rel = lane % period
        inside = (rel >= start) & (rel < start + 2 * half)
        idx_rows.append(jnp.where(inside, base + (rel - start) % half % n_freq, LANES - 1))
        sign_rows.append(jnp.where(inside, jnp.where(rel < start + half, -1.0, 1.0), 0.0))
        base += n_freq
    assert base < LANES
    inv = jnp.concatenate(inv_parts + [jnp.zeros((LANES - base,), F32)]).reshape(1, LANES)
    idx = jnp.stack(idx_rows).astype(jnp.int32)
    sign = jnp.stack(sign_rows).astype(F32)
    n_out = 2 * len(layouts)
    return pl.pallas_call(
        _rope_table_kernel,
        grid=(m // tm,),
        in_specs=[pl.BlockSpec((tm, 1), lambda i: (i, 0)), _resident(inv), _resident(idx), _resident(sign)],
        out_specs=[pl.BlockSpec((tm, LANES), lambda i: (i, 0))] * n_out,
        out_shape=[jax.ShapeDtypeStruct((m, LANES), F32)] * n_out,
        compiler_params=_params("parallel"),
        name="rope_tables",
    )(pos_col, inv, idx, sign)


def _ret_head_of_lane(width):
    lane = jnp.arange(width)
    return 2 * (lane // LANES) + (lane // (RET_DK // 2)) % 2


def _apply_rope_ret(x, cos, sin_signed):
    tiles = [x[:, t:t + LANES] for t in range(0, x.shape[-1], LANES)]
    return jnp.concatenate([xt * cos + pltpu.roll(xt, LANES // 2, 1) * sin_signed for xt in tiles], axis=-1)


def _ret_kernel(q_ref, k_ref, v_ref, g_ref, cos_ref, sin_ref, intra_ref, qdec_ref, kdec_ref,
                sdec_ref, bmask_ref, o_ref, state_sc):
    @pl.when(pl.program_id(0) == 0)
    def _():
        state_sc[...] = jnp.zeros_like(state_sc)

    lane = lax.broadcasted_iota(jnp.int32, (1, RET_QK), 1)
    head = 2 * (lane // LANES) + (lane // (RET_DK // 2)) % 2
    for b in range(q_ref.shape[0]):
        cos, sin = cos_ref[b], sin_ref[b]
        q = _apply_rope_ret(q_ref[b], cos, sin) * (RET_DK ** -0.5)
        k = _apply_rope_ret(k_ref[b], cos, sin)
        vb = v_ref[b].astype(BF16)
        kb = k.astype(BF16)
        state = state_sc[b]
        cross = _dot((q * qdec_ref[...]).astype(BF16), state.astype(BF16))
        outs = []
        for h in range(RET_HEADS):
            qh = jnp.where(head == h, q, 0.0).astype(BF16)
            att = _dot_nt(qh, kb) * intra_ref[h]
            sl = slice(h * RET_DV, (h + 1) * RET_DV)
            o = _dot(att.astype(BF16), vb[:, sl]) + cross[:, sl]
            mu = jnp.mean(o, axis=-1, keepdims=True)
            oc = o - mu
            outs.append(oc * lax.rsqrt(jnp.mean(oc * oc, axis=-1, keepdims=True) + EPS))
        o_ref[b] = (jnp.concatenate(outs, axis=-1) * _silu(g_ref[b])).astype(o_ref.dtype)
        kv = _dot_tn((k * kdec_ref[...]).astype(BF16), vb)
        state_sc[b] = state * sdec_ref[...] + kv * bmask_ref[...]


def _retention_consts():
    c = CHUNK
    log_g = jnp.log1p(-jnp.exp2(-5.0 - jnp.arange(RET_HEADS, dtype=F32)))
    i = jnp.arange(c, dtype=F32)
    rel = i[:, None] - i[None, :]
    intra = jnp.where(rel >= 0, jnp.exp(log_g[:, None, None] * jnp.maximum(rel, 0.0)), 0.0)
    row_h = _ret_head_of_lane(RET_QK)
    lg_lane = log_g[row_h]
    qdec = jnp.exp(lg_lane[None, :] * (i[:, None] + 1.0))
    kdec = jnp.exp(lg_lane[None, :] * (c - 1.0 - i[:, None]))
    col_h = jnp.repeat(jnp.arange(RET_HEADS), RET_DV)
    bmask = (row_h[:, None] == col_h[None, :]).astype(F32)
    sdec = bmask * jnp.exp(lg_lane * c)[:, None]
    return intra, qdec, kdec, sdec, bmask


def _retention(ret_p, cos, sin):
    c = CHUNK
    batch, seq, _ = ret_p.shape
    intra, qdec, kdec, sdec, bmask = _retention_consts()
    blk = lambda width, col: pl.BlockSpec((batch, c, width), lambda j: (0, j, col))
    return pl.pallas_call(
        _ret_kernel,
        grid=(seq // c,),
        in_specs=[blk(RET_QK, 0), blk(RET_QK, 1), blk(BRANCH_W, 1), blk(BRANCH_W, 2), blk(LANES, 0), blk(LANES, 0),
                  _resident(intra.shape), _resident(qdec.shape), _resident(kdec.shape),
                  _resident(sdec.shape), _resident(bmask.shape)],
        out_specs=blk(BRANCH_W, 0),
        out_shape=jax.ShapeDtypeStruct((batch, seq, BRANCH_W), BF16),
        scratch_shapes=[pltpu.VMEM((batch, RET_QK, BRANCH_W), F32)],
        compiler_params=_params("arbitrary"),
        name="retention",
    )(ret_p, ret_p, ret_p, ret_p, cos, sin, intra, qdec, kdec, sdec, bmask)


def _split3(x):
    hi = x.astype(BF16)
    r = x - hi.astype(F32)
    mid = r.astype(BF16)
    lo = (r - mid.astype(F32)).astype(BF16)
    return hi, mid, lo


def _dot_split3(a, b, *, left):
    if left:
        return sum(_dot(piece, b) for piece in _split3(a))
    return sum(_dot(a, piece) for piece in _split3(b))


def _ssd_kernel(xbc_ref, *refs):
    xe_sc, state_sc = refs[-2:]

    @pl.when(pl.program_id(0) == 0)
    def _():
        xe_sc[...] = jnp.zeros_like(xe_sc)
        state_sc[...] = jnp.zeros_like(state_sc)

    for b in range(xbc_ref.shape[0]):
        _ssd_chunk(b, xbc_ref, *refs)


def _ssd_chunk(b, xbc_ref, z_ref, dt_ref, cw_ref, cb_ref, dtb_ref, alog_ref, dskip_ref, ng_ref, tri_ref,
               expand_ref, o_ref, xe_sc, state_sc):
    L = CHUNK
    T = SSM_TAIL
    GW = SSM_INNER // SSM_GROUPS
    x_raw = xbc_ref[b]
    tail = xe_sc[b]
    rows8 = lax.broadcasted_iota(jnp.int32, (T, 1), 0)
    conv = cb_ref[...] + cw_ref[SSM_CONV - 1:SSM_CONV, :] * x_raw
    for s in range(1, SSM_CONV):
        shifted = pltpu.roll(x_raw, s, 0)
        head = jnp.where(rows8 < s, pltpu.roll(tail, s, 0), shifted[0:T, :])
        shifted = jnp.concatenate([head, shifted[T:, :]], axis=0)
        conv = conv + cw_ref[SSM_CONV - 1 - s:SSM_CONV - s, :] * shifted
    xe_sc[b] = x_raw[L - T:L, :]
    xa = _silu(conv)
    xs = xa[:, :SSM_INNER]
    bm = xa[:, SSM_INNER:SSM_INNER + SSM_GROUPS * SSM_STATE]
    cm = xa[:, SSM_INNER + SSM_GROUPS * SSM_STATE:]

    dtr = dt_ref[b] + dtb_ref[...]
    dt_h = jnp.maximum(dtr, 0.0) + jnp.log1p(jnp.exp(-jnp.abs(dtr)))
    da = dt_h * (-jnp.exp(alog_ref[...]))
    cs_t = _dot_split3(da, tri_ref[...], left=True)
    cs_t_last = cs_t[:, L - 1:L]
    per_head = jnp.concatenate([dt_h, jnp.exp(cs_t), jnp.exp(cs_t_last - cs_t), cs_t], axis=0)
    per_chan = _dot_tn(jnp.concatenate(_split3(per_head), axis=0), expand_ref[...])
    dt, ecs, eds, cs_c = (per_chan[:, i * SSM_INNER:(i + 1) * SSM_INNER] for i in range(4))
    xdt = xs * dt
    xdec = (xdt * eds).astype(BF16)
    state = state_sc[b]
    li = lax.broadcasted_iota(jnp.int32, (L, L), 0)
    si = lax.broadcasted_iota(jnp.int32, (L, L), 1)
    causal = li >= si
    lane_half = lax.broadcasted_iota(jnp.int32, (1, LANES), 1) // SSM_HEADDIM
    ys = []
    new_states = []
    for g in range(SSM_GROUPS):
        bg = bm[:, g * SSM_STATE:(g + 1) * SSM_STATE].astype(BF16)
        cg = cm[:, g * SSM_STATE:(g + 1) * SSM_STATE].astype(BF16)
        cb = _dot_nt(cg, bg)
        gsl = slice(g * GW, (g + 1) * GW)
        y_off = _dot(cg, state[:, gsl].astype(BF16)) * ecs[:, gsl]
        for p in range(GW // LANES):
            sl = slice(g * GW + p * LANES, g * GW + (p + 1) * LANES)
            xdt_blk = xdt[:, sl]
            acc = y_off[:, p * LANES:(p + 1) * LANES]
            for hh in range(LANES // SSM_HEADDIM):
                head = (g * GW + p * LANES) // SSM_HEADDIM + hh
                col = cs_c[:, head * SSM_HEADDIM:head * SSM_HEADDIM + 1]
                rowv = cs_t[head:head + 1, :]
                dec = jnp.where(causal, jnp.exp(jnp.minimum(col - rowv, 0.0)), 0.0)
                xm = jnp.where(lane_half == hh, xdt_blk, 0.0).astype(BF16)
                acc = acc + _dot((cb * dec).astype(BF16), xm)
            ys.append(acc)
        new_states.append(_dot_tn(bg, xdec[:, gsl]))
    y = jnp.concatenate(ys, axis=-1) + xs * dskip_ref[...]
    y = y * _silu(z_ref[b])
    o_ref[b] = _rms(y, ng_ref[...]).astype(o_ref.dtype)
    state_sc[b] = state * ecs[L - 1:L, :] + jnp.concatenate(new_states, axis=-1)


def _ssd(ssm_p, dt_t, conv_w, conv_b, dt_bias_col, a_log_col, d_full, norm_g):
    L = CHUNK
    batch, seq, _ = ssm_p.shape
    tri = (jnp.arange(L)[:, None] <= jnp.arange(L)[None, :]).astype(BF16)
    r = jnp.arange(3 * 4 * SSM_HEADS)
    c = jnp.arange(4 * SSM_INNER)
    expand = (((r[:, None] // SSM_HEADS) % 4 == c[None, :] // SSM_INNER)
              & (r[:, None] % SSM_HEADS == (c[None, :] % SSM_INNER) // SSM_HEADDIM)).astype(BF16)
    return pl.pallas_call(
        _ssd_kernel,
        grid=(seq // L,),
        in_specs=[pl.BlockSpec((batch, L, SSM_XBC), lambda j: (0, j, 0)),
                  pl.BlockSpec((batch, L, SSM_INNER), lambda j: (0, j, SSM_XBC // SSM_INNER)),
                  pl.BlockSpec((batch, SSM_HEADS, L), lambda j: (0, 0, j)),
                  _resident(conv_w.shape), _resident(conv_b.shape), _resident(dt_bias_col.shape),
                  _resident(a_log_col.shape), _resident(d_full.shape), _resident(norm_g.shape),
                  _resident(tri.shape), _resident(expand.shape)],
        out_specs=pl.BlockSpec((batch, L, SSM_INNER), lambda j: (0, j, 0)),
        out_shape=jax.ShapeDtypeStruct((batch, seq, SSM_INNER), BF16),
        scratch_shapes=[pltpu.VMEM((batch, SSM_TAIL, SSM_XBC), F32),
                        pltpu.VMEM((batch, SSM_STATE, SSM_INNER), F32)],
        compiler_params=_params("arbitrary"),
        name="ssd",
    )(ssm_p, ssm_p, dt_t, conv_w, conv_b, dt_bias_col, a_log_col, d_full, norm_g, tri, expand)


def _flash_kernel(*refs, tq, tk, n_maps, lambda_init):
    biased = n_maps == 2
    if biased:
        (pmin_ref, pmax_ref, qt_ref, k_ref, vt_ref, pq_ref, pk_ref, ftab_ref, lam_ref, sg_ref,
         o_ref, m_sc, l_sc, acc_sc, s0_sc, s1_sc) = refs
    else:
        qt_ref, k_ref, vt_ref, o_ref, m_sc, l_sc, acc_sc, s0_sc, s1_sc = refs
    bi = pl.program_id(0)
    qi = pl.program_id(1)
    hw = HEAD_W
    heads = k_ref.shape[-1] // hw
    cols = n_maps * tq
    hs = lambda h: slice(h * hw, (h + 1) * hw)
    ws = []
    for h in range(heads):
        qt = qt_ref[hs(h), :]
        if biased:
            half = lax.broadcasted_iota(jnp.int32, (hw, 1), 0) // DIFF_DH
            zero = jnp.zeros_like(qt)
            qt = jnp.concatenate([jnp.where(half == 0, qt, zero), jnp.where(half == 1, qt, zero)], axis=1)
        ws.append(qt)
    if biased:
        pq = pq_ref[...]
        ftabs = [jnp.broadcast_to(ftab_ref[h] * LOG2E, (tk, REL_TABLE)) for h in range(heads)]
    m_sc[...] = jnp.full(m_sc.shape, NEG_INF, F32)
    l_sc[...] = jnp.zeros_like(l_sc)
    acc_sc[...] = jnp.zeros_like(acc_sc)

    def near_biases(off, q_lo=0):
        idx = jnp.clip(pq[:, q_lo:] - pk_ref[pl.ds(off, tk), :], 0, REL_TABLE - 1)
        tiles = [jnp.concatenate([jnp.take_along_axis(ftabs[h], idx[:, c:c + LANES], axis=1,
                                                      mode="promise_in_bounds")
                                  for c in range(0, tq - q_lo, LANES)], axis=1) for h in range(heads)]

        def bias(h, cs):
            base = (cs.start // tq) * tq + q_lo
            return tiles[h][:, cs.start - base:cs.stop - base]
        return bias

    s_scs = (s0_sc, s1_sc)

    all_cols = tuple(slice(mi * tq, (mi + 1) * tq) for mi in range(n_maps))
    late_cols = tuple(slice(mi * tq + tk, (mi + 1) * tq) for mi in range(n_maps))

    def scores(off, slot, col_ranges=all_cols):
        for h in range(heads):
            for cs in col_ranges:
                s_scs[slot][h, :, cs] = _dot(k_ref[pl.ds(off, tk), hs(h)], ws[h][:, cs])

    def consume(off, slot, bias, mask, const=None, col_ranges=all_cols):
        for h in range(heads):
            for cs in col_ranges:
                s = s_scs[slot][h, :, cs]
                if bias is not None:
                    s = s + bias(h, cs)
                if mask is not None:
                    s = jnp.where(mask[:, cs], s, NEG_INF)
                m = m_sc[h, :, cs]
                m_tile = jnp.max(s, axis=0, keepdims=True)
                if const is not None:
                    m_tile = m_tile + const[h]
                m_new = jnp.maximum(m, m_tile)
                alpha = jnp.exp2(m - m_new)
                p = jnp.exp2(s - (m_new if const is None else m_new - const[h]))
                l_sc[h, :, cs] = alpha * l_sc[h, :, cs] + jnp.sum(p, axis=0, keepdims=True)
                acc_sc[h, :, cs] = (alpha * acc_sc[h, :, cs]
                                    + _dot(vt_ref[hs(h), pl.ds(off, tk)], p.astype(BF16)))
                m_sc[h, :, cs] = m_new

    assert tq == 2 * tk
    scores(0, 0)

    def body(jj, carry):
        off0 = pl.multiple_of(jj * tq, tq)
        off1 = pl.multiple_of(off0 + tk, tk)
        off2 = pl.multiple_of(off0 + tq, tq)

        def pair(near0, near1):
            const = [ftab_ref[h][:, REL_TABLE - 1:REL_TABLE] * LOG2E for h in range(heads)] if biased else None
            scores(off1, 1)
            consume(off0, 0, near_biases(off0) if near0 else None, None, None if near0 else const)
            scores(off2, 0)
            consume(off1, 1, near_biases(off1) if near1 else None, None, None if near1 else const)

        if biased:
            far0 = pmin_ref[bi, qi] - pmax_ref[bi, 2 * jj] >= REL_TABLE - 1
            far1 = pmin_ref[bi, qi] - pmax_ref[bi, 2 * jj + 1] >= REL_TABLE - 1

            @pl.when(far0 & far1)
            def _():
                pair(False, False)

            @pl.when(far0 & jnp.logical_not(far1))
            def _():
                pair(False, True)

            @pl.when(jnp.logical_not(far0))
            def _():
                pair(True, True)
        else:
            pair(False, False)
        return carry

    lax.fori_loop(0, qi, body, 0)
    for d in range(2):
        off = pl.multiple_of(qi * tq + d * tk, tk)
        if d == 0:
            scores(pl.multiple_of(off + tk, tk), 1, late_cols)
        key = lax.broadcasted_iota(jnp.int32, (tk, cols), 0) + d * tk
        qry = lax.broadcasted_iota(jnp.int32, (tk, cols), 1) % tq
        consume(off, d, near_biases(off, d * tk) if biased else None, key <= qry,
                col_ranges=all_cols if d == 0 else late_cols)

    if biased:
        lp = lam_ref[...]
        lam = (jnp.exp(jnp.sum(lp[0:1] * lp[1:2], axis=-1, keepdims=True))
               - jnp.exp(jnp.sum(lp[2:3] * lp[3:4], axis=-1, keepdims=True)) + lambda_init)
    for h in range(heads):
        o = acc_sc[h] / l_sc[h]
        if biased:
            o = (o[:, :tq] - lam * o[:, tq:]).T
            o_ref[:, hs(h)] = (_rms(o, sg_ref[...]) * (1.0 - lambda_init)).astype(o_ref.dtype)
        else:
            o_ref[:, hs(h)] = o.astype(o_ref.dtype).T


def _flash_scratch(heads, tk, cols):
    return [pltpu.VMEM((heads, 1, cols), F32), pltpu.VMEM((heads, 1, cols), F32),
            pltpu.VMEM((heads, HEAD_W, cols), F32),
            pltpu.VMEM((heads, tk, cols), F32), pltpu.VMEM((heads, tk, cols), F32)]


def _mla_attention(qt, k, vt, *, tq=ATTN_TQ, tk=ATTN_TK):
    b, s, w = k.shape
    return pl.pallas_call(
        functools.partial(_flash_kernel, tq=tq, tk=tk, n_maps=1, lambda_init=None),
        grid=(b, s // tq),
        in_specs=[pl.BlockSpec((None, w, tq), lambda bi, i: (bi, 0, i)),
                  pl.BlockSpec((None, s, w), lambda bi, i: (bi, 0, 0), pipeline_mode=pl.Buffered(1)),
                  pl.BlockSpec((None, w, s), lambda bi, i: (bi, 0, 0), pipeline_mode=pl.Buffered(1))],
        out_specs=pl.BlockSpec((None, tq, w), lambda bi, i: (bi, i, 0)),
        out_shape=jax.ShapeDtypeStruct((b, s, w), BF16),
        scratch_shapes=_flash_scratch(w // HEAD_W, tk, tq),
        compiler_params=_params("parallel", "arbitrary"),
        name="mla_attention",
    )(qt, k, vt)


def _diff_attention(qt, k, vt, pos, ftab, lam_params, sub_g, lambda_init, *, tq=ATTN_TQ, tk=ATTN_TK):
    b, s, w = k.shape
    pmin = jnp.min(pos.reshape(b, s // tq, tq), axis=-1)
    pmax = jnp.max(pos.reshape(b, s // tk, tk), axis=-1)
    pos_col = pos.reshape(b, s, 1)
    pos_row = pos.reshape(b, 1, s)
    grid_spec = pltpu.PrefetchScalarGridSpec(
        num_scalar_prefetch=2,
        grid=(b, s // tq),
        in_specs=[pl.BlockSpec((None, w, tq), lambda bi, i, *_: (bi, 0, i)),
                  pl.BlockSpec((None, s, w), lambda bi, i, *_: (bi, 0, 0), pipeline_mode=pl.Buffered(1)),
                  pl.BlockSpec((None, w, s), lambda bi, i, *_: (bi, 0, 0), pipeline_mode=pl.Buffered(1)),
                  pl.BlockSpec((None, 1, tq), lambda bi, i, *_: (bi, 0, i)),
                  pl.BlockSpec((None, s, 1), lambda bi, i, *_: (bi, 0, 0), pipeline_mode=pl.Buffered(1)),
                  pl.BlockSpec(ftab.shape, lambda bi, i, *_: (0, 0, 0)),
                  pl.BlockSpec(lam_params.shape, lambda bi, i, *_: (0, 0)),
                  pl.BlockSpec(sub_g.shape, lambda bi, i, *_: (0, 0))],
        out_specs=pl.BlockSpec((None, tq, w), lambda bi, i, *_: (bi, i, 0)),
        scratch_shapes=_flash_scratch(w // HEAD_W, tk, 2 * tq),
    )
    return pl.pallas_call(
        functools.partial(_flash_kernel, tq=tq, tk=tk, n_maps=2, lambda_init=lambda_init),
        grid_spec=grid_spec,
        out_shape=jax.ShapeDtypeStruct((b, s, w), BF16),
        compiler_params=_params("parallel", "arbitrary"),
        name="diff_attention",
    )(pmin, pmax, qt, k, vt, pos_row, pos_col, ftab, lam_params, sub_g)


def _merge_kernel(h_ref, g_ref, y0_ref, y1_ref, y2_ref, y3_ref, wg_ref, wb_ref, wo_ref, o_ref, *, col_chunk):
    h = h_ref[...]
    d = h.shape[-1]
    u = _rms(h, g_ref[...]).astype(BF16)
    ys = (y0_ref[...], y1_ref[...], y2_ref[...], y3_ref[...])
    parts = []
    for c0 in range(0, d, col_chunk):
        acc = None
        for i, y in enumerate(ys):
            gate = jax.nn.sigmoid(_dot(u, wg_ref[:, i * d + c0:i * d + c0 + col_chunk]))
            term = gate * _dot(y, wb_ref[i, :, c0:c0 + col_chunk])
            acc = term if acc is None else acc + term
        parts.append(acc.astype(BF16))
    merged = jnp.concatenate(parts, axis=-1)
    o_ref[...] = h + _dot(merged, wo_ref[...])


def _merge(h, g, ys, w_gate, w_branch, w_out, *, tm=1024, col_chunk=256):
    m, d = h.shape
    rows = lambda width: pl.BlockSpec((tm, width), lambda i: (i, 0))
    return pl.pallas_call(
        functools.partial(_merge_kernel, col_chunk=col_chunk),
        grid=(m // tm,),
        in_specs=[rows(d), _resident((1, d))] + [rows(BRANCH_W)] * N_BRANCH
                 + [_resident(w_gate), _resident(w_branch), _resident(w_out)],
        out_specs=rows(d),
        out_shape=jax.ShapeDtypeStruct((m, d), F32),
        compiler_params=_params("parallel"),
        name="merge",
    )(h, g, *ys, _operand(w_gate), _operand(w_branch), _operand(w_out))


def _xattn_kernel(h_ref, g_ref, wq_ref, k_ref, v_ref, wo_ref, o_ref):
    h = h_ref[...]
    d = h.shape[-1]
    dh = d // XA_HEADS
    u = _rms(h, g_ref[...]).astype(BF16)
    q = (_dot(u, wq_ref[...].astype(BF16)) * (dh ** -0.5 * LOG2E)).astype(BF16)
    out = h
    for hd in range(XA_HEADS):
        sl = slice(hd * dh, (hd + 1) * dh)
        s = _dot_nt(q[:, sl], k_ref[:, sl])
        p = jnp.exp2(s - jnp.max(s, axis=-1, keepdims=True))
        inv = 1.0 / jnp.sum(p, axis=-1, keepdims=True)
        o = _dot(p.astype(BF16), v_ref[:, sl]) * inv
        out = out + _dot(o.astype(BF16), wo_ref[sl, :].astype(BF16))
    o_ref[...] = out


def _cross_attention(h, g, wq, k, v, wo, batch, seq, *, tm=1024):
    m, d = h.shape
    mem_len = k.shape[0] // batch
    per = seq // tm
    return pl.pallas_call(
        _xattn_kernel,
        grid=(m // tm,),
        in_specs=[pl.BlockSpec((tm, d), lambda i: (i, 0)), _resident((1, d)), _resident(wq),
                  pl.BlockSpec((mem_len, d), lambda i: (i // per, 0)),
                  pl.BlockSpec((mem_len, d), lambda i: (i // per, 0)),
                  _resident(wo)],
        out_specs=pl.BlockSpec((tm, d), lambda i: (i, 0)),
        out_shape=jax.ShapeDtypeStruct((m, d), F32),
        compiler_params=_params("parallel"),
        name="cross_attention",
    )(h, g, _operand(wq), k, v, _operand(wo))


def _pad_cols(w, left, total):
    return jnp.pad(w, ((0, 0),) * (w.ndim - 1) + ((left, total - left - w.shape[-1]),))


def _w_in_plan():
    names = ("rq", "rk", "rv", "rg", "sz", "sxbc", "sdt", "cq", "ckv", "kr", "dq", "dk", "dv", "gl")
    src, start = {}, 0
    for n, s in zip(names, IN_SIZES):
        src[n] = (start, s)
        start += s
    half = MLA_ROPE // 2
    kr0 = src["kr"][0]
    moves, zeros, dst = [], [], 0
    rh = RET_DK // 2
    for n in ("rq", "rk"):
        for tile in range(RET_QK // LANES):
            for part in range(2):
                for hh in range(2):
                    moves.append((src[n][0] + (2 * tile + hh) * RET_DK + part * rh, rh, dst))
                    dst += rh
    for n in ("rv", "rg", "sxbc", "sz"):
        moves.append((*src[n], dst))
        dst += src[n][1]
    zeros.append((dst, SSD_DT_W))
    moves.append((*src["sdt"], dst))
    dst += SSD_DT_W
    for n in ("dq", "dk", "dv", "cq", "ckv"):
        moves.append((*src[n], dst))
        dst += src[n][1]
    zeros.append((dst, 2 * MLA_PAD))
    moves.append((kr0, MLA_ROPE, dst + MLA_NOPE))
    moves.append((kr0 + half, half, dst + MLA_PAD + MLA_NOPE))
    moves.append((kr0, half, dst + MLA_PAD + MLA_NOPE + half))
    dst += 2 * MLA_PAD
    return moves, zeros, src["gl"], dst


def _w_in_relayout_kernel(x_ref, mix_ref, gate_ref, *, moves, zeros, gate):
    for d0, width in zeros:
        mix_ref[:, d0:d0 + width] = jnp.zeros((mix_ref.shape[0], width), mix_ref.dtype)
    for s0, width, d0 in moves:
        mix_ref[:, d0:d0 + width] = x_ref[:, s0:s0 + width].astype(mix_ref.dtype)
    gate_ref[...] = x_ref[:, gate[0]:gate[0] + gate[1]].astype(gate_ref.dtype)


def _w_in_relayout(w_in, *, row_blocks=4):
    depth, r, c = w_in.shape
    moves, zeros, gate, mix_w = _w_in_plan()
    tr = r // row_blocks
    blk = lambda width: pl.BlockSpec((None, tr, width), lambda l, i: (l, i, 0))
    return pl.pallas_call(
        functools.partial(_w_in_relayout_kernel, moves=moves, zeros=zeros, gate=gate),
        grid=(depth, row_blocks),
        in_specs=[blk(c)],
        out_specs=[blk(mix_w), blk(gate[1])],
        out_shape=[jax.ShapeDtypeStruct((depth, r, mix_w), BF16), jax.ShapeDtypeStruct((depth, r, gate[1]), BF16)],
        compiler_params=_params("parallel", "parallel"),
        name="w_in_relayout",
    )(w_in)


def _mla_up_weights(w_uq, w_ukv):
    depth, d, _ = w_uq.shape
    half = MLA_ROPE // 2
    uq = w_uq.astype(BF16).reshape(depth, d, MLA_HEADS, MLA_QH)
    pe = uq[..., MLA_NOPE:]
    partner = jnp.concatenate([jnp.zeros_like(uq[..., :MLA_NOPE]), pe[..., half:], pe[..., :half]], axis=-1)
    pad = lambda a: _pad_cols(a, 0, MLA_PAD).reshape(depth, d, MLA_HEADS * MLA_PAD)
    uq_p = jnp.concatenate([pad(uq), pad(partner)], axis=-1)
    r = w_ukv.shape[1]
    ukv = w_ukv.astype(BF16).reshape(depth, r, MLA_HEADS, MLA_NOPE + MLA_V)
    k_nope = _pad_cols(ukv[..., :MLA_NOPE], 0, MLA_PAD)
    ukv_p = jnp.concatenate([k_nope.reshape(depth, r, -1), ukv[..., MLA_NOPE:].reshape(depth, r, -1)], axis=-1)
    return uq_p, ukv_p


def _t5_bucket_const():
    n = jnp.arange(REL_TABLE)
    exact = REL_BUCKETS // 2
    nf = jnp.maximum(n, 1).astype(F32)
    large = exact + (jnp.log(nf / exact) / math.log(REL_MAX_DIST / exact) * (REL_BUCKETS - exact)).astype(jnp.int32)
    return jnp.where(n < exact, n, jnp.minimum(large, REL_BUCKETS - 1))


MIX_OUTS = ((3 * BRANCH_W, F32, 1.0, False),
            (SSM_XBC + SSM_INNER, F32, 1.0, False),
            (SSD_DT_W, F32, 1.0, True),
            (BRANCH_W, BF16, DIFF_DH ** -0.5 * LOG2E, True),
            (BRANCH_W, BF16, 1.0, False),
            (BRANCH_W, BF16, 1.0, True))


def kernel(x, mem, positions, ffn1_norm, ffn1_w_gate, ffn1_w_up, ffn1_w_down, mix_norm, w_in, ssm_conv_w,
           ssm_conv_b, ssm_dt_bias, ssm_a_log, ssm_d, ssm_norm, mla_q_norm, mla_kv_norm, mla_w_uq, mla_w_ukv,
           diff_lambda, diff_norm, rel_bias, w_branch, w_out, xa_norm, mem_norm, xa_wq, xa_wk, xa_wv, xa_wo,
           ffn2_norm, ffn2_w_gate, ffn2_w_up, ffn2_w_down, final_norm):
    batch, seq, d = x.shape
    depth = w_in.shape[0]
    m = batch * seq
    h = x.reshape(m, d)
    mem2 = mem.reshape(batch * mem.shape[1], d)
    row = lambda a: a.reshape(1, -1)

    pos_col = positions.reshape(m, 1)
    ret_cos, ret_sin, mla_cos, mla_sin = _rope_tables(
        pos_col, [(LANES, LANES // 2, 0, RET_DK // 2), (MLA_PAD, MLA_ROPE // 2, MLA_NOPE, MLA_ROPE // 2)])
    ftab = rel_bias[_t5_bucket_const()].T.reshape(DIFF_HEADS, 1, REL_TABLE)

    w_mix, w_gate = _w_in_relayout(w_in)
    w_uq_p, w_ukv_p = _mla_up_weights(mla_w_uq, mla_w_ukv)
    ffn1 = (ffn1_w_gate, ffn1_w_up, ffn1_w_down)
    ffn2 = (ffn2_w_gate, ffn2_w_up, ffn2_w_down)
    w_branch_b = _cast_bf16(w_branch.reshape(depth, N_BRANCH * BRANCH_W, d)).reshape(w_branch.shape)
    w_out_b = _cast_bf16(w_out)

    for l in range(depth):
        at = lambda stack: _LayerOf(stack, l)
        h = _ffn(h, row(ffn1_norm[l]), *(at(w) for w in ffn1))
        mla = (mla_cos, mla_sin, row(mla_q_norm[l]), row(mla_kv_norm[l]), at(w_uq_p), at(w_ukv_p))
        ret_p, ssm_p, dt_t, dqt, dk, dvt, mqt, mk, mvt = _norm_proj(h, row(mix_norm[l]), at(w_mix), MIX_OUTS, seq,
                                                                   mla=mla, name="mix_proj")
        y_ret = _retention(ret_p.reshape(batch, seq, -1), ret_cos.reshape(batch, seq, -1),
                           ret_sin.reshape(batch, seq, -1)).reshape(m, BRANCH_W)
        y_ssm = _ssd(ssm_p.reshape(batch, seq, -1), dt_t, ssm_conv_w[l], row(ssm_conv_b[l]),
                     ssm_dt_bias[l].reshape(-1, 1), ssm_a_log[l].reshape(-1, 1),
                     row(jnp.repeat(ssm_d[l], SSM_HEADDIM)), row(ssm_norm[l])).reshape(m, SSM_INNER)
        y_mla = _mla_attention(mqt, mk.reshape(batch, seq, -1), mvt).reshape(m, BRANCH_W)
        lambda_init = 0.8 - 0.6 * math.exp(-0.3 * l)
        y_diff = _diff_attention(dqt, dk.reshape(batch, seq, -1), dvt, positions, ftab, diff_lambda[l],
                                 row(diff_norm[l]), lambda_init).reshape(m, BRANCH_W)
        h = _merge(h, row(mix_norm[l]), (y_ret, y_ssm, y_mla, y_diff), at(w_gate), at(w_branch_b), at(w_out_b))
        xk, xv = (_norm_proj(mem2, row(mem_norm[l]), at(w), [(d, BF16, 1.0, False)], mem.shape[1],
                             tm=mem.shape[1], name="mem_kv")[0] for w in (xa_wk, xa_wv))
        h = _cross_attention(h, row(xa_norm[l]), at(xa_wq), xk, xv, at(xa_wo), batch, seq)
        h = _ffn(h, row(ffn2_norm[l]), *(at(w) for w in ffn2),
                 final_g=row(final_norm) if l == depth - 1 else None)
    return h.reshape(batch, seq, d)
```

```python
import functools
import math
from typing import NamedTuple

import jax
import jax.numpy as jnp
import numpy as np
from jax import lax
from jax.experimental import pallas as pl
from jax.experimental.pallas import tpu as pltpu

F32 = jnp.float32
BF16 = jnp.bfloat16

EPS = 1e-6
NEG_INF = -1e30
LOG2E = math.log2(math.e)
ROPE_BASE = 10000.0
CHUNK = 128
BRANCH_W = 512
N_BRANCH = 4

RET_HEADS = 4
RET_DK = 64
RET_DV = 128
RET_QK = RET_HEADS * RET_DK

SSM_HEADDIM = 64
SSM_INNER = 512
SSM_HEADS = 8
SSM_GROUPS = 2
SSM_STATE = 128
SSM_CONV = 4
SSM_XBC = SSM_INNER + 2 * SSM_GROUPS * SSM_STATE
SSM_TAIL = 8
SSD_DT_W = 128

MLA_HEADS = 4
MLA_Q_RANK = 256
MLA_KV_RANK = 128
MLA_NOPE = 64
MLA_ROPE = 32
MLA_V = 128
MLA_QH = MLA_NOPE + MLA_ROPE
MLA_PAD = 128
MLA_IN_W = MLA_Q_RANK + MLA_KV_RANK + 2 * MLA_PAD
MLA_OUTS = ((MLA_HEADS * MLA_PAD, jnp.bfloat16, 1.0, True), (MLA_HEADS * MLA_PAD, jnp.bfloat16, 1.0, False),
            (MLA_HEADS * MLA_V, jnp.bfloat16, 1.0, True))
HEAD_W = 128
ATTN_TQ = 512
ATTN_TK = 256

DIFF_HEADS = 4
DIFF_DH = 64

REL_BUCKETS = 32
REL_MAX_DIST = 128
REL_TABLE = 128

XA_HEADS = 4

LANES = 128
VMEM_LIMIT = 56 * 1024 * 1024

IN_SIZES = (RET_QK, RET_QK, BRANCH_W, BRANCH_W,
            SSM_INNER, SSM_XBC, SSM_HEADS,
            MLA_Q_RANK, MLA_KV_RANK, MLA_ROPE,
            BRANCH_W, BRANCH_W, BRANCH_W,
            N_BRANCH * 1024)


def _params(*sem):
    return pltpu.CompilerParams(dimension_semantics=sem, vmem_limit_bytes=VMEM_LIMIT)


class _LayerOf(NamedTuple):
    stack: jax.Array
    layer: int

    @property
    def shape(self):
        return self.stack.shape[1:]


def _operand(x):
    return x.stack if isinstance(x, _LayerOf) else x


def _resident(x):
    if isinstance(x, _LayerOf):
        shape, lead, first = x.shape, (None,), (x.layer,)
    else:
        shape, lead, first = (x if isinstance(x, tuple) else x.shape), (), ()
    nd = len(shape)
    return pl.BlockSpec(lead + tuple(shape), lambda *_: first + (0,) * nd, pipeline_mode=pl.Buffered(1))


def _cast_kernel(x_ref, o_ref):
    o_ref[...] = x_ref[...].astype(o_ref.dtype)


def _cast_bf16(w, *, row_blocks=4):
    depth, r, c = w.shape
    tr = r // row_blocks
    spec = pl.BlockSpec((None, tr, c), lambda l, i: (l, i, 0))
    return pl.pallas_call(
        _cast_kernel,
        grid=(depth, row_blocks),
        in_specs=[spec],
        out_specs=spec,
        out_shape=jax.ShapeDtypeStruct(w.shape, BF16),
        compiler_params=_params("parallel", "parallel"),
        name="cast_bf16",
    )(w)


def _rms(x, g):
    return x * lax.rsqrt(jnp.mean(x * x, axis=-1, keepdims=True) + EPS) * g


def _silu(x):
    half = 0.5 * x
    return half + half * jnp.tanh(half)


def _dot(a, b):
    return jnp.dot(a, b, preferred_element_type=F32)


def _dot_nt(a, b):
    return lax.dot_general(a, b, (((1,), (1,)), ((), ())), preferred_element_type=F32)


def _dot_tn(a, b):
    return lax.dot_general(a, b, (((0,), (0,)), ((), ())), preferred_element_type=F32)


def _ffn_kernel(h_ref, g_ref, wg_ref, wu_ref, wd_ref, *rest, ff_chunk, final):
    if final:
        gf_ref, o_ref = rest
    else:
        (o_ref,) = rest
    h = h_ref[...]
    xn = _rms(h, g_ref[...]).astype(BF16)
    acc = jnp.zeros(h.shape, F32)
    d_ff = wg_ref.shape[1]
    for c0 in range(0, d_ff, ff_chunk):
        a = _dot(xn, wg_ref[:, c0:c0 + ff_chunk].astype(BF16))
        b = _dot(xn, wu_ref[:, c0:c0 + ff_chunk].astype(BF16))
        acc = acc + _dot((_silu(a) * b).astype(BF16), wd_ref[c0:c0 + ff_chunk, :].astype(BF16))
    out = h + 0.5 * acc
    if final:
        out = _rms(out, gf_ref[...])
    o_ref[...] = out


def _ffn(h, g, wg, wu, wd, final_g=None, *, tm=512, ff_chunk=256):
    m, d = h.shape
    d_ff = wg.shape[1]
    final = final_g is not None
    in_specs = [pl.BlockSpec((tm, d), lambda i: (i, 0)), _resident((1, d)),
                _resident(wg), _resident(wu), _resident(wd)]
    args = [h, g, _operand(wg), _operand(wu), _operand(wd)]
    if final:
        in_specs.append(_resident((1, d)))
        args.append(final_g)
    return pl.pallas_call(
        functools.partial(_ffn_kernel, ff_chunk=ff_chunk, final=final),
        grid=(m // tm,),
        in_specs=in_specs,
        out_specs=pl.BlockSpec((tm, d), lambda i: (i, 0)),
        out_shape=jax.ShapeDtypeStruct((m, d), F32),
        compiler_params=_params("parallel"),
        name="ffn",
    )(*args)


def _mla_qkv(p, cos, sin, q_norm, kv_norm, w_uq, w_ukv):
    kvw = MLA_HEADS * MLA_PAD
    cq = p[:, :MLA_Q_RANK]
    ckv = p[:, MLA_Q_RANK:MLA_Q_RANK + MLA_KV_RANK]
    kr0 = MLA_Q_RANK + MLA_KV_RANK
    kr = p[:, kr0:kr0 + MLA_PAD]
    kr_partner = p[:, kr0 + MLA_PAD:]
    qq = _dot(_rms(cq, q_norm).astype(BF16), w_uq)
    cos4 = jnp.concatenate([cos] * MLA_HEADS, axis=-1)
    sin4 = jnp.concatenate([sin] * MLA_HEADS, axis=-1)
    q = (qq[:, :kvw] * cos4 + qq[:, kvw:] * sin4) * (MLA_QH ** -0.5 * LOG2E)
    kv = _dot(_rms(ckv, kv_norm).astype(BF16), w_ukv)
    kpe = kr * cos + kr_partner * sin
    k = kv[:, :kvw] + jnp.concatenate([kpe] * MLA_HEADS, axis=-1)
    return q.astype(BF16).T, k.astype(BF16), kv[:, kvw:].astype(BF16).T


def _norm_proj_kernel(x_ref, g_ref, w_ref, *rest, outs, col_chunk, mla):
    if mla:
        cos_ref, sin_ref, qn_ref, kvn_ref, wuq_ref, wukv_ref = rest[:6]
        o_refs = rest[6:]
    else:
        o_refs = rest
    xn = _rms(x_ref[...], g_ref[...]).astype(BF16)
    off = 0
    for o_ref, (width, _, scale, transposed) in zip(o_refs, outs):
        for c0 in range(0, width, col_chunk):
            c1 = min(c0 + col_chunk, width)
            y = _dot(xn, w_ref[:, off + c0:off + c1])
            if scale != 1.0:
                y = y * scale
            if transposed:
                o_ref[c0:c1, :] = y.astype(o_ref.dtype).T
            else:
                o_ref[:, c0:c1] = y.astype(o_ref.dtype)
        off += width
    if mla:
        q_ref, k_ref, v_ref = o_refs[len(outs):]
        p = _dot(xn, w_ref[:, off:])
        q_ref[...], k_ref[...], v_ref[...] = _mla_qkv(p, cos_ref[...], sin_ref[...], qn_ref[...], kvn_ref[...],
                                                      wuq_ref[...], wukv_ref[...])


def _norm_proj(x, g, w, outs, seq, *, mla=None, tm=512, col_chunk=512, name="norm_proj"):
    m, d = x.shape
    assert sum(o[0] for o in outs) + (MLA_IN_W if mla else 0) == w.shape[1]
    per = seq // tm
    rows = lambda width: pl.BlockSpec((tm, width), lambda i: (i, 0))
    cols = lambda width: pl.BlockSpec((None, width, tm), lambda i: (i // per, 0, i % per))
    out_specs, out_shape = [], []
    for width, dt, _, transposed in tuple(outs) + (MLA_OUTS if mla else ()):
        if transposed:
            out_specs.append(cols(width))
            out_shape.append(jax.ShapeDtypeStruct((m // seq, width, seq), dt))
        else:
            out_specs.append(rows(width))
            out_shape.append(jax.ShapeDtypeStruct((m, width), dt))
    in_specs = [rows(d), _resident((1, d)), _resident(w)]
    args = [x, g, _operand(w)]
    if mla:
        cos, sin, q_norm, kv_norm, w_uq, w_ukv = mla
        in_specs += [rows(LANES), rows(LANES), _resident(q_norm), _resident(kv_norm), _resident(w_uq),
                     _resident(w_ukv)]
        args += [cos, sin, q_norm, kv_norm, _operand(w_uq), _operand(w_ukv)]
    return pl.pallas_call(
        functools.partial(_norm_proj_kernel, outs=tuple(outs), col_chunk=col_chunk, mla=mla is not None),
        grid=(m // tm,),
        in_specs=in_specs,
        out_specs=out_specs,
        out_shape=out_shape,
        compiler_params=_params("parallel"),
        name=name,
    )(*args)


def _rope_table_kernel(pos_ref, inv_ref, idx_ref, sign_ref, *o_refs):
    ang = pos_ref[...].astype(F32) * inv_ref[...]
    cos, sin = jnp.cos(ang), jnp.sin(ang)
    for t in range(len(o_refs) // 2):
        idx = jnp.broadcast_to(idx_ref[t:t + 1, :], ang.shape)
        o_refs[2 * t][...] = jnp.take_along_axis(cos, idx, axis=1, mode="promise_in_bounds")
        o_refs[2 * t + 1][...] = (jnp.take_along_axis(sin, idx, axis=1, mode="promise_in_bounds")
                                  * sign_ref[t:t + 1, :])


def _rope_tables(pos_col, layouts, *, tm=1024):
    m = pos_col.shape[0]
    lane = np.arange(LANES)
    inv_parts, idx_rows, sign_rows, base = [], [], [], 0
    for period, half, start, n_freq in layouts:
        inv_parts.append(jnp.exp(-math.log(ROPE_BASE) * jnp.arange(n_freq, dtype=F32) / n_freq))
        rel = lane % period
        inside = (rel >= start) & (rel < start + 2 * half)
        idx_rows.append(np.where(inside, base + (rel - start) % half % n_freq, LANES - 1))
        sign_rows.append(np.where(inside, np.where(rel < start + half, -1.0, 1.0), 0.0))
        base += n_freq
    assert base < LANES
    inv = jnp.concatenate(inv_parts + [jnp.zeros((LANES - base,), F32)]).reshape(1, LANES)
    idx = np.stack(idx_rows).astype(np.int32)
    sign = np.stack(sign_rows).astype(np.float32)
    n_out = 2 * len(layouts)
    return pl.pallas_call(
        _rope_table_kernel,
        grid=(m // tm,),
        in_specs=[pl.BlockSpec((tm, 1), lambda i: (i, 0)), _resident(inv), _resident(idx), _resident(sign)],
        out_specs=[pl.BlockSpec((tm, LANES), lambda i: (i, 0))] * n_out,
        out_shape=[jax.ShapeDtypeStruct((m, LANES), F32)] * n_out,
        compiler_params=_params("parallel"),
        name="rope_tables",
    )(pos_col, inv, idx, sign)


def _ret_head_of_lane(width):
    lane = np.arange(width)
    return 2 * (lane // LANES) + (lane // (RET_DK // 2)) % 2


def _apply_rope_ret(x, cos, sin_signed):
    tiles = [x[:, t:t + LANES] for t in range(0, x.shape[-1], LANES)]
    return jnp.concatenate([xt * cos + pltpu.roll(xt, LANES // 2, 1) * sin_signed for xt in tiles], axis=-1)


def _ret_kernel(q_ref, k_ref, v_ref, g_ref, cos_ref, sin_ref, intra_ref, qdec_ref, kdec_ref,
                sdec_ref, bmask_ref, o_ref, state_sc):
    @pl.when(pl.program_id(0) == 0)
    def _():
        state_sc[...] = jnp.zeros_like(state_sc)

    lane = lax.broadcasted_iota(jnp.int32, (1, RET_QK), 1)
    head = 2 * (lane // LANES) + (lane // (RET_DK // 2)) % 2
    for b in range(q_ref.shape[0]):
        cos, sin = cos_ref[b], sin_ref[b]
        q = _apply_rope_ret(q_ref[b], cos, sin) * (RET_DK ** -0.5)
        k = _apply_rope_ret(k_ref[b], cos, sin)
        vb = v_ref[b].astype(BF16)
        kb = k.astype(BF16)
        state = state_sc[b]
        cross = _dot((q * qdec_ref[...]).astype(BF16), state.astype(BF16))
        outs = []
        for h in range(RET_HEADS):
            qh = jnp.where(head == h, q, 0.0).astype(BF16)
            att = _dot_nt(qh, kb) * intra_ref[h]
            sl = slice(h * RET_DV, (h + 1) * RET_DV)
            o = _dot(att.astype(BF16), vb[:, sl]) + cross[:, sl]
            mu = jnp.mean(o, axis=-1, keepdims=True)
            oc = o - mu
            outs.append(oc * lax.rsqrt(jnp.mean(oc * oc, axis=-1, keepdims=True) + EPS))
        o_ref[b] = (jnp.concatenate(outs, axis=-1) * _silu(g_ref[b])).astype(o_ref.dtype)
        kv = _dot_tn((k * kdec_ref[...]).astype(BF16), vb)
        state_sc[b] = state * sdec_ref[...] + kv * bmask_ref[...]


def _retention_consts():
    c = CHUNK
    f = np.float32
    log_g = np.log1p(-np.exp2(f(-5.0) - np.arange(RET_HEADS, dtype=f)))
    i = np.arange(c, dtype=f)
    rel = i[:, None] - i[None, :]
    intra = np.where(rel >= 0, np.exp(log_g[:, None, None] * np.maximum(rel, f(0.0))), f(0.0)).astype(f)
    row_h = _ret_head_of_lane(RET_QK)
    lg_lane = log_g[row_h]
    qdec = np.exp(lg_lane[None, :] * (i[:, None] + f(1.0)))
    kdec = np.exp(lg_lane[None, :] * (f(c - 1.0) - i[:, None]))
    col_h = np.repeat(np.arange(RET_HEADS), RET_DV)
    bmask = (row_h[:, None] == col_h[None, :]).astype(f)
    sdec = bmask * np.exp(lg_lane * f(c))[:, None]
    return intra, qdec.astype(f), kdec.astype(f), sdec.astype(f), bmask


def _retention(ret_p, cos, sin):
    c = CHUNK
    batch, seq, _ = ret_p.shape
    intra, qdec, kdec, sdec, bmask = _retention_consts()
    blk = lambda width, col: pl.BlockSpec((batch, c, width), lambda j: (0, j, col))
    return pl.pallas_call(
        _ret_kernel,
        grid=(seq // c,),
        in_specs=[blk(RET_QK, 0), blk(RET_QK, 1), blk(BRANCH_W, 1), blk(BRANCH_W, 2), blk(LANES, 0), blk(LANES, 0),
                  _resident(intra.shape), _resident(qdec.shape), _resident(kdec.shape),
                  _resident(sdec.shape), _resident(bmask.shape)],
        out_specs=blk(BRANCH_W, 0),
        out_shape=jax.ShapeDtypeStruct((batch, seq, BRANCH_W), BF16),
        scratch_shapes=[pltpu.VMEM((batch, RET_QK, BRANCH_W), F32)],
        compiler_params=_params("arbitrary"),
        name="retention",
    )(ret_p, ret_p, ret_p, ret_p, cos, sin, intra, qdec, kdec, sdec, bmask)


def _split3(x):
    hi = x.astype(BF16)
    r = x - hi.astype(F32)
    mid = r.astype(BF16)
    lo = (r - mid.astype(F32)).astype(BF16)
    return hi, mid, lo


def _dot_split3(a, b):
    return sum(_dot(piece, b) for piece in _split3(a))


def _ssd_kernel(xbc_ref, *refs):
    xe_sc, state_sc = refs[-2:]

    @pl.when(pl.program_id(0) == 0)
    def _():
        xe_sc[...] = jnp.zeros_like(xe_sc)
        state_sc[...] = jnp.zeros_like(state_sc)

    for b in range(xbc_ref.shape[0]):
        _ssd_chunk(b, xbc_ref, *refs)


def _ssd_chunk(b, xbc_ref, z_ref, dt_ref, cw_ref, cb_ref, dtb_ref, alog_ref, dskip_ref, ng_ref, tri_ref,
               expand_ref, o_ref, xe_sc, state_sc):
    L = CHUNK
    T = SSM_TAIL
    GW = SSM_INNER // SSM_GROUPS
    x_raw = xbc_ref[b]
    tail = xe_sc[b]
    rows8 = lax.broadcasted_iota(jnp.int32, (T, 1), 0)
    conv = cb_ref[...] + cw_ref[SSM_CONV - 1:SSM_CONV, :] * x_raw
    for s in range(1, SSM_CONV):
        shifted = pltpu.roll(x_raw, s, 0)
        head = jnp.where(rows8 < s, pltpu.roll(tail, s, 0), shifted[0:T, :])
        shifted = jnp.concatenate([head, shifted[T:, :]], axis=0)
        conv = conv + cw_ref[SSM_CONV - 1 - s:SSM_CONV - s, :] * shifted
    xe_sc[b] = x_raw[L - T:L, :]
    xa = _silu(conv)
    xs = xa[:, :SSM_INNER]
    bm = xa[:, SSM_INNER:SSM_INNER + SSM_GROUPS * SSM_STATE]
    cm = xa[:, SSM_INNER + SSM_GROUPS * SSM_STATE:]

    dtr = dt_ref[b] + dtb_ref[...]
    dt_h = jnp.maximum(dtr, 0.0) + jnp.log1p(jnp.exp(-jnp.abs(dtr)))
    da = dt_h * (-jnp.exp(alog_ref[...]))
    cs_t = _dot_split3(da, tri_ref[...])
    cs_t_last = cs_t[:, L - 1:L]
    per_head = jnp.concatenate([dt_h, jnp.exp(cs_t), jnp.exp(cs_t_last - cs_t), cs_t], axis=0)
    per_chan = _dot_tn(jnp.concatenate(_split3(per_head), axis=0), expand_ref[...])
    dt, ecs, eds, cs_c = (per_chan[:, i * SSM_INNER:(i + 1) * SSM_INNER] for i in range(4))
    xdt = xs * dt
    xdec = (xdt * eds).astype(BF16)
    state = state_sc[b]
    li = lax.broadcasted_iota(jnp.int32, (L, L), 0)
    si = lax.broadcasted_iota(jnp.int32, (L, L), 1)
    causal = li >= si
    lane_half = lax.broadcasted_iota(jnp.int32, (1, LANES), 1) // SSM_HEADDIM
    ys = []
    new_states = []
    for g in range(SSM_GROUPS):
        bg = bm[:, g * SSM_STATE:(g + 1) * SSM_STATE].astype(BF16)
        cg = cm[:, g * SSM_STATE:(g + 1) * SSM_STATE].astype(BF16)
        cb = _dot_nt(cg, bg)
        gsl = slice(g * GW, (g + 1) * GW)
        y_off = _dot(cg, state[:, gsl].astype(BF16)) * ecs[:, gsl]
        for p in range(GW // LANES):
            sl = slice(g * GW + p * LANES, g * GW + (p + 1) * LANES)
            xdt_blk = xdt[:, sl]
            acc = y_off[:, p * LANES:(p + 1) * LANES]
            for hh in range(LANES // SSM_HEADDIM):
                head = (g * GW + p * LANES) // SSM_HEADDIM + hh
                col = cs_c[:, head * SSM_HEADDIM:head * SSM_HEADDIM + 1]
                rowv = cs_t[head:head + 1, :]
                dec = jnp.where(causal, jnp.exp(jnp.minimum(col - rowv, 0.0)), 0.0)
                xm = jnp.where(lane_half == hh, xdt_blk, 0.0).astype(BF16)
                acc = acc + _dot((cb * dec).astype(BF16), xm)
            ys.append(acc)
        new_states.append(_dot_tn(bg, xdec[:, gsl]))
    y = jnp.concatenate(ys, axis=-1) + xs * dskip_ref[...]
    y = y * _silu(z_ref[b])
    o_ref[b] = _rms(y, ng_ref[...]).astype(o_ref.dtype)
    state_sc[b] = state * ecs[L - 1:L, :] + jnp.concatenate(new_states, axis=-1)


def _ssd(ssm_p, dt_t, conv_w, conv_b, dt_bias_col, a_log_col, d_full, norm_g):
    L = CHUNK
    batch, seq, _ = ssm_p.shape
    tri = (np.arange(L)[:, None] <= np.arange(L)[None, :]).astype(BF16)
    r = np.arange(3 * 4 * SSM_HEADS)
    c = np.arange(4 * SSM_INNER)
    expand = (((r[:, None] // SSM_HEADS) % 4 == c[None, :] // SSM_INNER)
              & (r[:, None] % SSM_HEADS == (c[None, :] % SSM_INNER) // SSM_HEADDIM)).astype(BF16)
    return pl.pallas_call(
        _ssd_kernel,
        grid=(seq // L,),
        in_specs=[pl.BlockSpec((batch, L, SSM_XBC), lambda j: (0, j, 0)),
                  pl.BlockSpec((batch, L, SSM_INNER), lambda j: (0, j, SSM_XBC // SSM_INNER)),
                  pl.BlockSpec((batch, SSM_HEADS, L), lambda j: (0, 0, j)),
                  _resident(conv_w.shape), _resident(conv_b.shape), _resident(dt_bias_col.shape),
                  _resident(a_log_col.shape), _resident(d_full.shape), _resident(norm_g.shape),
                  _resident(tri.shape), _resident(expand.shape)],
        out_specs=pl.BlockSpec((batch, L, SSM_INNER), lambda j: (0, j, 0)),
        out_shape=jax.ShapeDtypeStruct((batch, seq, SSM_INNER), BF16),
        scratch_shapes=[pltpu.VMEM((batch, SSM_TAIL, SSM_XBC), F32),
                        pltpu.VMEM((batch, SSM_STATE, SSM_INNER), F32)],
        compiler_params=_params("arbitrary"),
        name="ssd",
    )(ssm_p, ssm_p, dt_t, conv_w, conv_b, dt_bias_col, a_log_col, d_full, norm_g, tri, expand)


def _flash_kernel(*refs, tq, tk, n_maps, lambda_init):
    biased = n_maps == 2
    if biased:
        (pmin_ref, pmax_ref, qt_ref, k_ref, vt_ref, pq_ref, pk_ref, ftab_ref, lam_ref, sg_ref,
         o_ref, m_sc, l_sc, acc_sc, s0_sc, s1_sc) = refs
    else:
        qt_ref, k_ref, vt_ref, o_ref, m_sc, l_sc, acc_sc, s0_sc, s1_sc = refs
    bi = pl.program_id(0)
    qi = pl.program_id(1)
    hw = HEAD_W
    heads = k_ref.shape[-1] // hw
    cols = n_maps * tq
    hs = lambda h: slice(h * hw, (h + 1) * hw)
    ws = []
    for h in range(heads):
        qt = qt_ref[hs(h), :]
        if biased:
            half = lax.broadcasted_iota(jnp.int32, (hw, 1), 0) // DIFF_DH
            zero = jnp.zeros_like(qt)
            qt = jnp.concatenate([jnp.where(half == 0, qt, zero), jnp.where(half == 1, qt, zero)], axis=1)
        ws.append(qt)
    if biased:
        pq = pq_ref[...]
        ftabs = [jnp.broadcast_to(ftab_ref[h] * LOG2E, (tk, REL_TABLE)) for h in range(heads)]
    m_sc[...] = jnp.full(m_sc.shape, NEG_INF, F32)
    l_sc[...] = jnp.zeros_like(l_sc)
    acc_sc[...] = jnp.zeros_like(acc_sc)

    def near_biases(off, q_lo=0):
        idx = jnp.clip(pq[:, q_lo:] - pk_ref[pl.ds(off, tk), :], 0, REL_TABLE - 1)
        tiles = [jnp.concatenate([jnp.take_along_axis(ftabs[h], idx[:, c:c + LANES], axis=1,
                                                      mode="promise_in_bounds")
                                  for c in range(0, tq - q_lo, LANES)], axis=1) for h in range(heads)]

        def bias(h, cs):
            base = (cs.start // tq) * tq + q_lo
            return tiles[h][:, cs.start - base:cs.stop - base]
        return bias

    s_scs = (s0_sc, s1_sc)

    all_cols = tuple(slice(mi * tq, (mi + 1) * tq) for mi in range(n_maps))
    late_cols = tuple(slice(mi * tq + tk, (mi + 1) * tq) for mi in range(n_maps))

    def scores(off, slot, col_ranges=all_cols):
        for h in range(heads):
            for cs in col_ranges:
                s_scs[slot][h, :, cs] = _dot(k_ref[pl.ds(off, tk), hs(h)], ws[h][:, cs])

    def consume(off, slot, bias, mask, const=None, col_ranges=all_cols):
        for h in range(heads):
            for cs in col_ranges:
                s = s_scs[slot][h, :, cs]
                if bias is not None:
                    s = s + bias(h, cs)
                if mask is not None:
                    s = jnp.where(mask[:, cs], s, NEG_INF)
                m = m_sc[h, :, cs]
                m_tile = jnp.max(s, axis=0, keepdims=True)
                if const is not None:
                    m_tile = m_tile + const[h]
                m_new = jnp.maximum(m, m_tile)
                alpha = jnp.exp2(m - m_new)
                p = jnp.exp2(s - (m_new if const is None else m_new - const[h]))
                l_sc[h, :, cs] = alpha * l_sc[h, :, cs] + jnp.sum(p, axis=0, keepdims=True)
                acc_sc[h, :, cs] = (alpha * acc_sc[h, :, cs]
                                    + _dot(vt_ref[hs(h), pl.ds(off, tk)], p.astype(BF16)))
                m_sc[h, :, cs] = m_new

    assert tq == 2 * tk
    scores(0, 0)

    def body(jj, carry):
        off0 = pl.multiple_of(jj * tq, tq)
        off1 = pl.multiple_of(off0 + tk, tk)
        off2 = pl.multiple_of(off0 + tq, tq)

        def pair(near0, near1):
            const = [ftab_ref[h][:, REL_TABLE - 1:REL_TABLE] * LOG2E for h in range(heads)] if biased else None
            scores(off1, 1)
            consume(off0, 0, near_biases(off0) if near0 else None, None, None if near0 else const)
            scores(off2, 0)
            consume(off1, 1, near_biases(off1) if near1 else None, None, None if near1 else const)

        if biased:
            far0 = pmin_ref[bi, qi] - pmax_ref[bi, 2 * jj] >= REL_TABLE - 1
            far1 = pmin_ref[bi, qi] - pmax_ref[bi, 2 * jj + 1] >= REL_TABLE - 1

            @pl.when(far0 & far1)
            def _():
                pair(False, False)

            @pl.when(far0 & jnp.logical_not(far1))
            def _():
                pair(False, True)

            @pl.when(jnp.logical_not(far0))
            def _():
                pair(True, True)
        else:
            pair(False, False)
        return carry

    lax.fori_loop(0, qi, body, 0)
    for d in range(2):
        off = pl.multiple_of(qi * tq + d * tk, tk)
        if d == 0:
            scores(pl.multiple_of(off + tk, tk), 1, late_cols)
        key = lax.broadcasted_iota(jnp.int32, (tk, cols), 0) + d * tk
        qry = lax.broadcasted_iota(jnp.int32, (tk, cols), 1) % tq
        consume(off, d, near_biases(off, d * tk) if biased else None, key <= qry,
                col_ranges=all_cols if d == 0 else late_cols)

    if biased:
        lp = lam_ref[...]
        lam = (jnp.exp(jnp.sum(lp[0:1] * lp[1:2], axis=-1, keepdims=True))
               - jnp.exp(jnp.sum(lp[2:3] * lp[3:4], axis=-1, keepdims=True)) + lambda_init)
    for h in range(heads):
        o = acc_sc[h] / l_sc[h]
        if biased:
            o = (o[:, :tq] - lam * o[:, tq:]).T
            o_ref[:, hs(h)] = (_rms(o, sg_ref[...]) * (1.0 - lambda_init)).astype(o_ref.dtype)
        else:
            o_ref[:, hs(h)] = o.astype(o_ref.dtype).T


def _flash_scratch(heads, tk, cols):
    return [pltpu.VMEM((heads, 1, cols), F32), pltpu.VMEM((heads, 1, cols), F32),
            pltpu.VMEM((heads, HEAD_W, cols), F32),
            pltpu.VMEM((heads, tk, cols), F32), pltpu.VMEM((heads, tk, cols), F32)]


def _mla_attention(qt, k, vt, *, tq=ATTN_TQ, tk=ATTN_TK):
    b, s, w = k.shape
    return pl.pallas_call(
        functools.partial(_flash_kernel, tq=tq, tk=tk, n_maps=1, lambda_init=None),
        grid=(b, s // tq),
        in_specs=[pl.BlockSpec((None, w, tq), lambda bi, i: (bi, 0, i)),
                  pl.BlockSpec((None, s, w), lambda bi, i: (bi, 0, 0), pipeline_mode=pl.Buffered(1)),
                  pl.BlockSpec((None, w, s), lambda bi, i: (bi, 0, 0), pipeline_mode=pl.Buffered(1))],
        out_specs=pl.BlockSpec((None, tq, w), lambda bi, i: (bi, i, 0)),
        out_shape=jax.ShapeDtypeStruct((b, s, w), BF16),
        scratch_shapes=_flash_scratch(w // HEAD_W, tk, tq),
        compiler_params=_params("parallel", "arbitrary"),
        name="mla_attention",
    )(qt, k, vt)


def _diff_attention(qt, k, vt, pos, ftab, lam_params, sub_g, lambda_init, *, tq=ATTN_TQ, tk=ATTN_TK):
    b, s, w = k.shape
    pmin = jnp.min(pos.reshape(b, s // tq, tq), axis=-1)
    pmax = jnp.max(pos.reshape(b, s // tk, tk), axis=-1)
    pos_col = pos.reshape(b, s, 1)
    pos_row = pos.reshape(b, 1, s)
    grid_spec = pltpu.PrefetchScalarGridSpec(
        num_scalar_prefetch=2,
        grid=(b, s // tq),
        in_specs=[pl.BlockSpec((None, w, tq), lambda bi, i, *_: (bi, 0, i)),
                  pl.BlockSpec((None, s, w), lambda bi, i, *_: (bi, 0, 0), pipeline_mode=pl.Buffered(1)),
                  pl.BlockSpec((None, w, s), lambda bi, i, *_: (bi, 0, 0), pipeline_mode=pl.Buffered(1)),
                  pl.BlockSpec((None, 1, tq), lambda bi, i, *_: (bi, 0, i)),
                  pl.BlockSpec((None, s, 1), lambda bi, i, *_: (bi, 0, 0), pipeline_mode=pl.Buffered(1)),
                  pl.BlockSpec(ftab.shape, lambda bi, i, *_: (0, 0, 0)),
                  pl.BlockSpec(lam_params.shape, lambda bi, i, *_: (0, 0)),
                  pl.BlockSpec(sub_g.shape, lambda bi, i, *_: (0, 0))],
        out_specs=pl.BlockSpec((None, tq, w), lambda bi, i, *_: (bi, i, 0)),
        scratch_shapes=_flash_scratch(w // HEAD_W, tk, 2 * tq),
    )
    return pl.pallas_call(
        functools.partial(_flash_kernel, tq=tq, tk=tk, n_maps=2, lambda_init=lambda_init),
        grid_spec=grid_spec,
        out_shape=jax.ShapeDtypeStruct((b, s, w), BF16),
        compiler_params=_params("parallel", "arbitrary"),
        name="diff_attention",
    )(pmin, pmax, qt, k, vt, pos_row, pos_col, ftab, lam_params, sub_g)


def _merge_kernel(h_ref, g_ref, y0_ref, y1_ref, y2_ref, y3_ref, wg_ref, wb_ref, wo_ref, o_ref, *, col_chunk):
    h = h_ref[...]
    d = h.shape[-1]
    u = _rms(h, g_ref[...]).astype(BF16)
    ys = (y0_ref[...], y1_ref[...], y2_ref[...], y3_ref[...])
    parts = []
    for c0 in range(0, d, col_chunk):
        acc = None
        for i, y in enumerate(ys):
            gate = jax.nn.sigmoid(_dot(u, wg_ref[:, i * d + c0:i * d + c0 + col_chunk]))
            term = gate * _dot(y, wb_ref[i, :, c0:c0 + col_chunk])
            acc = term if acc is None else acc + term
        parts.append(acc.astype(BF16))
    merged = jnp.concatenate(parts, axis=-1)
    o_ref[...] = h + _dot(merged, wo_ref[...])


def _merge(h, g, ys, w_gate, w_branch, w_out, *, tm=1024, col_chunk=256):
    m, d = h.shape
    rows = lambda width: pl.BlockSpec((tm, width), lambda i: (i, 0))
    return pl.pallas_call(
        functools.partial(_merge_kernel, col_chunk=col_chunk),
        grid=(m // tm,),
        in_specs=[rows(d), _resident((1, d))] + [rows(BRANCH_W)] * N_BRANCH
                 + [_resident(w_gate), _resident(w_branch), _resident(w_out)],
        out_specs=rows(d),
        out_shape=jax.ShapeDtypeStruct((m, d), F32),
        compiler_params=_params("parallel"),
        name="merge",
    )(h, g, *ys, _operand(w_gate), _operand(w_branch), _operand(w_out))


def _xattn_kernel(h_ref, g_ref, wq_ref, k_ref, v_ref, wo_ref, o_ref):
    h = h_ref[...]
    d = h.shape[-1]
    dh = d // XA_HEADS
    u = _rms(h, g_ref[...]).astype(BF16)
    q = (_dot(u, wq_ref[...].astype(BF16)) * (dh ** -0.5 * LOG2E)).astype(BF16)
    out = h
    for hd in range(XA_HEADS):
        sl = slice(hd * dh, (hd + 1) * dh)
        s = _dot_nt(q[:, sl], k_ref[:, sl])
        p = jnp.exp2(s - jnp.max(s, axis=-1, keepdims=True))
        inv = 1.0 / jnp.sum(p, axis=-1, keepdims=True)
        o = _dot(p.astype(BF16), v_ref[:, sl]) * inv
        out = out + _dot(o.astype(BF16), wo_ref[sl, :].astype(BF16))
    o_ref[...] = out


def _cross_attention(h, g, wq, k, v, wo, batch, seq, *, tm=1024):
    m, d = h.shape
    mem_len = k.shape[0] // batch
    per = seq // tm
    return pl.pallas_call(
        _xattn_kernel,
        grid=(m // tm,),
        in_specs=[pl.BlockSpec((tm, d), lambda i: (i, 0)), _resident((1, d)), _resident(wq),
                  pl.BlockSpec((mem_len, d), lambda i: (i // per, 0)),
                  pl.BlockSpec((mem_len, d), lambda i: (i // per, 0)),
                  _resident(wo)],
        out_specs=pl.BlockSpec((tm, d), lambda i: (i, 0)),
        out_shape=jax.ShapeDtypeStruct((m, d), F32),
        compiler_params=_params("parallel"),
        name="cross_attention",
    )(h, g, _operand(wq), k, v, _operand(wo))


def _pad_cols(w, left, total):
    return jnp.pad(w, ((0, 0),) * (w.ndim - 1) + ((left, total - left - w.shape[-1]),))


def _w_in_plan():
    names = ("rq", "rk", "rv", "rg", "sz", "sxbc", "sdt", "cq", "ckv", "kr", "dq", "dk", "dv", "gl")
    src, start = {}, 0
    for n, s in zip(names, IN_SIZES):
        src[n] = (start, s)
        start += s
    half = MLA_ROPE // 2
    kr0 = src["kr"][0]
    moves, zeros, dst = [], [], 0
    rh = RET_DK // 2
    for n in ("rq", "rk"):
        for tile in range(RET_QK // LANES):
            for part in range(2):
                for hh in range(2):
                    moves.append((src[n][0] + (2 * tile + hh) * RET_DK + part * rh, rh, dst))
                    dst += rh
    for n in ("rv", "rg", "sxbc", "sz"):
        moves.append((*src[n], dst))
        dst += src[n][1]
    zeros.append((dst, SSD_DT_W))
    moves.append((*src["sdt"], dst))
    dst += SSD_DT_W
    for n in ("dq", "dk", "dv", "cq", "ckv"):
        moves.append((*src[n], dst))
        dst += src[n][1]
    zeros.append((dst, 2 * MLA_PAD))
    moves.append((kr0, MLA_ROPE, dst + MLA_NOPE))
    moves.append((kr0 + half, half, dst + MLA_PAD + MLA_NOPE))
    moves.append((kr0, half, dst + MLA_PAD + MLA_NOPE + half))
    dst += 2 * MLA_PAD
    return moves, zeros, src["gl"], dst


def _w_in_relayout_kernel(x_ref, mix_ref, gate_ref, *, moves, zeros, gate):
    for d0, width in zeros:
        mix_ref[:, d0:d0 + width] = jnp.zeros((mix_ref.shape[0], width), mix_ref.dtype)
    for s0, width, d0 in moves:
        mix_ref[:, d0:d0 + width] = x_ref[:, s0:s0 + width].astype(mix_ref.dtype)
    gate_ref[...] = x_ref[:, gate[0]:gate[0] + gate[1]].astype(gate_ref.dtype)


def _w_in_relayout(w_in, *, row_blocks=4):
    depth, r, c = w_in.shape
    moves, zeros, gate, mix_w = _w_in_plan()
    tr = r // row_blocks
    blk = lambda width: pl.BlockSpec((None, tr, width), lambda l, i: (l, i, 0))
    return pl.pallas_call(
        functools.partial(_w_in_relayout_kernel, moves=moves, zeros=zeros, gate=gate),
        grid=(depth, row_blocks),
        in_specs=[blk(c)],
        out_specs=[blk(mix_w), blk(gate[1])],
        out_shape=[jax.ShapeDtypeStruct((depth, r, mix_w), BF16), jax.ShapeDtypeStruct((depth, r, gate[1]), BF16)],
        compiler_params=_params("parallel", "parallel"),
        name="w_in_relayout",
    )(w_in)


def _mla_up_weights(w_uq, w_ukv):
    depth, d, _ = w_uq.shape
    half = MLA_ROPE // 2
    uq = w_uq.astype(BF16).reshape(depth, d, MLA_HEADS, MLA_QH)
    pe = uq[..., MLA_NOPE:]
    partner = jnp.concatenate([jnp.zeros_like(uq[..., :MLA_NOPE]), pe[..., half:], pe[..., :half]], axis=-1)
    pad = lambda a: _pad_cols(a, 0, MLA_PAD).reshape(depth, d, MLA_HEADS * MLA_PAD)
    uq_p = jnp.concatenate([pad(uq), pad(partner)], axis=-1)
    r = w_ukv.shape[1]
    ukv = w_ukv.astype(BF16).reshape(depth, r, MLA_HEADS, MLA_NOPE + MLA_V)
    k_nope = _pad_cols(ukv[..., :MLA_NOPE], 0, MLA_PAD)
    ukv_p = jnp.concatenate([k_nope.reshape(depth, r, -1), ukv[..., MLA_NOPE:].reshape(depth, r, -1)], axis=-1)
    return uq_p, ukv_p


def _t5_bucket_const():
    f = np.float32
    n = np.arange(REL_TABLE)
    exact = REL_BUCKETS // 2
    nf = np.maximum(n, 1).astype(f)
    large = exact + (np.log(nf / f(exact)) / f(math.log(REL_MAX_DIST / exact)) * f(REL_BUCKETS - exact)).astype(np.int32)
    return np.where(n < exact, n, np.minimum(large, REL_BUCKETS - 1))


MIX_OUTS = ((3 * BRANCH_W, F32, 1.0, False),
            (SSM_XBC + SSM_INNER, F32, 1.0, False),
            (SSD_DT_W, F32, 1.0, True),
            (BRANCH_W, BF16, DIFF_DH ** -0.5 * LOG2E, True),
            (BRANCH_W, BF16, 1.0, False),
            (BRANCH_W, BF16, 1.0, True))


def kernel(x, mem, positions, ffn1_norm, ffn1_w_gate, ffn1_w_up, ffn1_w_down, mix_norm, w_in, ssm_conv_w,
           ssm_conv_b, ssm_dt_bias, ssm_a_log, ssm_d, ssm_norm, mla_q_norm, mla_kv_norm, mla_w_uq, mla_w_ukv,
           diff_lambda, diff_norm, rel_bias, w_branch, w_out, xa_norm, mem_norm, xa_wq, xa_wk, xa_wv, xa_wo,
           ffn2_norm, ffn2_w_gate, ffn2_w_up, ffn2_w_down, final_norm):
    batch, seq, d = x.shape
    depth = w_in.shape[0]
    m = batch * seq
    h = x.reshape(m, d)
    mem2 = mem.reshape(batch * mem.shape[1], d)
    row = lambda a: a.reshape(1, -1)

    pos_col = positions.reshape(m, 1)
    ret_cos, ret_sin, mla_cos, mla_sin = _rope_tables(
        pos_col, [(LANES, LANES // 2, 0, RET_DK // 2), (MLA_PAD, MLA_ROPE // 2, MLA_NOPE, MLA_ROPE // 2)])
    ftab = rel_bias[_t5_bucket_const()].T.reshape(DIFF_HEADS, 1, REL_TABLE)

    w_mix, w_gate = _w_in_relayout(w_in)
    w_uq_p, w_ukv_p = _mla_up_weights(mla_w_uq, mla_w_ukv)
    ffn1 = (ffn1_w_gate, ffn1_w_up, ffn1_w_down)
    ffn2 = (ffn2_w_gate, ffn2_w_up, ffn2_w_down)
    w_branch_b = _cast_bf16(w_branch.reshape(depth, N_BRANCH * BRANCH_W, d)).reshape(w_branch.shape)
    w_out_b = _cast_bf16(w_out)
    w_kv = jnp.concatenate([_cast_bf16(xa_wk), _cast_bf16(xa_wv)], axis=-1)

    for l in range(depth):
        at = lambda stack: _LayerOf(stack, l)
        h = _ffn(h, row(ffn1_norm[l]), *(at(w) for w in ffn1))
        mla = (mla_cos, mla_sin, row(mla_q_norm[l]), row(mla_kv_norm[l]), at(w_uq_p), at(w_ukv_p))
        ret_p, ssm_p, dt_t, dqt, dk, dvt, mqt, mk, mvt = _norm_proj(h, row(mix_norm[l]), at(w_mix), MIX_OUTS, seq,
                                                                   mla=mla, name="mix_proj")
        y_ret = _retention(ret_p.reshape(batch, seq, -1), ret_cos.reshape(batch, seq, -1),
                           ret_sin.reshape(batch, seq, -1)).reshape(m, BRANCH_W)
        y_ssm = _ssd(ssm_p.reshape(batch, seq, -1), dt_t, ssm_conv_w[l], row(ssm_conv_b[l]),
                     ssm_dt_bias[l].reshape(-1, 1), ssm_a_log[l].reshape(-1, 1),
                     row(jnp.repeat(ssm_d[l], SSM_HEADDIM)), row(ssm_norm[l])).reshape(m, SSM_INNER)
        y_mla = _mla_attention(mqt, mk.reshape(batch, seq, -1), mvt).reshape(m, BRANCH_W)
        lambda_init = 0.8 - 0.6 * math.exp(-0.3 * l)
        y_diff = _diff_attention(dqt, dk.reshape(batch, seq, -1), dvt, positions, ftab, diff_lambda[l],
                                 row(diff_norm[l]), lambda_init).reshape(m, BRANCH_W)
        h = _merge(h, row(mix_norm[l]), (y_ret, y_ssm, y_mla, y_diff), at(w_gate), at(w_branch_b), at(w_out_b))
        xk, xv = _norm_proj(mem2, row(mem_norm[l]), at(w_kv), [(d, BF16, 1.0, False)] * 2, mem.shape[1],
                            tm=mem.shape[1], name="mem_kv")
        h = _cross_attention(h, row(xa_norm[l]), at(xa_wq), xk, xv, at(xa_wo), batch, seq)
        h = _ffn(h, row(ffn2_norm[l]), *(at(w) for w in ffn2),
                 final_g=row(final_norm) if l == depth - 1 else None)
    return h.reshape(batch, seq, d)
```

```python
import functools
import math
from typing import NamedTuple

import jax
import jax.numpy as jnp
import numpy as np
from jax import lax
from jax.experimental import pallas as pl
from jax.experimental.pallas import tpu as pltpu

F32 = jnp.float32
BF16 = jnp.bfloat16

EPS = 1e-6
NEG_INF = -1e30
LOG2E = math.log2(math.e)
ROPE_BASE = 10000.0
CHUNK = 128
BRANCH_W = 512
N_BRANCH = 4

RET_HEADS = 4
RET_DK = 64
RET_DV = 128
RET_QK = RET_HEADS * RET_DK

SSM_HEADDIM = 64
SSM_INNER = 512
SSM_HEADS = 8
SSM_GROUPS = 2
SSM_STATE = 128
SSM_CONV = 4
SSM_XBC = SSM_INNER + 2 * SSM_GROUPS * SSM_STATE
SSM_TAIL = 8
SSD_DT_W = 128

MLA_HEADS = 4
MLA_Q_RANK = 256
MLA_KV_RANK = 128
MLA_NOPE = 64
MLA_ROPE = 32
MLA_V = 128
MLA_QH = MLA_NOPE + MLA_ROPE
MLA_PAD = 128
MLA_IN_W = MLA_Q_RANK + MLA_KV_RANK + 2 * MLA_PAD
MLA_OUTS = ((MLA_HEADS * MLA_PAD, jnp.bfloat16, 1.0, True), (MLA_HEADS * MLA_PAD, jnp.bfloat16, 1.0, False),
            (MLA_HEADS * MLA_V, jnp.bfloat16, 1.0, True))
HEAD_W = 128
ATTN_TQ = 512
ATTN_TK = 256

DIFF_HEADS = 4
DIFF_DH = 64

REL_BUCKETS = 32
REL_MAX_DIST = 128
REL_TABLE = 128

XA_HEADS = 4

LANES = 128
VMEM_LIMIT = 56 * 1024 * 1024

IN_SIZES = (RET_QK, RET_QK, BRANCH_W, BRANCH_W,
            SSM_INNER, SSM_XBC, SSM_HEADS,
            MLA_Q_RANK, MLA_KV_RANK, MLA_ROPE,
            BRANCH_W, BRANCH_W, BRANCH_W,
            N_BRANCH * 1024)


def _params(*sem):
    return pltpu.CompilerParams(dimension_semantics=sem, vmem_limit_bytes=VMEM_LIMIT)


class _LayerOf(NamedTuple):
    stack: jax.Array
    layer: int

    @property
    def shape(self):
        return self.stack.shape[1:]


def _operand(x):
    return x.stack if isinstance(x, _LayerOf) else x


def _resident(x):
    if isinstance(x, _LayerOf):
        shape, lead, first = x.shape, (None,), (x.layer,)
    else:
        shape, lead, first = (x if isinstance(x, tuple) else x.shape), (), ()
    nd = len(shape)
    return pl.BlockSpec(lead + tuple(shape), lambda *_: first + (0,) * nd, pipeline_mode=pl.Buffered(1))


def _cast_kernel(x_ref, o_ref):
    o_ref[...] = x_ref[...].astype(o_ref.dtype)


def _cast_bf16(w, *, row_blocks=4):
    depth, r, c = w.shape
    tr = r // row_blocks
    spec = pl.BlockSpec((None, tr, c), lambda l, i: (l, i, 0))
    return pl.pallas_call(
        _cast_kernel,
        grid=(depth, row_blocks),
        in_specs=[spec],
        out_specs=spec,
        out_shape=jax.ShapeDtypeStruct(w.shape, BF16),
        compiler_params=_params("parallel", "parallel"),
        name="cast_bf16",
    )(w)


def _rms(x, g):
    return x * lax.rsqrt(jnp.mean(x * x, axis=-1, keepdims=True) + EPS) * g


def _silu(x):
    half = 0.5 * x
    return half + half * jnp.tanh(half)


def _dot(a, b):
    return jnp.dot(a, b, preferred_element_type=F32)


def _dot_nt(a, b):
    return lax.dot_general(a, b, (((1,), (1,)), ((), ())), preferred_element_type=F32)


def _dot_tn(a, b):
    return lax.dot_general(a, b, (((0,), (0,)), ((), ())), preferred_element_type=F32)


def _ffn_kernel(h_ref, g_ref, wg_ref, wu_ref, wd_ref, *rest, ff_chunk, final):
    if final:
        gf_ref, o_ref = rest
    else:
        (o_ref,) = rest
    h = h_ref[...]
    xn = _rms(h, g_ref[...]).astype(BF16)
    acc = jnp.zeros(h.shape, F32)
    d_ff = wg_ref.shape[1]
    for c0 in range(0, d_ff, ff_chunk):
        a = _dot(xn, wg_ref[:, c0:c0 + ff_chunk].astype(BF16))
        b = _dot(xn, wu_ref[:, c0:c0 + ff_chunk].astype(BF16))
        acc = acc + _dot((_silu(a) * b).astype(BF16), wd_ref[c0:c0 + ff_chunk, :].astype(BF16))
    out = h + 0.5 * acc
    if final:
        out = _rms(out, gf_ref[...])
    o_ref[...] = out


def _ffn(h, g, wg, wu, wd, final_g=None, *, tm=512, ff_chunk=256):
    m, d = h.shape
    d_ff = wg.shape[1]
    final = final_g is not None
    in_specs = [pl.BlockSpec((tm, d), lambda i: (i, 0)), _resident((1, d)),
                _resident(wg), _resident(wu), _resident(wd)]
    args = [h, g, _operand(wg), _operand(wu), _operand(wd)]
    if final:
        in_specs.append(_resident((1, d)))
        args.append(final_g)
    return pl.pallas_call(
        functools.partial(_ffn_kernel, ff_chunk=ff_chunk, final=final),
        grid=(m // tm,),
        in_specs=in_specs,
        out_specs=pl.BlockSpec((tm, d), lambda i: (i, 0)),
        out_shape=jax.ShapeDtypeStruct((m, d), F32),
        compiler_params=_params("parallel"),
        name="ffn",
    )(*args)


def _mla_qkv(p, cos, sin, q_norm, kv_norm, w_uq, w_ukv):
    kvw = MLA_HEADS * MLA_PAD
    cq = p[:, :MLA_Q_RANK]
    ckv = p[:, MLA_Q_RANK:MLA_Q_RANK + MLA_KV_RANK]
    kr0 = MLA_Q_RANK + MLA_KV_RANK
    kr = p[:, kr0:kr0 + MLA_PAD]
    kr_partner = p[:, kr0 + MLA_PAD:]
    qq = _dot(_rms(cq, q_norm).astype(BF16), w_uq)
    cos4 = jnp.concatenate([cos] * MLA_HEADS, axis=-1)
    sin4 = jnp.concatenate([sin] * MLA_HEADS, axis=-1)
    q = (qq[:, :kvw] * cos4 + qq[:, kvw:] * sin4) * (MLA_QH ** -0.5 * LOG2E)
    kv = _dot(_rms(ckv, kv_norm).astype(BF16), w_ukv)
    kpe = kr * cos + kr_partner * sin
    k = kv[:, :kvw] + jnp.concatenate([kpe] * MLA_HEADS, axis=-1)
    return q.astype(BF16).T, k.astype(BF16), kv[:, kvw:].astype(BF16).T


def _norm_proj_kernel(x_ref, g_ref, w_ref, *rest, outs, col_chunk, mla):
    if mla:
        cos_ref, sin_ref, qn_ref, kvn_ref, wuq_ref, wukv_ref = rest[:6]
        o_refs = rest[6:]
    else:
        o_refs = rest
    xn = _rms(x_ref[...], g_ref[...]).astype(BF16)
    off = 0
    for o_ref, (width, _, scale, transposed) in zip(o_refs, outs):
        for c0 in range(0, width, col_chunk):
            c1 = min(c0 + col_chunk, width)
            y = _dot(xn, w_ref[:, off + c0:off + c1])
            if scale != 1.0:
                y = y * scale
            if transposed:
                o_ref[c0:c1, :] = y.astype(o_ref.dtype).T
            else:
                o_ref[:, c0:c1] = y.astype(o_ref.dtype)
        off += width
    if mla:
        q_ref, k_ref, v_ref = o_refs[len(outs):]
        p = _dot(xn, w_ref[:, off:])
        q_ref[...], k_ref[...], v_ref[...] = _mla_qkv(p, cos_ref[...], sin_ref[...], qn_ref[...], kvn_ref[...],
                                                      wuq_ref[...], wukv_ref[...])


def _norm_proj(x, g, w, outs, seq, *, mla=None, tm=512, col_chunk=512, name="norm_proj"):
    m, d = x.shape
    assert sum(o[0] for o in outs) + (MLA_IN_W if mla else 0) == w.shape[1]
    per = seq // tm
    rows = lambda width: pl.BlockSpec((tm, width), lambda i: (i, 0))
    cols = lambda width: pl.BlockSpec((None, width, tm), lambda i: (i // per, 0, i % per))
    out_specs, out_shape = [], []
    for width, dt, _, transposed in tuple(outs) + (MLA_OUTS if mla else ()):
        if transposed:
            out_specs.append(cols(width))
            out_shape.append(jax.ShapeDtypeStruct((m // seq, width, seq), dt))
        else:
            out_specs.append(rows(width))
            out_shape.append(jax.ShapeDtypeStruct((m, width), dt))
    in_specs = [rows(d), _resident((1, d)), _resident(w)]
    args = [x, g, _operand(w)]
    if mla:
        cos, sin, q_norm, kv_norm, w_uq, w_ukv = mla
        in_specs += [rows(LANES), rows(LANES), _resident(q_norm), _resident(kv_norm), _resident(w_uq),
                     _resident(w_ukv)]
        args += [cos, sin, q_norm, kv_norm, _operand(w_uq), _operand(w_ukv)]
    return pl.pallas_call(
        functools.partial(_norm_proj_kernel, outs=tuple(outs), col_chunk=col_chunk, mla=mla is not None),
        grid=(m // tm,),
        in_specs=in_specs,
        out_specs=out_specs,
        out_shape=out_shape,
        compiler_params=_params("parallel"),
        name=name,
    )(*args)


def _rope_table_kernel(pos_ref, inv_ref, idx_ref, sign_ref, *o_refs):
    ang = pos_ref[...].astype(F32) * inv_ref[...]
    cos, sin = jnp.cos(ang), jnp.sin(ang)
    for t in range(len(o_refs) // 2):
        idx = jnp.broadcast_to(idx_ref[t:t + 1, :], ang.shape)
        o_refs[2 * t][...] = jnp.take_along_axis(cos, idx, axis=1, mode="promise_in_bounds")
        o_refs[2 * t + 1][...] = (jnp.take_along_axis(sin, idx, axis=1, mode="promise_in_bounds")
                                  * sign_ref[t:t + 1, :])


def _rope_tables(pos_col, layouts, *, tm=1024):
    m = pos_col.shape[0]
    lane = np.arange(LANES)
    inv_parts, idx_rows, sign_rows, base = [], [], [], 0
    for period, half, start, n_freq in layouts:
        inv_parts.append(jnp.exp(-math.log(ROPE_BASE) * jnp.arange(n_freq, dtype=F32) / n_freq))
        rel = lane % period
        inside = (rel >= start) & (rel < start + 2 * half)
        idx_rows.append(np.where(inside, base + (rel - start) % half % n_freq, LANES - 1))
        sign_rows.append(np.where(inside, np.where(rel < start + half, -1.0, 1.0), 0.0))
        base += n_freq
    assert base < LANES
    inv = jnp.concatenate(inv_parts + [jnp.zeros((LANES - base,), F32)]).reshape(1, LANES)
    idx = np.stack(idx_rows).astype(np.int32)
    sign = np.stack(sign_rows).astype(np.float32)
    n_out = 2 * len(layouts)
    return pl.pallas_call(
        _rope_table_kernel,
        grid=(m // tm,),
        in_specs=[pl.BlockSpec((tm, 1), lambda i: (i, 0)), _resident(inv), _resident(idx), _resident(sign)],
        out_specs=[pl.BlockSpec((tm, LANES), lambda i: (i, 0))] * n_out,
        out_shape=[jax.ShapeDtypeStruct((m, LANES), F32)] * n_out,
        compiler_params=_params("parallel"),
        name="rope_tables",
    )(pos_col, inv, idx, sign)


def _ret_head_of_lane(width):
    lane = np.arange(width)
    return 2 * (lane // LANES) + (lane // (RET_DK // 2)) % 2


def _apply_rope_ret(x, cos, sin_signed):
    tiles = [x[:, t:t + LANES] for t in range(0, x.shape[-1], LANES)]
    return jnp.concatenate([xt * cos + pltpu.roll(xt, LANES // 2, 1) * sin_signed for xt in tiles], axis=-1)


def _ret_kernel(q_ref, k_ref, v_ref, g_ref, cos_ref, sin_ref, intra_ref, qdec_ref, kdec_ref,
                sdec_ref, bmask_ref, o_ref, state_sc):
    @pl.when(pl.program_id(0) == 0)
    def _():
        state_sc[...] = jnp.zeros_like(state_sc)

    lane = lax.broadcasted_iota(jnp.int32, (1, RET_QK), 1)
    head = 2 * (lane // LANES) + (lane // (RET_DK // 2)) % 2
    for b in range(q_ref.shape[0]):
        cos, sin = cos_ref[b], sin_ref[b]
        q = _apply_rope_ret(q_ref[b], cos, sin) * (RET_DK ** -0.5)
        k = _apply_rope_ret(k_ref[b], cos, sin)
        vb = v_ref[b].astype(BF16)
        kb = k.astype(BF16)
        state = state_sc[b]
        cross = _dot((q * qdec_ref[...]).astype(BF16), state.astype(BF16))
        outs = []
        for h in range(RET_HEADS):
            qh = jnp.where(head == h, q, 0.0).astype(BF16)
            att = _dot_nt(qh, kb) * intra_ref[h]
            sl = slice(h * RET_DV, (h + 1) * RET_DV)
            o = _dot(att.astype(BF16), vb[:, sl]) + cross[:, sl]
            mu = jnp.mean(o, axis=-1, keepdims=True)
            oc = o - mu
            outs.append(oc * lax.rsqrt(jnp.mean(oc * oc, axis=-1, keepdims=True) + EPS))
        o_ref[b] = (jnp.concatenate(outs, axis=-1) * _silu(g_ref[b])).astype(o_ref.dtype)
        kv = _dot_tn((k * kdec_ref[...]).astype(BF16), vb)
        state_sc[b] = state * sdec_ref[...] + kv * bmask_ref[...]


def _retention_consts():
    c = CHUNK
    f = np.float32
    log_g = np.log1p(-np.exp2(f(-5.0) - np.arange(RET_HEADS, dtype=f)))
    i = np.arange(c, dtype=f)
    rel = i[:, None] - i[None, :]
    intra = np.where(rel >= 0, np.exp(log_g[:, None, None] * np.maximum(rel, f(0.0))), f(0.0)).astype(f)
    row_h = _ret_head_of_lane(RET_QK)
    lg_lane = log_g[row_h]
    qdec = np.exp(lg_lane[None, :] * (i[:, None] + f(1.0)))
    kdec = np.exp(lg_lane[None, :] * (f(c - 1.0) - i[:, None]))
    col_h = np.repeat(np.arange(RET_HEADS), RET_DV)
    bmask = (row_h[:, None] == col_h[None, :]).astype(f)
    sdec = bmask * np.exp(lg_lane * f(c))[:, None]
    return intra, qdec.astype(f), kdec.astype(f), sdec.astype(f), bmask


def _retention(ret_p, cos, sin):
    c = CHUNK
    batch, seq, _ = ret_p.shape
    intra, qdec, kdec, sdec, bmask = _retention_consts()
    blk = lambda width, col: pl.BlockSpec((batch, c, width), lambda j: (0, j, col))
    return pl.pallas_call(
        _ret_kernel,
        grid=(seq // c,),
        in_specs=[blk(RET_QK, 0), blk(RET_QK, 1), blk(BRANCH_W, 1), blk(BRANCH_W, 2), blk(LANES, 0), blk(LANES, 0),
                  _resident(intra.shape), _resident(qdec.shape), _resident(kdec.shape),
                  _resident(sdec.shape), _resident(bmask.shape)],
        out_specs=blk(BRANCH_W, 0),
        out_shape=jax.ShapeDtypeStruct((batch, seq, BRANCH_W), BF16),
        scratch_shapes=[pltpu.VMEM((batch, RET_QK, BRANCH_W), F32)],
        compiler_params=_params("arbitrary"),
        name="retention",
    )(ret_p, ret_p, ret_p, ret_p, cos, sin, intra, qdec, kdec, sdec, bmask)


def _split3(x):
    hi = x.astype(BF16)
    r = x - hi.astype(F32)
    mid = r.astype(BF16)
    lo = (r - mid.astype(F32)).astype(BF16)
    return hi, mid, lo


def _dot_split3(a, b):
    return sum(_dot(piece, b) for piece in _split3(a))


def _ssd_kernel(xbc_ref, *refs):
    xe_sc, state_sc = refs[-2:]

    @pl.when(pl.program_id(0) == 0)
    def _():
        xe_sc[...] = jnp.zeros_like(xe_sc)
        state_sc[...] = jnp.zeros_like(state_sc)

    for b in range(xbc_ref.shape[0]):
        _ssd_chunk(b, xbc_ref, *refs)


def _ssd_chunk(b, xbc_ref, z_ref, dt_ref, cw_ref, cb_ref, dtb_ref, alog_ref, dskip_ref, ng_ref, tri_ref,
               expand_ref, o_ref, xe_sc, state_sc):
    L = CHUNK
    T = SSM_TAIL
    GW = SSM_INNER // SSM_GROUPS
    x_raw = xbc_ref[b]
    tail = xe_sc[b]
    rows8 = lax.broadcasted_iota(jnp.int32, (T, 1), 0)
    conv = cb_ref[...] + cw_ref[SSM_CONV - 1:SSM_CONV, :] * x_raw
    for s in range(1, SSM_CONV):
        shifted = pltpu.roll(x_raw, s, 0)
        head = jnp.where(rows8 < s, pltpu.roll(tail, s, 0), shifted[0:T, :])
        shifted = jnp.concatenate([head, shifted[T:, :]], axis=0)
        conv = conv + cw_ref[SSM_CONV - 1 - s:SSM_CONV - s, :] * shifted
    xe_sc[b] = x_raw[L - T:L, :]
    xa = _silu(conv)
    xs = xa[:, :SSM_INNER]
    bm = xa[:, SSM_INNER:SSM_INNER + SSM_GROUPS * SSM_STATE]
    cm = xa[:, SSM_INNER + SSM_GROUPS * SSM_STATE:]

    dtr = dt_ref[b] + dtb_ref[...]
    dt_h = jnp.maximum(dtr, 0.0) + jnp.log1p(jnp.exp(-jnp.abs(dtr)))
    da = dt_h * (-jnp.exp(alog_ref[...]))
    cs_t = _dot_split3(da, tri_ref[...])
    cs_t_last = cs_t[:, L - 1:L]
    per_head = jnp.concatenate([dt_h, jnp.exp(cs_t), jnp.exp(cs_t_last - cs_t), cs_t], axis=0)
    per_chan = _dot_tn(jnp.concatenate(_split3(per_head), axis=0), expand_ref[...])
    dt, ecs, eds, cs_c = (per_chan[:, i * SSM_INNER:(i + 1) * SSM_INNER] for i in range(4))
    xdt = xs * dt
    xdec = (xdt * eds).astype(BF16)
    state = state_sc[b]
    li = lax.broadcasted_iota(jnp.int32, (L, L), 0)
    si = lax.broadcasted_iota(jnp.int32, (L, L), 1)
    causal = li >= si
    lane_half = lax.broadcasted_iota(jnp.int32, (1, LANES), 1) // SSM_HEADDIM
    ys = []
    new_states = []
    for g in range(SSM_GROUPS):
        bg = bm[:, g * SSM_STATE:(g + 1) * SSM_STATE].astype(BF16)
        cg = cm[:, g * SSM_STATE:(g + 1) * SSM_STATE].astype(BF16)
        cb = _dot_nt(cg, bg)
        gsl = slice(g * GW, (g + 1) * GW)
        y_off = _dot(cg, state[:, gsl].astype(BF16)) * ecs[:, gsl]
        for p in range(GW // LANES):
            sl = slice(g * GW + p * LANES, g * GW + (p + 1) * LANES)
            xdt_blk = xdt[:, sl]
            acc = y_off[:, p * LANES:(p + 1) * LANES]
            for hh in range(LANES // SSM_HEADDIM):
                head = (g * GW + p * LANES) // SSM_HEADDIM + hh
                col = cs_c[:, head * SSM_HEADDIM:head * SSM_HEADDIM + 1]
                rowv = cs_t[head:head + 1, :]
                dec = jnp.where(causal, jnp.exp(jnp.minimum(col - rowv, 0.0)), 0.0)
                xm = jnp.where(lane_half == hh, xdt_blk, 0.0).astype(BF16)
                acc = acc + _dot((cb * dec).astype(BF16), xm)
            ys.append(acc)
        new_states.append(_dot_tn(bg, xdec[:, gsl]))
    y = jnp.concatenate(ys, axis=-1) + xs * dskip_ref[...]
    y = y * _silu(z_ref[b])
    o_ref[b] = _rms(y, ng_ref[...]).astype(o_ref.dtype)
    state_sc[b] = state * ecs[L - 1:L, :] + jnp.concatenate(new_states, axis=-1)


def _ssd(ssm_p, dt_t, conv_w, conv_b, dt_bias_col, a_log_col, d_full, norm_g):
    L = CHUNK
    batch, seq, _ = ssm_p.shape
    tri = (np.arange(L)[:, None] <= np.arange(L)[None, :]).astype(BF16)
    r = np.arange(3 * 4 * SSM_HEADS)
    c = np.arange(4 * SSM_INNER)
    expand = (((r[:, None] // SSM_HEADS) % 4 == c[None, :] // SSM_INNER)
              & (r[:, None] % SSM_HEADS == (c[None, :] % SSM_INNER) // SSM_HEADDIM)).astype(BF16)
    return pl.pallas_call(
        _ssd_kernel,
        grid=(seq // L,),
        in_specs=[pl.BlockSpec((batch, L, SSM_XBC), lambda j: (0, j, 0)),
                  pl.BlockSpec((batch, L, SSM_INNER), lambda j: (0, j, SSM_XBC // SSM_INNER)),
                  pl.BlockSpec((batch, SSM_HEADS, L), lambda j: (0, 0, j)),
                  _resident(conv_w.shape), _resident(conv_b.shape), _resident(dt_bias_col.shape),
                  _resident(a_log_col.shape), _resident(d_full.shape), _resident(norm_g.shape),
                  _resident(tri.shape), _resident(expand.shape)],
        out_specs=pl.BlockSpec((batch, L, SSM_INNER), lambda j: (0, j, 0)),
        out_shape=jax.ShapeDtypeStruct((batch, seq, SSM_INNER), BF16),
        scratch_shapes=[pltpu.VMEM((batch, SSM_TAIL, SSM_XBC), F32),
                        pltpu.VMEM((batch, SSM_STATE, SSM_INNER), F32)],
        compiler_params=_params("arbitrary"),
        name="ssd",
    )(ssm_p, ssm_p, dt_t, conv_w, conv_b, dt_bias_col, a_log_col, d_full, norm_g, tri, expand)


def _flash_kernel(*refs, tq, tk, n_maps, lambda_init):
    biased = n_maps == 2
    if biased:
        (pmin_ref, pmax_ref, qt_ref, k_ref, vt_ref, pq_ref, pk_ref, ftab_ref, lam_ref, sg_ref,
         o_ref, m_sc, l_sc, acc_sc, s0_sc, s1_sc) = refs
    else:
        qt_ref, k_ref, vt_ref, o_ref, m_sc, l_sc, acc_sc, s0_sc, s1_sc = refs
    bi = pl.program_id(0)
    qi = pl.program_id(1)
    hw = HEAD_W
    heads = k_ref.shape[-1] // hw
    cols = n_maps * tq
    hs = lambda h: slice(h * hw, (h + 1) * hw)
    ws = []
    for h in range(heads):
        qt = qt_ref[hs(h), :]
        if biased:
            half = lax.broadcasted_iota(jnp.int32, (hw, 1), 0) // DIFF_DH
            zero = jnp.zeros_like(qt)
            qt = jnp.concatenate([jnp.where(half == 0, qt, zero), jnp.where(half == 1, qt, zero)], axis=1)
        ws.append(qt)
    if biased:
        pq = pq_ref[...]
        ftabs = [jnp.broadcast_to(ftab_ref[h] * LOG2E, (tk, REL_TABLE)) for h in range(heads)]
    m_sc[...] = jnp.full(m_sc.shape, NEG_INF, F32)
    l_sc[...] = jnp.zeros_like(l_sc)
    acc_sc[...] = jnp.zeros_like(acc_sc)

    def near_biases(off, q_lo=0):
        idx = jnp.clip(pq[:, q_lo:] - pk_ref[pl.ds(off, tk), :], 0, REL_TABLE - 1)
        tiles = [jnp.concatenate([jnp.take_along_axis(ftabs[h], idx[:, c:c + LANES], axis=1,
                                                      mode="promise_in_bounds")
                                  for c in range(0, tq - q_lo, LANES)], axis=1) for h in range(heads)]

        def bias(h, cs):
            base = (cs.start // tq) * tq + q_lo
            return tiles[h][:, cs.start - base:cs.stop - base]
        return bias

    s_scs = (s0_sc, s1_sc)

    all_cols = tuple(slice(mi * tq, (mi + 1) * tq) for mi in range(n_maps))
    late_cols = tuple(slice(mi * tq + tk, (mi + 1) * tq) for mi in range(n_maps))

    def scores(off, slot, col_ranges=all_cols):
        for h in range(heads):
            for cs in col_ranges:
                s_scs[slot][h, :, cs] = _dot(k_ref[pl.ds(off, tk), hs(h)], ws[h][:, cs])

    def consume(off, slot, bias, mask, const=None, col_ranges=all_cols):
        for h in range(heads):
            for cs in col_ranges:
                s = s_scs[slot][h, :, cs]
                if bias is not None:
                    s = s + bias(h, cs)
                if mask is not None:
                    s = jnp.where(mask[:, cs], s, NEG_INF)
                m = m_sc[h, :, cs]
                m_tile = jnp.max(s, axis=0, keepdims=True)
                if const is not None:
                    m_tile = m_tile + const[h]
                m_new = jnp.maximum(m, m_tile)
                alpha = jnp.exp2(m - m_new)
                p = jnp.exp2(s - (m_new if const is None else m_new - const[h]))
                l_sc[h, :, cs] = alpha * l_sc[h, :, cs] + jnp.sum(p, axis=0, keepdims=True)
                acc_sc[h, :, cs] = (alpha * acc_sc[h, :, cs]
                                    + _dot(vt_ref[hs(h), pl.ds(off, tk)], p.astype(BF16)))
                m_sc[h, :, cs] = m_new

    assert tq == 2 * tk
    scores(0, 0)

    def body(jj, carry):
        off0 = pl.multiple_of(jj * tq, tq)
        off1 = pl.multiple_of(off0 + tk, tk)
        off2 = pl.multiple_of(off0 + tq, tq)

        def pair(near0, near1):
            const = [ftab_ref[h][:, REL_TABLE - 1:REL_TABLE] * LOG2E for h in range(heads)] if biased else None
            scores(off1, 1)
            consume(off0, 0, near_biases(off0) if near0 else None, None, None if near0 else const)
            scores(off2, 0)
            consume(off1, 1, near_biases(off1) if near1 else None, None, None if near1 else const)

        if biased:
            far0 = pmin_ref[bi, qi] - pmax_ref[bi, 2 * jj] >= REL_TABLE - 1
            far1 = pmin_ref[bi, qi] - pmax_ref[bi, 2 * jj + 1] >= REL_TABLE - 1

            @pl.when(far0 & far1)
            def _():
                pair(False, False)

            @pl.when(far0 & jnp.logical_not(far1))
            def _():
                pair(False, True)

            @pl.when(jnp.logical_not(far0))
            def _():
                pair(True, True)
        else:
            pair(False, False)
        return carry

    lax.fori_loop(0, qi, body, 0)
    for d in range(2):
        off = pl.multiple_of(qi * tq + d * tk, tk)
        if d == 0:
            scores(pl.multiple_of(off + tk, tk), 1, late_cols)
        key = lax.broadcasted_iota(jnp.int32, (tk, cols), 0) + d * tk
        qry = lax.broadcasted_iota(jnp.int32, (tk, cols), 1) % tq
        consume(off, d, near_biases(off, d * tk) if biased else None, key <= qry,
                col_ranges=all_cols if d == 0 else late_cols)

    if biased:
        lp = lam_ref[...]
        lam = (jnp.exp(jnp.sum(lp[0:1] * lp[1:2], axis=-1, keepdims=True))
               - jnp.exp(jnp.sum(lp[2:3] * lp[3:4], axis=-1, keepdims=True)) + lambda_init)
    for h in range(heads):
        o = acc_sc[h] / l_sc[h]
        if biased:
            o = (o[:, :tq] - lam * o[:, tq:]).T
            o_ref[:, hs(h)] = (_rms(o, sg_ref[...]) * (1.0 - lambda_init)).astype(o_ref.dtype)
        else:
            o_ref[:, hs(h)] = o.astype(o_ref.dtype).T


def _flash_scratch(heads, tk, cols):
    return [pltpu.VMEM((heads, 1, cols), F32), pltpu.VMEM((heads, 1, cols), F32),
            pltpu.VMEM((heads, HEAD_W, cols), F32),
            pltpu.VMEM((heads, tk, cols), F32), pltpu.VMEM((heads, tk, cols), F32)]


def _mla_attention(qt, k, vt, *, tq=ATTN_TQ, tk=ATTN_TK):
    b, s, w = k.shape
    return pl.pallas_call(
        functools.partial(_flash_kernel, tq=tq, tk=tk, n_maps=1, lambda_init=None),
        grid=(b, s // tq),
        in_specs=[pl.BlockSpec((None, w, tq), lambda bi, i: (bi, 0, i)),
                  pl.BlockSpec((None, s, w), lambda bi, i: (bi, 0, 0), pipeline_mode=pl.Buffered(1)),
                  pl.BlockSpec((None, w, s), lambda bi, i: (bi, 0, 0), pipeline_mode=pl.Buffered(1))],
        out_specs=pl.BlockSpec((None, tq, w), lambda bi, i: (bi, i, 0)),
        out_shape=jax.ShapeDtypeStruct((b, s, w), BF16),
        scratch_shapes=_flash_scratch(w // HEAD_W, tk, tq),
        compiler_params=_params("parallel", "arbitrary"),
        name="mla_attention",
    )(qt, k, vt)


def _diff_attention(qt, k, vt, pos, ftab, lam_params, sub_g, lambda_init, *, tq=ATTN_TQ, tk=ATTN_TK):
    b, s, w = k.shape
    pmin = jnp.min(pos.reshape(b, s // tq, tq), axis=-1)
    pmax = jnp.max(pos.reshape(b, s // tk, tk), axis=-1)
    pos_col = pos.reshape(b, s, 1)
    pos_row = pos.reshape(b, 1, s)
    grid_spec = pltpu.PrefetchScalarGridSpec(
        num_scalar_prefetch=2,
        grid=(b, s // tq),
        in_specs=[pl.BlockSpec((None, w, tq), lambda bi, i, *_: (bi, 0, i)),
                  pl.BlockSpec((None, s, w), lambda bi, i, *_: (bi, 0, 0), pipeline_mode=pl.Buffered(1)),
                  pl.BlockSpec((None, w, s), lambda bi, i, *_: (bi, 0, 0), pipeline_mode=pl.Buffered(1)),
                  pl.BlockSpec((None, 1, tq), lambda bi, i, *_: (bi, 0, i)),
                  pl.BlockSpec((None, s, 1), lambda bi, i, *_: (bi, 0, 0), pipeline_mode=pl.Buffered(1)),
                  pl.BlockSpec(ftab.shape, lambda bi, i, *_: (0, 0, 0)),
                  pl.BlockSpec(lam_params.shape, lambda bi, i, *_: (0, 0)),
                  pl.BlockSpec(sub_g.shape, lambda bi, i, *_: (0, 0))],
        out_specs=pl.BlockSpec((None, tq, w), lambda bi, i, *_: (bi, i, 0)),
        scratch_shapes=_flash_scratch(w // HEAD_W, tk, 2 * tq),
    )
    return pl.pallas_call(
        functools.partial(_flash_kernel, tq=tq, tk=tk, n_maps=2, lambda_init=lambda_init),
        grid_spec=grid_spec,
        out_shape=jax.ShapeDtypeStruct((b, s, w), BF16),
        compiler_params=_params("parallel", "arbitrary"),
        name="diff_attention",
    )(pmin, pmax, qt, k, vt, pos_row, pos_col, ftab, lam_params, sub_g)


def _merge_kernel(h_ref, g_ref, y0_ref, y1_ref, y2_ref, y3_ref, wg_ref, wb_ref, wo_ref, o_ref, *, col_chunk):
    h = h_ref[...]
    d = h.shape[-1]
    u = _rms(h, g_ref[...]).astype(BF16)
    ys = (y0_ref[...], y1_ref[...], y2_ref[...], y3_ref[...])
    parts = []
    for c0 in range(0, d, col_chunk):
        acc = None
        for i, y in enumerate(ys):
            gate = jax.nn.sigmoid(_dot(u, wg_ref[:, i * d + c0:i * d + c0 + col_chunk]))
            term = gate * _dot(y, wb_ref[i, :, c0:c0 + col_chunk].astype(BF16))
            acc = term if acc is None else acc + term
        parts.append(acc.astype(BF16))
    merged = jnp.concatenate(parts, axis=-1)
    o_ref[...] = h + _dot(merged, wo_ref[...].astype(BF16))


def _merge(h, g, ys, w_gate, w_branch, w_out, *, tm=1024, col_chunk=256):
    m, d = h.shape
    rows = lambda width: pl.BlockSpec((tm, width), lambda i: (i, 0))
    return pl.pallas_call(
        functools.partial(_merge_kernel, col_chunk=col_chunk),
        grid=(m // tm,),
        in_specs=[rows(d), _resident((1, d))] + [rows(BRANCH_W)] * N_BRANCH
                 + [_resident(w_gate), _resident(w_branch), _resident(w_out)],
        out_specs=rows(d),
        out_shape=jax.ShapeDtypeStruct((m, d), F32),
        compiler_params=_params("parallel"),
        name="merge",
    )(h, g, *ys, _operand(w_gate), _operand(w_branch), _operand(w_out))


def _xattn_kernel(h_ref, g_ref, wq_ref, k_ref, v_ref, wo_ref, o_ref):
    h = h_ref[...]
    d = h.shape[-1]
    dh = d // XA_HEADS
    u = _rms(h, g_ref[...]).astype(BF16)
    q = (_dot(u, wq_ref[...].astype(BF16)) * (dh ** -0.5 * LOG2E)).astype(BF16)
    out = h
    for hd in range(XA_HEADS):
        sl = slice(hd * dh, (hd + 1) * dh)
        s = _dot_nt(q[:, sl], k_ref[:, sl])
        p = jnp.exp2(s - jnp.max(s, axis=-1, keepdims=True))
        inv = 1.0 / jnp.sum(p, axis=-1, keepdims=True)
        o = _dot(p.astype(BF16), v_ref[:, sl]) * inv
        out = out + _dot(o.astype(BF16), wo_ref[sl, :].astype(BF16))
    o_ref[...] = out


def _cross_attention(h, g, wq, k, v, wo, batch, seq, *, tm=1024):
    m, d = h.shape
    mem_len = k.shape[0] // batch
    per = seq // tm
    return pl.pallas_call(
        _xattn_kernel,
        grid=(m // tm,),
        in_specs=[pl.BlockSpec((tm, d), lambda i: (i, 0)), _resident((1, d)), _resident(wq),
                  pl.BlockSpec((mem_len, d), lambda i: (i // per, 0)),
                  pl.BlockSpec((mem_len, d), lambda i: (i // per, 0)),
                  _resident(wo)],
        out_specs=pl.BlockSpec((tm, d), lambda i: (i, 0)),
        out_shape=jax.ShapeDtypeStruct((m, d), F32),
        compiler_params=_params("parallel"),
        name="cross_attention",
    )(h, g, _operand(wq), k, v, _operand(wo))


def _pad_cols(w, left, total):
    return jnp.pad(w, ((0, 0),) * (w.ndim - 1) + ((left, total - left - w.shape[-1]),))


def _w_in_plan():
    names = ("rq", "rk", "rv", "rg", "sz", "sxbc", "sdt", "cq", "ckv", "kr", "dq", "dk", "dv", "gl")
    src, start = {}, 0
    for n, s in zip(names, IN_SIZES):
        src[n] = (start, s)
        start += s
    half = MLA_ROPE // 2
    kr0 = src["kr"][0]
    moves, zeros, dst = [], [], 0
    rh = RET_DK // 2
    for n in ("rq", "rk"):
        for tile in range(RET_QK // LANES):
            for part in range(2):
                for hh in range(2):
                    moves.append((src[n][0] + (2 * tile + hh) * RET_DK + part * rh, rh, dst))
                    dst += rh
    for n in ("rv", "rg", "sxbc", "sz"):
        moves.append((*src[n], dst))
        dst += src[n][1]
    zeros.append((dst, SSD_DT_W))
    moves.append((*src["sdt"], dst))
    dst += SSD_DT_W
    for n in ("dq", "dk", "dv", "cq", "ckv"):
        moves.append((*src[n], dst))
        dst += src[n][1]
    zeros.append((dst, 2 * MLA_PAD))
    moves.append((kr0, MLA_ROPE, dst + MLA_NOPE))
    moves.append((kr0 + half, half, dst + MLA_PAD + MLA_NOPE))
    moves.append((kr0, half, dst + MLA_PAD + MLA_NOPE + half))
    dst += 2 * MLA_PAD
    return moves, zeros, src["gl"], dst


def _w_in_relayout_kernel(x_ref, mix_ref, gate_ref, *, moves, zeros, gate):
    for d0, width in zeros:
        mix_ref[:, d0:d0 + width] = jnp.zeros((mix_ref.shape[0], width), mix_ref.dtype)
    for s0, width, d0 in moves:
        mix_ref[:, d0:d0 + width] = x_ref[:, s0:s0 + width].astype(mix_ref.dtype)
    gate_ref[...] = x_ref[:, gate[0]:gate[0] + gate[1]].astype(gate_ref.dtype)


def _w_in_relayout(w_in, *, row_blocks=4):
    depth, r, c = w_in.shape
    moves, zeros, gate, mix_w = _w_in_plan()
    tr = r // row_blocks
    blk = lambda width: pl.BlockSpec((None, tr, width), lambda l, i: (l, i, 0))
    return pl.pallas_call(
        functools.partial(_w_in_relayout_kernel, moves=moves, zeros=zeros, gate=gate),
        grid=(depth, row_blocks),
        in_specs=[blk(c)],
        out_specs=[blk(mix_w), blk(gate[1])],
        out_shape=[jax.ShapeDtypeStruct((depth, r, mix_w), BF16), jax.ShapeDtypeStruct((depth, r, gate[1]), BF16)],
        compiler_params=_params("parallel", "parallel"),
        name="w_in_relayout",
    )(w_in)


def _mla_up_weights(w_uq, w_ukv):
    depth, d, _ = w_uq.shape
    half = MLA_ROPE // 2
    uq = w_uq.astype(BF16).reshape(depth, d, MLA_HEADS, MLA_QH)
    pe = uq[..., MLA_NOPE:]
    partner = jnp.concatenate([jnp.zeros_like(uq[..., :MLA_NOPE]), pe[..., half:], pe[..., :half]], axis=-1)
    pad = lambda a: _pad_cols(a, 0, MLA_PAD).reshape(depth, d, MLA_HEADS * MLA_PAD)
    uq_p = jnp.concatenate([pad(uq), pad(partner)], axis=-1)
    r = w_ukv.shape[1]
    ukv = w_ukv.astype(BF16).reshape(depth, r, MLA_HEADS, MLA_NOPE + MLA_V)
    k_nope = _pad_cols(ukv[..., :MLA_NOPE], 0, MLA_PAD)
    ukv_p = jnp.concatenate([k_nope.reshape(depth, r, -1), ukv[..., MLA_NOPE:].reshape(depth, r, -1)], axis=-1)
    return uq_p, ukv_p


def _t5_bucket_const():
    f = np.float32
    n = np.arange(REL_TABLE)
    exact = REL_BUCKETS // 2
    nf = np.maximum(n, 1).astype(f)
    large = exact + (np.log(nf / f(exact)) / f(math.log(REL_MAX_DIST / exact)) * f(REL_BUCKETS - exact)).astype(np.int32)
    return np.where(n < exact, n, np.minimum(large, REL_BUCKETS - 1))


MIX_OUTS = ((3 * BRANCH_W, F32, 1.0, False),
            (SSM_XBC + SSM_INNER, F32, 1.0, False),
            (SSD_DT_W, F32, 1.0, True),
            (BRANCH_W, BF16, DIFF_DH ** -0.5 * LOG2E, True),
            (BRANCH_W, BF16, 1.0, False),
            (BRANCH_W, BF16, 1.0, True))


def kernel(x, mem, positions, ffn1_norm, ffn1_w_gate, ffn1_w_up, ffn1_w_down, mix_norm, w_in, ssm_conv_w,
           ssm_conv_b, ssm_dt_bias, ssm_a_log, ssm_d, ssm_norm, mla_q_norm, mla_kv_norm, mla_w_uq, mla_w_ukv,
           diff_lambda, diff_norm, rel_bias, w_branch, w_out, xa_norm, mem_norm, xa_wq, xa_wk, xa_wv, xa_wo,
           ffn2_norm, ffn2_w_gate, ffn2_w_up, ffn2_w_down, final_norm):
    batch, seq, d = x.shape
    depth = w_in.shape[0]
    m = batch * seq
    h = x.reshape(m, d)
    mem2 = mem.reshape(batch * mem.shape[1], d)
    row = lambda a: a.reshape(1, -1)

    pos_col = positions.reshape(m, 1)
    ret_cos, ret_sin, mla_cos, mla_sin = _rope_tables(
        pos_col, [(LANES, LANES // 2, 0, RET_DK // 2), (MLA_PAD, MLA_ROPE // 2, MLA_NOPE, MLA_ROPE // 2)])
    ftab = rel_bias[_t5_bucket_const()].T.reshape(DIFF_HEADS, 1, REL_TABLE)

    w_mix, w_gate = _w_in_relayout(w_in)
    w_uq_p, w_ukv_p = _mla_up_weights(mla_w_uq, mla_w_ukv)
    ffn1 = (ffn1_w_gate, ffn1_w_up, ffn1_w_down)
    ffn2 = (ffn2_w_gate, ffn2_w_up, ffn2_w_down)
    w_branch_b, w_out_b = w_branch, w_out
    w_kv = jnp.concatenate([_cast_bf16(xa_wk), _cast_bf16(xa_wv)], axis=-1)

    for l in range(depth):
        at = lambda stack: _LayerOf(stack, l)
        h = _ffn(h, row(ffn1_norm[l]), *(at(w) for w in ffn1))
        mla = (mla_cos, mla_sin, row(mla_q_norm[l]), row(mla_kv_norm[l]), at(w_uq_p), at(w_ukv_p))
        ret_p, ssm_p, dt_t, dqt, dk, dvt, mqt, mk, mvt = _norm_proj(h, row(mix_norm[l]), at(w_mix), MIX_OUTS, seq,
                                                                   mla=mla, name="mix_proj")
        y_ret = _retention(ret_p.reshape(batch, seq, -1), ret_cos.reshape(batch, seq, -1),
                           ret_sin.reshape(batch, seq, -1)).reshape(m, BRANCH_W)
        y_ssm = _ssd(ssm_p.reshape(batch, seq, -1), dt_t, ssm_conv_w[l], row(ssm_conv_b[l]),
                     ssm_dt_bias[l].reshape(-1, 1), ssm_a_log[l].reshape(-1, 1),
                     row(jnp.repeat(ssm_d[l], SSM_HEADDIM)), row(ssm_norm[l])).reshape(m, SSM_INNER)
        y_mla = _mla_attention(mqt, mk.reshape(batch, seq, -1), mvt).reshape(m, BRANCH_W)
        lambda_init = 0.8 - 0.6 * math.exp(-0.3 * l)
        y_diff = _diff_attention(dqt, dk.reshape(batch, seq, -1), dvt, positions, ftab, diff_lambda[l],
                                 row(diff_norm[l]), lambda_init).reshape(m, BRANCH_W)
        h = _merge(h, row(mix_norm[l]), (y_ret, y_ssm, y_mla, y_diff), at(w_gate), at(w_branch_b), at(w_out_b))
        xk, xv = _norm_proj(mem2, row(mem_norm[l]), at(w_kv), [(d, BF16, 1.0, False)] * 2, mem.shape[1],
                            tm=mem.shape[1], name="mem_kv")
        h = _cross_attention(h, row(xa_norm[l]), at(xa_wq), xk, xv, at(xa_wo), batch, seq)
        h = _ffn(h, row(ffn2_norm[l]), *(at(w) for w in ffn2),
                 final_g=row(final_norm) if l == depth - 1 else None)
    return h.reshape(batch, seq, d)
```
